```python
import math
import jax, jax.numpy as jnp
from jax import lax
import numpy as np

D_MODEL = 1024
BATCH = 8
SEQ = 2048
DEPTH = 4

CHUNK = 64
N_MIXERS = 2
Q_BLOCK = 128
DA_HEADS = 8
DA_HEAD_DIM = D_MODEL // (2 * DA_HEADS)
T5_BUCKETS = 32
T5_MAX_DIST = 128
CA_HEADS = 16
CA_HEAD_DIM = D_MODEL // CA_HEADS
BAND_PREV = 8
BAND = (BAND_PREV + 1) * CHUNK
REL_CLIP = 4 * CHUNK
FF_DENSE = 256 * int(math.ceil(8 * D_MODEL / 3 / 256))
N_EXPERTS = 8
TOP_K = 2
FF_EXPERT = 7 * D_MODEL // 2
RMS_EPS = 1e-6
SUBLN_EPS = 1e-5
NEG_INF = -1e30
N_EVEN = (DEPTH + 1) // 2
N_ODD = DEPTH // 2

kernel_name = "hybrid_diffattn_chunkattn_moe_trunk"


def rmsnorm(x, g, eps=RMS_EPS):
    x32 = x.astype(jnp.float32)
    y = x32 * lax.rsqrt(jnp.mean(x32 * x32, axis=-1, keepdims=True) + eps)
    return y.astype(x.dtype) * g


def swiglu(h, w_gate_up, w_down):
    g, u = jnp.split(h @ w_gate_up, 2, axis=-1)
    return (jax.nn.silu(g) * u) @ w_down


def t5_bucket(rel):
    nb = T5_BUCKETS // 2
    max_exact = nb // 2
    ret = (rel > 0).astype(jnp.int32) * nb
    n = jnp.abs(rel)
    n_f = jnp.maximum(n, 1).astype(jnp.float32)
    large = max_exact + (jnp.log(n_f / max_exact) / math.log(T5_MAX_DIST / max_exact)
                         * (nb - max_exact)).astype(jnp.int32)
    large = jnp.minimum(large, nb - 1)
    return ret + jnp.where(n < max_exact, n, large)


def _masked_softmax_map(qm, km, bias, allowed, scale):
    s = jnp.einsum('bqhd,bkhd->bhqk', qm, km).astype(jnp.float32) * scale + bias
    return jax.nn.softmax(jnp.where(allowed, s, NEG_INF), axis=-1)


def diff_attention(h, w_qkv, w_o, lq1, lk1, lq2, lk2, subln_g, t5_table, layer_idx):
    B, S, _ = h.shape
    q, k, v = jnp.split(h @ w_qkv, 3, axis=-1)
    q = q.reshape(B, S, DA_HEADS, 2, DA_HEAD_DIM)
    k = k.reshape(B, S, DA_HEADS, 2, DA_HEAD_DIM)
    v = v.reshape(B, S, DA_HEADS, 2 * DA_HEAD_DIM)
    lam_init = 0.8 - 0.6 * math.exp(-0.3 * layer_idx)
    lam = (jnp.exp(jnp.sum(lq1.astype(jnp.float32) * lk1.astype(jnp.float32)))
           - jnp.exp(jnp.sum(lq2.astype(jnp.float32) * lk2.astype(jnp.float32))) + lam_init)
    scale = DA_HEAD_DIM ** -0.5
    pos = jnp.arange(S, dtype=jnp.int32)
    outs = []
    for s0 in range(0, S, Q_BLOCK):
        s1 = s0 + Q_BLOCK
        qb, kb, vb = q[:, s0:s1], k[:, :s1], v[:, :s1]
        qi, ki = pos[s0:s1], pos[:s1]
        bias = jnp.transpose(t5_table[t5_bucket(ki[None, :] - qi[:, None])], (2, 0, 1)).astype(jnp.float32)
        allowed = (ki[None, :] // CHUNK) <= (qi[:, None] // CHUNK)
        p = (_masked_softmax_map(qb[..., 0, :], kb[..., 0, :], bias, allowed, scale)
             - lam * _masked_softmax_map(qb[..., 1, :], kb[..., 1, :], bias, allowed, scale))
        outs.append(jnp.einsum('bhqk,bkhe->bqhe', p.astype(v.dtype), vb))
    o = jnp.concatenate(outs, axis=1)
    o = rmsnorm(o, subln_g, SUBLN_EPS) * (1.0 - lam_init)
    return o.reshape(B, S, D_MODEL) @ w_o


def chunk_attention(h, w_qkv, w_o, rel_table):
    B, S, _ = h.shape
    n_chunks = S // CHUNK
    q, k, v = jnp.split(h @ w_qkv, 3, axis=-1)
    q = q.reshape(B, S, CA_HEADS, CA_HEAD_DIM)
    k = k.reshape(B, S, CA_HEADS, CA_HEAD_DIM)
    v = v.reshape(B, S, CA_HEADS, CA_HEAD_DIM)
    pad = ((0, 0), (BAND_PREV * CHUNK, 0), (0, 0), (0, 0))
    kp, vp = jnp.pad(k, pad), jnp.pad(v, pad)
    i = jnp.arange(CHUNK, dtype=jnp.int32)
    j = jnp.arange(BAND, dtype=jnp.int32)
    dist = BAND_PREV * CHUNK + i[:, None] - j[None, :]
    bias = jnp.transpose(rel_table[jnp.clip(dist, -REL_CLIP, REL_CLIP) + REL_CLIP],
                         (2, 0, 1)).astype(jnp.float32)
    scale = CA_HEAD_DIM ** -0.5

    def one_chunk(c):
        qc = lax.dynamic_slice_in_dim(q, c * CHUNK, CHUNK, axis=1)
        kc = lax.dynamic_slice_in_dim(kp, c * CHUNK, BAND, axis=1)
        vc = lax.dynamic_slice_in_dim(vp, c * CHUNK, BAND, axis=1)
        s = jnp.einsum('bqhd,bkhd->bhqk', qc, kc).astype(jnp.float32) * scale + bias
        valid = j >= (BAND_PREV - c) * CHUNK
        p = jax.nn.softmax(jnp.where(valid, s, NEG_INF), axis=-1)
        return jnp.einsum('bhqk,bkhd->bqhd', p.astype(vc.dtype), vc)

    o = lax.map(one_chunk, jnp.arange(n_chunks, dtype=jnp.int32))
    o = jnp.transpose(o, (1, 0, 2, 3, 4)).reshape(B, S, D_MODEL)
    return o @ w_o


def moe_swiglu(h, router, w_gate_up, w_down):
    B, S, D = h.shape
    hf = h.reshape(-1, D)
    logits = (hf @ router).astype(jnp.float32)
    top_vals, top_idx = lax.top_k(logits, TOP_K)
    gates = jax.nn.softmax(top_vals, axis=-1)
    combine = jnp.sum(jax.nn.one_hot(top_idx, N_EXPERTS, dtype=jnp.float32) * gates[..., None], axis=1)
    combine = combine.astype(h.dtype)
    y = jnp.zeros_like(hf)
    for e in range(N_EXPERTS):
        y = y + combine[:, e:e + 1] * swiglu(hf, w_gate_up[e], w_down[e])
    return y.reshape(B, S, D)


def setup_inputs(seed: int = 0) -> dict:
    key = jax.random.key(seed)
    ks = jax.random.split(key, 21)
    nrm = lambda k, shape, s: jax.random.normal(k, shape, jnp.float32) * s
    D = D_MODEL
    return {
        "x": nrm(ks[0], (BATCH, SEQ, D), 1.0),
        "attn_norm": 1.0 + nrm(ks[1], (DEPTH, D), 0.02),
        "ffn_norm": 1.0 + nrm(ks[2], (DEPTH, D), 0.02),
        "final_norm": 1.0 + nrm(ks[3], (D,), 0.02),
        "t5_rel_bias": nrm(ks[4], (T5_BUCKETS, DA_HEADS), 0.3),
        "diff_w_qkv": nrm(ks[5], (N_EVEN, D, 3 * D), D ** -0.5),
        "diff_w_o": nrm(ks[6], (N_EVEN, D, D), D ** -0.5),
        "diff_lambda_q1": nrm(ks[7], (N_EVEN, DA_HEAD_DIM), 0.1),
        "diff_lambda_k1": nrm(ks[8], (N_EVEN, DA_HEAD_DIM), 0.1),
        "diff_lambda_q2": nrm(ks[9], (N_EVEN, DA_HEAD_DIM), 0.1),
        "diff_lambda_k2": nrm(ks[10], (N_EVEN, DA_HEAD_DIM), 0.1),
        "diff_subln": 1.0 + nrm(ks[11], (N_EVEN, 2 * DA_HEAD_DIM), 0.02),
        "chunk_w_qkv": nrm(ks[12], (N_ODD, D, 3 * D), D ** -0.5),
        "chunk_w_o": nrm(ks[13], (N_ODD, D, D), D ** -0.5),
        "chunk_rel_bias": nrm(ks[14], (N_ODD, 2 * REL_CLIP + 1, CA_HEADS), 0.3),
        "dense_w_gate_up": nrm(ks[15], (N_EVEN, D, 2 * FF_DENSE), D ** -0.5),
        "dense_w_down": nrm(ks[16], (N_EVEN, FF_DENSE, D), FF_DENSE ** -0.5),
        "moe_router": nrm(ks[17], (N_ODD, D, N_EXPERTS), D ** -0.5),
        "moe_w_gate_up": nrm(ks[18], (N_ODD, N_EXPERTS, D, 2 * FF_EXPERT), D ** -0.5),
        "moe_w_down": nrm(ks[19], (N_ODD, N_EXPERTS, FF_EXPERT, D), FF_EXPERT ** -0.5),
    }


def reference(x, attn_norm, ffn_norm, final_norm, t5_rel_bias, diff_w_qkv, diff_w_o,
              diff_lambda_q1, diff_lambda_k1, diff_lambda_q2, diff_lambda_k2, diff_subln,
              chunk_w_qkv, chunk_w_o, chunk_rel_bias, dense_w_gate_up, dense_w_down,
              moe_router, moe_w_gate_up, moe_w_down):
    for i in range(DEPTH):
        j = i // 2
        h = rmsnorm(x, attn_norm[i])
        if i % N_MIXERS == 0:
            x = x + diff_attention(h, diff_w_qkv[j], diff_w_o[j], diff_lambda_q1[j], diff_lambda_k1[j],
                                   diff_lambda_q2[j], diff_lambda_k2[j], diff_subln[j], t5_rel_bias, i)
        else:
            x = x + chunk_attention(h, chunk_w_qkv[j], chunk_w_o[j], chunk_rel_bias[j])
        h = rmsnorm(x, ffn_norm[i])
        if i % 2 == 0:
            x = x + swiglu(h, dense_w_gate_up[j], dense_w_down[j])
        else:
            x = x + moe_swiglu(h, moe_router[j], moe_w_gate_up[j], moe_w_down[j])
    return rmsnorm(x, final_norm)
```

```python
import functools
import math

import numpy as np
import jax
import jax.numpy as jnp
from jax import lax
from jax.experimental import pallas as pl
from jax.experimental.pallas import tpu as pltpu

F32 = jnp.float32
BF16 = jnp.bfloat16

D_MODEL = 1024
DEPTH = 4
CHUNK = 64
DA_HEADS = 8
DA_HEAD_DIM = 64
T5_BUCKETS = 32
T5_MAX_DIST = 128
CA_HEADS = 16
CA_HEAD_DIM = 64
BAND_PREV = 8
REL_CLIP = 4 * CHUNK
N_EXPERTS = 8
RMS_EPS = 1e-6
SUBLN_EPS = 1e-5
NEG_INF = -1e30

LANES = 128
BF16_ROWS = 16
VMEM_LIMIT = 56 * 1024 * 1024
TM = 512
TQ = 256
TT = 256
PAIR = 2 * CA_HEAD_DIM


def _cparams(*sem):
    return pltpu.CompilerParams(dimension_semantics=sem, vmem_limit_bytes=VMEM_LIMIT)


def _rmsnorm(x32, g, eps):
    return x32 * lax.rsqrt(jnp.mean(x32 * x32, axis=-1, keepdims=True) + eps) * g


def _softmax_rows(s):
    m = jnp.max(s, axis=-1, keepdims=True)
    p = jnp.exp(s - m)
    return p / jnp.sum(p, axis=-1, keepdims=True)


def _dot_nt(a, b):
    return lax.dot_general(a, b, (((1,), (1,)), ((), ())), preferred_element_type=F32)


def _norm_matmul_kernel(x_ref, g_ref, w_ref, o_ref):
    h = _rmsnorm(x_ref[...], g_ref[...], RMS_EPS).astype(BF16)
    o_ref[...] = jnp.dot(h, w_ref[...], preferred_element_type=F32).astype(o_ref.dtype)


def _norm_matmul(x, g, w):
    n, d = x.shape
    n_out = w.shape[1]
    return pl.pallas_call(
        _norm_matmul_kernel,
        grid=(n // TM,),
        in_specs=[pl.BlockSpec((TM, d), lambda i: (i, 0)),
                  pl.BlockSpec((1, d), lambda i: (0, 0)),
                  pl.BlockSpec((d, n_out), lambda i: (0, 0))],
        out_specs=pl.BlockSpec((TM, n_out), lambda i: (i, 0)),
        out_shape=jax.ShapeDtypeStruct((n, n_out), BF16),
        compiler_params=_cparams("parallel"),
        name="norm_qkv",
    )(x, g.reshape(1, d), w)


def _matmul_residual_kernel(a_ref, w_ref, x_ref, o_ref):
    o_ref[...] = x_ref[...] + jnp.dot(a_ref[...], w_ref[...], preferred_element_type=F32)


def _matmul_residual(a, w, x):
    n, d = x.shape
    return pl.pallas_call(
        _matmul_residual_kernel,
        grid=(n // TM,),
        in_specs=[pl.BlockSpec((TM, a.shape[1]), lambda i: (i, 0)),
                  pl.BlockSpec(w.shape, lambda i: (0, 0)),
                  pl.BlockSpec((TM, d), lambda i: (i, 0))],
        out_specs=pl.BlockSpec((TM, d), lambda i: (i, 0)),
        out_shape=jax.ShapeDtypeStruct((n, d), F32),
        compiler_params=_cparams("parallel"),
        name="wo_residual",
    )(a, w, x)


def _t5_bucket_np(rel):
    nb = T5_BUCKETS // 2
    max_exact = nb // 2
    ret = (rel > 0).astype(np.int32) * nb
    n = np.abs(rel)
    n_f = np.maximum(n, 1).astype(np.float32)
    large = max_exact + (np.log(n_f / np.float32(max_exact)) / np.float32(math.log(T5_MAX_DIST / max_exact))
                         * np.float32(nb - max_exact)).astype(np.int32)
    large = np.minimum(large, nb - 1)
    return ret + np.where(n < max_exact, n, large)


def _t5_bias_tiles(table, seq):
    i = np.arange(TQ)[:, None]
    j = np.arange(TQ)[None, :]
    tiles = []
    for d in range(seq // TQ):
        b = table[_t5_bucket_np((j - i) - d * TQ)].astype(F32)
        if d == 0:
            allowed = (j // CHUNK) <= (i // CHUNK)
            b = jnp.where(allowed[:, :, None], b, NEG_INF)
        tiles.append(b)
    return jnp.transpose(jnp.stack(tiles), (3, 0, 1, 2))


def _diff_attn_kernel(q_ref, k_ref, v_ref, bias_ref, lam_ref, g_ref, o_ref, *, lam_init):
    lv = lam_ref[...]
    lam = (jnp.exp(jnp.sum(lv[0:1] * lv[1:2], axis=-1, keepdims=True))
           - jnp.exp(jnp.sum(lv[2:3] * lv[3:4], axis=-1, keepdims=True)) + lam_init)
    first_map = lax.broadcasted_iota(jnp.int32, (1, 2 * DA_HEAD_DIM), 1) < DA_HEAD_DIM
    scale = DA_HEAD_DIM ** -0.5
    seq = q_ref.shape[0]
    for qi in range(seq // TQ):
        q = q_ref[qi * TQ:(qi + 1) * TQ, :] * jnp.asarray(scale, BF16)
        zero = jnp.zeros_like(q)
        n_k = (qi + 1) * TQ
        k = k_ref[0:n_k, :]
        v = v_ref[0:n_k, :]
        bias = jnp.concatenate([bias_ref[qi - kj] for kj in range(qi + 1)], axis=1)
        p1 = _softmax_rows(_dot_nt(jnp.where(first_map, q, zero), k) + bias)
        p2 = _softmax_rows(_dot_nt(jnp.where(first_map, zero, q), k) + bias)
        p = (p1 - lam * p2).astype(BF16)
        o = jnp.dot(p, v, preferred_element_type=F32)
        o = _rmsnorm(o, g_ref[...], SUBLN_EPS) * (1.0 - lam_init)
        o_ref[qi * TQ:(qi + 1) * TQ, :] = o.astype(o_ref.dtype)


def _diff_attention(qkv, bias_tiles, lam_vecs, subln_g, lam_init):
    b, s, _ = qkv.shape
    hd = 2 * DA_HEAD_DIM
    n_off = bias_tiles.shape[1]
    return pl.pallas_call(
        functools.partial(_diff_attn_kernel, lam_init=lam_init),
        grid=(b, DA_HEADS),
        in_specs=[pl.BlockSpec((None, s, hd), lambda i, h: (i, 0, h)),
                  pl.BlockSpec((None, s, hd), lambda i, h: (i, 0, DA_HEADS + h)),
                  pl.BlockSpec((None, s, hd), lambda i, h: (i, 0, 2 * DA_HEADS + h)),
                  pl.BlockSpec((None, n_off, TQ, TQ), lambda i, h: (h, 0, 0, 0)),
                  pl.BlockSpec((4, DA_HEAD_DIM), lambda i, h: (0, 0)),
                  pl.BlockSpec((1, hd), lambda i, h: (0, 0))],
        out_specs=pl.BlockSpec((None, s, hd), lambda i, h: (i, 0, h)),
        out_shape=jax.ShapeDtypeStruct((b, s, D_MODEL), BF16),
        compiler_params=_cparams("parallel", "parallel"),
        name="diff_attn",
    )(qkv, qkv, qkv, bias_tiles, lam_vecs, subln_g.reshape(1, hd))


BAND_KEYS = BAND_PREV * CHUNK + TQ


def _band_bias(rel_table):
    i = np.arange(TQ)[:, None]
    j = np.arange(BAND_KEYS)[None, :]
    dist = BAND_PREV * CHUNK + i - j
    b = rel_table[np.clip(dist, -REL_CLIP, REL_CLIP) + REL_CLIP].astype(F32)
    q_chunk, k_chunk = i // CHUNK, j // CHUNK
    allowed = (k_chunk >= q_chunk) & (k_chunk <= q_chunk + BAND_PREV)
    return jnp.transpose(jnp.where(allowed[:, :, None], b, NEG_INF), (2, 0, 1))


def _chunk_attn_kernel(q_ref, k_ref, v_ref, bias_ref, o_ref):
    first_head = lax.broadcasted_iota(jnp.int32, (1, PAIR), 1) < CA_HEAD_DIM
    scale = CA_HEAD_DIM ** -0.5
    seq = q_ref.shape[0]
    for qi in range(seq // TQ):
        q0 = qi * TQ
        k0 = max(0, q0 - BAND_PREV * CHUNK)
        n_k = q0 + TQ - k0
        q = q_ref[q0:q0 + TQ, :] * jnp.asarray(scale, BF16)
        zero = jnp.zeros_like(q)
        k = k_ref[k0:q0 + TQ, :]
        v = v_ref[k0:q0 + TQ, :]
        outs = []
        for hh in range(2):
            qm = jnp.where(first_head, q, zero) if hh == 0 else jnp.where(first_head, zero, q)
            s = _dot_nt(qm, k) + bias_ref[hh, :, BAND_KEYS - n_k:]
            p = _softmax_rows(s).astype(BF16)
            outs.append(jnp.dot(p, v, preferred_element_type=F32))
        o_ref[q0:q0 + TQ, :] = jnp.where(first_head, outs[0], outs[1]).astype(o_ref.dtype)


def _chunk_attention(qkv, band_bias):
    b, s, _ = qkv.shape
    n_pairs = CA_HEADS // 2
    return pl.pallas_call(
        _chunk_attn_kernel,
        grid=(b, n_pairs),
        in_specs=[pl.BlockSpec((None, s, PAIR), lambda i, p: (i, 0, p)),
                  pl.BlockSpec((None, s, PAIR), lambda i, p: (i, 0, n_pairs + p)),
                  pl.BlockSpec((None, s, PAIR), lambda i, p: (i, 0, 2 * n_pairs + p)),
                  pl.BlockSpec((2, TQ, BAND_KEYS), lambda i, p: (p, 0, 0))],
        out_specs=pl.BlockSpec((None, s, PAIR), lambda i, p: (i, 0, p)),
        out_shape=jax.ShapeDtypeStruct((b, s, D_MODEL), BF16),
        compiler_params=_cparams("parallel", "parallel"),
        name="chunk_attn",
    )(qkv, qkv, qkv, band_bias)


def _swiglu_partial(h, wg, wu, wd):
    g = jnp.dot(h, wg, preferred_element_type=F32)
    u = jnp.dot(h, wu, preferred_element_type=F32)
    a = (g * (1.0 / (1.0 + jnp.exp(-g))) * u).astype(BF16)
    return jnp.dot(a, wd, preferred_element_type=F32)


def _dense_ffn_kernel(x_ref, g_ref, wg_ref, wu_ref, wd_ref, o_ref, h_ref, acc_ref):
    f = pl.program_id(1)

    @pl.when(f == 0)
    def _():
        h_ref[...] = _rmsnorm(x_ref[...], g_ref[...], RMS_EPS).astype(BF16)
        acc_ref[...] = jnp.zeros_like(acc_ref)

    acc_ref[...] += _swiglu_partial(h_ref[...], wg_ref[...], wu_ref[...], wd_ref[...])

    @pl.when(f == pl.num_programs(1) - 1)
    def _():
        o_ref[...] = x_ref[...] + acc_ref[...]


def _dense_ffn(x, g, w_gate_up, w_down, tf):
    n, d = x.shape
    ff = w_down.shape[0]
    nf = ff // tf
    return pl.pallas_call(
        _dense_ffn_kernel,
        grid=(n // TM, nf),
        in_specs=[pl.BlockSpec((TM, d), lambda i, f: (i, 0)),
                  pl.BlockSpec((1, d), lambda i, f: (0, 0)),
                  pl.BlockSpec((d, tf), lambda i, f: (0, f)),
                  pl.BlockSpec((d, tf), lambda i, f: (0, nf + f)),
                  pl.BlockSpec((tf, d), lambda i, f: (f, 0))],
        out_specs=pl.BlockSpec((TM, d), lambda i, f: (i, 0)),
        out_shape=jax.ShapeDtypeStruct((n, d), F32),
        scratch_shapes=[pltpu.VMEM((TM, d), BF16), pltpu.VMEM((TM, d), F32)],
        compiler_params=_cparams("parallel", "arbitrary"),
        name="dense_ffn",
    )(x, g.reshape(1, d), w_gate_up, w_gate_up, w_down)


def _router_kernel(x_ref, g_ref, r_ref, h_ref, gate_ref, mask_ref, mask_t_ref, cnt_ref):
    h = _rmsnorm(x_ref[...], g_ref[...], RMS_EPS)
    h_ref[...] = h.astype(BF16)
    logits = jnp.dot(h, r_ref[...], preferred_element_type=F32, precision=lax.Precision.HIGHEST)
    lane = lax.broadcasted_iota(jnp.int32, logits.shape, 1)
    logits = jnp.where(lane < N_EXPERTS, logits, -jnp.inf)
    v1 = jnp.max(logits, axis=-1, keepdims=True)
    i1 = jnp.min(jnp.where(logits == v1, lane, LANES), axis=-1, keepdims=True)
    m1 = lane == i1
    rest = jnp.where(m1, -jnp.inf, logits)
    v2 = jnp.max(rest, axis=-1, keepdims=True)
    i2 = jnp.min(jnp.where(rest == v2, lane, LANES), axis=-1, keepdims=True)
    m2 = lane == i2
    e2 = jnp.exp(v2 - v1)
    g1 = 1.0 / (1.0 + e2)
    g2 = e2 / (1.0 + e2)
    gate_ref[...] = jnp.where(m1, g1, 0.0) + jnp.where(m2, g2, 0.0)
    mask = jnp.where(m1 | m2, 1.0, 0.0)
    mask_ref[...] = mask
    mask_t_ref[...] = mask.T[0:N_EXPERTS, :]
    cnt_ref[...] = jnp.broadcast_to(jnp.sum(mask, axis=0, keepdims=True), cnt_ref.shape)


def _router(x, g, router):
    n, d = x.shape
    t = n // TT
    r_pad = jnp.zeros((d, LANES), F32).at[:, :N_EXPERTS].set(router)
    return pl.pallas_call(
        _router_kernel,
        grid=(t,),
        in_specs=[pl.BlockSpec((TT, d), lambda i: (i, 0)),
                  pl.BlockSpec((1, d), lambda i: (0, 0)),
                  pl.BlockSpec((d, LANES), lambda i: (0, 0))],
        out_specs=[pl.BlockSpec((TT, d), lambda i: (i, 0)),
                   pl.BlockSpec((TT, LANES), lambda i: (i, 0)),
                   pl.BlockSpec((TT, LANES), lambda i: (i, 0)),
                   pl.BlockSpec((N_EXPERTS, TT), lambda i: (0, i)),
                   pl.BlockSpec((None, 8, LANES), lambda i: (i, 0, 0))],
        out_shape=[jax.ShapeDtypeStruct((n, d), BF16),
                   jax.ShapeDtypeStruct((n, LANES), F32),
                   jax.ShapeDtypeStruct((n, LANES), F32),
                   jax.ShapeDtypeStruct((N_EXPERTS, n), F32),
                   jax.ShapeDtypeStruct((t, 8, LANES), F32)],
        compiler_params=_cparams("parallel"),
        name="router",
    )(x, g.reshape(1, d), r_pad)


def _segment_copies(n_rows, make_copy):
    off = jnp.int32(0)
    size = TT
    while size >= BF16_ROWS:
        take = (n_rows & size) != 0

        @pl.when(take)
        def _(off=off, size=size):
            make_copy(off, size)

        off = off + jnp.where(take, size, 0)
        size //= 2


def _dispatch_kernel(start_ref, rows_ref, mask_t_ref, h_ref, xs_in_ref, xs_ref, blk_ref, sem_ref):
    del xs_in_ref
    t = pl.program_id(0)
    slot = t % 2
    mask_t = mask_t_ref[...]
    row = lax.broadcasted_iota(jnp.int32, (TT, TT), 0)
    col = lax.broadcasted_iota(jnp.int32, (TT, TT), 1)
    before = jnp.where(row < col, 1.0, 0.0).astype(BF16)
    rank = jnp.dot(mask_t.astype(BF16), before, preferred_element_type=F32)
    row_f = row.astype(F32)
    h = h_ref[...]
    for e in range(N_EXPERTS):
        pick = (row_f == rank[e:e + 1, :]) & (mask_t[e:e + 1, :] > 0.0)
        sel = jnp.where(pick, 1.0, 0.0).astype(BF16)
        blk_ref[slot, e] = jnp.dot(sel, h, preferred_element_type=F32).astype(BF16)

    def copies(step, step_slot, wait):
        for e in range(N_EXPERTS):
            start = pl.multiple_of(start_ref[step * N_EXPERTS + e], BF16_ROWS)

            def make_copy(off, size, e=e, start=start):
                src = blk_ref.at[step_slot, e, pl.ds(pl.multiple_of(off, BF16_ROWS), size)]
                dst = xs_ref.at[pl.ds(pl.multiple_of(start + off, BF16_ROWS), size)]
                cp = pltpu.make_async_copy(src, dst, sem_ref.at[step_slot])
                cp.wait() if wait else cp.start()

            _segment_copies(rows_ref[step * N_EXPERTS + e], make_copy)

    copies(t, slot, wait=False)

    @pl.when(t > 0)
    def _():
        copies(t - 1, 1 - slot, wait=True)

    @pl.when(t == pl.num_programs(0) - 1)
    def _():
        copies(t, slot, wait=True)


def _dispatch(seg_start, seg_rows, mask_t, h, n_rows):
    n, d = h.shape
    t = n // TT
    xs0 = jnp.zeros((n_rows, d), BF16)
    grid_spec = pltpu.PrefetchScalarGridSpec(
        num_scalar_prefetch=2,
        grid=(t,),
        in_specs=[pl.BlockSpec((N_EXPERTS, TT), lambda i, *_: (0, i)),
                  pl.BlockSpec((TT, d), lambda i, *_: (i, 0)),
                  pl.BlockSpec(memory_space=pl.ANY)],
        out_specs=pl.BlockSpec(memory_space=pl.ANY),
        scratch_shapes=[pltpu.VMEM((2, N_EXPERTS, TT, d), BF16), pltpu.SemaphoreType.DMA((2,))],
    )
    return pl.pallas_call(
        _dispatch_kernel,
        grid_spec=grid_spec,
        out_shape=jax.ShapeDtypeStruct((n_rows, d), BF16),
        input_output_aliases={4: 0},
        compiler_params=_cparams("arbitrary"),
        name="moe_dispatch",
    )(seg_start, seg_rows, mask_t, h, xs0)


def _expert_kernel(tile_expert_ref, tile_valid_ref, xs_ref, wg_ref, wu_ref, wd_ref, o_ref, acc_ref):
    del tile_expert_ref
    j = pl.program_id(0)
    f = pl.program_id(1)
    last = pl.num_programs(1) - 1
    valid = tile_valid_ref[j] > 0

    @pl.when(valid)
    def _():
        @pl.when(f == 0)
        def _():
            acc_ref[...] = jnp.zeros_like(acc_ref)

        acc_ref[...] += _swiglu_partial(xs_ref[...], wg_ref[...], wu_ref[...], wd_ref[...])

        @pl.when(f == last)
        def _():
            o_ref[...] = acc_ref[...].astype(o_ref.dtype)

    @pl.when(jnp.logical_not(valid) & (f == last))
    def _():
        o_ref[...] = jnp.zeros_like(o_ref)


def _expert_ffn(tile_expert, tile_valid, xs, w_gate_up, w_down, tf):
    n_rows, d = xs.shape
    ff = w_down.shape[1]
    nf = ff // tf

    def f_eff(j, f, tv):
        return jnp.where(tv[j] > 0, f, nf - 1)

    grid_spec = pltpu.PrefetchScalarGridSpec(
        num_scalar_prefetch=2,
        grid=(n_rows // TM, nf),
        in_specs=[pl.BlockSpec((TM, d), lambda j, f, te, tv: (j, 0)),
                  pl.BlockSpec((None, d, tf), lambda j, f, te, tv: (te[j], 0, f_eff(j, f, tv))),
                  pl.BlockSpec((None, d, tf), lambda j, f, te, tv: (te[j], 0, nf + f_eff(j, f, tv))),
                  pl.BlockSpec((None, tf, d), lambda j, f, te, tv: (te[j], f_eff(j, f, tv), 0))],
        out_specs=pl.BlockSpec((TM, d), lambda j, f, te, tv: (j, 0)),
        scratch_shapes=[pltpu.VMEM((TM, d), F32)],
    )
    return pl.pallas_call(
        _expert_kernel,
        grid_spec=grid_spec,
        out_shape=jax.ShapeDtypeStruct((n_rows, d), BF16),
        compiler_params=_cparams("parallel", "arbitrary"),
        name="moe_experts",
    )(tile_expert, tile_valid, xs, w_gate_up, w_gate_up, w_down)


def _combine_kernel(start_ref, rows_ref, x_ref, gate_ref, mask_ref, ys_ref, o_ref, win_ref, sem_ref):
    t = pl.program_id(0)
    n_t = pl.num_programs(0)
    slot = t % 2

    def copies(step, step_slot, wait):
        for e in range(N_EXPERTS):
            start = pl.multiple_of(start_ref[step * N_EXPERTS + e], BF16_ROWS)

            def make_copy(off, size, e=e, start=start):
                src = ys_ref.at[pl.ds(pl.multiple_of(start + off, BF16_ROWS), size)]
                dst = win_ref.at[step_slot, e, pl.ds(pl.multiple_of(off, BF16_ROWS), size)]
                cp = pltpu.make_async_copy(src, dst, sem_ref.at[step_slot])
                cp.wait() if wait else cp.start()

            _segment_copies(rows_ref[step * N_EXPERTS + e], make_copy)

    @pl.when(t == 0)
    def _():
        win_ref[...] = jnp.zeros_like(win_ref)
        copies(0, 0, wait=False)

    @pl.when(t + 1 < n_t)
    def _():
        copies(t + 1, 1 - slot, wait=False)

    copies(t, slot, wait=True)

    mask = mask_ref[...]
    gate = gate_ref[...]
    row = lax.broadcasted_iota(jnp.int32, (TT, TT), 0)
    col = lax.broadcasted_iota(jnp.int32, (TT, TT), 1)
    before = jnp.where(col < row, 1.0, 0.0).astype(BF16)
    rank = jnp.dot(before, mask.astype(BF16), preferred_element_type=F32)
    col_f = col.astype(F32)
    y = x_ref[...]
    for e in range(N_EXPERTS):
        pick = (col_f == rank[:, e:e + 1]) & (mask[:, e:e + 1] > 0.0)
        sel = jnp.where(pick, 1.0, 0.0).astype(BF16)
        y = y + gate[:, e:e + 1] * jnp.dot(sel, win_ref[slot, e], preferred_element_type=F32)
    o_ref[...] = y


def _combine(seg_start, seg_rows, x, gate, mask, ys):
    n, d = x.shape
    t = n // TT
    grid_spec = pltpu.PrefetchScalarGridSpec(
        num_scalar_prefetch=2,
        grid=(t,),
        in_specs=[pl.BlockSpec((TT, d), lambda i, *_: (i, 0)),
                  pl.BlockSpec((TT, LANES), lambda i, *_: (i, 0)),
                  pl.BlockSpec((TT, LANES), lambda i, *_: (i, 0)),
                  pl.BlockSpec(memory_space=pl.ANY)],
        out_specs=pl.BlockSpec((TT, d), lambda i, *_: (i, 0)),
        scratch_shapes=[pltpu.VMEM((2, N_EXPERTS, TT, d), BF16), pltpu.SemaphoreType.DMA((2,))],
    )
    return pl.pallas_call(
        _combine_kernel,
        grid_spec=grid_spec,
        out_shape=jax.ShapeDtypeStruct((n, d), F32),
        compiler_params=_cparams("arbitrary"),
        name="moe_combine",
    )(seg_start, seg_rows, x, gate, mask, ys)


def _round_up(a, m):
    return (a + m - 1) // m * m


def _moe_ffn(x, g, router, w_gate_up, w_down, tf):
    n, d = x.shape
    t = n // TT
    h, gate, mask, mask_t, cnt = _router(x, g, router)
    cnt = cnt[:, 0, :N_EXPERTS].astype(jnp.int32)
    seg_rows = _round_up(cnt, BF16_ROWS)
    region = _round_up(jnp.sum(seg_rows, axis=0), TM)
    region_end = jnp.cumsum(region)
    seg_start = (region_end - region)[None, :] + jnp.cumsum(seg_rows, axis=0) - seg_rows
    n_tiles = (_round_up(2 * n + t * N_EXPERTS * (BF16_ROWS - 1), TM) + N_EXPERTS * TM) // TM
    tile_end = region_end // TM
    tile_ids = jnp.arange(n_tiles, dtype=jnp.int32)
    tile_valid = (tile_ids < tile_end[-1]).astype(jnp.int32)
    tile_expert = jnp.searchsorted(tile_end, jnp.minimum(tile_ids, tile_end[-1] - 1), side="right")
    tile_expert = jnp.minimum(tile_expert, N_EXPERTS - 1).astype(jnp.int32)
    seg_start = seg_start.reshape(-1).astype(jnp.int32)
    seg_rows = seg_rows.reshape(-1).astype(jnp.int32)

    xs = _dispatch(seg_start, seg_rows, mask_t, h, n_tiles * TM)
    ys = _expert_ffn(tile_expert, tile_valid, xs, w_gate_up, w_down, tf)
    return _combine(seg_start, seg_rows, x, gate, mask, ys)


def _final_norm_kernel(x_ref, g_ref, o_ref):
    o_ref[...] = _rmsnorm(x_ref[...], g_ref[...], RMS_EPS)


def _final_norm(x, g):
    n, d = x.shape
    return pl.pallas_call(
        _final_norm_kernel,
        grid=(n // TM,),
        in_specs=[pl.BlockSpec((TM, d), lambda i: (i, 0)), pl.BlockSpec((1, d), lambda i: (0, 0))],
        out_specs=pl.BlockSpec((TM, d), lambda i: (i, 0)),
        out_shape=jax.ShapeDtypeStruct((n, d), F32),
        compiler_params=_cparams("parallel"),
        name="final_norm",
    )(x, g.reshape(1, d))


def kernel(x, attn_norm, ffn_norm, final_norm, t5_rel_bias, diff_w_qkv, diff_w_o, diff_lambda_q1, diff_lambda_k1, diff_lambda_q2, diff_lambda_k2, diff_subln, chunk_w_qkv, chunk_w_o, chunk_rel_bias, dense_w_gate_up, dense_w_down, moe_router, moe_w_gate_up, moe_w_down):
    b, s, d = x.shape
    n = b * s
    xf = x.reshape(n, d)
    t5_tiles = _t5_bias_tiles(t5_rel_bias, s)
    for i in range(DEPTH):
        j = i // 2
        if i % 2 == 0:
            qkv = _norm_matmul(xf, attn_norm[i], diff_w_qkv[j].astype(BF16))
            lam_init = 0.8 - 0.6 * math.exp(-0.3 * i)
            lam_vecs = jnp.stack([diff_lambda_q1[j], diff_lambda_k1[j], diff_lambda_q2[j], diff_lambda_k2[j]])
            a = _diff_attention(qkv.reshape(b, s, 3 * d), t5_tiles, lam_vecs, diff_subln[j], lam_init)
            xf = _matmul_residual(a.reshape(n, d), diff_w_o[j].astype(BF16), xf)
            xf = _dense_ffn(xf, ffn_norm[i], dense_w_gate_up[j].astype(BF16), dense_w_down[j].astype(BF16),
                            tf=dense_w_down.shape[1] // 2)
        else:
            qkv = _norm_matmul(xf, attn_norm[i], chunk_w_qkv[j].astype(BF16))
            a = _chunk_attention(qkv.reshape(b, s, 3 * d), _band_bias(chunk_rel_bias[j]))
            xf = _matmul_residual(a.reshape(n, d), chunk_w_o[j].astype(BF16), xf)
            xf = _moe_ffn(xf, ffn_norm[i], moe_router[j], moe_w_gate_up[j].astype(BF16),
                          moe_w_down[j].astype(BF16), tf=moe_w_down.shape[2] // 2)
    return _final_norm(xf, final_norm).reshape(b, s, d)
```

```python
import functools
import math

import numpy as np
import jax
import jax.numpy as jnp
from jax import lax
from jax.experimental import pallas as pl
from jax.experimental.pallas import tpu as pltpu

F32 = jnp.float32
BF16 = jnp.bfloat16

D_MODEL = 1024
DEPTH = 4
CHUNK = 64
DA_HEADS = 8
DA_HEAD_DIM = 64
T5_BUCKETS = 32
T5_MAX_DIST = 128
CA_HEADS = 16
CA_HEAD_DIM = 64
BAND_PREV = 8
REL_CLIP = 4 * CHUNK
N_EXPERTS = 8
RMS_EPS = 1e-6
SUBLN_EPS = 1e-5
NEG_INF = -1e30

LANES = 128
BF16_ROWS = 16
VMEM_LIMIT = 56 * 1024 * 1024
TM = 512
TQ = 256
TT = 256
PAIR = 2 * CA_HEAD_DIM


def _round_up(a, m):
    return (a + m - 1) // m * m


def _cparams(*sem):
    return pltpu.CompilerParams(dimension_semantics=sem, vmem_limit_bytes=VMEM_LIMIT)


def _rmsnorm(x32, g, eps):
    return x32 * lax.rsqrt(jnp.mean(x32 * x32, axis=-1, keepdims=True) + eps) * g


def _softmax_rows(s):
    m = jnp.max(s, axis=-1, keepdims=True)
    p = jnp.exp(s - m)
    return p / jnp.sum(p, axis=-1, keepdims=True)


def _dot_nt(a, b):
    return lax.dot_general(a, b, (((1,), (1,)), ((), ())), preferred_element_type=F32)


def _norm_matmul_kernel(x_ref, g_ref, w_ref, o_ref):
    h = _rmsnorm(x_ref[...], g_ref[...], RMS_EPS).astype(BF16)
    o_ref[...] = jnp.dot(h, w_ref[...], preferred_element_type=F32).astype(o_ref.dtype)


def _norm_matmul(x, g, w):
    n, d = x.shape
    n_out = w.shape[1]
    return pl.pallas_call(
        _norm_matmul_kernel,
        grid=(n // TM,),
        in_specs=[pl.BlockSpec((TM, d), lambda i: (i, 0)),
                  pl.BlockSpec((1, d), lambda i: (0, 0)),
                  pl.BlockSpec((d, n_out), lambda i: (0, 0))],
        out_specs=pl.BlockSpec((TM, n_out), lambda i: (i, 0)),
        out_shape=jax.ShapeDtypeStruct((n, n_out), BF16),
        compiler_params=_cparams("parallel"),
        name="norm_qkv",
    )(x, g.reshape(1, d), w)


def _matmul_residual_kernel(a_ref, w_ref, x_ref, o_ref):
    o_ref[...] = x_ref[...] + jnp.dot(a_ref[...], w_ref[...], preferred_element_type=F32)


def _matmul_residual(a, w, x):
    n, d = x.shape
    return pl.pallas_call(
        _matmul_residual_kernel,
        grid=(n // TM,),
        in_specs=[pl.BlockSpec((TM, a.shape[1]), lambda i: (i, 0)),
                  pl.BlockSpec(w.shape, lambda i: (0, 0)),
                  pl.BlockSpec((TM, d), lambda i: (i, 0))],
        out_specs=pl.BlockSpec((TM, d), lambda i: (i, 0)),
        out_shape=jax.ShapeDtypeStruct((n, d), F32),
        compiler_params=_cparams("parallel"),
        name="wo_residual",
    )(a, w, x)


def _t5_bucket_np(rel):
    nb = T5_BUCKETS // 2
    max_exact = nb // 2
    ret = (rel > 0).astype(np.int32) * nb
    n = np.abs(rel)
    n_f = np.maximum(n, 1).astype(np.float32)
    large = max_exact + (np.log(n_f / np.float32(max_exact)) / np.float32(math.log(T5_MAX_DIST / max_exact))
                         * np.float32(nb - max_exact)).astype(np.int32)
    large = np.minimum(large, nb - 1)
    return ret + np.where(n < max_exact, n, large)


def _toeplitz(w, n_rows, n_cols):
    h, r = w.shape
    full = jnp.broadcast_to(w[:, None, :], (h, n_rows, r)).reshape(h, n_rows * r)
    return full[:, :n_rows * (r - 1)].reshape(h, n_rows, r - 1)[:, :, :n_cols]


def _t5_bias_rows(table, seq):
    period = _round_up(seq + TQ, LANES)
    m = np.arange(period)
    diag = np.where(m < seq, m, np.where(m > period - TQ, m - period, 0))
    w = table[_t5_bucket_np(diag - (seq - TQ))].astype(F32).T
    bias = _toeplitz(w, TQ, seq)
    i = np.arange(TQ)[:, None]
    j = np.arange(seq)[None, :] - (seq - TQ)
    allowed = (j < 0) | ((j // CHUNK) <= (i // CHUNK))
    return jnp.where(allowed[None], bias, NEG_INF)


def _diff_attn_kernel(q_ref, k_ref, v_ref, bias_ref, lam_ref, g_ref, o_ref, *, lam_init):
    lv = lam_ref[...]
    lam = (jnp.exp(jnp.sum(lv[0:1] * lv[1:2], axis=-1, keepdims=True))
           - jnp.exp(jnp.sum(lv[2:3] * lv[3:4], axis=-1, keepdims=True)) + lam_init)
    first_map = lax.broadcasted_iota(jnp.int32, (1, 2 * DA_HEAD_DIM), 1) < DA_HEAD_DIM
    scale = DA_HEAD_DIM ** -0.5
    seq = q_ref.shape[0]
    for qi in range(seq // TQ):
        q = q_ref[qi * TQ:(qi + 1) * TQ, :] * jnp.asarray(scale, BF16)
        zero = jnp.zeros_like(q)
        n_k = (qi + 1) * TQ
        k = k_ref[0:n_k, :]
        v = v_ref[0:n_k, :]
        bias = bias_ref[:, seq - n_k:]
        p1 = _softmax_rows(_dot_nt(jnp.where(first_map, q, zero), k) + bias)
        p2 = _softmax_rows(_dot_nt(jnp.where(first_map, zero, q), k) + bias)
        p = (p1 - lam * p2).astype(BF16)
        o = jnp.dot(p, v, preferred_element_type=F32)
        o = _rmsnorm(o, g_ref[...], SUBLN_EPS) * (1.0 - lam_init)
        o_ref[qi * TQ:(qi + 1) * TQ, :] = o.astype(o_ref.dtype)


def _diff_attention(qkv, bias_rows, lam_vecs, subln_g, lam_init):
    b, s, _ = qkv.shape
    hd = 2 * DA_HEAD_DIM
    return pl.pallas_call(
        functools.partial(_diff_attn_kernel, lam_init=lam_init),
        grid=(b, DA_HEADS),
        in_specs=[pl.BlockSpec((None, s, hd), lambda i, h: (i, 0, h)),
                  pl.BlockSpec((None, s, hd), lambda i, h: (i, 0, DA_HEADS + h)),
                  pl.BlockSpec((None, s, hd), lambda i, h: (i, 0, 2 * DA_HEADS + h)),
                  pl.BlockSpec((None, TQ, s), lambda i, h: (h, 0, 0)),
                  pl.BlockSpec((4, DA_HEAD_DIM), lambda i, h: (0, 0)),
                  pl.BlockSpec((1, hd), lambda i, h: (0, 0))],
        out_specs=pl.BlockSpec((None, s, hd), lambda i, h: (i, 0, h)),
        out_shape=jax.ShapeDtypeStruct((b, s, D_MODEL), BF16),
        compiler_params=_cparams("parallel", "parallel"),
        name="diff_attn",
    )(qkv, qkv, qkv, bias_rows, lam_vecs, subln_g.reshape(1, hd))


BAND_KEYS = BAND_PREV * CHUNK + TQ


def _band_bias(rel_table):
    period = _round_up(BAND_KEYS + TQ, LANES)
    m = np.arange(period)
    diag = np.where(m < BAND_KEYS, m, np.where(m > period - TQ, m - period, 0))
    dist = BAND_PREV * CHUNK - diag
    w = rel_table[np.clip(dist, -REL_CLIP, REL_CLIP) + REL_CLIP].astype(F32).T
    bias = _toeplitz(w, TQ, BAND_KEYS)
    q_chunk = np.arange(TQ)[:, None] // CHUNK
    k_chunk = np.arange(BAND_KEYS)[None, :] // CHUNK
    allowed = (k_chunk >= q_chunk) & (k_chunk <= q_chunk + BAND_PREV)
    return jnp.where(allowed[None], bias, NEG_INF)


def _chunk_attn_kernel(q_ref, k_ref, v_ref, bias_ref, o_ref):
    first_head = lax.broadcasted_iota(jnp.int32, (1, PAIR), 1) < CA_HEAD_DIM
    scale = CA_HEAD_DIM ** -0.5
    seq = q_ref.shape[0]
    for qi in range(seq // TQ):
        q0 = qi * TQ
        k0 = max(0, q0 - BAND_PREV * CHUNK)
        n_k = q0 + TQ - k0
        q = q_ref[q0:q0 + TQ, :] * jnp.asarray(scale, BF16)
        zero = jnp.zeros_like(q)
        k = k_ref[k0:q0 + TQ, :]
        v = v_ref[k0:q0 + TQ, :]
        outs = []
        for hh in range(2):
            qm = jnp.where(first_head, q, zero) if hh == 0 else jnp.where(first_head, zero, q)
            s = _dot_nt(qm, k) + bias_ref[hh, :, BAND_KEYS - n_k:]
            p = _softmax_rows(s).astype(BF16)
            outs.append(jnp.dot(p, v, preferred_element_type=F32))
        o_ref[q0:q0 + TQ, :] = jnp.where(first_head, outs[0], outs[1]).astype(o_ref.dtype)


def _chunk_attention(qkv, band_bias):
    b, s, _ = qkv.shape
    n_pairs = CA_HEADS // 2
    return pl.pallas_call(
        _chunk_attn_kernel,
        grid=(b, n_pairs),
        in_specs=[pl.BlockSpec((None, s, PAIR), lambda i, p: (i, 0, p)),
                  pl.BlockSpec((None, s, PAIR), lambda i, p: (i, 0, n_pairs + p)),
                  pl.BlockSpec((None, s, PAIR), lambda i, p: (i, 0, 2 * n_pairs + p)),
                  pl.BlockSpec((2, TQ, BAND_KEYS), lambda i, p: (p, 0, 0))],
        out_specs=pl.BlockSpec((None, s, PAIR), lambda i, p: (i, 0, p)),
        out_shape=jax.ShapeDtypeStruct((b, s, D_MODEL), BF16),
        compiler_params=_cparams("parallel", "parallel"),
        name="chunk_attn",
    )(qkv, qkv, qkv, band_bias)


def _swiglu_partial(h, wg, wu, wd):
    g = jnp.dot(h, wg, preferred_element_type=F32)
    u = jnp.dot(h, wu, preferred_element_type=F32)
    a = (g * (1.0 / (1.0 + jnp.exp(-g))) * u).astype(BF16)
    return jnp.dot(a, wd, preferred_element_type=F32)


def _dense_ffn_kernel(x_ref, g_ref, wg_ref, wu_ref, wd_ref, o_ref, h_ref, acc_ref):
    f = pl.program_id(1)

    @pl.when(f == 0)
    def _():
        h_ref[...] = _rmsnorm(x_ref[...], g_ref[...], RMS_EPS).astype(BF16)
        acc_ref[...] = jnp.zeros_like(acc_ref)

    acc_ref[...] += _swiglu_partial(h_ref[...], wg_ref[...], wu_ref[...], wd_ref[...])

    @pl.when(f == pl.num_programs(1) - 1)
    def _():
        o_ref[...] = x_ref[...] + acc_ref[...]


def _dense_ffn(x, g, w_gate_up, w_down, tf):
    n, d = x.shape
    ff = w_down.shape[0]
    nf = ff // tf
    return pl.pallas_call(
        _dense_ffn_kernel,
        grid=(n // TM, nf),
        in_specs=[pl.BlockSpec((TM, d), lambda i, f: (i, 0)),
                  pl.BlockSpec((1, d), lambda i, f: (0, 0)),
                  pl.BlockSpec((d, tf), lambda i, f: (0, f)),
                  pl.BlockSpec((d, tf), lambda i, f: (0, nf + f)),
                  pl.BlockSpec((tf, d), lambda i, f: (f, 0))],
        out_specs=pl.BlockSpec((TM, d), lambda i, f: (i, 0)),
        out_shape=jax.ShapeDtypeStruct((n, d), F32),
        scratch_shapes=[pltpu.VMEM((TM, d), BF16), pltpu.VMEM((TM, d), F32)],
        compiler_params=_cparams("parallel", "arbitrary"),
        name="dense_ffn",
    )(x, g.reshape(1, d), w_gate_up, w_gate_up, w_down)


def _router_kernel(x_ref, g_ref, r_ref, h_ref, gate_ref, mask_ref, mask_t_ref, cnt_ref):
    h = _rmsnorm(x_ref[...], g_ref[...], RMS_EPS)
    h_ref[...] = h.astype(BF16)
    logits = jnp.dot(h, r_ref[...], preferred_element_type=F32, precision=lax.Precision.HIGHEST)
    lane = lax.broadcasted_iota(jnp.int32, logits.shape, 1)
    logits = jnp.where(lane < N_EXPERTS, logits, -jnp.inf)
    v1 = jnp.max(logits, axis=-1, keepdims=True)
    i1 = jnp.min(jnp.where(logits == v1, lane, LANES), axis=-1, keepdims=True)
    m1 = lane == i1
    rest = jnp.where(m1, -jnp.inf, logits)
    v2 = jnp.max(rest, axis=-1, keepdims=True)
    i2 = jnp.min(jnp.where(rest == v2, lane, LANES), axis=-1, keepdims=True)
    m2 = lane == i2
    e2 = jnp.exp(v2 - v1)
    g1 = 1.0 / (1.0 + e2)
    g2 = e2 / (1.0 + e2)
    gate_ref[...] = jnp.where(m1, g1, 0.0) + jnp.where(m2, g2, 0.0)
    mask = jnp.where(m1 | m2, 1.0, 0.0)
    mask_ref[...] = mask
    mask_t_ref[...] = mask.T[0:N_EXPERTS, :]
    cnt_ref[...] = jnp.broadcast_to(jnp.sum(mask, axis=0, keepdims=True), cnt_ref.shape)


def _router(x, g, router):
    n, d = x.shape
    t = n // TT
    r_pad = jnp.zeros((d, LANES), F32).at[:, :N_EXPERTS].set(router)
    return pl.pallas_call(
        _router_kernel,
        grid=(t,),
        in_specs=[pl.BlockSpec((TT, d), lambda i: (i, 0)),
                  pl.BlockSpec((1, d), lambda i: (0, 0)),
                  pl.BlockSpec((d, LANES), lambda i: (0, 0))],
        out_specs=[pl.BlockSpec((TT, d), lambda i: (i, 0)),
                   pl.BlockSpec((TT, LANES), lambda i: (i, 0)),
                   pl.BlockSpec((TT, LANES), lambda i: (i, 0)),
                   pl.BlockSpec((N_EXPERTS, TT), lambda i: (0, i)),
                   pl.BlockSpec((None, 8, LANES), lambda i: (i, 0, 0))],
        out_shape=[jax.ShapeDtypeStruct((n, d), BF16),
                   jax.ShapeDtypeStruct((n, LANES), F32),
                   jax.ShapeDtypeStruct((n, LANES), F32),
                   jax.ShapeDtypeStruct((N_EXPERTS, n), F32),
                   jax.ShapeDtypeStruct((t, 8, LANES), F32)],
        compiler_params=_cparams("parallel"),
        name="router",
    )(x, g.reshape(1, d), r_pad)


def _segment_copies(n_rows, make_copy):
    off = jnp.int32(0)
    size = TT
    while size >= BF16_ROWS:
        take = (n_rows & size) != 0

        @pl.when(take)
        def _(off=off, size=size):
            make_copy(off, size)

        off = off + jnp.where(take, size, 0)
        size //= 2


def _dispatch_kernel(start_ref, rows_ref, mask_t_ref, h_ref, xs_in_ref, xs_ref, blk_ref, sem_ref):
    del xs_in_ref
    t = pl.program_id(0)
    slot = t % 2
    mask_t = mask_t_ref[...]
    row = lax.broadcasted_iota(jnp.int32, (TT, TT), 0)
    col = lax.broadcasted_iota(jnp.int32, (TT, TT), 1)
    before = jnp.where(row < col, 1.0, 0.0).astype(BF16)
    rank = jnp.dot(mask_t.astype(BF16), before, preferred_element_type=F32)
    row_f = row.astype(F32)
    h = h_ref[...]
    for e in range(N_EXPERTS):
        pick = (row_f == rank[e:e + 1, :]) & (mask_t[e:e + 1, :] > 0.0)
        sel = jnp.where(pick, 1.0, 0.0).astype(BF16)
        blk_ref[slot, e] = jnp.dot(sel, h, preferred_element_type=F32).astype(BF16)

    def copies(step, step_slot, wait):
        for e in range(N_EXPERTS):
            start = pl.multiple_of(start_ref[step * N_EXPERTS + e], BF16_ROWS)

            def make_copy(off, size, e=e, start=start):
                src = blk_ref.at[step_slot, e, pl.ds(pl.multiple_of(off, BF16_ROWS), size)]
                dst = xs_ref.at[pl.ds(pl.multiple_of(start + off, BF16_ROWS), size)]
                cp = pltpu.make_async_copy(src, dst, sem_ref.at[step_slot])
                cp.wait() if wait else cp.start()

            _segment_copies(rows_ref[step * N_EXPERTS + e], make_copy)

    copies(t, slot, wait=False)

    @pl.when(t > 0)
    def _():
        copies(t - 1, 1 - slot, wait=True)

    @pl.when(t == pl.num_programs(0) - 1)
    def _():
        copies(t, slot, wait=True)


def _dispatch(seg_start, seg_rows, mask_t, h, n_rows):
    n, d = h.shape
    t = n // TT
    xs0 = jnp.zeros((n_rows, d), BF16)
    grid_spec = pltpu.PrefetchScalarGridSpec(
        num_scalar_prefetch=2,
        grid=(t,),
        in_specs=[pl.BlockSpec((N_EXPERTS, TT), lambda i, *_: (0, i)),
                  pl.BlockSpec((TT, d), lambda i, *_: (i, 0)),
                  pl.BlockSpec(memory_space=pl.ANY)],
        out_specs=pl.BlockSpec(memory_space=pl.ANY),
        scratch_shapes=[pltpu.VMEM((2, N_EXPERTS, TT, d), BF16), pltpu.SemaphoreType.DMA((2,))],
    )
    return pl.pallas_call(
        _dispatch_kernel,
        grid_spec=grid_spec,
        out_shape=jax.ShapeDtypeStruct((n_rows, d), BF16),
        input_output_aliases={4: 0},
        compiler_params=_cparams("arbitrary"),
        name="moe_dispatch",
    )(seg_start, seg_rows, mask_t, h, xs0)


def _expert_kernel(tile_expert_ref, tile_valid_ref, xs_ref, wg_ref, wu_ref, wd_ref, o_ref, acc_ref):
    del tile_expert_ref
    j = pl.program_id(0)
    f = pl.program_id(1)
    last = pl.num_programs(1) - 1
    valid = tile_valid_ref[j] > 0

    @pl.when(valid)
    def _():
        @pl.when(f == 0)
        def _():
            acc_ref[...] = jnp.zeros_like(acc_ref)

        acc_ref[...] += _swiglu_partial(xs_ref[...], wg_ref[...], wu_ref[...], wd_ref[...])

        @pl.when(f == last)
        def _():
            o_ref[...] = acc_ref[...].astype(o_ref.dtype)

    @pl.when(jnp.logical_not(valid) & (f == last))
    def _():
        o_ref[...] = jnp.zeros_like(o_ref)


def _expert_ffn(tile_expert, tile_valid, xs, w_gate_up, w_down, tf):
    n_rows, d = xs.shape
    ff = w_down.shape[1]
    nf = ff // tf

    def f_eff(j, f, tv):
        return jnp.where(tv[j] > 0, f, nf - 1)

    grid_spec = pltpu.PrefetchScalarGridSpec(
        num_scalar_prefetch=2,
        grid=(n_rows // TM, nf),
        in_specs=[pl.BlockSpec((TM, d), lambda j, f, te, tv: (j, 0)),
                  pl.BlockSpec((None, d, tf), lambda j, f, te, tv: (te[j], 0, f_eff(j, f, tv))),
                  pl.BlockSpec((None, d, tf), lambda j, f, te, tv: (te[j], 0, nf + f_eff(j, f, tv))),
                  pl.BlockSpec((None, tf, d), lambda j, f, te, tv: (te[j], f_eff(j, f, tv), 0))],
        out_specs=pl.BlockSpec((TM, d), lambda j, f, te, tv: (j, 0)),
        scratch_shapes=[pltpu.VMEM((TM, d), F32)],
    )
    return pl.pallas_call(
        _expert_kernel,
        grid_spec=grid_spec,
        out_shape=jax.ShapeDtypeStruct((n_rows, d), BF16),
        compiler_params=_cparams("parallel", "arbitrary"),
        name="moe_experts",
    )(tile_expert, tile_valid, xs, w_gate_up, w_gate_up, w_down)


def _combine_kernel(start_ref, rows_ref, x_ref, gate_ref, mask_ref, ys_ref, o_ref, win_ref, sem_ref):
    t = pl.program_id(0)
    n_t = pl.num_programs(0)
    slot = t % 2

    def copies(step, step_slot, wait):
        for e in range(N_EXPERTS):
            start = pl.multiple_of(start_ref[step * N_EXPERTS + e], BF16_ROWS)

            def make_copy(off, size, e=e, start=start):
                src = ys_ref.at[pl.ds(pl.multiple_of(start + off, BF16_ROWS), size)]
                dst = win_ref.at[step_slot, e, pl.ds(pl.multiple_of(off, BF16_ROWS), size)]
                cp = pltpu.make_async_copy(src, dst, sem_ref.at[step_slot])
                cp.wait() if wait else cp.start()

            _segment_copies(rows_ref[step * N_EXPERTS + e], make_copy)

    @pl.when(t == 0)
    def _():
        win_ref[...] = jnp.zeros_like(win_ref)
        copies(0, 0, wait=False)

    @pl.when(t + 1 < n_t)
    def _():
        copies(t + 1, 1 - slot, wait=False)

    copies(t, slot, wait=True)

    mask = mask_ref[...]
    gate = gate_ref[...]
    row = lax.broadcasted_iota(jnp.int32, (TT, TT), 0)
    col = lax.broadcasted_iota(jnp.int32, (TT, TT), 1)
    before = jnp.where(col < row, 1.0, 0.0).astype(BF16)
    rank = jnp.dot(before, mask.astype(BF16), preferred_element_type=F32)
    col_f = col.astype(F32)
    y = x_ref[...]
    for e in range(N_EXPERTS):
        pick = (col_f == rank[:, e:e + 1]) & (mask[:, e:e + 1] > 0.0)
        sel = jnp.where(pick, 1.0, 0.0).astype(BF16)
        y = y + gate[:, e:e + 1] * jnp.dot(sel, win_ref[slot, e], preferred_element_type=F32)
    o_ref[...] = y


def _combine(seg_start, seg_rows, x, gate, mask, ys):
    n, d = x.shape
    t = n // TT
    grid_spec = pltpu.PrefetchScalarGridSpec(
        num_scalar_prefetch=2,
        grid=(t,),
        in_specs=[pl.BlockSpec((TT, d), lambda i, *_: (i, 0)),
                  pl.BlockSpec((TT, LANES), lambda i, *_: (i, 0)),
                  pl.BlockSpec((TT, LANES), lambda i, *_: (i, 0)),
                  pl.BlockSpec(memory_space=pl.ANY)],
        out_specs=pl.BlockSpec((TT, d), lambda i, *_: (i, 0)),
        scratch_shapes=[pltpu.VMEM((2, N_EXPERTS, TT, d), BF16), pltpu.SemaphoreType.DMA((2,))],
    )
    return pl.pallas_call(
        _combine_kernel,
        grid_spec=grid_spec,
        out_shape=jax.ShapeDtypeStruct((n, d), F32),
        compiler_params=_cparams("arbitrary"),
        name="moe_combine",
    )(seg_start, seg_rows, x, gate, mask, ys)


def _moe_ffn(x, g, router, w_gate_up, w_down, tf):
    n, d = x.shape
    t = n // TT
    h, gate, mask, mask_t, cnt = _router(x, g, router)
    cnt = cnt[:, 0, :N_EXPERTS].astype(jnp.int32)
    seg_rows = _round_up(cnt, BF16_ROWS)
    region = _round_up(jnp.sum(seg_rows, axis=0), TM)
    region_end = jnp.cumsum(region)
    seg_start = (region_end - region)[None, :] + jnp.cumsum(seg_rows, axis=0) - seg_rows
    n_tiles = (_round_up(2 * n + t * N_EXPERTS * (BF16_ROWS - 1), TM) + N_EXPERTS * TM) // TM
    tile_end = region_end // TM
    tile_ids = jnp.arange(n_tiles, dtype=jnp.int32)
    tile_valid = (tile_ids < tile_end[-1]).astype(jnp.int32)
    tile_expert = jnp.searchsorted(tile_end, jnp.minimum(tile_ids, tile_end[-1] - 1), side="right")
    tile_expert = jnp.minimum(tile_expert, N_EXPERTS - 1).astype(jnp.int32)
    seg_start = seg_start.reshape(-1).astype(jnp.int32)
    seg_rows = seg_rows.reshape(-1).astype(jnp.int32)

    xs = _dispatch(seg_start, seg_rows, mask_t, h, n_tiles * TM)
    ys = _expert_ffn(tile_expert, tile_valid, xs, w_gate_up, w_down, tf)
    return _combine(seg_start, seg_rows, x, gate, mask, ys)


def _final_norm_kernel(x_ref, g_ref, o_ref):
    o_ref[...] = _rmsnorm(x_ref[...], g_ref[...], RMS_EPS)


def _final_norm(x, g):
    n, d = x.shape
    return pl.pallas_call(
        _final_norm_kernel,
        grid=(n // TM,),
        in_specs=[pl.BlockSpec((TM, d), lambda i: (i, 0)), pl.BlockSpec((1, d), lambda i: (0, 0))],
        out_specs=pl.BlockSpec((TM, d), lambda i: (i, 0)),
        out_shape=jax.ShapeDtypeStruct((n, d), F32),
        compiler_params=_cparams("parallel"),
        name="final_norm",
    )(x, g.reshape(1, d))


def kernel(x, attn_norm, ffn_norm, final_norm, t5_rel_bias, diff_w_qkv, diff_w_o, diff_lambda_q1, diff_lambda_k1, diff_lambda_q2, diff_lambda_k2, diff_subln, chunk_w_qkv, chunk_w_o, chunk_rel_bias, dense_w_gate_up, dense_w_down, moe_router, moe_w_gate_up, moe_w_down):
    b, s, d = x.shape
    n = b * s
    xf = x.reshape(n, d)
    t5_rows = _t5_bias_rows(t5_rel_bias, s)
    for i in range(DEPTH):
        j = i // 2
        if i % 2 == 0:
            qkv = _norm_matmul(xf, attn_norm[i], diff_w_qkv[j].astype(BF16))
            lam_init = 0.8 - 0.6 * math.exp(-0.3 * i)
            lam_vecs = jnp.stack([diff_lambda_q1[j], diff_lambda_k1[j], diff_lambda_q2[j], diff_lambda_k2[j]])
            a = _diff_attention(qkv.reshape(b, s, 3 * d), t5_rows, lam_vecs, diff_subln[j], lam_init)
            xf = _matmul_residual(a.reshape(n, d), diff_w_o[j].astype(BF16), xf)
            xf = _dense_ffn(xf, ffn_norm[i], dense_w_gate_up[j].astype(BF16), dense_w_down[j].astype(BF16),
                            tf=dense_w_down.shape[1] // 2)
        else:
            qkv = _norm_matmul(xf, attn_norm[i], chunk_w_qkv[j].astype(BF16))
            a = _chunk_attention(qkv.reshape(b, s, 3 * d), _band_bias(chunk_rel_bias[j]))
            xf = _matmul_residual(a.reshape(n, d), chunk_w_o[j].astype(BF16), xf)
            xf = _moe_ffn(xf, ffn_norm[i], moe_router[j], moe_w_gate_up[j].astype(BF16),
                          moe_w_down[j].astype(BF16), tf=moe_w_down.shape[2] // 2)
    return _final_norm(xf, final_norm).reshape(b, s, d)
```

```python
import functools
import math

import numpy as np
import jax
import jax.numpy as jnp
from jax import lax
from jax.experimental import pallas as pl
from jax.experimental.pallas import tpu as pltpu

F32 = jnp.float32
BF16 = jnp.bfloat16

D_MODEL = 1024
DEPTH = 4
CHUNK = 64
DA_HEADS = 8
DA_HEAD_DIM = 64
T5_BUCKETS = 32
T5_MAX_DIST = 128
CA_HEADS = 16
CA_HEAD_DIM = 64
BAND_PREV = 8
REL_CLIP = 4 * CHUNK
N_EXPERTS = 8
RMS_EPS = 1e-6
SUBLN_EPS = 1e-5
NEG_INF = -1e30

LANES = 128
BF16_ROWS = 16
VMEM_LIMIT = 56 * 1024 * 1024
TM = 512
TQ = 256
TT = 256
KB = 256
N_SLOTS = 4
ONES_ROWS = BF16_ROWS
LOG2E = math.log2(math.e)
Q_SCALE = DA_HEAD_DIM ** -0.5 * LOG2E
PAIR = 2 * CA_HEAD_DIM


def _round_up(a, m):
    return (a + m - 1) // m * m


def _cparams(*sem):
    return pltpu.CompilerParams(dimension_semantics=sem, vmem_limit_bytes=VMEM_LIMIT)


def _rmsnorm(x32, g, eps):
    return x32 * lax.rsqrt(jnp.mean(x32 * x32, axis=-1, keepdims=True) + eps) * g


def _dot_nt(a, b):
    return lax.dot_general(a, b, (((1,), (1,)), ((), ())), preferred_element_type=F32)


def _norm_matmul_kernel(x_ref, g_ref, w_ref, o_ref):
    h = _rmsnorm(x_ref[...], g_ref[...], RMS_EPS).astype(BF16)
    d = x_ref.shape[1]
    q = jnp.dot(h, w_ref[:, :d], preferred_element_type=F32) * Q_SCALE
    o_ref[:, :d] = q.astype(o_ref.dtype)
    o_ref[:, d:] = jnp.dot(h, w_ref[:, d:], preferred_element_type=F32).astype(o_ref.dtype)


def _norm_matmul(x, g, w):
    n, d = x.shape
    n_out = w.shape[1]
    return pl.pallas_call(
        _norm_matmul_kernel,
        grid=(n // TM,),
        in_specs=[pl.BlockSpec((TM, d), lambda i: (i, 0)),
                  pl.BlockSpec((1, d), lambda i: (0, 0)),
                  pl.BlockSpec((d, n_out), lambda i: (0, 0))],
        out_specs=pl.BlockSpec((TM, n_out), lambda i: (i, 0)),
        out_shape=jax.ShapeDtypeStruct((n, n_out), BF16),
        compiler_params=_cparams("parallel"),
        name="norm_qkv",
    )(x, g.reshape(1, d), w)


def _matmul_residual_kernel(a_ref, w_ref, x_ref, o_ref):
    o_ref[...] = x_ref[...] + jnp.dot(a_ref[...], w_ref[...], preferred_element_type=F32)


def _matmul_residual(a, w, x):
    n, d = x.shape
    return pl.pallas_call(
        _matmul_residual_kernel,
        grid=(n // TM,),
        in_specs=[pl.BlockSpec((TM, a.shape[1]), lambda i: (i, 0)),
                  pl.BlockSpec(w.shape, lambda i: (0, 0)),
                  pl.BlockSpec((TM, d), lambda i: (i, 0))],
        out_specs=pl.BlockSpec((TM, d), lambda i: (i, 0)),
        out_shape=jax.ShapeDtypeStruct((n, d), F32),
        compiler_params=_cparams("parallel"),
        name="wo_residual",
    )(a, w, x)


def _t5_bucket_np(rel):
    nb = T5_BUCKETS // 2
    max_exact = nb // 2
    ret = (rel > 0).astype(np.int32) * nb
    n = np.abs(rel)
    n_f = np.maximum(n, 1).astype(np.float32)
    large = max_exact + (np.log(n_f / np.float32(max_exact)) / np.float32(math.log(T5_MAX_DIST / max_exact))
                         * np.float32(nb - max_exact)).astype(np.int32)
    large = np.minimum(large, nb - 1)
    return ret + np.where(n < max_exact, n, large)


def _lookup_kernel(idx_ref, t_ref, o_ref):
    rows = lax.broadcasted_iota(jnp.int32, (t_ref.shape[1], idx_ref.shape[1]), 0)
    onehot = jnp.where(rows == idx_ref[...], 1.0, 0.0)
    o_ref[...] = jnp.dot(t_ref[...], onehot, preferred_element_type=F32, precision=lax.Precision.HIGHEST)


def _table_lookup_t(table, idx):
    r, h = table.shape
    table_t = jnp.zeros((h, _round_up(r, LANES)), F32).at[:, :r].set(table.T)
    return pl.pallas_call(
        _lookup_kernel,
        out_shape=jax.ShapeDtypeStruct((h, len(idx)), F32),
        name="bias_lookup",
    )(jnp.asarray(idx, jnp.int32).reshape(1, -1), table_t)


def _toeplitz(w, n_rows, n_cols):
    h, r = w.shape
    full = jnp.broadcast_to(w[:, None, :], (h, n_rows, r)).reshape(h, n_rows * r)
    return full[:, :n_rows * (r - 1)].reshape(h, n_rows, r - 1)[:, :, :n_cols]


def _diagonals(n_cols):
    period = _round_up(n_cols + TQ, LANES)
    m = np.arange(period)
    return np.where(m < n_cols, m, np.where(m > period - TQ, m - period, 0))


def _t5_bias_t(table, seq):
    w = _table_lookup_t(table, _t5_bucket_np(_diagonals(seq) - (seq - TQ))) * LOG2E
    bias = _toeplitz(w, TQ, seq)
    i = np.arange(TQ)[:, None]
    j = np.arange(seq)[None, :] - (seq - TQ)
    allowed = (j < 0) | ((j // CHUNK) <= (i // CHUNK))
    return jnp.swapaxes(jnp.where(allowed[None], bias, NEG_INF), 1, 2)


def _load_vt(v_ref, vt_ref):
    n_v = v_ref.shape[1]
    vt_ref[0:n_v, :] = v_ref[...].T
    vt_ref[n_v:, :] = jnp.ones((ONES_ROWS, v_ref.shape[0]), BF16)


def _attend_t(k_ref, k0, n_k, qm, bias_rows, vt_ref, s_ref, p_ref):
    m = None
    for kb in range(0, n_k, KB):
        s = _dot_nt(k_ref[k0 + kb:k0 + kb + KB, :], qm) + bias_rows(kb)
        s_ref[kb:kb + KB, :] = s
        block_max = jnp.max(s, axis=0, keepdims=True)
        m = block_max if m is None else jnp.maximum(m, block_max)
    for kb in range(0, n_k, KB):
        p_ref[kb:kb + KB, :] = jnp.exp2(s_ref[kb:kb + KB, :] - m).astype(BF16)
    o_t = jnp.dot(vt_ref[:, k0:k0 + n_k], p_ref[0:n_k, :], preferred_element_type=F32)
    n_v = vt_ref.shape[0] - ONES_ROWS
    return o_t[0:n_v] * (1.0 / o_t[n_v:n_v + 1])


def _diff_attn_kernel(q_ref, k_ref, v_ref, bias_ref, lam_ref, g_ref, o_ref, vt_ref, s_ref, p_ref, *, lam_init):
    lv = lam_ref[...]
    lam = (jnp.exp(jnp.sum(lv[0:1] * lv[1:2], axis=-1, keepdims=True))
           - jnp.exp(jnp.sum(lv[2:3] * lv[3:4], axis=-1, keepdims=True)) + lam_init)
    first_map = lax.broadcasted_iota(jnp.int32, (1, 2 * DA_HEAD_DIM), 1) < DA_HEAD_DIM
    seq = q_ref.shape[0]
    _load_vt(v_ref, vt_ref)
    for qi in range(seq // TQ):
        q = q_ref[qi * TQ:(qi + 1) * TQ, :]
        zero = jnp.zeros_like(q)
        n_k = (qi + 1) * TQ

        def bias_rows(kb, n_k=n_k):
            return bias_ref[seq - n_k + kb:seq - n_k + kb + KB, :]

        slot = (2 * qi) % N_SLOTS
        o1 = _attend_t(k_ref, 0, n_k, jnp.where(first_map, q, zero), bias_rows, vt_ref,
                       s_ref.at[slot], p_ref.at[slot])
        o2 = _attend_t(k_ref, 0, n_k, jnp.where(first_map, zero, q), bias_rows, vt_ref,
                       s_ref.at[slot + 1], p_ref.at[slot + 1])
        o_t = o1 - lam * o2
        o_t = o_t * lax.rsqrt(jnp.mean(o_t * o_t, axis=0, keepdims=True) + SUBLN_EPS) * g_ref[...]
        o_ref[qi * TQ:(qi + 1) * TQ, :] = (o_t * (1.0 - lam_init)).T.astype(o_ref.dtype)


def _diff_attention(qkv, bias_t, lam_vecs, subln_g, lam_init):
    b, s, _ = qkv.shape
    hd = 2 * DA_HEAD_DIM
    return pl.pallas_call(
        functools.partial(_diff_attn_kernel, lam_init=lam_init),
        grid=(b, DA_HEADS),
        in_specs=[pl.BlockSpec((None, s, hd), lambda i, h: (i, 0, h)),
                  pl.BlockSpec((None, s, hd), lambda i, h: (i, 0, DA_HEADS + h)),
                  pl.BlockSpec((None, s, hd), lambda i, h: (i, 0, 2 * DA_HEADS + h)),
                  pl.BlockSpec((None, s, TQ), lambda i, h: (h, 0, 0)),
                  pl.BlockSpec((4, DA_HEAD_DIM), lambda i, h: (0, 0)),
                  pl.BlockSpec((hd, 1), lambda i, h: (0, 0))],
        out_specs=pl.BlockSpec((None, s, hd), lambda i, h: (i, 0, h)),
        out_shape=jax.ShapeDtypeStruct((b, s, D_MODEL), BF16),
        scratch_shapes=[pltpu.VMEM((hd + ONES_ROWS, s), BF16), pltpu.VMEM((N_SLOTS, s, TQ), F32),
                        pltpu.VMEM((N_SLOTS, s, TQ), BF16)],
        compiler_params=_cparams("parallel", "parallel"),
        name="diff_attn",
    )(qkv, qkv, qkv, bias_t, lam_vecs, subln_g.reshape(hd, 1))


BAND_KEYS = BAND_PREV * CHUNK + TQ


def _band_bias_t(rel_table):
    dist = BAND_PREV * CHUNK - _diagonals(BAND_KEYS)
    w = _table_lookup_t(rel_table, np.clip(dist, -REL_CLIP, REL_CLIP) + REL_CLIP) * LOG2E
    bias = _toeplitz(w, TQ, BAND_KEYS)
    q_chunk = np.arange(TQ)[:, None] // CHUNK
    k_chunk = np.arange(BAND_KEYS)[None, :] // CHUNK
    allowed = (k_chunk >= q_chunk) & (k_chunk <= q_chunk + BAND_PREV)
    return jnp.swapaxes(jnp.where(allowed[None], bias, NEG_INF), 1, 2)


def _chunk_attn_kernel(q_ref, k_ref, v_ref, bias_ref, o_ref, vt_ref, s_ref, p_ref):
    first_head = lax.broadcasted_iota(jnp.int32, (1, PAIR), 1) < CA_HEAD_DIM
    first_head_rows = lax.broadcasted_iota(jnp.int32, (PAIR, 1), 0) < CA_HEAD_DIM
    seq = q_ref.shape[0]
    _load_vt(v_ref, vt_ref)
    for qi in range(seq // TQ):
        q0 = qi * TQ
        k0 = max(0, q0 - BAND_PREV * CHUNK)
        n_k = q0 + TQ - k0
        q = q_ref[q0:q0 + TQ, :]
        zero = jnp.zeros_like(q)
        outs = []
        for hh in range(2):
            def bias_rows(kb, hh=hh, n_k=n_k):
                return bias_ref[hh, BAND_KEYS - n_k + kb:BAND_KEYS - n_k + kb + KB, :]

            qm = jnp.where(first_head, q, zero) if hh == 0 else jnp.where(first_head, zero, q)
            slot = (2 * qi + hh) % N_SLOTS
            outs.append(_attend_t(k_ref, k0, n_k, qm, bias_rows, vt_ref, s_ref.at[slot], p_ref.at[slot]))
        o_ref[q0:q0 + TQ, :] = jnp.where(first_head_rows, outs[0], outs[1]).T.astype(o_ref.dtype)


def _chunk_attention(qkv, band_bias_t):
    b, s, _ = qkv.shape
    n_pairs = CA_HEADS // 2
    return pl.pallas_call(
        _chunk_attn_kernel,
        grid=(b, n_pairs),
        in_specs=[pl.BlockSpec((None, s, PAIR), lambda i, p: (i, 0, p)),
                  pl.BlockSpec((None, s, PAIR), lambda i, p: (i, 0, n_pairs + p)),
                  pl.BlockSpec((None, s, PAIR), lambda i, p: (i, 0, 2 * n_pairs + p)),
                  pl.BlockSpec((2, BAND_KEYS, TQ), lambda i, p: (p, 0, 0))],
        out_specs=pl.BlockSpec((None, s, PAIR), lambda i, p: (i, 0, p)),
        out_shape=jax.ShapeDtypeStruct((b, s, D_MODEL), BF16),
        scratch_shapes=[pltpu.VMEM((PAIR + ONES_ROWS, s), BF16), pltpu.VMEM((N_SLOTS, BAND_KEYS, TQ), F32),
                        pltpu.VMEM((N_SLOTS, BAND_KEYS, TQ), BF16)],
        compiler_params=_cparams("parallel", "parallel"),
        name="chunk_attn",
    )(qkv, qkv, qkv, band_bias_t)


def _swiglu_partial(h, wg, wu, wd):
    g = jnp.dot(h, wg, preferred_element_type=F32)
    u = jnp.dot(h, wu, preferred_element_type=F32)
    a = (g * (1.0 / (1.0 + jnp.exp(-g))) * u).astype(BF16)
    return jnp.dot(a, wd, preferred_element_type=F32)


def _dense_ffn_kernel(x_ref, g_ref, wg_ref, wu_ref, wd_ref, o_ref, h_ref, acc_ref):
    f = pl.program_id(1)

    @pl.when(f == 0)
    def _():
        h_ref[...] = _rmsnorm(x_ref[...], g_ref[...], RMS_EPS).astype(BF16)
        acc_ref[...] = jnp.zeros_like(acc_ref)

    acc_ref[...] += _swiglu_partial(h_ref[...], wg_ref[...], wu_ref[...], wd_ref[...])

    @pl.when(f == pl.num_programs(1) - 1)
    def _():
        o_ref[...] = x_ref[...] + acc_ref[...]


def _dense_ffn(x, g, w_gate_up, w_down, tf):
    n, d = x.shape
    ff = w_down.shape[0]
    nf = ff // tf
    return pl.pallas_call(
        _dense_ffn_kernel,
        grid=(n // TM, nf),
        in_specs=[pl.BlockSpec((TM, d), lambda i, f: (i, 0)),
                  pl.BlockSpec((1, d), lambda i, f: (0, 0)),
                  pl.BlockSpec((d, tf), lambda i, f: (0, f)),
                  pl.BlockSpec((d, tf), lambda i, f: (0, nf + f)),
                  pl.BlockSpec((tf, d), lambda i, f: (f, 0))],
        out_specs=pl.BlockSpec((TM, d), lambda i, f: (i, 0)),
        out_shape=jax.ShapeDtypeStruct((n, d), F32),
        scratch_shapes=[pltpu.VMEM((TM, d), BF16), pltpu.VMEM((TM, d), F32)],
        compiler_params=_cparams("parallel", "arbitrary"),
        name="dense_ffn",
    )(x, g.reshape(1, d), w_gate_up, w_gate_up, w_down)


def _router_kernel(x_ref, g_ref, r_ref, h_ref, gate_ref, mask_ref, mask_t_ref, cnt_ref):
    h = _rmsnorm(x_ref[...], g_ref[...], RMS_EPS)
    h_ref[...] = h.astype(BF16)
    logits = jnp.dot(h, r_ref[...], preferred_element_type=F32, precision=lax.Precision.HIGHEST)
    lane = lax.broadcasted_iota(jnp.int32, logits.shape, 1)
    logits = jnp.where(lane < N_EXPERTS, logits, -jnp.inf)
    v1 = jnp.max(logits, axis=-1, keepdims=True)
    i1 = jnp.min(jnp.where(logits == v1, lane, LANES), axis=-1, keepdims=True)
    m1 = lane == i1
    rest = jnp.where(m1, -jnp.inf, logits)
    v2 = jnp.max(rest, axis=-1, keepdims=True)
    i2 = jnp.min(jnp.where(rest == v2, lane, LANES), axis=-1, keepdims=True)
    m2 = lane == i2
    e2 = jnp.exp(v2 - v1)
    g1 = 1.0 / (1.0 + e2)
    g2 = e2 / (1.0 + e2)
    gate_ref[...] = jnp.where(m1, g1, 0.0) + jnp.where(m2, g2, 0.0)
    mask = jnp.where(m1 | m2, 1.0, 0.0)
    mask_ref[...] = mask
    mask_t_ref[...] = mask.T[0:N_EXPERTS, :]
    cnt_ref[...] = jnp.broadcast_to(jnp.sum(mask, axis=0, keepdims=True), cnt_ref.shape)


def _router(x, g, router):
    n, d = x.shape
    t = n // TT
    r_pad = jnp.zeros((d, LANES), F32).at[:, :N_EXPERTS].set(router)
    return pl.pallas_call(
        _router_kernel,
        grid=(t,),
        in_specs=[pl.BlockSpec((TT, d), lambda i: (i, 0)),
                  pl.BlockSpec((1, d), lambda i: (0, 0)),
                  pl.BlockSpec((d, LANES), lambda i: (0, 0))],
        out_specs=[pl.BlockSpec((TT, d), lambda i: (i, 0)),
                   pl.BlockSpec((TT, LANES), lambda i: (i, 0)),
                   pl.BlockSpec((TT, LANES), lambda i: (i, 0)),
                   pl.BlockSpec((N_EXPERTS, TT), lambda i: (0, i)),
                   pl.BlockSpec((None, 8, LANES), lambda i: (i, 0, 0))],
        out_shape=[jax.ShapeDtypeStruct((n, d), BF16),
                   jax.ShapeDtypeStruct((n, LANES), F32),
                   jax.ShapeDtypeStruct((n, LANES), F32),
                   jax.ShapeDtypeStruct((N_EXPERTS, n), F32),
                   jax.ShapeDtypeStruct((t, 8, LANES), F32)],
        compiler_params=_cparams("parallel"),
        name="router",
    )(x, g.reshape(1, d), r_pad)


def _segment_copies(n_rows, make_copy):
    off = jnp.int32(0)
    size = TT
    while size >= BF16_ROWS:
        take = (n_rows & size) != 0

        @pl.when(take)
        def _(off=off, size=size):
            make_copy(off, size)

        off = off + jnp.where(take, size, 0)
        size //= 2


def _dispatch_kernel(start_ref, rows_ref, mask_t_ref, h_ref, xs_in_ref, xs_ref, blk_ref, sem_ref):
    del xs_in_ref
    t = pl.program_id(0)
    slot = t % 2
    mask_t = mask_t_ref[...]
    row = lax.broadcasted_iota(jnp.int32, (TT, TT), 0)
    col = lax.broadcasted_iota(jnp.int32, (TT, TT), 1)
    before = jnp.where(row < col, 1.0, 0.0).astype(BF16)
    rank = jnp.dot(mask_t.astype(BF16), before, preferred_element_type=F32)
    row_f = row.astype(F32)
    h = h_ref[...]
    for e in range(N_EXPERTS):
        pick = (row_f == rank[e:e + 1, :]) & (mask_t[e:e + 1, :] > 0.0)
        sel = jnp.where(pick, 1.0, 0.0).astype(BF16)
        blk_ref[slot, e] = jnp.dot(sel, h, preferred_element_type=F32).astype(BF16)

    def copies(step, step_slot, wait):
        for e in range(N_EXPERTS):
            start = pl.multiple_of(start_ref[step * N_EXPERTS + e], BF16_ROWS)

            def make_copy(off, size, e=e, start=start):
                src = blk_ref.at[step_slot, e, pl.ds(pl.multiple_of(off, BF16_ROWS), size)]
                dst = xs_ref.at[pl.ds(pl.multiple_of(start + off, BF16_ROWS), size)]
                cp = pltpu.make_async_copy(src, dst, sem_ref.at[step_slot])
                cp.wait() if wait else cp.start()

            _segment_copies(rows_ref[step * N_EXPERTS + e], make_copy)

    copies(t, slot, wait=False)

    @pl.when(t > 0)
    def _():
        copies(t - 1, 1 - slot, wait=True)

    @pl.when(t == pl.num_programs(0) - 1)
    def _():
        copies(t, slot, wait=True)


def _dispatch(seg_start, seg_rows, mask_t, h, n_rows):
    n, d = h.shape
    t = n // TT
    xs0 = jnp.zeros((n_rows, d), BF16)
    grid_spec = pltpu.PrefetchScalarGridSpec(
        num_scalar_prefetch=2,
        grid=(t,),
        in_specs=[pl.BlockSpec((N_EXPERTS, TT), lambda i, *_: (0, i)),
                  pl.BlockSpec((TT, d), lambda i, *_: (i, 0)),
                  pl.BlockSpec(memory_space=pl.ANY)],
        out_specs=pl.BlockSpec(memory_space=pl.ANY),
        scratch_shapes=[pltpu.VMEM((2, N_EXPERTS, TT, d), BF16), pltpu.SemaphoreType.DMA((2,))],
    )
    return pl.pallas_call(
        _dispatch_kernel,
        grid_spec=grid_spec,
        out_shape=jax.ShapeDtypeStruct((n_rows, d), BF16),
        input_output_aliases={4: 0},
        compiler_params=_cparams("arbitrary"),
        name="moe_dispatch",
    )(seg_start, seg_rows, mask_t, h, xs0)


def _expert_kernel(tile_expert_ref, tile_valid_ref, xs_ref, wg_ref, wu_ref, wd_ref, o_ref, acc_ref):
    del tile_expert_ref
    j = pl.program_id(0)
    f = pl.program_id(1)
    last = pl.num_programs(1) - 1
    valid = tile_valid_ref[j] > 0

    @pl.when(valid)
    def _():
        @pl.when(f == 0)
        def _():
            acc_ref[...] = jnp.zeros_like(acc_ref)

        acc_ref[...] += _swiglu_partial(xs_ref[...], wg_ref[...], wu_ref[...], wd_ref[...])

        @pl.when(f == last)
        def _():
            o_ref[...] = acc_ref[...].astype(o_ref.dtype)

    @pl.when(jnp.logical_not(valid) & (f == last))
    def _():
        o_ref[...] = jnp.zeros_like(o_ref)


def _expert_ffn(tile_expert, tile_valid, xs, w_gate_up, w_down, tf):
    n_rows, d = xs.shape
    ff = w_down.shape[1]
    nf = ff // tf

    def f_eff(j, f, tv):
        return jnp.where(tv[j] > 0, f, nf - 1)

    grid_spec = pltpu.PrefetchScalarGridSpec(
        num_scalar_prefetch=2,
        grid=(n_rows // TM, nf),
        in_specs=[pl.BlockSpec((TM, d), lambda j, f, te, tv: (j, 0)),
                  pl.BlockSpec((None, d, tf), lambda j, f, te, tv: (te[j], 0, f_eff(j, f, tv))),
                  pl.BlockSpec((None, d, tf), lambda j, f, te, tv: (te[j], 0, nf + f_eff(j, f, tv))),
                  pl.BlockSpec((None, tf, d), lambda j, f, te, tv: (te[j], f_eff(j, f, tv), 0))],
        out_specs=pl.BlockSpec((TM, d), lambda j, f, te, tv: (j, 0)),
        scratch_shapes=[pltpu.VMEM((TM, d), F32)],
    )
    return pl.pallas_call(
        _expert_kernel,
        grid_spec=grid_spec,
        out_shape=jax.ShapeDtypeStruct((n_rows, d), BF16),
        compiler_params=_cparams("parallel", "arbitrary"),
        name="moe_experts",
    )(tile_expert, tile_valid, xs, w_gate_up, w_gate_up, w_down)


def _combine_kernel(start_ref, rows_ref, x_ref, gate_ref, mask_ref, ys_ref, o_ref, win_ref, sem_ref):
    t = pl.program_id(0)
    n_t = pl.num_programs(0)
    slot = t % 2

    def copies(step, step_slot, wait):
        for e in range(N_EXPERTS):
            start = pl.multiple_of(start_ref[step * N_EXPERTS + e], BF16_ROWS)

            def make_copy(off, size, e=e, start=start):
                src = ys_ref.at[pl.ds(pl.multiple_of(start + off, BF16_ROWS), size)]
                dst = win_ref.at[step_slot, e, pl.ds(pl.multiple_of(off, BF16_ROWS), size)]
                cp = pltpu.make_async_copy(src, dst, sem_ref.at[step_slot])
                cp.wait() if wait else cp.start()

            _segment_copies(rows_ref[step * N_EXPERTS + e], make_copy)

    @pl.when(t == 0)
    def _():
        win_ref[...] = jnp.zeros_like(win_ref)
        copies(0, 0, wait=False)

    @pl.when(t + 1 < n_t)
    def _():
        copies(t + 1, 1 - slot, wait=False)

    copies(t, slot, wait=True)

    mask = mask_ref[...]
    gate = gate_ref[...]
    row = lax.broadcasted_iota(jnp.int32, (TT, TT), 0)
    col = lax.broadcasted_iota(jnp.int32, (TT, TT), 1)
    before = jnp.where(col < row, 1.0, 0.0).astype(BF16)
    rank = jnp.dot(before, mask.astype(BF16), preferred_element_type=F32)
    col_f = col.astype(F32)
    y = x_ref[...]
    for e in range(N_EXPERTS):
        pick = (col_f == rank[:, e:e + 1]) & (mask[:, e:e + 1] > 0.0)
        sel = jnp.where(pick, 1.0, 0.0).astype(BF16)
        y = y + gate[:, e:e + 1] * jnp.dot(sel, win_ref[slot, e], preferred_element_type=F32)
    o_ref[...] = y


def _combine(seg_start, seg_rows, x, gate, mask, ys):
    n, d = x.shape
    t = n // TT
    grid_spec = pltpu.PrefetchScalarGridSpec(
        num_scalar_prefetch=2,
        grid=(t,),
        in_specs=[pl.BlockSpec((TT, d), lambda i, *_: (i, 0)),
                  pl.BlockSpec((TT, LANES), lambda i, *_: (i, 0)),
                  pl.BlockSpec((TT, LANES), lambda i, *_: (i, 0)),
                  pl.BlockSpec(memory_space=pl.ANY)],
        out_specs=pl.BlockSpec((TT, d), lambda i, *_: (i, 0)),
        scratch_shapes=[pltpu.VMEM((2, N_EXPERTS, TT, d), BF16), pltpu.SemaphoreType.DMA((2,))],
    )
    return pl.pallas_call(
        _combine_kernel,
        grid_spec=grid_spec,
        out_shape=jax.ShapeDtypeStruct((n, d), F32),
        compiler_params=_cparams("arbitrary"),
        name="moe_combine",
    )(seg_start, seg_rows, x, gate, mask, ys)


def _moe_ffn(x, g, router, w_gate_up, w_down, tf):
    n, d = x.shape
    t = n // TT
    h, gate, mask, mask_t, cnt = _router(x, g, router)
    cnt = cnt[:, 0, :N_EXPERTS].astype(jnp.int32)
    seg_rows = _round_up(cnt, BF16_ROWS)
    region = _round_up(jnp.sum(seg_rows, axis=0), TM)
    region_end = jnp.cumsum(region)
    seg_start = (region_end - region)[None, :] + jnp.cumsum(seg_rows, axis=0) - seg_rows
    n_tiles = (_round_up(2 * n + t * N_EXPERTS * (BF16_ROWS - 1), TM) + N_EXPERTS * TM) // TM
    tile_end = region_end // TM
    tile_ids = jnp.arange(n_tiles, dtype=jnp.int32)
    tile_valid = (tile_ids < tile_end[-1]).astype(jnp.int32)
    last_tile = jnp.minimum(tile_ids, tile_end[-1] - 1)
    tile_expert = jnp.sum((last_tile[:, None] >= tile_end[None, :]).astype(jnp.int32), axis=1)
    tile_expert = jnp.minimum(tile_expert, N_EXPERTS - 1)
    seg_start = seg_start.reshape(-1).astype(jnp.int32)
    seg_rows = seg_rows.reshape(-1).astype(jnp.int32)

    xs = _dispatch(seg_start, seg_rows, mask_t, h, n_tiles * TM)
    ys = _expert_ffn(tile_expert, tile_valid, xs, w_gate_up, w_down, tf)
    return _combine(seg_start, seg_rows, x, gate, mask, ys)


def _final_norm_kernel(x_ref, g_ref, o_ref):
    o_ref[...] = _rmsnorm(x_ref[...], g_ref[...], RMS_EPS)


def _final_norm(x, g):
    n, d = x.shape
    return pl.pallas_call(
        _final_norm_kernel,
        grid=(n // TM,),
        in_specs=[pl.BlockSpec((TM, d), lambda i: (i, 0)), pl.BlockSpec((1, d), lambda i: (0, 0))],
        out_specs=pl.BlockSpec((TM, d), lambda i: (i, 0)),
        out_shape=jax.ShapeDtypeStruct((n, d), F32),
        compiler_params=_cparams("parallel"),
        name="final_norm",
    )(x, g.reshape(1, d))


def kernel(x, attn_norm, ffn_norm, final_norm, t5_rel_bias, diff_w_qkv, diff_w_o, diff_lambda_q1, diff_lambda_k1, diff_lambda_q2, diff_lambda_k2, diff_subln, chunk_w_qkv, chunk_w_o, chunk_rel_bias, dense_w_gate_up, dense_w_down, moe_router, moe_w_gate_up, moe_w_down):
    b, s, d = x.shape
    n = b * s
    xf = x.reshape(n, d)
    t5_bias = _t5_bias_t(t5_rel_bias, s)
    for i in range(DEPTH):
        j = i // 2
        if i % 2 == 0:
            qkv = _norm_matmul(xf, attn_norm[i], diff_w_qkv[j].astype(BF16))
            lam_init = 0.8 - 0.6 * math.exp(-0.3 * i)
            lam_vecs = jnp.stack([diff_lambda_q1[j], diff_lambda_k1[j], diff_lambda_q2[j], diff_lambda_k2[j]])
            a = _diff_attention(qkv.reshape(b, s, 3 * d), t5_bias, lam_vecs, diff_subln[j], lam_init)
            xf = _matmul_residual(a.reshape(n, d), diff_w_o[j].astype(BF16), xf)
            xf = _dense_ffn(xf, ffn_norm[i], dense_w_gate_up[j].astype(BF16), dense_w_down[j].astype(BF16),
                            tf=dense_w_down.shape[1] // 2)
        else:
            qkv = _norm_matmul(xf, attn_norm[i], chunk_w_qkv[j].astype(BF16))
            a = _chunk_attention(qkv.reshape(b, s, 3 * d), _band_bias_t(chunk_rel_bias[j]))
            xf = _matmul_residual(a.reshape(n, d), chunk_w_o[j].astype(BF16), xf)
            xf = _moe_ffn(xf, ffn_norm[i], moe_router[j], moe_w_gate_up[j].astype(BF16),
                          moe_w_down[j].astype(BF16), tf=moe_w_down.shape[2] // 2)
    return _final_norm(xf, final_norm).reshape(b, s, d)
```

```python
import functools
import math

import numpy as np
import jax
import jax.numpy as jnp
from jax import lax
from jax.experimental import pallas as pl
from jax.experimental.pallas import tpu as pltpu

F32 = jnp.float32
BF16 = jnp.bfloat16

D_MODEL = 1024
DEPTH = 4
CHUNK = 64
DA_HEADS = 8
DA_HEAD_DIM = 64
T5_BUCKETS = 32
T5_MAX_DIST = 128
CA_HEADS = 16
CA_HEAD_DIM = 64
BAND_PREV = 8
REL_CLIP = 4 * CHUNK
N_EXPERTS = 8
RMS_EPS = 1e-6
SUBLN_EPS = 1e-5
NEG_INF = -1e30

LANES = 128
BF16_ROWS = 16
VMEM_LIMIT = 56 * 1024 * 1024
TM = 512
TQ = 256
TT = 256
KB = 256
N_SLOTS = 4
ONES_ROWS = BF16_ROWS
LOG2E = math.log2(math.e)
Q_SCALE = DA_HEAD_DIM ** -0.5 * LOG2E
PAIR = 2 * CA_HEAD_DIM


def _round_up(a, m):
    return (a + m - 1) // m * m


def _cparams(*sem):
    return pltpu.CompilerParams(dimension_semantics=sem, vmem_limit_bytes=VMEM_LIMIT)


def _rmsnorm(x32, g, eps):
    return x32 * lax.rsqrt(jnp.mean(x32 * x32, axis=-1, keepdims=True) + eps) * g


def _dot_nt(a, b):
    return lax.dot_general(a, b, (((1,), (1,)), ((), ())), preferred_element_type=F32)


def _norm_matmul_kernel(x_ref, g_ref, w_ref, o_ref):
    h = _rmsnorm(x_ref[...], g_ref[...], RMS_EPS).astype(BF16)
    d = x_ref.shape[1]
    q = jnp.dot(h, w_ref[:, :d], preferred_element_type=F32) * Q_SCALE
    o_ref[:, :d] = q.astype(o_ref.dtype)
    o_ref[:, d:] = jnp.dot(h, w_ref[:, d:], preferred_element_type=F32).astype(o_ref.dtype)


def _norm_matmul(x, g, w):
    n, d = x.shape
    n_out = w.shape[1]
    return pl.pallas_call(
        _norm_matmul_kernel,
        grid=(n // TM,),
        in_specs=[pl.BlockSpec((TM, d), lambda i: (i, 0)),
                  pl.BlockSpec((1, d), lambda i: (0, 0)),
                  pl.BlockSpec((d, n_out), lambda i: (0, 0))],
        out_specs=pl.BlockSpec((TM, n_out), lambda i: (i, 0)),
        out_shape=jax.ShapeDtypeStruct((n, n_out), BF16),
        compiler_params=_cparams("parallel"),
        name="norm_qkv",
    )(x, g.reshape(1, d), w)


def _matmul_residual_kernel(a_ref, w_ref, x_ref, o_ref):
    o_ref[...] = x_ref[...] + jnp.dot(a_ref[...], w_ref[...], preferred_element_type=F32)


def _matmul_residual(a, w, x):
    n, d = x.shape
    return pl.pallas_call(
        _matmul_residual_kernel,
        grid=(n // TM,),
        in_specs=[pl.BlockSpec((TM, a.shape[1]), lambda i: (i, 0)),
                  pl.BlockSpec(w.shape, lambda i: (0, 0)),
                  pl.BlockSpec((TM, d), lambda i: (i, 0))],
        out_specs=pl.BlockSpec((TM, d), lambda i: (i, 0)),
        out_shape=jax.ShapeDtypeStruct((n, d), F32),
        compiler_params=_cparams("parallel"),
        name="wo_residual",
    )(a, w, x)


def _t5_bucket_np(rel):
    nb = T5_BUCKETS // 2
    max_exact = nb // 2
    ret = (rel > 0).astype(np.int32) * nb
    n = np.abs(rel)
    n_f = np.maximum(n, 1).astype(np.float32)
    large = max_exact + (np.log(n_f / np.float32(max_exact)) / np.float32(math.log(T5_MAX_DIST / max_exact))
                         * np.float32(nb - max_exact)).astype(np.int32)
    large = np.minimum(large, nb - 1)
    return ret + np.where(n < max_exact, n, large)


def _lookup_kernel(idx_ref, t_ref, o_ref):
    rows = lax.broadcasted_iota(jnp.int32, (t_ref.shape[1], idx_ref.shape[1]), 0)
    onehot = jnp.where(rows == idx_ref[...], 1.0, 0.0)
    o_ref[...] = jnp.dot(t_ref[...], onehot, preferred_element_type=F32, precision=lax.Precision.HIGHEST)


def _table_lookup_t(table, idx):
    r, h = table.shape
    table_t = jnp.zeros((h, _round_up(r, LANES)), F32).at[:, :r].set(table.T)
    return pl.pallas_call(
        _lookup_kernel,
        out_shape=jax.ShapeDtypeStruct((h, len(idx)), F32),
        name="bias_lookup",
    )(jnp.asarray(idx, jnp.int32).reshape(1, -1), table_t)


def _toeplitz_kernel(win_ref, mask_ref, o_ref):
    for blk in range(o_ref.shape[0] // TQ):
        x = jnp.broadcast_to(win_ref[:, blk * 2 * TQ:(blk + 1) * 2 * TQ], (TQ, 2 * TQ))
        skew = pltpu.roll(x, 0, 1, stride=1, stride_axis=0)
        o_ref[blk * TQ:(blk + 1) * TQ, :] = skew[:, :TQ] + mask_ref[blk * TQ:(blk + 1) * TQ, :]


def _window_diagonals(n_keys):
    m = np.arange(2 * TQ)
    back = np.where(m < TQ, -m, 2 * TQ - m)
    d = np.arange(0, n_keys, TQ)[:, None] + back[None, :]
    return np.clip(d, 1 - TQ, n_keys - 1).reshape(-1)


def _toeplitz_bias_t(table, idx, allowed):
    win = _table_lookup_t(table, idx) * LOG2E
    h = win.shape[0]
    n_keys = allowed.shape[0]
    mask_add = jnp.asarray(np.where(allowed, 0.0, NEG_INF).astype(np.float32))
    return pl.pallas_call(
        _toeplitz_kernel,
        grid=(h,),
        in_specs=[pl.BlockSpec((None, 1, win.shape[1]), lambda i: (i, 0, 0)),
                  pl.BlockSpec((n_keys, TQ), lambda i: (0, 0))],
        out_specs=pl.BlockSpec((None, n_keys, TQ), lambda i: (i, 0, 0)),
        out_shape=jax.ShapeDtypeStruct((h, n_keys, TQ), F32),
        compiler_params=_cparams("parallel"),
        name="toeplitz_bias",
    )(win.reshape(h, 1, -1), mask_add)


def _t5_bias_t(table, seq):
    idx = _t5_bucket_np(_window_diagonals(seq) - (seq - TQ))
    i = np.arange(TQ)[None, :]
    j = np.arange(seq)[:, None] - (seq - TQ)
    allowed = (j < 0) | ((j // CHUNK) <= (i // CHUNK))
    return _toeplitz_bias_t(table, idx, allowed)


def _load_vt(v_ref, vt_ref):
    n_v = v_ref.shape[1]
    vt_ref[0:n_v, :] = v_ref[...].T
    vt_ref[n_v:, :] = jnp.ones((ONES_ROWS, v_ref.shape[0]), BF16)


def _attend_t(k_ref, k0, n_k, qm, bias_rows, vt_ref, s_ref, p_ref):
    m = None
    for kb in range(0, n_k, KB):
        s = _dot_nt(k_ref[k0 + kb:k0 + kb + KB, :], qm) + bias_rows(kb)
        s_ref[kb:kb + KB, :] = s
        block_max = jnp.max(s, axis=0, keepdims=True)
        m = block_max if m is None else jnp.maximum(m, block_max)
    for kb in range(0, n_k, KB):
        p_ref[kb:kb + KB, :] = jnp.exp2(s_ref[kb:kb + KB, :] - m).astype(BF16)
    o_t = jnp.dot(vt_ref[:, k0:k0 + n_k], p_ref[0:n_k, :], preferred_element_type=F32)
    n_v = vt_ref.shape[0] - ONES_ROWS
    return o_t[0:n_v] * (1.0 / o_t[n_v:n_v + 1])


def _diff_attn_kernel(q_ref, k_ref, v_ref, bias_ref, lam_ref, g_ref, o_ref, vt_ref, s_ref, p_ref, *, lam_init):
    lv = lam_ref[...]
    lam = (jnp.exp(jnp.sum(lv[0:1] * lv[1:2], axis=-1, keepdims=True))
           - jnp.exp(jnp.sum(lv[2:3] * lv[3:4], axis=-1, keepdims=True)) + lam_init)
    first_map = lax.broadcasted_iota(jnp.int32, (1, 2 * DA_HEAD_DIM), 1) < DA_HEAD_DIM
    seq = q_ref.shape[0]
    _load_vt(v_ref, vt_ref)
    for qi in range(seq // TQ):
        q = q_ref[qi * TQ:(qi + 1) * TQ, :]
        zero = jnp.zeros_like(q)
        n_k = (qi + 1) * TQ

        def bias_rows(kb, n_k=n_k):
            return bias_ref[seq - n_k + kb:seq - n_k + kb + KB, :]

        slot = (2 * qi) % N_SLOTS
        o1 = _attend_t(k_ref, 0, n_k, jnp.where(first_map, q, zero), bias_rows, vt_ref,
                       s_ref.at[slot], p_ref.at[slot])
        o2 = _attend_t(k_ref, 0, n_k, jnp.where(first_map, zero, q), bias_rows, vt_ref,
                       s_ref.at[slot + 1], p_ref.at[slot + 1])
        o_t = o1 - lam * o2
        o_t = o_t * lax.rsqrt(jnp.mean(o_t * o_t, axis=0, keepdims=True) + SUBLN_EPS) * g_ref[...]
        o_ref[qi * TQ:(qi + 1) * TQ, :] = (o_t * (1.0 - lam_init)).T.astype(o_ref.dtype)


def _diff_attention(qkv, bias_t, lam_vecs, subln_g, lam_init):
    b, s, _ = qkv.shape
    hd = 2 * DA_HEAD_DIM
    return pl.pallas_call(
        functools.partial(_diff_attn_kernel, lam_init=lam_init),
        grid=(b, DA_HEADS),
        in_specs=[pl.BlockSpec((None, s, hd), lambda i, h: (i, 0, h)),
                  pl.BlockSpec((None, s, hd), lambda i, h: (i, 0, DA_HEADS + h)),
                  pl.BlockSpec((None, s, hd), lambda i, h: (i, 0, 2 * DA_HEADS + h)),
                  pl.BlockSpec((None, s, TQ), lambda i, h: (h, 0, 0)),
                  pl.BlockSpec((4, DA_HEAD_DIM), lambda i, h: (0, 0)),
                  pl.BlockSpec((hd, 1), lambda i, h: (0, 0))],
        out_specs=pl.BlockSpec((None, s, hd), lambda i, h: (i, 0, h)),
        out_shape=jax.ShapeDtypeStruct((b, s, D_MODEL), BF16),
        scratch_shapes=[pltpu.VMEM((hd + ONES_ROWS, s), BF16), pltpu.VMEM((N_SLOTS, s, TQ), F32),
                        pltpu.VMEM((N_SLOTS, s, TQ), BF16)],
        compiler_params=_cparams("parallel", "parallel"),
        name="diff_attn",
    )(qkv, qkv, qkv, bias_t, lam_vecs, subln_g.reshape(hd, 1))


BAND_KEYS = BAND_PREV * CHUNK + TQ


def _band_bias_t(rel_table):
    dist = BAND_PREV * CHUNK - _window_diagonals(BAND_KEYS)
    q_chunk = np.arange(TQ)[None, :] // CHUNK
    k_chunk = np.arange(BAND_KEYS)[:, None] // CHUNK
    allowed = (k_chunk >= q_chunk) & (k_chunk <= q_chunk + BAND_PREV)
    return _toeplitz_bias_t(rel_table, np.clip(dist, -REL_CLIP, REL_CLIP) + REL_CLIP, allowed)


def _chunk_attn_kernel(q_ref, k_ref, v_ref, bias_ref, o_ref, vt_ref, s_ref, p_ref):
    first_head = lax.broadcasted_iota(jnp.int32, (1, PAIR), 1) < CA_HEAD_DIM
    first_head_rows = lax.broadcasted_iota(jnp.int32, (PAIR, 1), 0) < CA_HEAD_DIM
    seq = q_ref.shape[0]
    _load_vt(v_ref, vt_ref)
    for qi in range(seq // TQ):
        q0 = qi * TQ
        k0 = max(0, q0 - BAND_PREV * CHUNK)
        n_k = q0 + TQ - k0
        q = q_ref[q0:q0 + TQ, :]
        zero = jnp.zeros_like(q)
        outs = []
        for hh in range(2):
            def bias_rows(kb, hh=hh, n_k=n_k):
                return bias_ref[hh, BAND_KEYS - n_k + kb:BAND_KEYS - n_k + kb + KB, :]

            qm = jnp.where(first_head, q, zero) if hh == 0 else jnp.where(first_head, zero, q)
            slot = (2 * qi + hh) % N_SLOTS
            outs.append(_attend_t(k_ref, k0, n_k, qm, bias_rows, vt_ref, s_ref.at[slot], p_ref.at[slot]))
        o_ref[q0:q0 + TQ, :] = jnp.where(first_head_rows, outs[0], outs[1]).T.astype(o_ref.dtype)


def _chunk_attention(qkv, band_bias_t):
    b, s, _ = qkv.shape
    n_pairs = CA_HEADS // 2
    return pl.pallas_call(
        _chunk_attn_kernel,
        grid=(b, n_pairs),
        in_specs=[pl.BlockSpec((None, s, PAIR), lambda i, p: (i, 0, p)),
                  pl.BlockSpec((None, s, PAIR), lambda i, p: (i, 0, n_pairs + p)),
                  pl.BlockSpec((None, s, PAIR), lambda i, p: (i, 0, 2 * n_pairs + p)),
                  pl.BlockSpec((2, BAND_KEYS, TQ), lambda i, p: (p, 0, 0))],
        out_specs=pl.BlockSpec((None, s, PAIR), lambda i, p: (i, 0, p)),
        out_shape=jax.ShapeDtypeStruct((b, s, D_MODEL), BF16),
        scratch_shapes=[pltpu.VMEM((PAIR + ONES_ROWS, s), BF16), pltpu.VMEM((N_SLOTS, BAND_KEYS, TQ), F32),
                        pltpu.VMEM((N_SLOTS, BAND_KEYS, TQ), BF16)],
        compiler_params=_cparams("parallel", "parallel"),
        name="chunk_attn",
    )(qkv, qkv, qkv, band_bias_t)


def _swiglu_partial(h, wg, wu, wd):
    g = jnp.dot(h, wg, preferred_element_type=F32)
    u = jnp.dot(h, wu, preferred_element_type=F32)
    a = (g * (1.0 / (1.0 + jnp.exp(-g))) * u).astype(BF16)
    return jnp.dot(a, wd, preferred_element_type=F32)


def _dense_ffn_kernel(x_ref, g_ref, wg_ref, wu_ref, wd_ref, o_ref, h_ref, acc_ref):
    f = pl.program_id(1)

    @pl.when(f == 0)
    def _():
        h_ref[...] = _rmsnorm(x_ref[...], g_ref[...], RMS_EPS).astype(BF16)
        acc_ref[...] = jnp.zeros_like(acc_ref)

    acc_ref[...] += _swiglu_partial(h_ref[...], wg_ref[...], wu_ref[...], wd_ref[...])

    @pl.when(f == pl.num_programs(1) - 1)
    def _():
        o_ref[...] = x_ref[...] + acc_ref[...]


def _dense_ffn(x, g, w_gate_up, w_down, tf):
    n, d = x.shape
    ff = w_down.shape[0]
    nf = ff // tf
    return pl.pallas_call(
        _dense_ffn_kernel,
        grid=(n // TM, nf),
        in_specs=[pl.BlockSpec((TM, d), lambda i, f: (i, 0)),
                  pl.BlockSpec((1, d), lambda i, f: (0, 0)),
                  pl.BlockSpec((d, tf), lambda i, f: (0, f)),
                  pl.BlockSpec((d, tf), lambda i, f: (0, nf + f)),
                  pl.BlockSpec((tf, d), lambda i, f: (f, 0))],
        out_specs=pl.BlockSpec((TM, d), lambda i, f: (i, 0)),
        out_shape=jax.ShapeDtypeStruct((n, d), F32),
        scratch_shapes=[pltpu.VMEM((TM, d), BF16), pltpu.VMEM((TM, d), F32)],
        compiler_params=_cparams("parallel", "arbitrary"),
        name="dense_ffn",
    )(x, g.reshape(1, d), w_gate_up, w_gate_up, w_down)


def _router_kernel(x_ref, g_ref, r_ref, h_ref, gate_ref, mask_ref, mask_t_ref, cnt_ref):
    h = _rmsnorm(x_ref[...], g_ref[...], RMS_EPS)
    h_ref[...] = h.astype(BF16)
    logits = jnp.dot(h, r_ref[...], preferred_element_type=F32, precision=lax.Precision.HIGHEST)
    lane = lax.broadcasted_iota(jnp.int32, logits.shape, 1)
    logits = jnp.where(lane < N_EXPERTS, logits, -jnp.inf)
    v1 = jnp.max(logits, axis=-1, keepdims=True)
    i1 = jnp.min(jnp.where(logits == v1, lane, LANES), axis=-1, keepdims=True)
    m1 = lane == i1
    rest = jnp.where(m1, -jnp.inf, logits)
    v2 = jnp.max(rest, axis=-1, keepdims=True)
    i2 = jnp.min(jnp.where(rest == v2, lane, LANES), axis=-1, keepdims=True)
    m2 = lane == i2
    e2 = jnp.exp(v2 - v1)
    g1 = 1.0 / (1.0 + e2)
    g2 = e2 / (1.0 + e2)
    gate_ref[...] = jnp.where(m1, g1, 0.0) + jnp.where(m2, g2, 0.0)
    mask = jnp.where(m1 | m2, 1.0, 0.0)
    mask_ref[...] = mask
    mask_t_ref[...] = mask.T[0:N_EXPERTS, :]
    cnt_ref[...] = jnp.broadcast_to(jnp.sum(mask, axis=0, keepdims=True), cnt_ref.shape)


def _router(x, g, router):
    n, d = x.shape
    t = n // TT
    r_pad = jnp.zeros((d, LANES), F32).at[:, :N_EXPERTS].set(router)
    return pl.pallas_call(
        _router_kernel,
        grid=(t,),
        in_specs=[pl.BlockSpec((TT, d), lambda i: (i, 0)),
                  pl.BlockSpec((1, d), lambda i: (0, 0)),
                  pl.BlockSpec((d, LANES), lambda i: (0, 0))],
        out_specs=[pl.BlockSpec((TT, d), lambda i: (i, 0)),
                   pl.BlockSpec((TT, LANES), lambda i: (i, 0)),
                   pl.BlockSpec((TT, LANES), lambda i: (i, 0)),
                   pl.BlockSpec((N_EXPERTS, TT), lambda i: (0, i)),
                   pl.BlockSpec((None, 8, LANES), lambda i: (i, 0, 0))],
        out_shape=[jax.ShapeDtypeStruct((n, d), BF16),
                   jax.ShapeDtypeStruct((n, LANES), F32),
                   jax.ShapeDtypeStruct((n, LANES), F32),
                   jax.ShapeDtypeStruct((N_EXPERTS, n), F32),
                   jax.ShapeDtypeStruct((t, 8, LANES), F32)],
        compiler_params=_cparams("parallel"),
        name="router",
    )(x, g.reshape(1, d), r_pad)


def _segment_copies(n_rows, make_copy):
    off = jnp.int32(0)
    size = TT
    while size >= BF16_ROWS:
        take = (n_rows & size) != 0

        @pl.when(take)
        def _(off=off, size=size):
            make_copy(off, size)

        off = off + jnp.where(take, size, 0)
        size //= 2


def _dispatch_kernel(start_ref, rows_ref, mask_t_ref, h_ref, xs_in_ref, xs_ref, blk_ref, sem_ref):
    del xs_in_ref
    t = pl.program_id(0)
    slot = t % 2
    mask_t = mask_t_ref[...]
    row = lax.broadcasted_iota(jnp.int32, (TT, TT), 0)
    col = lax.broadcasted_iota(jnp.int32, (TT, TT), 1)
    before = jnp.where(row < col, 1.0, 0.0).astype(BF16)
    rank = jnp.dot(mask_t.astype(BF16), before, preferred_element_type=F32)
    row_f = row.astype(F32)
    h = h_ref[...]
    for e in range(N_EXPERTS):
        pick = (row_f == rank[e:e + 1, :]) & (mask_t[e:e + 1, :] > 0.0)
        sel = jnp.where(pick, 1.0, 0.0).astype(BF16)
        blk_ref[slot, e] = jnp.dot(sel, h, preferred_element_type=F32).astype(BF16)

    def copies(step, step_slot, wait):
        for e in range(N_EXPERTS):
            start = pl.multiple_of(start_ref[step * N_EXPERTS + e], BF16_ROWS)

            def make_copy(off, size, e=e, start=start):
                src = blk_ref.at[step_slot, e, pl.ds(pl.multiple_of(off, BF16_ROWS), size)]
                dst = xs_ref.at[pl.ds(pl.multiple_of(start + off, BF16_ROWS), size)]
                cp = pltpu.make_async_copy(src, dst, sem_ref.at[step_slot])
                cp.wait() if wait else cp.start()

            _segment_copies(rows_ref[step * N_EXPERTS + e], make_copy)

    copies(t, slot, wait=False)

    @pl.when(t > 0)
    def _():
        copies(t - 1, 1 - slot, wait=True)

    @pl.when(t == pl.num_programs(0) - 1)
    def _():
        copies(t, slot, wait=True)


def _dispatch(seg_start, seg_rows, mask_t, h, n_rows):
    n, d = h.shape
    t = n // TT
    xs0 = jnp.zeros((n_rows, d), BF16)
    grid_spec = pltpu.PrefetchScalarGridSpec(
        num_scalar_prefetch=2,
        grid=(t,),
        in_specs=[pl.BlockSpec((N_EXPERTS, TT), lambda i, *_: (0, i)),
                  pl.BlockSpec((TT, d), lambda i, *_: (i, 0)),
                  pl.BlockSpec(memory_space=pl.ANY)],
        out_specs=pl.BlockSpec(memory_space=pl.ANY),
        scratch_shapes=[pltpu.VMEM((2, N_EXPERTS, TT, d), BF16), pltpu.SemaphoreType.DMA((2,))],
    )
    return pl.pallas_call(
        _dispatch_kernel,
        grid_spec=grid_spec,
        out_shape=jax.ShapeDtypeStruct((n_rows, d), BF16),
        input_output_aliases={4: 0},
        compiler_params=_cparams("arbitrary"),
        name="moe_dispatch",
    )(seg_start, seg_rows, mask_t, h, xs0)


def _expert_kernel(tile_expert_ref, tile_valid_ref, xs_ref, wg_ref, wu_ref, wd_ref, o_ref, acc_ref):
    del tile_expert_ref
    j = pl.program_id(0)
    f = pl.program_id(1)
    last = pl.num_programs(1) - 1
    valid = tile_valid_ref[j] > 0

    @pl.when(valid)
    def _():
        @pl.when(f == 0)
        def _():
            acc_ref[...] = jnp.zeros_like(acc_ref)

        acc_ref[...] += _swiglu_partial(xs_ref[...], wg_ref[...], wu_ref[...], wd_ref[...])

        @pl.when(f == last)
        def _():
            o_ref[...] = acc_ref[...].astype(o_ref.dtype)

    @pl.when(jnp.logical_not(valid) & (f == last))
    def _():
        o_ref[...] = jnp.zeros_like(o_ref)


def _expert_ffn(tile_expert, tile_valid, xs, w_gate_up, w_down, tf):
    n_rows, d = xs.shape
    ff = w_down.shape[1]
    nf = ff // tf

    def f_eff(j, f, tv):
        return jnp.where(tv[j] > 0, f, nf - 1)

    grid_spec = pltpu.PrefetchScalarGridSpec(
        num_scalar_prefetch=2,
        grid=(n_rows // TM, nf),
        in_specs=[pl.BlockSpec((TM, d), lambda j, f, te, tv: (j, 0)),
                  pl.BlockSpec((None, d, tf), lambda j, f, te, tv: (te[j], 0, f_eff(j, f, tv))),
                  pl.BlockSpec((None, d, tf), lambda j, f, te, tv: (te[j], 0, nf + f_eff(j, f, tv))),
                  pl.BlockSpec((None, tf, d), lambda j, f, te, tv: (te[j], f_eff(j, f, tv), 0))],
        out_specs=pl.BlockSpec((TM, d), lambda j, f, te, tv: (j, 0)),
        scratch_shapes=[pltpu.VMEM((TM, d), F32)],
    )
    return pl.pallas_call(
        _expert_kernel,
        grid_spec=grid_spec,
        out_shape=jax.ShapeDtypeStruct((n_rows, d), BF16),
        compiler_params=_cparams("parallel", "arbitrary"),
        name="moe_experts",
    )(tile_expert, tile_valid, xs, w_gate_up, w_gate_up, w_down)


def _combine_kernel(start_ref, rows_ref, x_ref, gate_ref, mask_ref, out_gain_ref, ys_ref, o_ref, win_ref, sem_ref,
                    *, out_norm):
    t = pl.program_id(0)
    n_t = pl.num_programs(0)
    slot = t % 2

    def copies(step, step_slot, wait):
        for e in range(N_EXPERTS):
            start = pl.multiple_of(start_ref[step * N_EXPERTS + e], BF16_ROWS)

            def make_copy(off, size, e=e, start=start):
                src = ys_ref.at[pl.ds(pl.multiple_of(start + off, BF16_ROWS), size)]
                dst = win_ref.at[step_slot, e, pl.ds(pl.multiple_of(off, BF16_ROWS), size)]
                cp = pltpu.make_async_copy(src, dst, sem_ref.at[step_slot])
                cp.wait() if wait else cp.start()

            _segment_copies(rows_ref[step * N_EXPERTS + e], make_copy)

    @pl.when(t == 0)
    def _():
        win_ref[...] = jnp.zeros_like(win_ref)
        copies(0, 0, wait=False)

    @pl.when(t + 1 < n_t)
    def _():
        copies(t + 1, 1 - slot, wait=False)

    copies(t, slot, wait=True)

    mask = mask_ref[...]
    gate = gate_ref[...]
    row = lax.broadcasted_iota(jnp.int32, (TT, TT), 0)
    col = lax.broadcasted_iota(jnp.int32, (TT, TT), 1)
    before = jnp.where(col < row, 1.0, 0.0).astype(BF16)
    rank = jnp.dot(before, mask.astype(BF16), preferred_element_type=F32)
    col_f = col.astype(F32)
    y = x_ref[...]
    for e in range(N_EXPERTS):
        pick = (col_f == rank[:, e:e + 1]) & (mask[:, e:e + 1] > 0.0)
        sel = jnp.where(pick, 1.0, 0.0).astype(BF16)
        y = y + gate[:, e:e + 1] * jnp.dot(sel, win_ref[slot, e], preferred_element_type=F32)
    o_ref[...] = _rmsnorm(y, out_gain_ref[...], RMS_EPS) if out_norm else y


def _combine(seg_start, seg_rows, x, gate, mask, ys, out_gain, out_norm):
    n, d = x.shape
    t = n // TT
    grid_spec = pltpu.PrefetchScalarGridSpec(
        num_scalar_prefetch=2,
        grid=(t,),
        in_specs=[pl.BlockSpec((TT, d), lambda i, *_: (i, 0)),
                  pl.BlockSpec((TT, LANES), lambda i, *_: (i, 0)),
                  pl.BlockSpec((TT, LANES), lambda i, *_: (i, 0)),
                  pl.BlockSpec((1, d), lambda i, *_: (0, 0)),
                  pl.BlockSpec(memory_space=pl.ANY)],
        out_specs=pl.BlockSpec((TT, d), lambda i, *_: (i, 0)),
        scratch_shapes=[pltpu.VMEM((2, N_EXPERTS, TT, d), BF16), pltpu.SemaphoreType.DMA((2,))],
    )
    return pl.pallas_call(
        functools.partial(_combine_kernel, out_norm=out_norm),
        grid_spec=grid_spec,
        out_shape=jax.ShapeDtypeStruct((n, d), F32),
        compiler_params=_cparams("arbitrary"),
        name="moe_combine",
    )(seg_start, seg_rows, x, gate, mask, out_gain.reshape(1, d), ys)


def _moe_ffn(x, g, router, w_gate_up, w_down, tf, out_gain, out_norm):
    n, d = x.shape
    t = n // TT
    h, gate, mask, mask_t, cnt = _router(x, g, router)
    cnt = cnt[:, 0, :N_EXPERTS].astype(jnp.int32)
    seg_rows = _round_up(cnt, BF16_ROWS)
    region = _round_up(jnp.sum(seg_rows, axis=0), TM)
    region_end = jnp.cumsum(region)
    seg_start = (region_end - region)[None, :] + jnp.cumsum(seg_rows, axis=0) - seg_rows
    n_tiles = (_round_up(2 * n + t * N_EXPERTS * (BF16_ROWS - 1), TM) + N_EXPERTS * TM) // TM
    tile_end = region_end // TM
    tile_ids = jnp.arange(n_tiles, dtype=jnp.int32)
    tile_valid = (tile_ids < tile_end[-1]).astype(jnp.int32)
    last_tile = jnp.minimum(tile_ids, tile_end[-1] - 1)
    tile_expert = jnp.sum((last_tile[:, None] >= tile_end[None, :]).astype(jnp.int32), axis=1)
    tile_expert = jnp.minimum(tile_expert, N_EXPERTS - 1)
    seg_start = seg_start.reshape(-1).astype(jnp.int32)
    seg_rows = seg_rows.reshape(-1).astype(jnp.int32)

    xs = _dispatch(seg_start, seg_rows, mask_t, h, n_tiles * TM)
    ys = _expert_ffn(tile_expert, tile_valid, xs, w_gate_up, w_down, tf)
    return _combine(seg_start, seg_rows, x, gate, mask, ys, out_gain, out_norm)


def kernel(x, attn_norm, ffn_norm, final_norm, t5_rel_bias, diff_w_qkv, diff_w_o, diff_lambda_q1, diff_lambda_k1, diff_lambda_q2, diff_lambda_k2, diff_subln, chunk_w_qkv, chunk_w_o, chunk_rel_bias, dense_w_gate_up, dense_w_down, moe_router, moe_w_gate_up, moe_w_down):
    b, s, d = x.shape
    n = b * s
    xf = x.reshape(n, d)
    t5_bias = _t5_bias_t(t5_rel_bias, s)
    for i in range(DEPTH):
        j = i // 2
        if i % 2 == 0:
            qkv = _norm_matmul(xf, attn_norm[i], diff_w_qkv[j].astype(BF16))
            lam_init = 0.8 - 0.6 * math.exp(-0.3 * i)
            lam_vecs = jnp.stack([diff_lambda_q1[j], diff_lambda_k1[j], diff_lambda_q2[j], diff_lambda_k2[j]])
            a = _diff_attention(qkv.reshape(b, s, 3 * d), t5_bias, lam_vecs, diff_subln[j], lam_init)
            xf = _matmul_residual(a.reshape(n, d), diff_w_o[j].astype(BF16), xf)
            xf = _dense_ffn(xf, ffn_norm[i], dense_w_gate_up[j].astype(BF16), dense_w_down[j].astype(BF16),
                            tf=dense_w_down.shape[1] // 2)
        else:
            qkv = _norm_matmul(xf, attn_norm[i], chunk_w_qkv[j].astype(BF16))
            a = _chunk_attention(qkv.reshape(b, s, 3 * d), _band_bias_t(chunk_rel_bias[j]))
            xf = _matmul_residual(a.reshape(n, d), chunk_w_o[j].astype(BF16), xf)
            xf = _moe_ffn(xf, ffn_norm[i], moe_router[j], moe_w_gate_up[j].astype(BF16),
                          moe_w_down[j].astype(BF16), tf=moe_w_down.shape[2] // 2,
                          out_gain=final_norm, out_norm=i == DEPTH - 1)
    return xf.reshape(b, s, d)
```

```python
import functools
import math

import numpy as np
import jax
import jax.numpy as jnp
from jax import lax
from jax.experimental import pallas as pl
from jax.experimental.pallas import tpu as pltpu

F32 = jnp.float32
BF16 = jnp.bfloat16

D_MODEL = 1024
DEPTH = 4
CHUNK = 64
DA_HEADS = 8
DA_HEAD_DIM = 64
T5_BUCKETS = 32
T5_MAX_DIST = 128
CA_HEADS = 16
CA_HEAD_DIM = 64
BAND_PREV = 8
REL_CLIP = 4 * CHUNK
N_EXPERTS = 8
RMS_EPS = 1e-6
SUBLN_EPS = 1e-5
NEG_INF = -1e30

LANES = 128
BF16_ROWS = 16
VMEM_LIMIT = 56 * 1024 * 1024
TM = 512
TE = 2 * TM
TQ = 256
TT = 256
KB = 256
N_SLOTS = 4
ONES_ROWS = BF16_ROWS
LOG2E = math.log2(math.e)
Q_SCALE = DA_HEAD_DIM ** -0.5 * LOG2E
PAIR = 2 * CA_HEAD_DIM


def _round_up(a, m):
    return (a + m - 1) // m * m


def _cparams(*sem):
    return pltpu.CompilerParams(dimension_semantics=sem, vmem_limit_bytes=VMEM_LIMIT)


def _rmsnorm(x32, g, eps):
    return x32 * lax.rsqrt(jnp.mean(x32 * x32, axis=-1, keepdims=True) + eps) * g


def _dot_nt(a, b):
    return lax.dot_general(a, b, (((1,), (1,)), ((), ())), preferred_element_type=F32)


def _norm_matmul_kernel(x_ref, g_ref, w_ref, o_ref):
    h = _rmsnorm(x_ref[...], g_ref[...], RMS_EPS).astype(BF16)
    d = x_ref.shape[1]
    q = jnp.dot(h, w_ref[:, :d], preferred_element_type=F32) * Q_SCALE
    o_ref[:, :d] = q.astype(o_ref.dtype)
    o_ref[:, d:] = jnp.dot(h, w_ref[:, d:], preferred_element_type=F32).astype(o_ref.dtype)


def _norm_matmul(x, g, w):
    n, d = x.shape
    n_out = w.shape[1]
    return pl.pallas_call(
        _norm_matmul_kernel,
        grid=(n // TM,),
        in_specs=[pl.BlockSpec((TM, d), lambda i: (i, 0)),
                  pl.BlockSpec((1, d), lambda i: (0, 0)),
                  pl.BlockSpec((d, n_out), lambda i: (0, 0))],
        out_specs=pl.BlockSpec((TM, n_out), lambda i: (i, 0)),
        out_shape=jax.ShapeDtypeStruct((n, n_out), BF16),
        compiler_params=_cparams("parallel"),
        name="norm_qkv",
    )(x, g.reshape(1, d), w)


def _matmul_residual_kernel(a_ref, w_ref, x_ref, o_ref):
    o_ref[...] = x_ref[...] + jnp.dot(a_ref[...], w_ref[...], preferred_element_type=F32)


def _matmul_residual(a, w, x):
    n, d = x.shape
    return pl.pallas_call(
        _matmul_residual_kernel,
        grid=(n // TM,),
        in_specs=[pl.BlockSpec((TM, a.shape[1]), lambda i: (i, 0)),
                  pl.BlockSpec(w.shape, lambda i: (0, 0)),
                  pl.BlockSpec((TM, d), lambda i: (i, 0))],
        out_specs=pl.BlockSpec((TM, d), lambda i: (i, 0)),
        out_shape=jax.ShapeDtypeStruct((n, d), F32),
        compiler_params=_cparams("parallel"),
        name="wo_residual",
    )(a, w, x)


def _t5_bucket_np(rel):
    nb = T5_BUCKETS // 2
    max_exact = nb // 2
    ret = (rel > 0).astype(np.int32) * nb
    n = np.abs(rel)
    n_f = np.maximum(n, 1).astype(np.float32)
    large = max_exact + (np.log(n_f / np.float32(max_exact)) / np.float32(math.log(T5_MAX_DIST / max_exact))
                         * np.float32(nb - max_exact)).astype(np.int32)
    large = np.minimum(large, nb - 1)
    return ret + np.where(n < max_exact, n, large)


def _lookup_kernel(idx_ref, t_ref, o_ref):
    rows = lax.broadcasted_iota(jnp.int32, (t_ref.shape[1], idx_ref.shape[1]), 0)
    onehot = jnp.where(rows == idx_ref[...], 1.0, 0.0)
    o_ref[...] = jnp.dot(t_ref[...], onehot, preferred_element_type=F32, precision=lax.Precision.HIGHEST)


def _table_lookup_t(table, idx):
    r, h = table.shape
    table_t = jnp.zeros((h, _round_up(r, LANES)), F32).at[:, :r].set(table.T)
    return pl.pallas_call(
        _lookup_kernel,
        out_shape=jax.ShapeDtypeStruct((h, len(idx)), F32),
        name="bias_lookup",
    )(jnp.asarray(idx, jnp.int32).reshape(1, -1), table_t)


def _toeplitz_kernel(win_ref, mask_ref, o_ref):
    for blk in range(o_ref.shape[0] // TQ):
        x = jnp.broadcast_to(win_ref[:, blk * 2 * TQ:(blk + 1) * 2 * TQ], (TQ, 2 * TQ))
        skew = pltpu.roll(x, 0, 1, stride=1, stride_axis=0)
        o_ref[blk * TQ:(blk + 1) * TQ, :] = skew[:, :TQ] + mask_ref[blk * TQ:(blk + 1) * TQ, :]


def _window_diagonals(n_keys):
    m = np.arange(2 * TQ)
    back = np.where(m < TQ, -m, 2 * TQ - m)
    d = np.arange(0, n_keys, TQ)[:, None] + back[None, :]
    return np.clip(d, 1 - TQ, n_keys - 1).reshape(-1)


def _toeplitz_bias_t(table, idx, allowed):
    win = _table_lookup_t(table, idx) * LOG2E
    h = win.shape[0]
    n_keys = allowed.shape[0]
    mask_add = jnp.asarray(np.where(allowed, 0.0, NEG_INF).astype(np.float32))
    return pl.pallas_call(
        _toeplitz_kernel,
        grid=(h,),
        in_specs=[pl.BlockSpec((None, 1, win.shape[1]), lambda i: (i, 0, 0)),
                  pl.BlockSpec((n_keys, TQ), lambda i: (0, 0))],
        out_specs=pl.BlockSpec((None, n_keys, TQ), lambda i: (i, 0, 0)),
        out_shape=jax.ShapeDtypeStruct((h, n_keys, TQ), F32),
        compiler_params=_cparams("parallel"),
        name="toeplitz_bias",
    )(win.reshape(h, 1, -1), mask_add)


def _t5_bias_t(table, seq):
    idx = _t5_bucket_np(_window_diagonals(seq) - (seq - TQ))
    i = np.arange(TQ)[None, :]
    j = np.arange(seq)[:, None] - (seq - TQ)
    allowed = (j < 0) | ((j // CHUNK) <= (i // CHUNK))
    return _toeplitz_bias_t(table, idx, allowed)


def _load_vt(v_ref, vt_ref):
    n_v = v_ref.shape[1]
    vt_ref[0:n_v, :] = v_ref[...].T
    vt_ref[n_v:, :] = jnp.ones((ONES_ROWS, v_ref.shape[0]), BF16)


def _attend_t(k_ref, k0, n_k, qm, bias_rows, vt_ref, s_ref, p_ref):
    m = None
    for kb in range(0, n_k, KB):
        s = _dot_nt(k_ref[k0 + kb:k0 + kb + KB, :], qm) + bias_rows(kb)
        s_ref[kb:kb + KB, :] = s
        block_max = jnp.max(s, axis=0, keepdims=True)
        m = block_max if m is None else jnp.maximum(m, block_max)
    for kb in range(0, n_k, KB):
        p_ref[kb:kb + KB, :] = jnp.exp2(s_ref[kb:kb + KB, :] - m).astype(BF16)
    o_t = jnp.dot(vt_ref[:, k0:k0 + n_k], p_ref[0:n_k, :], preferred_element_type=F32)
    n_v = vt_ref.shape[0] - ONES_ROWS
    return o_t[0:n_v] * (1.0 / o_t[n_v:n_v + 1])


def _diff_attn_kernel(q_ref, k_ref, v_ref, bias_ref, lam_ref, g_ref, o_ref, vt_ref, s_ref, p_ref, *, lam_init):
    lv = lam_ref[...]
    lam = (jnp.exp(jnp.sum(lv[0:1] * lv[1:2], axis=-1, keepdims=True))
           - jnp.exp(jnp.sum(lv[2:3] * lv[3:4], axis=-1, keepdims=True)) + lam_init)
    first_map = lax.broadcasted_iota(jnp.int32, (1, 2 * DA_HEAD_DIM), 1) < DA_HEAD_DIM
    seq = q_ref.shape[0]
    _load_vt(v_ref, vt_ref)
    for qi in range(seq // TQ):
        q = q_ref[qi * TQ:(qi + 1) * TQ, :]
        zero = jnp.zeros_like(q)
        n_k = (qi + 1) * TQ

        def bias_rows(kb, n_k=n_k):
            return bias_ref[seq - n_k + kb:seq - n_k + kb + KB, :]

        slot = (2 * qi) % N_SLOTS
        o1 = _attend_t(k_ref, 0, n_k, jnp.where(first_map, q, zero), bias_rows, vt_ref,
                       s_ref.at[slot], p_ref.at[slot])
        o2 = _attend_t(k_ref, 0, n_k, jnp.where(first_map, zero, q), bias_rows, vt_ref,
                       s_ref.at[slot + 1], p_ref.at[slot + 1])
        o_t = o1 - lam * o2
        o_t = o_t * lax.rsqrt(jnp.mean(o_t * o_t, axis=0, keepdims=True) + SUBLN_EPS) * g_ref[...]
        o_ref[qi * TQ:(qi + 1) * TQ, :] = (o_t * (1.0 - lam_init)).T.astype(o_ref.dtype)


def _diff_attention(qkv, bias_t, lam_vecs, subln_g, lam_init):
    b, s, _ = qkv.shape
    hd = 2 * DA_HEAD_DIM
    return pl.pallas_call(
        functools.partial(_diff_attn_kernel, lam_init=lam_init),
        grid=(b, DA_HEADS),
        in_specs=[pl.BlockSpec((None, s, hd), lambda i, h: (i, 0, h)),
                  pl.BlockSpec((None, s, hd), lambda i, h: (i, 0, DA_HEADS + h)),
                  pl.BlockSpec((None, s, hd), lambda i, h: (i, 0, 2 * DA_HEADS + h)),
                  pl.BlockSpec((None, s, TQ), lambda i, h: (h, 0, 0)),
                  pl.BlockSpec((4, DA_HEAD_DIM), lambda i, h: (0, 0)),
                  pl.BlockSpec((hd, 1), lambda i, h: (0, 0))],
        out_specs=pl.BlockSpec((None, s, hd), lambda i, h: (i, 0, h)),
        out_shape=jax.ShapeDtypeStruct((b, s, D_MODEL), BF16),
        scratch_shapes=[pltpu.VMEM((hd + ONES_ROWS, s), BF16), pltpu.VMEM((N_SLOTS, s, TQ), F32),
                        pltpu.VMEM((N_SLOTS, s, TQ), BF16)],
        compiler_params=_cparams("parallel", "parallel"),
        name="diff_attn",
    )(qkv, qkv, qkv, bias_t, lam_vecs, subln_g.reshape(hd, 1))


BAND_KEYS = BAND_PREV * CHUNK + TQ


def _band_bias_t(rel_table):
    dist = BAND_PREV * CHUNK - _window_diagonals(BAND_KEYS)
    q_chunk = np.arange(TQ)[None, :] // CHUNK
    k_chunk = np.arange(BAND_KEYS)[:, None] // CHUNK
    allowed = (k_chunk >= q_chunk) & (k_chunk <= q_chunk + BAND_PREV)
    return _toeplitz_bias_t(rel_table, np.clip(dist, -REL_CLIP, REL_CLIP) + REL_CLIP, allowed)


def _chunk_attn_kernel(q_ref, k_ref, v_ref, bias_ref, o_ref, vt_ref, s_ref, p_ref):
    first_head = lax.broadcasted_iota(jnp.int32, (1, PAIR), 1) < CA_HEAD_DIM
    first_head_rows = lax.broadcasted_iota(jnp.int32, (PAIR, 1), 0) < CA_HEAD_DIM
    seq = q_ref.shape[0]
    _load_vt(v_ref, vt_ref)
    for qi in range(seq // TQ):
        q0 = qi * TQ
        k0 = max(0, q0 - BAND_PREV * CHUNK)
        n_k = q0 + TQ - k0
        q = q_ref[q0:q0 + TQ, :]
        zero = jnp.zeros_like(q)
        outs = []
        for hh in range(2):
            def bias_rows(kb, hh=hh, n_k=n_k):
                return bias_ref[hh, BAND_KEYS - n_k + kb:BAND_KEYS - n_k + kb + KB, :]

            qm = jnp.where(first_head, q, zero) if hh == 0 else jnp.where(first_head, zero, q)
            slot = (2 * qi + hh) % N_SLOTS
            outs.append(_attend_t(k_ref, k0, n_k, qm, bias_rows, vt_ref, s_ref.at[slot], p_ref.at[slot]))
        o_ref[q0:q0 + TQ, :] = jnp.where(first_head_rows, outs[0], outs[1]).T.astype(o_ref.dtype)


def _chunk_attention(qkv, band_bias_t):
    b, s, _ = qkv.shape
    n_pairs = CA_HEADS // 2
    return pl.pallas_call(
        _chunk_attn_kernel,
        grid=(b, n_pairs),
        in_specs=[pl.BlockSpec((None, s, PAIR), lambda i, p: (i, 0, p)),
                  pl.BlockSpec((None, s, PAIR), lambda i, p: (i, 0, n_pairs + p)),
                  pl.BlockSpec((None, s, PAIR), lambda i, p: (i, 0, 2 * n_pairs + p)),
                  pl.BlockSpec((2, BAND_KEYS, TQ), lambda i, p: (p, 0, 0))],
        out_specs=pl.BlockSpec((None, s, PAIR), lambda i, p: (i, 0, p)),
        out_shape=jax.ShapeDtypeStruct((b, s, D_MODEL), BF16),
        scratch_shapes=[pltpu.VMEM((PAIR + ONES_ROWS, s), BF16), pltpu.VMEM((N_SLOTS, BAND_KEYS, TQ), F32),
                        pltpu.VMEM((N_SLOTS, BAND_KEYS, TQ), BF16)],
        compiler_params=_cparams("parallel", "parallel"),
        name="chunk_attn",
    )(qkv, qkv, qkv, band_bias_t)


def _swiglu_partial(h, wg, wu, wd):
    g = jnp.dot(h, wg, preferred_element_type=F32)
    u = jnp.dot(h, wu, preferred_element_type=F32)
    a = (g * (1.0 / (1.0 + jnp.exp(-g))) * u).astype(BF16)
    return jnp.dot(a, wd, preferred_element_type=F32)


def _dense_ffn_kernel(x_ref, g_ref, wg_ref, wu_ref, wd_ref, o_ref, h_ref, acc_ref):
    f = pl.program_id(1)

    @pl.when(f == 0)
    def _():
        h_ref[...] = _rmsnorm(x_ref[...], g_ref[...], RMS_EPS).astype(BF16)
        acc_ref[...] = jnp.zeros_like(acc_ref)

    acc_ref[...] += _swiglu_partial(h_ref[...], wg_ref[...], wu_ref[...], wd_ref[...])

    @pl.when(f == pl.num_programs(1) - 1)
    def _():
        o_ref[...] = x_ref[...] + acc_ref[...]


def _dense_ffn(x, g, w_gate_up, w_down, tf):
    n, d = x.shape
    ff = w_down.shape[0]
    nf = ff // tf
    return pl.pallas_call(
        _dense_ffn_kernel,
        grid=(n // TM, nf),
        in_specs=[pl.BlockSpec((TM, d), lambda i, f: (i, 0)),
                  pl.BlockSpec((1, d), lambda i, f: (0, 0)),
                  pl.BlockSpec((d, tf), lambda i, f: (0, f)),
                  pl.BlockSpec((d, tf), lambda i, f: (0, nf + f)),
                  pl.BlockSpec((tf, d), lambda i, f: (f, 0))],
        out_specs=pl.BlockSpec((TM, d), lambda i, f: (i, 0)),
        out_shape=jax.ShapeDtypeStruct((n, d), F32),
        scratch_shapes=[pltpu.VMEM((TM, d), BF16), pltpu.VMEM((TM, d), F32)],
        compiler_params=_cparams("parallel", "arbitrary"),
        name="dense_ffn",
    )(x, g.reshape(1, d), w_gate_up, w_gate_up, w_down)


def _router_kernel(x_ref, g_ref, r_ref, h_ref, gate_ref, mask_ref, mask_t_ref, cnt_ref):
    h = _rmsnorm(x_ref[...], g_ref[...], RMS_EPS)
    h_ref[...] = h.astype(BF16)
    logits = jnp.dot(h, r_ref[...], preferred_element_type=F32, precision=lax.Precision.HIGHEST)
    lane = lax.broadcasted_iota(jnp.int32, logits.shape, 1)
    logits = jnp.where(lane < N_EXPERTS, logits, -jnp.inf)
    v1 = jnp.max(logits, axis=-1, keepdims=True)
    i1 = jnp.min(jnp.where(logits == v1, lane, LANES), axis=-1, keepdims=True)
    m1 = lane == i1
    rest = jnp.where(m1, -jnp.inf, logits)
    v2 = jnp.max(rest, axis=-1, keepdims=True)
    i2 = jnp.min(jnp.where(rest == v2, lane, LANES), axis=-1, keepdims=True)
    m2 = lane == i2
    e2 = jnp.exp(v2 - v1)
    g1 = 1.0 / (1.0 + e2)
    g2 = e2 / (1.0 + e2)
    gate_ref[...] = jnp.where(m1, g1, 0.0) + jnp.where(m2, g2, 0.0)
    mask = jnp.where(m1 | m2, 1.0, 0.0)
    mask_ref[...] = mask
    mask_t_ref[...] = mask.T[0:N_EXPERTS, :]
    cnt_ref[...] = jnp.broadcast_to(jnp.sum(mask, axis=0, keepdims=True), cnt_ref.shape)


def _router(x, g, router):
    n, d = x.shape
    t = n // TT
    r_pad = jnp.zeros((d, LANES), F32).at[:, :N_EXPERTS].set(router)
    return pl.pallas_call(
        _router_kernel,
        grid=(t,),
        in_specs=[pl.BlockSpec((TT, d), lambda i: (i, 0)),
                  pl.BlockSpec((1, d), lambda i: (0, 0)),
                  pl.BlockSpec((d, LANES), lambda i: (0, 0))],
        out_specs=[pl.BlockSpec((TT, d), lambda i: (i, 0)),
                   pl.BlockSpec((TT, LANES), lambda i: (i, 0)),
                   pl.BlockSpec((TT, LANES), lambda i: (i, 0)),
                   pl.BlockSpec((N_EXPERTS, TT), lambda i: (0, i)),
                   pl.BlockSpec((None, 8, LANES), lambda i: (i, 0, 0))],
        out_shape=[jax.ShapeDtypeStruct((n, d), BF16),
                   jax.ShapeDtypeStruct((n, LANES), F32),
                   jax.ShapeDtypeStruct((n, LANES), F32),
                   jax.ShapeDtypeStruct((N_EXPERTS, n), F32),
                   jax.ShapeDtypeStruct((t, 8, LANES), F32)],
        compiler_params=_cparams("parallel"),
        name="router",
    )(x, g.reshape(1, d), r_pad)


def _segment_copies(n_rows, make_copy):
    off = jnp.int32(0)
    size = TT
    while size >= BF16_ROWS:
        take = (n_rows & size) != 0

        @pl.when(take)
        def _(off=off, size=size):
            make_copy(off, size)

        off = off + jnp.where(take, size, 0)
        size //= 2


def _dispatch_kernel(start_ref, rows_ref, mask_t_ref, h_ref, xs_in_ref, xs_ref, blk_ref, sem_ref):
    del xs_in_ref
    t = pl.program_id(0)
    slot = t % 2
    mask_t = mask_t_ref[...]
    row = lax.broadcasted_iota(jnp.int32, (TT, TT), 0)
    col = lax.broadcasted_iota(jnp.int32, (TT, TT), 1)
    before = jnp.where(row < col, 1.0, 0.0).astype(BF16)
    rank = jnp.dot(mask_t.astype(BF16), before, preferred_element_type=F32)
    row_f = row.astype(F32)
    h = h_ref[...]
    for e in range(N_EXPERTS):
        pick = (row_f == rank[e:e + 1, :]) & (mask_t[e:e + 1, :] > 0.0)
        sel = jnp.where(pick, 1.0, 0.0).astype(BF16)
        blk_ref[slot, e] = jnp.dot(sel, h, preferred_element_type=F32).astype(BF16)

    def copies(step, step_slot, wait):
        for e in range(N_EXPERTS):
            start = pl.multiple_of(start_ref[step * N_EXPERTS + e], BF16_ROWS)

            def make_copy(off, size, e=e, start=start):
                src = blk_ref.at[step_slot, e, pl.ds(pl.multiple_of(off, BF16_ROWS), size)]
                dst = xs_ref.at[pl.ds(pl.multiple_of(start + off, BF16_ROWS), size)]
                cp = pltpu.make_async_copy(src, dst, sem_ref.at[step_slot])
                cp.wait() if wait else cp.start()

            _segment_copies(rows_ref[step * N_EXPERTS + e], make_copy)

    copies(t, slot, wait=False)

    @pl.when(t > 0)
    def _():
        copies(t - 1, 1 - slot, wait=True)

    @pl.when(t == pl.num_programs(0) - 1)
    def _():
        copies(t, slot, wait=True)


def _dispatch(seg_start, seg_rows, mask_t, h, n_rows):
    n, d = h.shape
    t = n // TT
    xs0 = jnp.zeros((n_rows, d), BF16)
    grid_spec = pltpu.PrefetchScalarGridSpec(
        num_scalar_prefetch=2,
        grid=(t,),
        in_specs=[pl.BlockSpec((N_EXPERTS, TT), lambda i, *_: (0, i)),
                  pl.BlockSpec((TT, d), lambda i, *_: (i, 0)),
                  pl.BlockSpec(memory_space=pl.ANY)],
        out_specs=pl.BlockSpec(memory_space=pl.ANY),
        scratch_shapes=[pltpu.VMEM((2, N_EXPERTS, TT, d), BF16), pltpu.SemaphoreType.DMA((2,))],
    )
    return pl.pallas_call(
        _dispatch_kernel,
        grid_spec=grid_spec,
        out_shape=jax.ShapeDtypeStruct((n_rows, d), BF16),
        input_output_aliases={4: 0},
        compiler_params=_cparams("arbitrary"),
        name="moe_dispatch",
    )(seg_start, seg_rows, mask_t, h, xs0)


def _expert_kernel(tile_expert_ref, tile_rows_ref, xs_ref, wg_ref, wu_ref, wd_ref, o_ref,
                   wg_bf_ref, wu_bf_ref, wd_bf_ref, acc_ref):
    del tile_expert_ref
    j = pl.program_id(0)
    f = pl.program_id(1)
    last = pl.num_programs(1) - 1
    rows = tile_rows_ref[j]

    @pl.when(rows > 0)
    def _():
        wg_bf_ref[...] = wg_ref[...].astype(BF16)
        wu_bf_ref[...] = wu_ref[...].astype(BF16)
        wd_bf_ref[...] = wd_ref[...].astype(BF16)

    for lo in range(0, TE, TM):
        @pl.when(rows > lo)
        def _(lo=lo):
            part = _swiglu_partial(xs_ref[lo:lo + TM, :], wg_bf_ref[...], wu_bf_ref[...], wd_bf_ref[...])

            @pl.when(f == 0)
            def _():
                acc_ref[lo:lo + TM, :] = part

            @pl.when(f > 0)
            def _():
                acc_ref[lo:lo + TM, :] += part

            @pl.when(f == last)
            def _():
                o_ref[lo:lo + TM, :] = acc_ref[lo:lo + TM, :].astype(o_ref.dtype)

        @pl.when((rows <= lo) & (f == last))
        def _(lo=lo):
            o_ref[lo:lo + TM, :] = jnp.zeros((TM, o_ref.shape[1]), o_ref.dtype)


def _expert_ffn(tile_expert, tile_rows, xs, w_gate_up, w_down, layer, tf):
    n_rows, d = xs.shape
    ff = w_down.shape[2]
    nf = ff // tf

    def f_eff(j, f, tr):
        return jnp.where(tr[j] > 0, f, nf - 1)

    grid_spec = pltpu.PrefetchScalarGridSpec(
        num_scalar_prefetch=2,
        grid=(n_rows // TE, nf),
        in_specs=[pl.BlockSpec((TE, d), lambda j, f, te, tr: (j, 0)),
                  pl.BlockSpec((None, None, d, tf), lambda j, f, te, tr: (layer, te[j], 0, f_eff(j, f, tr))),
                  pl.BlockSpec((None, None, d, tf), lambda j, f, te, tr: (layer, te[j], 0, nf + f_eff(j, f, tr))),
                  pl.BlockSpec((None, None, tf, d), lambda j, f, te, tr: (layer, te[j], f_eff(j, f, tr), 0))],
        out_specs=pl.BlockSpec((TE, d), lambda j, f, te, tr: (j, 0)),
        scratch_shapes=[pltpu.VMEM((d, tf), BF16), pltpu.VMEM((d, tf), BF16), pltpu.VMEM((tf, d), BF16),
                        pltpu.VMEM((TE, d), F32)],
    )
    return pl.pallas_call(
        _expert_kernel,
        grid_spec=grid_spec,
        out_shape=jax.ShapeDtypeStruct((n_rows, d), BF16),
        compiler_params=_cparams("parallel", "arbitrary"),
        name="moe_experts",
    )(tile_expert, tile_rows, xs, w_gate_up, w_gate_up, w_down)


def _combine_kernel(start_ref, rows_ref, x_ref, gate_ref, mask_ref, out_gain_ref, ys_ref, o_ref, win_ref, sem_ref,
                    *, out_norm):
    t = pl.program_id(0)
    n_t = pl.num_programs(0)
    slot = t % 2

    def copies(step, step_slot, wait):
        for e in range(N_EXPERTS):
            start = pl.multiple_of(start_ref[step * N_EXPERTS + e], BF16_ROWS)

            def make_copy(off, size, e=e, start=start):
                src = ys_ref.at[pl.ds(pl.multiple_of(start + off, BF16_ROWS), size)]
                dst = win_ref.at[step_slot, e, pl.ds(pl.multiple_of(off, BF16_ROWS), size)]
                cp = pltpu.make_async_copy(src, dst, sem_ref.at[step_slot])
                cp.wait() if wait else cp.start()

            _segment_copies(rows_ref[step * N_EXPERTS + e], make_copy)

    @pl.when(t == 0)
    def _():
        win_ref[...] = jnp.zeros_like(win_ref)
        copies(0, 0, wait=False)

    @pl.when(t + 1 < n_t)
    def _():
        copies(t + 1, 1 - slot, wait=False)

    copies(t, slot, wait=True)

    mask = mask_ref[...]
    gate = gate_ref[...]
    row = lax.broadcasted_iota(jnp.int32, (TT, TT), 0)
    col = lax.broadcasted_iota(jnp.int32, (TT, TT), 1)
    before = jnp.where(col < row, 1.0, 0.0).astype(BF16)
    rank = jnp.dot(before, mask.astype(BF16), preferred_element_type=F32)
    col_f = col.astype(F32)
    y = x_ref[...]
    for e in range(N_EXPERTS):
        pick = (col_f == rank[:, e:e + 1]) & (mask[:, e:e + 1] > 0.0)
        sel = jnp.where(pick, 1.0, 0.0).astype(BF16)
        y = y + gate[:, e:e + 1] * jnp.dot(sel, win_ref[slot, e], preferred_element_type=F32)
    o_ref[...] = _rmsnorm(y, out_gain_ref[...], RMS_EPS) if out_norm else y


def _combine(seg_start, seg_rows, x, gate, mask, ys, out_gain, out_norm):
    n, d = x.shape
    t = n // TT
    grid_spec = pltpu.PrefetchScalarGridSpec(
        num_scalar_prefetch=2,
        grid=(t,),
        in_specs=[pl.BlockSpec((TT, d), lambda i, *_: (i, 0)),
                  pl.BlockSpec((TT, LANES), lambda i, *_: (i, 0)),
                  pl.BlockSpec((TT, LANES), lambda i, *_: (i, 0)),
                  pl.BlockSpec((1, d), lambda i, *_: (0, 0)),
                  pl.BlockSpec(memory_space=pl.ANY)],
        out_specs=pl.BlockSpec((TT, d), lambda i, *_: (i, 0)),
        scratch_shapes=[pltpu.VMEM((2, N_EXPERTS, TT, d), BF16), pltpu.SemaphoreType.DMA((2,))],
    )
    return pl.pallas_call(
        functools.partial(_combine_kernel, out_norm=out_norm),
        grid_spec=grid_spec,
        out_shape=jax.ShapeDtypeStruct((n, d), F32),
        compiler_params=_cparams("arbitrary"),
        name="moe_combine",
    )(seg_start, seg_rows, x, gate, mask, out_gain.reshape(1, d), ys)


def _moe_ffn(x, g, router, w_gate_up, w_down, layer, tf, out_gain, out_norm):
    n, d = x.shape
    t = n // TT
    h, gate, mask, mask_t, cnt = _router(x, g, router)
    cnt = cnt[:, 0, :N_EXPERTS].astype(jnp.int32)
    seg_rows = _round_up(cnt, BF16_ROWS)
    expert_rows = jnp.sum(seg_rows, axis=0)
    region = _round_up(expert_rows, TE)
    region_end = jnp.cumsum(region)
    seg_start = (region_end - region)[None, :] + jnp.cumsum(seg_rows, axis=0) - seg_rows
    n_tiles = (_round_up(2 * n + t * N_EXPERTS * (BF16_ROWS - 1), TE) + N_EXPERTS * TE) // TE
    tile_end = region_end // TE
    tile_ids = jnp.arange(n_tiles, dtype=jnp.int32)
    last_tile = jnp.minimum(tile_ids, tile_end[-1] - 1)
    tile_expert = jnp.sum((last_tile[:, None] >= tile_end[None, :]).astype(jnp.int32), axis=1)
    tile_expert = jnp.minimum(tile_expert, N_EXPERTS - 1)
    rows_before = (tile_ids - (tile_end - region // TE)[tile_expert]) * TE
    tile_rows = jnp.where(tile_ids < tile_end[-1], jnp.clip(expert_rows[tile_expert] - rows_before, 0, TE), 0)
    seg_start = seg_start.reshape(-1).astype(jnp.int32)
    seg_rows = seg_rows.reshape(-1).astype(jnp.int32)

    xs = _dispatch(seg_start, seg_rows, mask_t, h, n_tiles * TE)
    ys = _expert_ffn(tile_expert, tile_rows.astype(jnp.int32), xs, w_gate_up, w_down, layer, tf)
    return _combine(seg_start, seg_rows, x, gate, mask, ys, out_gain, out_norm)


def kernel(x, attn_norm, ffn_norm, final_norm, t5_rel_bias, diff_w_qkv, diff_w_o, diff_lambda_q1, diff_lambda_k1, diff_lambda_q2, diff_lambda_k2, diff_subln, chunk_w_qkv, chunk_w_o, chunk_rel_bias, dense_w_gate_up, dense_w_down, moe_router, moe_w_gate_up, moe_w_down):
    b, s, d = x.shape
    n = b * s
    xf = x.reshape(n, d)
    t5_bias = _t5_bias_t(t5_rel_bias, s)
    for i in range(DEPTH):
        j = i // 2
        if i % 2 == 0:
            qkv = _norm_matmul(xf, attn_norm[i], diff_w_qkv[j].astype(BF16))
            lam_init = 0.8 - 0.6 * math.exp(-0.3 * i)
            lam_vecs = jnp.stack([diff_lambda_q1[j], diff_lambda_k1[j], diff_lambda_q2[j], diff_lambda_k2[j]])
            a = _diff_attention(qkv.reshape(b, s, 3 * d), t5_bias, lam_vecs, diff_subln[j], lam_init)
            xf = _matmul_residual(a.reshape(n, d), diff_w_o[j].astype(BF16), xf)
            xf = _dense_ffn(xf, ffn_norm[i], dense_w_gate_up[j].astype(BF16), dense_w_down[j].astype(BF16),
                            tf=dense_w_down.shape[1] // 2)
        else:
            qkv = _norm_matmul(xf, attn_norm[i], chunk_w_qkv[j].astype(BF16))
            a = _chunk_attention(qkv.reshape(b, s, 3 * d), _band_bias_t(chunk_rel_bias[j]))
            xf = _matmul_residual(a.reshape(n, d), chunk_w_o[j].astype(BF16), xf)
            xf = _moe_ffn(xf, ffn_norm[i], moe_router[j], moe_w_gate_up, moe_w_down, layer=j,
                          tf=moe_w_down.shape[2] // 4,
                          out_gain=final_norm, out_norm=i == DEPTH - 1)
    return xf.reshape(b, s, d)
```

```python
import functools
import math

import numpy as np
import jax
import jax.numpy as jnp
from jax import lax
from jax.experimental import pallas as pl
from jax.experimental.pallas import tpu as pltpu

F32 = jnp.float32
BF16 = jnp.bfloat16

D_MODEL = 1024
DEPTH = 4
CHUNK = 64
DA_HEADS = 8
DA_HEAD_DIM = 64
T5_BUCKETS = 32
T5_MAX_DIST = 128
CA_HEADS = 16
CA_HEAD_DIM = 64
BAND_PREV = 8
REL_CLIP = 4 * CHUNK
N_EXPERTS = 8
RMS_EPS = 1e-6
SUBLN_EPS = 1e-5
NEG_INF = -1e30

LANES = 128
BF16_ROWS = 16
VMEM_LIMIT = 56 * 1024 * 1024
TM = 512
TE = 4 * TM
TQ = 256
TT = 256
KB = 256
N_SLOTS = 4
ONES_ROWS = BF16_ROWS
LOG2E = math.log2(math.e)
Q_SCALE = DA_HEAD_DIM ** -0.5 * LOG2E
PAIR = 2 * CA_HEAD_DIM


def _round_up(a, m):
    return (a + m - 1) // m * m


def _cparams(*sem):
    return pltpu.CompilerParams(dimension_semantics=sem, vmem_limit_bytes=VMEM_LIMIT)


def _rmsnorm(x32, g, eps):
    return x32 * lax.rsqrt(jnp.mean(x32 * x32, axis=-1, keepdims=True) + eps) * g


def _dot_nt(a, b):
    return lax.dot_general(a, b, (((1,), (1,)), ((), ())), preferred_element_type=F32)


def _norm_matmul_kernel(x_ref, g_ref, w_ref, o_ref):
    h = _rmsnorm(x_ref[...], g_ref[...], RMS_EPS).astype(BF16)
    d = x_ref.shape[1]
    q = jnp.dot(h, w_ref[:, :d], preferred_element_type=F32) * Q_SCALE
    o_ref[:, :d] = q.astype(o_ref.dtype)
    o_ref[:, d:] = jnp.dot(h, w_ref[:, d:], preferred_element_type=F32).astype(o_ref.dtype)


def _norm_matmul(x, g, w):
    n, d = x.shape
    n_out = w.shape[1]
    return pl.pallas_call(
        _norm_matmul_kernel,
        grid=(n // TM,),
        in_specs=[pl.BlockSpec((TM, d), lambda i: (i, 0)),
                  pl.BlockSpec((1, d), lambda i: (0, 0)),
                  pl.BlockSpec((d, n_out), lambda i: (0, 0))],
        out_specs=pl.BlockSpec((TM, n_out), lambda i: (i, 0)),
        out_shape=jax.ShapeDtypeStruct((n, n_out), BF16),
        compiler_params=_cparams("parallel"),
        name="norm_qkv",
    )(x, g.reshape(1, d), w)


def _matmul_residual_kernel(a_ref, w_ref, x_ref, o_ref):
    o_ref[...] = x_ref[...] + jnp.dot(a_ref[...], w_ref[...], preferred_element_type=F32)


def _matmul_residual(a, w, x):
    n, d = x.shape
    return pl.pallas_call(
        _matmul_residual_kernel,
        grid=(n // TM,),
        in_specs=[pl.BlockSpec((TM, a.shape[1]), lambda i: (i, 0)),
                  pl.BlockSpec(w.shape, lambda i: (0, 0)),
                  pl.BlockSpec((TM, d), lambda i: (i, 0))],
        out_specs=pl.BlockSpec((TM, d), lambda i: (i, 0)),
        out_shape=jax.ShapeDtypeStruct((n, d), F32),
        compiler_params=_cparams("parallel"),
        name="wo_residual",
    )(a, w, x)


def _t5_bucket_np(rel):
    nb = T5_BUCKETS // 2
    max_exact = nb // 2
    ret = (rel > 0).astype(np.int32) * nb
    n = np.abs(rel)
    n_f = np.maximum(n, 1).astype(np.float32)
    large = max_exact + (np.log(n_f / np.float32(max_exact)) / np.float32(math.log(T5_MAX_DIST / max_exact))
                         * np.float32(nb - max_exact)).astype(np.int32)
    large = np.minimum(large, nb - 1)
    return ret + np.where(n < max_exact, n, large)


def _lookup_kernel(idx_ref, t_ref, o_ref):
    rows = lax.broadcasted_iota(jnp.int32, (t_ref.shape[1], idx_ref.shape[1]), 0)
    onehot = jnp.where(rows == idx_ref[...], 1.0, 0.0)
    o_ref[...] = jnp.dot(t_ref[...], onehot, preferred_element_type=F32, precision=lax.Precision.HIGHEST)


def _table_lookup_t(table, idx):
    r, h = table.shape
    table_t = jnp.zeros((h, _round_up(r, LANES)), F32).at[:, :r].set(table.T)
    return pl.pallas_call(
        _lookup_kernel,
        out_shape=jax.ShapeDtypeStruct((h, len(idx)), F32),
        name="bias_lookup",
    )(jnp.asarray(idx, jnp.int32).reshape(1, -1), table_t)


def _toeplitz_kernel(win_ref, mask_ref, o_ref):
    for blk in range(o_ref.shape[0] // TQ):
        x = jnp.broadcast_to(win_ref[:, blk * 2 * TQ:(blk + 1) * 2 * TQ], (TQ, 2 * TQ))
        skew = pltpu.roll(x, 0, 1, stride=1, stride_axis=0)
        o_ref[blk * TQ:(blk + 1) * TQ, :] = skew[:, :TQ] + mask_ref[blk * TQ:(blk + 1) * TQ, :]


def _window_diagonals(n_keys):
    m = np.arange(2 * TQ)
    back = np.where(m < TQ, -m, 2 * TQ - m)
    d = np.arange(0, n_keys, TQ)[:, None] + back[None, :]
    return np.clip(d, 1 - TQ, n_keys - 1).reshape(-1)


def _toeplitz_bias_t(table, idx, allowed):
    win = _table_lookup_t(table, idx) * LOG2E
    h = win.shape[0]
    n_keys = allowed.shape[0]
    mask_add = jnp.asarray(np.where(allowed, 0.0, NEG_INF).astype(np.float32))
    return pl.pallas_call(
        _toeplitz_kernel,
        grid=(h,),
        in_specs=[pl.BlockSpec((None, 1, win.shape[1]), lambda i: (i, 0, 0)),
                  pl.BlockSpec((n_keys, TQ), lambda i: (0, 0))],
        out_specs=pl.BlockSpec((None, n_keys, TQ), lambda i: (i, 0, 0)),
        out_shape=jax.ShapeDtypeStruct((h, n_keys, TQ), F32),
        compiler_params=_cparams("parallel"),
        name="toeplitz_bias",
    )(win.reshape(h, 1, -1), mask_add)


def _t5_bias_t(table, seq):
    idx = _t5_bucket_np(_window_diagonals(seq) - (seq - TQ))
    i = np.arange(TQ)[None, :]
    j = np.arange(seq)[:, None] - (seq - TQ)
    allowed = (j < 0) | ((j // CHUNK) <= (i // CHUNK))
    return _toeplitz_bias_t(table, idx, allowed)


def _load_vt(v_ref, vt_ref):
    n_v = v_ref.shape[1]
    vt_ref[0:n_v, :] = v_ref[...].T
    vt_ref[n_v:, :] = jnp.ones((ONES_ROWS, v_ref.shape[0]), BF16)


def _attend_t(k_ref, k0, n_k, qm, bias_rows, vt_ref, s_ref, p_ref):
    m = None
    for kb in range(0, n_k, KB):
        s = _dot_nt(k_ref[k0 + kb:k0 + kb + KB, :], qm) + bias_rows(kb)
        s_ref[kb:kb + KB, :] = s
        block_max = jnp.max(s, axis=0, keepdims=True)
        m = block_max if m is None else jnp.maximum(m, block_max)
    for kb in range(0, n_k, KB):
        p_ref[kb:kb + KB, :] = jnp.exp2(s_ref[kb:kb + KB, :] - m).astype(BF16)
    o_t = jnp.dot(vt_ref[:, k0:k0 + n_k], p_ref[0:n_k, :], preferred_element_type=F32)
    n_v = vt_ref.shape[0] - ONES_ROWS
    return o_t[0:n_v] * (1.0 / o_t[n_v:n_v + 1])


def _diff_attn_kernel(q_ref, k_ref, v_ref, bias_ref, lam_ref, g_ref, o_ref, vt_ref, s_ref, p_ref, *, lam_init):
    lv = lam_ref[...]
    lam = (jnp.exp(jnp.sum(lv[0:1] * lv[1:2], axis=-1, keepdims=True))
           - jnp.exp(jnp.sum(lv[2:3] * lv[3:4], axis=-1, keepdims=True)) + lam_init)
    first_map = lax.broadcasted_iota(jnp.int32, (1, 2 * DA_HEAD_DIM), 1) < DA_HEAD_DIM
    seq = q_ref.shape[0]
    _load_vt(v_ref, vt_ref)
    for qi in range(seq // TQ):
        q = q_ref[qi * TQ:(qi + 1) * TQ, :]
        zero = jnp.zeros_like(q)
        n_k = (qi + 1) * TQ

        def bias_rows(kb, n_k=n_k):
            return bias_ref[seq - n_k + kb:seq - n_k + kb + KB, :]

        slot = (2 * qi) % N_SLOTS
        o1 = _attend_t(k_ref, 0, n_k, jnp.where(first_map, q, zero), bias_rows, vt_ref,
                       s_ref.at[slot], p_ref.at[slot])
        o2 = _attend_t(k_ref, 0, n_k, jnp.where(first_map, zero, q), bias_rows, vt_ref,
                       s_ref.at[slot + 1], p_ref.at[slot + 1])
        o_t = o1 - lam * o2
        o_t = o_t * lax.rsqrt(jnp.mean(o_t * o_t, axis=0, keepdims=True) + SUBLN_EPS) * g_ref[...]
        o_ref[qi * TQ:(qi + 1) * TQ, :] = (o_t * (1.0 - lam_init)).T.astype(o_ref.dtype)


def _diff_attention(qkv, bias_t, lam_vecs, subln_g, lam_init):
    b, s, _ = qkv.shape
    hd = 2 * DA_HEAD_DIM
    return pl.pallas_call(
        functools.partial(_diff_attn_kernel, lam_init=lam_init),
        grid=(b, DA_HEADS),
        in_specs=[pl.BlockSpec((None, s, hd), lambda i, h: (i, 0, h)),
                  pl.BlockSpec((None, s, hd), lambda i, h: (i, 0, DA_HEADS + h)),
                  pl.BlockSpec((None, s, hd), lambda i, h: (i, 0, 2 * DA_HEADS + h)),
                  pl.BlockSpec((None, s, TQ), lambda i, h: (h, 0, 0)),
                  pl.BlockSpec((4, DA_HEAD_DIM), lambda i, h: (0, 0)),
                  pl.BlockSpec((hd, 1), lambda i, h: (0, 0))],
        out_specs=pl.BlockSpec((None, s, hd), lambda i, h: (i, 0, h)),
        out_shape=jax.ShapeDtypeStruct((b, s, D_MODEL), BF16),
        scratch_shapes=[pltpu.VMEM((hd + ONES_ROWS, s), BF16), pltpu.VMEM((N_SLOTS, s, TQ), F32),
                        pltpu.VMEM((N_SLOTS, s, TQ), BF16)],
        compiler_params=_cparams("parallel", "parallel"),
        name="diff_attn",
    )(qkv, qkv, qkv, bias_t, lam_vecs, subln_g.reshape(hd, 1))


BAND_KEYS = BAND_PREV * CHUNK + TQ


def _band_bias_t(rel_table):
    dist = BAND_PREV * CHUNK - _window_diagonals(BAND_KEYS)
    q_chunk = np.arange(TQ)[None, :] // CHUNK
    k_chunk = np.arange(BAND_KEYS)[:, None] // CHUNK
    allowed = (k_chunk >= q_chunk) & (k_chunk <= q_chunk + BAND_PREV)
    return _toeplitz_bias_t(rel_table, np.clip(dist, -REL_CLIP, REL_CLIP) + REL_CLIP, allowed)


def _chunk_attn_kernel(q_ref, k_ref, v_ref, bias_ref, o_ref, vt_ref, s_ref, p_ref):
    first_head = lax.broadcasted_iota(jnp.int32, (1, PAIR), 1) < CA_HEAD_DIM
    first_head_rows = lax.broadcasted_iota(jnp.int32, (PAIR, 1), 0) < CA_HEAD_DIM
    seq = q_ref.shape[0]
    _load_vt(v_ref, vt_ref)
    for qi in range(seq // TQ):
        q0 = qi * TQ
        k0 = max(0, q0 - BAND_PREV * CHUNK)
        n_k = q0 + TQ - k0
        q = q_ref[q0:q0 + TQ, :]
        zero = jnp.zeros_like(q)
        outs = []
        for hh in range(2):
            def bias_rows(kb, hh=hh, n_k=n_k):
                return bias_ref[hh, BAND_KEYS - n_k + kb:BAND_KEYS - n_k + kb + KB, :]

            qm = jnp.where(first_head, q, zero) if hh == 0 else jnp.where(first_head, zero, q)
            slot = (2 * qi + hh) % N_SLOTS
            outs.append(_attend_t(k_ref, k0, n_k, qm, bias_rows, vt_ref, s_ref.at[slot], p_ref.at[slot]))
        o_ref[q0:q0 + TQ, :] = jnp.where(first_head_rows, outs[0], outs[1]).T.astype(o_ref.dtype)


def _chunk_attention(qkv, band_bias_t):
    b, s, _ = qkv.shape
    n_pairs = CA_HEADS // 2
    return pl.pallas_call(
        _chunk_attn_kernel,
        grid=(b, n_pairs),
        in_specs=[pl.BlockSpec((None, s, PAIR), lambda i, p: (i, 0, p)),
                  pl.BlockSpec((None, s, PAIR), lambda i, p: (i, 0, n_pairs + p)),
                  pl.BlockSpec((None, s, PAIR), lambda i, p: (i, 0, 2 * n_pairs + p)),
                  pl.BlockSpec((2, BAND_KEYS, TQ), lambda i, p: (p, 0, 0))],
        out_specs=pl.BlockSpec((None, s, PAIR), lambda i, p: (i, 0, p)),
        out_shape=jax.ShapeDtypeStruct((b, s, D_MODEL), BF16),
        scratch_shapes=[pltpu.VMEM((PAIR + ONES_ROWS, s), BF16), pltpu.VMEM((N_SLOTS, BAND_KEYS, TQ), F32),
                        pltpu.VMEM((N_SLOTS, BAND_KEYS, TQ), BF16)],
        compiler_params=_cparams("parallel", "parallel"),
        name="chunk_attn",
    )(qkv, qkv, qkv, band_bias_t)


def _swiglu_partial(h, wg, wu, wd):
    g = jnp.dot(h, wg, preferred_element_type=F32)
    u = jnp.dot(h, wu, preferred_element_type=F32)
    a = (g * (1.0 / (1.0 + jnp.exp(-g))) * u).astype(BF16)
    return jnp.dot(a, wd, preferred_element_type=F32)


def _dense_ffn_kernel(x_ref, g_ref, wg_ref, wu_ref, wd_ref, o_ref, h_ref, acc_ref):
    f = pl.program_id(1)

    @pl.when(f == 0)
    def _():
        h_ref[...] = _rmsnorm(x_ref[...], g_ref[...], RMS_EPS).astype(BF16)
        acc_ref[...] = jnp.zeros_like(acc_ref)

    acc_ref[...] += _swiglu_partial(h_ref[...], wg_ref[...], wu_ref[...], wd_ref[...])

    @pl.when(f == pl.num_programs(1) - 1)
    def _():
        o_ref[...] = x_ref[...] + acc_ref[...]


def _dense_ffn(x, g, w_gate_up, w_down, tf):
    n, d = x.shape
    ff = w_down.shape[0]
    nf = ff // tf
    return pl.pallas_call(
        _dense_ffn_kernel,
        grid=(n // TM, nf),
        in_specs=[pl.BlockSpec((TM, d), lambda i, f: (i, 0)),
                  pl.BlockSpec((1, d), lambda i, f: (0, 0)),
                  pl.BlockSpec((d, tf), lambda i, f: (0, f)),
                  pl.BlockSpec((d, tf), lambda i, f: (0, nf + f)),
                  pl.BlockSpec((tf, d), lambda i, f: (f, 0))],
        out_specs=pl.BlockSpec((TM, d), lambda i, f: (i, 0)),
        out_shape=jax.ShapeDtypeStruct((n, d), F32),
        scratch_shapes=[pltpu.VMEM((TM, d), BF16), pltpu.VMEM((TM, d), F32)],
        compiler_params=_cparams("parallel", "arbitrary"),
        name="dense_ffn",
    )(x, g.reshape(1, d), w_gate_up, w_gate_up, w_down)


def _router_kernel(x_ref, g_ref, r_ref, h_ref, gate_ref, mask_ref, mask_t_ref, cnt_ref):
    h = _rmsnorm(x_ref[...], g_ref[...], RMS_EPS)
    h_ref[...] = h.astype(BF16)
    logits = jnp.dot(h, r_ref[...], preferred_element_type=F32, precision=lax.Precision.HIGHEST)
    lane = lax.broadcasted_iota(jnp.int32, logits.shape, 1)
    logits = jnp.where(lane < N_EXPERTS, logits, -jnp.inf)
    v1 = jnp.max(logits, axis=-1, keepdims=True)
    i1 = jnp.min(jnp.where(logits == v1, lane, LANES), axis=-1, keepdims=True)
    m1 = lane == i1
    rest = jnp.where(m1, -jnp.inf, logits)
    v2 = jnp.max(rest, axis=-1, keepdims=True)
    i2 = jnp.min(jnp.where(rest == v2, lane, LANES), axis=-1, keepdims=True)
    m2 = lane == i2
    e2 = jnp.exp(v2 - v1)
    g1 = 1.0 / (1.0 + e2)
    g2 = e2 / (1.0 + e2)
    gate_ref[...] = jnp.where(m1, g1, 0.0) + jnp.where(m2, g2, 0.0)
    mask = jnp.where(m1 | m2, 1.0, 0.0)
    mask_ref[...] = mask
    mask_t_ref[...] = mask.T[0:N_EXPERTS, :]
    cnt_ref[...] = jnp.broadcast_to(jnp.sum(mask, axis=0, keepdims=True), cnt_ref.shape)


def _router(x, g, router):
    n, d = x.shape
    t = n // TT
    r_pad = jnp.zeros((d, LANES), F32).at[:, :N_EXPERTS].set(router)
    return pl.pallas_call(
        _router_kernel,
        grid=(t,),
        in_specs=[pl.BlockSpec((TT, d), lambda i: (i, 0)),
                  pl.BlockSpec((1, d), lambda i: (0, 0)),
                  pl.BlockSpec((d, LANES), lambda i: (0, 0))],
        out_specs=[pl.BlockSpec((TT, d), lambda i: (i, 0)),
                   pl.BlockSpec((TT, LANES), lambda i: (i, 0)),
                   pl.BlockSpec((TT, LANES), lambda i: (i, 0)),
                   pl.BlockSpec((N_EXPERTS, TT), lambda i: (0, i)),
                   pl.BlockSpec((None, 8, LANES), lambda i: (i, 0, 0))],
        out_shape=[jax.ShapeDtypeStruct((n, d), BF16),
                   jax.ShapeDtypeStruct((n, LANES), F32),
                   jax.ShapeDtypeStruct((n, LANES), F32),
                   jax.ShapeDtypeStruct((N_EXPERTS, n), F32),
                   jax.ShapeDtypeStruct((t, 8, LANES), F32)],
        compiler_params=_cparams("parallel"),
        name="router",
    )(x, g.reshape(1, d), r_pad)


def _segment_copies(n_rows, make_copy):
    off = jnp.int32(0)
    size = TT
    while size >= BF16_ROWS:
        take = (n_rows & size) != 0

        @pl.when(take)
        def _(off=off, size=size):
            make_copy(off, size)

        off = off + jnp.where(take, size, 0)
        size //= 2


def _dispatch_kernel(start_ref, rows_ref, mask_t_ref, h_ref, xs_in_ref, xs_ref, blk_ref, sem_ref):
    del xs_in_ref
    t = pl.program_id(0)
    slot = t % 2
    mask_t = mask_t_ref[...]
    row = lax.broadcasted_iota(jnp.int32, (TT, TT), 0)
    col = lax.broadcasted_iota(jnp.int32, (TT, TT), 1)
    before = jnp.where(row < col, 1.0, 0.0).astype(BF16)
    rank = jnp.dot(mask_t.astype(BF16), before, preferred_element_type=F32)
    row_f = row.astype(F32)
    h = h_ref[...]
    for e in range(N_EXPERTS):
        pick = (row_f == rank[e:e + 1, :]) & (mask_t[e:e + 1, :] > 0.0)
        sel = jnp.where(pick, 1.0, 0.0).astype(BF16)
        blk_ref[slot, e] = jnp.dot(sel, h, preferred_element_type=F32).astype(BF16)

    def copies(step, step_slot, wait):
        for e in range(N_EXPERTS):
            start = pl.multiple_of(start_ref[step * N_EXPERTS + e], BF16_ROWS)

            def make_copy(off, size, e=e, start=start):
                src = blk_ref.at[step_slot, e, pl.ds(pl.multiple_of(off, BF16_ROWS), size)]
                dst = xs_ref.at[pl.ds(pl.multiple_of(start + off, BF16_ROWS), size)]
                cp = pltpu.make_async_copy(src, dst, sem_ref.at[step_slot])
                cp.wait() if wait else cp.start()

            _segment_copies(rows_ref[step * N_EXPERTS + e], make_copy)

    copies(t, slot, wait=False)

    @pl.when(t > 0)
    def _():
        copies(t - 1, 1 - slot, wait=True)

    @pl.when(t == pl.num_programs(0) - 1)
    def _():
        copies(t, slot, wait=True)


def _dispatch(seg_start, seg_rows, mask_t, h, n_rows):
    n, d = h.shape
    t = n // TT
    xs0 = jnp.zeros((n_rows, d), BF16)
    grid_spec = pltpu.PrefetchScalarGridSpec(
        num_scalar_prefetch=2,
        grid=(t,),
        in_specs=[pl.BlockSpec((N_EXPERTS, TT), lambda i, *_: (0, i)),
                  pl.BlockSpec((TT, d), lambda i, *_: (i, 0)),
                  pl.BlockSpec(memory_space=pl.ANY)],
        out_specs=pl.BlockSpec(memory_space=pl.ANY),
        scratch_shapes=[pltpu.VMEM((2, N_EXPERTS, TT, d), BF16), pltpu.SemaphoreType.DMA((2,))],
    )
    return pl.pallas_call(
        _dispatch_kernel,
        grid_spec=grid_spec,
        out_shape=jax.ShapeDtypeStruct((n_rows, d), BF16),
        input_output_aliases={4: 0},
        compiler_params=_cparams("arbitrary"),
        name="moe_dispatch",
    )(seg_start, seg_rows, mask_t, h, xs0)


def _expert_kernel(tile_expert_ref, tile_rows_ref, xs_ref, wg_ref, wu_ref, wd_ref, o_ref,
                   wg_bf_ref, wu_bf_ref, wd_bf_ref, acc_ref):
    del tile_expert_ref
    j = pl.program_id(0)
    f = pl.program_id(1)
    last = pl.num_programs(1) - 1
    rows = tile_rows_ref[j]

    @pl.when(rows > 0)
    def _():
        wg_bf_ref[...] = wg_ref[...].astype(BF16)
        wu_bf_ref[...] = wu_ref[...].astype(BF16)
        wd_bf_ref[...] = wd_ref[...].astype(BF16)

    for lo in range(0, TE, TM):
        @pl.when(rows > lo)
        def _(lo=lo):
            part = _swiglu_partial(xs_ref[lo:lo + TM, :], wg_bf_ref[...], wu_bf_ref[...], wd_bf_ref[...])

            @pl.when(f == 0)
            def _():
                acc_ref[lo:lo + TM, :] = part

            @pl.when(f > 0)
            def _():
                acc_ref[lo:lo + TM, :] += part

            @pl.when(f == last)
            def _():
                o_ref[lo:lo + TM, :] = acc_ref[lo:lo + TM, :].astype(o_ref.dtype)

        @pl.when((rows <= lo) & (f == last))
        def _(lo=lo):
            o_ref[lo:lo + TM, :] = jnp.zeros((TM, o_ref.shape[1]), o_ref.dtype)


def _expert_ffn(tile_expert, tile_rows, xs, w_gate_up, w_down, layer, tf):
    n_rows, d = xs.shape
    ff = w_down.shape[2]
    nf = ff // tf

    def f_eff(j, f, tr):
        return jnp.where(tr[j] > 0, f, nf - 1)

    grid_spec = pltpu.PrefetchScalarGridSpec(
        num_scalar_prefetch=2,
        grid=(n_rows // TE, nf),
        in_specs=[pl.BlockSpec((TE, d), lambda j, f, te, tr: (j, 0)),
                  pl.BlockSpec((None, None, d, tf), lambda j, f, te, tr: (layer, te[j], 0, f_eff(j, f, tr))),
                  pl.BlockSpec((None, None, d, tf), lambda j, f, te, tr: (layer, te[j], 0, nf + f_eff(j, f, tr))),
                  pl.BlockSpec((None, None, tf, d), lambda j, f, te, tr: (layer, te[j], f_eff(j, f, tr), 0))],
        out_specs=pl.BlockSpec((TE, d), lambda j, f, te, tr: (j, 0)),
        scratch_shapes=[pltpu.VMEM((d, tf), BF16), pltpu.VMEM((d, tf), BF16), pltpu.VMEM((tf, d), BF16),
                        pltpu.VMEM((TE, d), F32)],
    )
    return pl.pallas_call(
        _expert_kernel,
        grid_spec=grid_spec,
        out_shape=jax.ShapeDtypeStruct((n_rows, d), BF16),
        compiler_params=_cparams("parallel", "arbitrary"),
        name="moe_experts",
    )(tile_expert, tile_rows, xs, w_gate_up, w_gate_up, w_down)


def _combine_kernel(start_ref, rows_ref, x_ref, gate_ref, mask_ref, out_gain_ref, ys_ref, o_ref, win_ref, sem_ref,
                    *, out_norm):
    t = pl.program_id(0)
    n_t = pl.num_programs(0)
    slot = t % 2

    def copies(step, step_slot, wait):
        for e in range(N_EXPERTS):
            start = pl.multiple_of(start_ref[step * N_EXPERTS + e], BF16_ROWS)

            def make_copy(off, size, e=e, start=start):
                src = ys_ref.at[pl.ds(pl.multiple_of(start + off, BF16_ROWS), size)]
                dst = win_ref.at[step_slot, e, pl.ds(pl.multiple_of(off, BF16_ROWS), size)]
                cp = pltpu.make_async_copy(src, dst, sem_ref.at[step_slot])
                cp.wait() if wait else cp.start()

            _segment_copies(rows_ref[step * N_EXPERTS + e], make_copy)

    @pl.when(t == 0)
    def _():
        win_ref[...] = jnp.zeros_like(win_ref)
        copies(0, 0, wait=False)

    @pl.when(t + 1 < n_t)
    def _():
        copies(t + 1, 1 - slot, wait=False)

    copies(t, slot, wait=True)

    mask = mask_ref[...]
    gate = gate_ref[...]
    row = lax.broadcasted_iota(jnp.int32, (TT, TT), 0)
    col = lax.broadcasted_iota(jnp.int32, (TT, TT), 1)
    before = jnp.where(col < row, 1.0, 0.0).astype(BF16)
    rank = jnp.dot(before, mask.astype(BF16), preferred_element_type=F32)
    col_f = col.astype(F32)
    y = x_ref[...]
    for e in range(N_EXPERTS):
        pick = (col_f == rank[:, e:e + 1]) & (mask[:, e:e + 1] > 0.0)
        sel = jnp.where(pick, 1.0, 0.0).astype(BF16)
        y = y + gate[:, e:e + 1] * jnp.dot(sel, win_ref[slot, e], preferred_element_type=F32)
    o_ref[...] = _rmsnorm(y, out_gain_ref[...], RMS_EPS) if out_norm else y


def _combine(seg_start, seg_rows, x, gate, mask, ys, out_gain, out_norm):
    n, d = x.shape
    t = n // TT
    grid_spec = pltpu.PrefetchScalarGridSpec(
        num_scalar_prefetch=2,
        grid=(t,),
        in_specs=[pl.BlockSpec((TT, d), lambda i, *_: (i, 0)),
                  pl.BlockSpec((TT, LANES), lambda i, *_: (i, 0)),
                  pl.BlockSpec((TT, LANES), lambda i, *_: (i, 0)),
                  pl.BlockSpec((1, d), lambda i, *_: (0, 0)),
                  pl.BlockSpec(memory_space=pl.ANY)],
        out_specs=pl.BlockSpec((TT, d), lambda i, *_: (i, 0)),
        scratch_shapes=[pltpu.VMEM((2, N_EXPERTS, TT, d), BF16), pltpu.SemaphoreType.DMA((2,))],
    )
    return pl.pallas_call(
        functools.partial(_combine_kernel, out_norm=out_norm),
        grid_spec=grid_spec,
        out_shape=jax.ShapeDtypeStruct((n, d), F32),
        compiler_params=_cparams("arbitrary"),
        name="moe_combine",
    )(seg_start, seg_rows, x, gate, mask, out_gain.reshape(1, d), ys)


def _moe_ffn(x, g, router, w_gate_up, w_down, layer, tf, out_gain, out_norm):
    n, d = x.shape
    t = n // TT
    h, gate, mask, mask_t, cnt = _router(x, g, router)
    cnt = cnt[:, 0, :N_EXPERTS].astype(jnp.int32)
    seg_rows = _round_up(cnt, BF16_ROWS)
    expert_rows = jnp.sum(seg_rows, axis=0)
    region = _round_up(expert_rows, TE)
    region_end = jnp.cumsum(region)
    seg_start = (region_end - region)[None, :] + jnp.cumsum(seg_rows, axis=0) - seg_rows
    n_tiles = (_round_up(2 * n + t * N_EXPERTS * (BF16_ROWS - 1), TE) + N_EXPERTS * TE) // TE
    tile_end = region_end // TE
    tile_ids = jnp.arange(n_tiles, dtype=jnp.int32)
    last_tile = jnp.minimum(tile_ids, tile_end[-1] - 1)
    tile_expert = jnp.sum((last_tile[:, None] >= tile_end[None, :]).astype(jnp.int32), axis=1)
    tile_expert = jnp.minimum(tile_expert, N_EXPERTS - 1)
    rows_before = (tile_ids - (tile_end - region // TE)[tile_expert]) * TE
    tile_rows = jnp.where(tile_ids < tile_end[-1], jnp.clip(expert_rows[tile_expert] - rows_before, 0, TE), 0)
    seg_start = seg_start.reshape(-1).astype(jnp.int32)
    seg_rows = seg_rows.reshape(-1).astype(jnp.int32)

    xs = _dispatch(seg_start, seg_rows, mask_t, h, n_tiles * TE)
    ys = _expert_ffn(tile_expert, tile_rows.astype(jnp.int32), xs, w_gate_up, w_down, layer, tf)
    return _combine(seg_start, seg_rows, x, gate, mask, ys, out_gain, out_norm)


def kernel(x, attn_norm, ffn_norm, final_norm, t5_rel_bias, diff_w_qkv, diff_w_o, diff_lambda_q1, diff_lambda_k1, diff_lambda_q2, diff_lambda_k2, diff_subln, chunk_w_qkv, chunk_w_o, chunk_rel_bias, dense_w_gate_up, dense_w_down, moe_router, moe_w_gate_up, moe_w_down):
    b, s, d = x.shape
    n = b * s
    xf = x.reshape(n, d)
    t5_bias = _t5_bias_t(t5_rel_bias, s)
    for i in range(DEPTH):
        j = i // 2
        if i % 2 == 0:
            qkv = _norm_matmul(xf, attn_norm[i], diff_w_qkv[j].astype(BF16))
            lam_init = 0.8 - 0.6 * math.exp(-0.3 * i)
            lam_vecs = jnp.stack([diff_lambda_q1[j], diff_lambda_k1[j], diff_lambda_q2[j], diff_lambda_k2[j]])
            a = _diff_attention(qkv.reshape(b, s, 3 * d), t5_bias, lam_vecs, diff_subln[j], lam_init)
            xf = _matmul_residual(a.reshape(n, d), diff_w_o[j].astype(BF16), xf)
            xf = _dense_ffn(xf, ffn_norm[i], dense_w_gate_up[j].astype(BF16), dense_w_down[j].astype(BF16),
                            tf=dense_w_down.shape[1] // 2)
        else:
            qkv = _norm_matmul(xf, attn_norm[i], chunk_w_qkv[j].astype(BF16))
            a = _chunk_attention(qkv.reshape(b, s, 3 * d), _band_bias_t(chunk_rel_bias[j]))
            xf = _matmul_residual(a.reshape(n, d), chunk_w_o[j].astype(BF16), xf)
            xf = _moe_ffn(xf, ffn_norm[i], moe_router[j], moe_w_gate_up, moe_w_down, layer=j,
                          tf=moe_w_down.shape[2] // 7,
                          out_gain=final_norm, out_norm=i == DEPTH - 1)
    return xf.reshape(b, s, d)
```

```python
import functools
import math

import numpy as np
import jax
import jax.numpy as jnp
from jax import lax
from jax.experimental import pallas as pl
from jax.experimental.pallas import tpu as pltpu

F32 = jnp.float32
BF16 = jnp.bfloat16

D_MODEL = 1024
DEPTH = 4
CHUNK = 64
DA_HEADS = 8
DA_HEAD_DIM = 64
T5_BUCKETS = 32
T5_MAX_DIST = 128
CA_HEADS = 16
CA_HEAD_DIM = 64
BAND_PREV = 8
REL_CLIP = 4 * CHUNK
N_EXPERTS = 8
RMS_EPS = 1e-6
SUBLN_EPS = 1e-5
NEG_INF = -1e30

LANES = 128
BF16_ROWS = 16
VMEM_LIMIT = 56 * 1024 * 1024
TM = 512
TE = 4 * TM
TQ = 256
TT = 256
KB = 256
N_SLOTS = 4
ONES_ROWS = BF16_ROWS
LOG2E = math.log2(math.e)
Q_SCALE = DA_HEAD_DIM ** -0.5 * LOG2E
PAIR = 2 * CA_HEAD_DIM


def _round_up(a, m):
    return (a + m - 1) // m * m


def _cparams(*sem):
    return pltpu.CompilerParams(dimension_semantics=sem, vmem_limit_bytes=VMEM_LIMIT)


def _rmsnorm(x32, g, eps):
    return x32 * lax.rsqrt(jnp.mean(x32 * x32, axis=-1, keepdims=True) + eps) * g


def _dot_nt(a, b):
    return lax.dot_general(a, b, (((1,), (1,)), ((), ())), preferred_element_type=F32)


def _norm_matmul_kernel(x_ref, g_ref, w_ref, o_ref):
    h = _rmsnorm(x_ref[...], g_ref[...], RMS_EPS).astype(BF16)
    d = x_ref.shape[1]
    q = jnp.dot(h, w_ref[:, :d], preferred_element_type=F32) * Q_SCALE
    o_ref[:, :d] = q.astype(o_ref.dtype)
    o_ref[:, d:] = jnp.dot(h, w_ref[:, d:], preferred_element_type=F32).astype(o_ref.dtype)


def _norm_matmul(x, g, w):
    n, d = x.shape
    n_out = w.shape[1]
    return pl.pallas_call(
        _norm_matmul_kernel,
        grid=(n // TM,),
        in_specs=[pl.BlockSpec((TM, d), lambda i: (i, 0)),
                  pl.BlockSpec((1, d), lambda i: (0, 0)),
                  pl.BlockSpec((d, n_out), lambda i: (0, 0))],
        out_specs=pl.BlockSpec((TM, n_out), lambda i: (i, 0)),
        out_shape=jax.ShapeDtypeStruct((n, n_out), BF16),
        compiler_params=_cparams("parallel"),
        name="norm_qkv",
    )(x, g.reshape(1, d), w)


def _matmul_residual_kernel(a_ref, w_ref, x_ref, o_ref):
    o_ref[...] = x_ref[...] + jnp.dot(a_ref[...], w_ref[...], preferred_element_type=F32)


def _matmul_residual(a, w, x):
    n, d = x.shape
    return pl.pallas_call(
        _matmul_residual_kernel,
        grid=(n // TM,),
        in_specs=[pl.BlockSpec((TM, a.shape[1]), lambda i: (i, 0)),
                  pl.BlockSpec(w.shape, lambda i: (0, 0)),
                  pl.BlockSpec((TM, d), lambda i: (i, 0))],
        out_specs=pl.BlockSpec((TM, d), lambda i: (i, 0)),
        out_shape=jax.ShapeDtypeStruct((n, d), F32),
        compiler_params=_cparams("parallel"),
        name="wo_residual",
    )(a, w, x)


def _t5_bucket_np(rel):
    nb = T5_BUCKETS // 2
    max_exact = nb // 2
    ret = (rel > 0).astype(np.int32) * nb
    n = np.abs(rel)
    n_f = np.maximum(n, 1).astype(np.float32)
    large = max_exact + (np.log(n_f / np.float32(max_exact)) / np.float32(math.log(T5_MAX_DIST / max_exact))
                         * np.float32(nb - max_exact)).astype(np.int32)
    large = np.minimum(large, nb - 1)
    return ret + np.where(n < max_exact, n, large)


def _lookup_kernel(idx_ref, t_ref, o_ref):
    rows = lax.broadcasted_iota(jnp.int32, (t_ref.shape[1], idx_ref.shape[1]), 0)
    onehot = jnp.where(rows == idx_ref[...], 1.0, 0.0)
    o_ref[...] = jnp.dot(t_ref[...], onehot, preferred_element_type=F32, precision=lax.Precision.HIGHEST)


def _table_lookup_t(table, idx):
    r, h = table.shape
    table_t = jnp.zeros((h, _round_up(r, LANES)), F32).at[:, :r].set(table.T)
    return pl.pallas_call(
        _lookup_kernel,
        out_shape=jax.ShapeDtypeStruct((h, len(idx)), F32),
        name="bias_lookup",
    )(jnp.asarray(idx, jnp.int32).reshape(1, -1), table_t)


def _toeplitz_kernel(win_ref, mask_ref, o_ref):
    for blk in range(o_ref.shape[0] // TQ):
        x = jnp.broadcast_to(win_ref[:, blk * 2 * TQ:(blk + 1) * 2 * TQ], (TQ, 2 * TQ))
        skew = pltpu.roll(x, 0, 1, stride=1, stride_axis=0)
        o_ref[blk * TQ:(blk + 1) * TQ, :] = skew[:, :TQ] + mask_ref[blk * TQ:(blk + 1) * TQ, :]


def _window_diagonals(n_keys):
    m = np.arange(2 * TQ)
    back = np.where(m < TQ, -m, 2 * TQ - m)
    d = np.arange(0, n_keys, TQ)[:, None] + back[None, :]
    return np.clip(d, 1 - TQ, n_keys - 1).reshape(-1)


def _toeplitz_bias_t(table, idx, allowed):
    win = _table_lookup_t(table, idx) * LOG2E
    h = win.shape[0]
    n_keys = allowed.shape[0]
    mask_add = jnp.asarray(np.where(allowed, 0.0, NEG_INF).astype(np.float32))
    return pl.pallas_call(
        _toeplitz_kernel,
        grid=(h,),
        in_specs=[pl.BlockSpec((None, 1, win.shape[1]), lambda i: (i, 0, 0)),
                  pl.BlockSpec((n_keys, TQ), lambda i: (0, 0))],
        out_specs=pl.BlockSpec((None, n_keys, TQ), lambda i: (i, 0, 0)),
        out_shape=jax.ShapeDtypeStruct((h, n_keys, TQ), F32),
        compiler_params=_cparams("parallel"),
        name="toeplitz_bias",
    )(win.reshape(h, 1, -1), mask_add)


def _t5_bias_t(table, seq):
    idx = _t5_bucket_np(_window_diagonals(seq) - (seq - TQ))
    i = np.arange(TQ)[None, :]
    j = np.arange(seq)[:, None] - (seq - TQ)
    allowed = (j < 0) | ((j // CHUNK) <= (i // CHUNK))
    return _toeplitz_bias_t(table, idx, allowed)


def _load_vt(v_ref, vt_ref):
    n_v = v_ref.shape[1]
    vt_ref[0:n_v, :] = v_ref[...].T
    vt_ref[n_v:, :] = jnp.ones((ONES_ROWS, v_ref.shape[0]), BF16)


def _attend_t(k_ref, k0, n_k, qm, bias_rows, vt_ref, s_ref, p_ref):
    m = None
    for kb in range(0, n_k, KB):
        s = _dot_nt(k_ref[k0 + kb:k0 + kb + KB, :], qm) + bias_rows(kb)
        s_ref[kb:kb + KB, :] = s
        block_max = jnp.max(s, axis=0, keepdims=True)
        m = block_max if m is None else jnp.maximum(m, block_max)
    for kb in range(0, n_k, KB):
        p_ref[kb:kb + KB, :] = jnp.exp2(s_ref[kb:kb + KB, :] - m).astype(BF16)
    o_t = jnp.dot(vt_ref[:, k0:k0 + n_k], p_ref[0:n_k, :], preferred_element_type=F32)
    n_v = vt_ref.shape[0] - ONES_ROWS
    return o_t[0:n_v] * (1.0 / o_t[n_v:n_v + 1])


def _diff_attn_kernel(q_ref, k_ref, v_ref, bias_ref, lam_ref, g_ref, o_ref, vt_ref, s_ref, p_ref, *, lam_init):
    lv = lam_ref[...]
    lam = (jnp.exp(jnp.sum(lv[0:1] * lv[1:2], axis=-1, keepdims=True))
           - jnp.exp(jnp.sum(lv[2:3] * lv[3:4], axis=-1, keepdims=True)) + lam_init)
    first_map = lax.broadcasted_iota(jnp.int32, (1, 2 * DA_HEAD_DIM), 1) < DA_HEAD_DIM
    seq = q_ref.shape[0]
    _load_vt(v_ref, vt_ref)
    for qi in range(seq // TQ):
        q = q_ref[qi * TQ:(qi + 1) * TQ, :]
        zero = jnp.zeros_like(q)
        n_k = (qi + 1) * TQ

        def bias_rows(kb, n_k=n_k):
            return bias_ref[seq - n_k + kb:seq - n_k + kb + KB, :]

        slot = (2 * qi) % N_SLOTS
        o1 = _attend_t(k_ref, 0, n_k, jnp.where(first_map, q, zero), bias_rows, vt_ref,
                       s_ref.at[slot], p_ref.at[slot])
        o2 = _attend_t(k_ref, 0, n_k, jnp.where(first_map, zero, q), bias_rows, vt_ref,
                       s_ref.at[slot + 1], p_ref.at[slot + 1])
        o_t = o1 - lam * o2
        o_t = o_t * lax.rsqrt(jnp.mean(o_t * o_t, axis=0, keepdims=True) + SUBLN_EPS) * g_ref[...]
        o_ref[qi * TQ:(qi + 1) * TQ, :] = (o_t * (1.0 - lam_init)).T.astype(o_ref.dtype)


def _diff_attention(qkv, bias_t, lam_vecs, subln_g, lam_init):
    b, s, _ = qkv.shape
    hd = 2 * DA_HEAD_DIM
    return pl.pallas_call(
        functools.partial(_diff_attn_kernel, lam_init=lam_init),
        grid=(b, DA_HEADS),
        in_specs=[pl.BlockSpec((None, s, hd), lambda i, h: (i, 0, h)),
                  pl.BlockSpec((None, s, hd), lambda i, h: (i, 0, DA_HEADS + h)),
                  pl.BlockSpec((None, s, hd), lambda i, h: (i, 0, 2 * DA_HEADS + h)),
                  pl.BlockSpec((None, s, TQ), lambda i, h: (h, 0, 0)),
                  pl.BlockSpec((4, DA_HEAD_DIM), lambda i, h: (0, 0)),
                  pl.BlockSpec((hd, 1), lambda i, h: (0, 0))],
        out_specs=pl.BlockSpec((None, s, hd), lambda i, h: (i, 0, h)),
        out_shape=jax.ShapeDtypeStruct((b, s, D_MODEL), BF16),
        scratch_shapes=[pltpu.VMEM((hd + ONES_ROWS, s), BF16), pltpu.VMEM((N_SLOTS, s, TQ), F32),
                        pltpu.VMEM((N_SLOTS, s, TQ), BF16)],
        compiler_params=_cparams("parallel", "parallel"),
        name="diff_attn",
    )(qkv, qkv, qkv, bias_t, lam_vecs, subln_g.reshape(hd, 1))


BAND_KEYS = BAND_PREV * CHUNK + TQ


def _band_bias_t(rel_table):
    dist = BAND_PREV * CHUNK - _window_diagonals(BAND_KEYS)
    q_chunk = np.arange(TQ)[None, :] // CHUNK
    k_chunk = np.arange(BAND_KEYS)[:, None] // CHUNK
    allowed = (k_chunk >= q_chunk) & (k_chunk <= q_chunk + BAND_PREV)
    return _toeplitz_bias_t(rel_table, np.clip(dist, -REL_CLIP, REL_CLIP) + REL_CLIP, allowed)


def _chunk_attn_kernel(q_ref, k_ref, v_ref, bias_ref, o_ref, vt_ref, s_ref, p_ref):
    first_head = lax.broadcasted_iota(jnp.int32, (1, PAIR), 1) < CA_HEAD_DIM
    first_head_rows = lax.broadcasted_iota(jnp.int32, (PAIR, 1), 0) < CA_HEAD_DIM
    seq = q_ref.shape[0]
    _load_vt(v_ref, vt_ref)
    for qi in range(seq // TQ):
        q0 = qi * TQ
        k0 = max(0, q0 - BAND_PREV * CHUNK)
        n_k = q0 + TQ - k0
        q = q_ref[q0:q0 + TQ, :]
        zero = jnp.zeros_like(q)
        outs = []
        for hh in range(2):
            def bias_rows(kb, hh=hh, n_k=n_k):
                return bias_ref[hh, BAND_KEYS - n_k + kb:BAND_KEYS - n_k + kb + KB, :]

            qm = jnp.where(first_head, q, zero) if hh == 0 else jnp.where(first_head, zero, q)
            slot = (2 * qi + hh) % N_SLOTS
            outs.append(_attend_t(k_ref, k0, n_k, qm, bias_rows, vt_ref, s_ref.at[slot], p_ref.at[slot]))
        o_ref[q0:q0 + TQ, :] = jnp.where(first_head_rows, outs[0], outs[1]).T.astype(o_ref.dtype)


def _chunk_attention(qkv, band_bias_t):
    b, s, _ = qkv.shape
    n_pairs = CA_HEADS // 2
    return pl.pallas_call(
        _chunk_attn_kernel,
        grid=(b, n_pairs),
        in_specs=[pl.BlockSpec((None, s, PAIR), lambda i, p: (i, 0, p)),
                  pl.BlockSpec((None, s, PAIR), lambda i, p: (i, 0, n_pairs + p)),
                  pl.BlockSpec((None, s, PAIR), lambda i, p: (i, 0, 2 * n_pairs + p)),
                  pl.BlockSpec((2, BAND_KEYS, TQ), lambda i, p: (p, 0, 0))],
        out_specs=pl.BlockSpec((None, s, PAIR), lambda i, p: (i, 0, p)),
        out_shape=jax.ShapeDtypeStruct((b, s, D_MODEL), BF16),
        scratch_shapes=[pltpu.VMEM((PAIR + ONES_ROWS, s), BF16), pltpu.VMEM((N_SLOTS, BAND_KEYS, TQ), F32),
                        pltpu.VMEM((N_SLOTS, BAND_KEYS, TQ), BF16)],
        compiler_params=_cparams("parallel", "parallel"),
        name="chunk_attn",
    )(qkv, qkv, qkv, band_bias_t)


def _swiglu_partial(h, wg, wu, wd):
    g = jnp.dot(h, wg, preferred_element_type=F32)
    u = jnp.dot(h, wu, preferred_element_type=F32)
    a = (g * (1.0 / (1.0 + jnp.exp(-g))) * u).astype(BF16)
    return jnp.dot(a, wd, preferred_element_type=F32)


def _dense_ffn_kernel(x_ref, g_ref, wg_ref, wu_ref, wd_ref, o_ref, h_ref, acc_ref):
    f = pl.program_id(1)

    @pl.when(f == 0)
    def _():
        h_ref[...] = _rmsnorm(x_ref[...], g_ref[...], RMS_EPS).astype(BF16)
        acc_ref[...] = jnp.zeros_like(acc_ref)

    acc_ref[...] += _swiglu_partial(h_ref[...], wg_ref[...], wu_ref[...], wd_ref[...])

    @pl.when(f == pl.num_programs(1) - 1)
    def _():
        o_ref[...] = x_ref[...] + acc_ref[...]


def _dense_ffn(x, g, w_gate_up, w_down, tf):
    n, d = x.shape
    ff = w_down.shape[0]
    nf = ff // tf
    return pl.pallas_call(
        _dense_ffn_kernel,
        grid=(n // TM, nf),
        in_specs=[pl.BlockSpec((TM, d), lambda i, f: (i, 0)),
                  pl.BlockSpec((1, d), lambda i, f: (0, 0)),
                  pl.BlockSpec((d, tf), lambda i, f: (0, f)),
                  pl.BlockSpec((d, tf), lambda i, f: (0, nf + f)),
                  pl.BlockSpec((tf, d), lambda i, f: (f, 0))],
        out_specs=pl.BlockSpec((TM, d), lambda i, f: (i, 0)),
        out_shape=jax.ShapeDtypeStruct((n, d), F32),
        scratch_shapes=[pltpu.VMEM((TM, d), BF16), pltpu.VMEM((TM, d), F32)],
        compiler_params=_cparams("parallel", "arbitrary"),
        name="dense_ffn",
    )(x, g.reshape(1, d), w_gate_up, w_gate_up, w_down)


def _router_kernel(x_ref, g_ref, r_ref, h_ref, gate_ref, mask_ref, mask_t_ref, cnt_ref):
    h = _rmsnorm(x_ref[...], g_ref[...], RMS_EPS)
    h_ref[...] = h.astype(BF16)
    logits = jnp.dot(h, r_ref[...], preferred_element_type=F32, precision=lax.Precision.HIGHEST)
    lane = lax.broadcasted_iota(jnp.int32, logits.shape, 1)
    logits = jnp.where(lane < N_EXPERTS, logits, -jnp.inf)
    v1 = jnp.max(logits, axis=-1, keepdims=True)
    i1 = jnp.min(jnp.where(logits == v1, lane, LANES), axis=-1, keepdims=True)
    m1 = lane == i1
    rest = jnp.where(m1, -jnp.inf, logits)
    v2 = jnp.max(rest, axis=-1, keepdims=True)
    i2 = jnp.min(jnp.where(rest == v2, lane, LANES), axis=-1, keepdims=True)
    m2 = lane == i2
    e2 = jnp.exp(v2 - v1)
    g1 = 1.0 / (1.0 + e2)
    g2 = e2 / (1.0 + e2)
    gate_ref[...] = jnp.where(m1, g1, 0.0) + jnp.where(m2, g2, 0.0)
    mask = jnp.where(m1 | m2, 1.0, 0.0)
    mask_ref[...] = mask
    mask_t_ref[...] = mask.T[0:N_EXPERTS, :]
    cnt_ref[...] = jnp.broadcast_to(jnp.sum(mask, axis=0, keepdims=True), cnt_ref.shape)


def _router(x, g, router):
    n, d = x.shape
    t = n // TT
    r_pad = jnp.zeros((d, LANES), F32).at[:, :N_EXPERTS].set(router)
    return pl.pallas_call(
        _router_kernel,
        grid=(t,),
        in_specs=[pl.BlockSpec((TT, d), lambda i: (i, 0)),
                  pl.BlockSpec((1, d), lambda i: (0, 0)),
                  pl.BlockSpec((d, LANES), lambda i: (0, 0))],
        out_specs=[pl.BlockSpec((TT, d), lambda i: (i, 0)),
                   pl.BlockSpec((TT, LANES), lambda i: (i, 0)),
                   pl.BlockSpec((TT, LANES), lambda i: (i, 0)),
                   pl.BlockSpec((N_EXPERTS, TT), lambda i: (0, i)),
                   pl.BlockSpec((None, 8, LANES), lambda i: (i, 0, 0))],
        out_shape=[jax.ShapeDtypeStruct((n, d), BF16),
                   jax.ShapeDtypeStruct((n, LANES), F32),
                   jax.ShapeDtypeStruct((n, LANES), F32),
                   jax.ShapeDtypeStruct((N_EXPERTS, n), F32),
                   jax.ShapeDtypeStruct((t, 8, LANES), F32)],
        compiler_params=_cparams("parallel"),
        name="router",
    )(x, g.reshape(1, d), r_pad)


def _segment_copies(n_rows, make_copy):
    off = jnp.int32(0)
    size = TT
    while size >= BF16_ROWS:
        take = (n_rows & size) != 0

        @pl.when(take)
        def _(off=off, size=size):
            make_copy(off, size)

        off = off + jnp.where(take, size, 0)
        size //= 2


def _dispatch_kernel(start_ref, rows_ref, mask_t_ref, h_ref, xs_in_ref, xs_ref, blk_ref, sem_ref):
    del xs_in_ref
    t = pl.program_id(0)
    slot = t % 2
    mask_t = mask_t_ref[...]
    row = lax.broadcasted_iota(jnp.int32, (TT, TT), 0)
    col = lax.broadcasted_iota(jnp.int32, (TT, TT), 1)
    before = jnp.where(row < col, 1.0, 0.0).astype(BF16)
    rank = jnp.dot(mask_t.astype(BF16), before, preferred_element_type=F32)
    row_f = row.astype(F32)
    h = h_ref[...]
    for e in range(N_EXPERTS):
        pick = (row_f == rank[e:e + 1, :]) & (mask_t[e:e + 1, :] > 0.0)
        sel = jnp.where(pick, 1.0, 0.0).astype(BF16)
        blk_ref[slot, e] = jnp.dot(sel, h, preferred_element_type=F32).astype(BF16)

    def copies(step, step_slot, wait):
        for e in range(N_EXPERTS):
            start = pl.multiple_of(start_ref[step * N_EXPERTS + e], BF16_ROWS)

            def make_copy(off, size, e=e, start=start):
                src = blk_ref.at[step_slot, e, pl.ds(pl.multiple_of(off, BF16_ROWS), size)]
                dst = xs_ref.at[pl.ds(pl.multiple_of(start + off, BF16_ROWS), size)]
                cp = pltpu.make_async_copy(src, dst, sem_ref.at[step_slot])
                cp.wait() if wait else cp.start()

            _segment_copies(rows_ref[step * N_EXPERTS + e], make_copy)

    copies(t, slot, wait=False)

    @pl.when(t > 0)
    def _():
        copies(t - 1, 1 - slot, wait=True)

    @pl.when(t == pl.num_programs(0) - 1)
    def _():
        copies(t, slot, wait=True)


def _dispatch(seg_start, seg_rows, mask_t, h, n_rows):
    n, d = h.shape
    t = n // TT
    xs0 = jnp.zeros((n_rows, d), BF16)
    grid_spec = pltpu.PrefetchScalarGridSpec(
        num_scalar_prefetch=2,
        grid=(t,),
        in_specs=[pl.BlockSpec((N_EXPERTS, TT), lambda i, *_: (0, i)),
                  pl.BlockSpec((TT, d), lambda i, *_: (i, 0)),
                  pl.BlockSpec(memory_space=pl.ANY)],
        out_specs=pl.BlockSpec(memory_space=pl.ANY),
        scratch_shapes=[pltpu.VMEM((2, N_EXPERTS, TT, d), BF16), pltpu.SemaphoreType.DMA((2,))],
    )
    return pl.pallas_call(
        _dispatch_kernel,
        grid_spec=grid_spec,
        out_shape=jax.ShapeDtypeStruct((n_rows, d), BF16),
        input_output_aliases={4: 0},
        compiler_params=_cparams("arbitrary"),
        name="moe_dispatch",
    )(seg_start, seg_rows, mask_t, h, xs0)


def _expert_kernel(tile_expert_ref, tile_rows_ref, xs_ref, wg_ref, wu_ref, wd_ref, o_ref,
                   wg_bf_ref, wu_bf_ref, wd_bf_ref, acc_ref):
    del tile_expert_ref
    j = pl.program_id(0)
    f = pl.program_id(1)
    last = pl.num_programs(1) - 1
    rows = tile_rows_ref[j]

    @pl.when(rows > 0)
    def _():
        wg_bf_ref[...] = wg_ref[...].astype(BF16)
        wu_bf_ref[...] = wu_ref[...].astype(BF16)
        wd_bf_ref[...] = wd_ref[...].astype(BF16)

    @pl.when(f == 0)
    def _():
        acc_ref[...] = jnp.zeros_like(acc_ref)

    @pl.when(rows == TE)
    def _():
        acc_ref[...] += _swiglu_partial(xs_ref[...], wg_bf_ref[...], wu_bf_ref[...], wd_bf_ref[...])

    @pl.when((rows > 0) & (rows < TE))
    def _():
        for lo in range(0, TE, TM):
            @pl.when(rows > lo)
            def _(lo=lo):
                acc_ref[lo:lo + TM, :] += _swiglu_partial(xs_ref[lo:lo + TM, :], wg_bf_ref[...], wu_bf_ref[...],
                                                          wd_bf_ref[...])

    @pl.when(f == last)
    def _():
        o_ref[...] = acc_ref[...].astype(o_ref.dtype)


def _expert_ffn(tile_expert, tile_rows, xs, w_gate_up, w_down, layer, tf):
    n_rows, d = xs.shape
    ff = w_down.shape[2]
    nf = ff // tf

    def f_eff(j, f, tr):
        return jnp.where(tr[j] > 0, f, nf - 1)

    grid_spec = pltpu.PrefetchScalarGridSpec(
        num_scalar_prefetch=2,
        grid=(n_rows // TE, nf),
        in_specs=[pl.BlockSpec((TE, d), lambda j, f, te, tr: (j, 0)),
                  pl.BlockSpec((None, None, d, tf), lambda j, f, te, tr: (layer, te[j], 0, f_eff(j, f, tr))),
                  pl.BlockSpec((None, None, d, tf), lambda j, f, te, tr: (layer, te[j], 0, nf + f_eff(j, f, tr))),
                  pl.BlockSpec((None, None, tf, d), lambda j, f, te, tr: (layer, te[j], f_eff(j, f, tr), 0))],
        out_specs=pl.BlockSpec((TE, d), lambda j, f, te, tr: (j, 0)),
        scratch_shapes=[pltpu.VMEM((d, tf), BF16), pltpu.VMEM((d, tf), BF16), pltpu.VMEM((tf, d), BF16),
                        pltpu.VMEM((TE, d), F32)],
    )
    return pl.pallas_call(
        _expert_kernel,
        grid_spec=grid_spec,
        out_shape=jax.ShapeDtypeStruct((n_rows, d), BF16),
        compiler_params=_cparams("parallel", "arbitrary"),
        name="moe_experts",
    )(tile_expert, tile_rows, xs, w_gate_up, w_gate_up, w_down)


def _combine_kernel(start_ref, rows_ref, x_ref, gate_ref, mask_ref, out_gain_ref, ys_ref, o_ref, win_ref, sem_ref,
                    *, out_norm):
    t = pl.program_id(0)
    n_t = pl.num_programs(0)
    slot = t % 2

    def copies(step, step_slot, wait):
        for e in range(N_EXPERTS):
            start = pl.multiple_of(start_ref[step * N_EXPERTS + e], BF16_ROWS)

            def make_copy(off, size, e=e, start=start):
                src = ys_ref.at[pl.ds(pl.multiple_of(start + off, BF16_ROWS), size)]
                dst = win_ref.at[step_slot, e, pl.ds(pl.multiple_of(off, BF16_ROWS), size)]
                cp = pltpu.make_async_copy(src, dst, sem_ref.at[step_slot])
                cp.wait() if wait else cp.start()

            _segment_copies(rows_ref[step * N_EXPERTS + e], make_copy)

    @pl.when(t == 0)
    def _():
        win_ref[...] = jnp.zeros_like(win_ref)
        copies(0, 0, wait=False)

    @pl.when(t + 1 < n_t)
    def _():
        copies(t + 1, 1 - slot, wait=False)

    copies(t, slot, wait=True)

    mask = mask_ref[...]
    gate = gate_ref[...]
    row = lax.broadcasted_iota(jnp.int32, (TT, TT), 0)
    col = lax.broadcasted_iota(jnp.int32, (TT, TT), 1)
    before = jnp.where(col < row, 1.0, 0.0).astype(BF16)
    rank = jnp.dot(before, mask.astype(BF16), preferred_element_type=F32)
    col_f = col.astype(F32)
    y = x_ref[...]
    for e in range(N_EXPERTS):
        pick = (col_f == rank[:, e:e + 1]) & (mask[:, e:e + 1] > 0.0)
        sel = jnp.where(pick, 1.0, 0.0).astype(BF16)
        y = y + gate[:, e:e + 1] * jnp.dot(sel, win_ref[slot, e], preferred_element_type=F32)
    o_ref[...] = _rmsnorm(y, out_gain_ref[...], RMS_EPS) if out_norm else y


def _combine(seg_start, seg_rows, x, gate, mask, ys, out_gain, out_norm):
    n, d = x.shape
    t = n // TT
    grid_spec = pltpu.PrefetchScalarGridSpec(
        num_scalar_prefetch=2,
        grid=(t,),
        in_specs=[pl.BlockSpec((TT, d), lambda i, *_: (i, 0)),
                  pl.BlockSpec((TT, LANES), lambda i, *_: (i, 0)),
                  pl.BlockSpec((TT, LANES), lambda i, *_: (i, 0)),
                  pl.BlockSpec((1, d), lambda i, *_: (0, 0)),
                  pl.BlockSpec(memory_space=pl.ANY)],
        out_specs=pl.BlockSpec((TT, d), lambda i, *_: (i, 0)),
        scratch_shapes=[pltpu.VMEM((2, N_EXPERTS, TT, d), BF16), pltpu.SemaphoreType.DMA((2,))],
    )
    return pl.pallas_call(
        functools.partial(_combine_kernel, out_norm=out_norm),
        grid_spec=grid_spec,
        out_shape=jax.ShapeDtypeStruct((n, d), F32),
        compiler_params=_cparams("arbitrary"),
        name="moe_combine",
    )(seg_start, seg_rows, x, gate, mask, out_gain.reshape(1, d), ys)


def _moe_ffn(x, g, router, w_gate_up, w_down, layer, tf, out_gain, out_norm):
    n, d = x.shape
    t = n // TT
    h, gate, mask, mask_t, cnt = _router(x, g, router)
    cnt = cnt[:, 0, :N_EXPERTS].astype(jnp.int32)
    seg_rows = _round_up(cnt, BF16_ROWS)
    expert_rows = jnp.sum(seg_rows, axis=0)
    region = _round_up(expert_rows, TE)
    region_end = jnp.cumsum(region)
    seg_start = (region_end - region)[None, :] + jnp.cumsum(seg_rows, axis=0) - seg_rows
    n_tiles = (_round_up(2 * n + t * N_EXPERTS * (BF16_ROWS - 1), TE) + N_EXPERTS * TE) // TE
    tile_end = region_end // TE
    tile_ids = jnp.arange(n_tiles, dtype=jnp.int32)
    last_tile = jnp.minimum(tile_ids, tile_end[-1] - 1)
    tile_expert = jnp.sum((last_tile[:, None] >= tile_end[None, :]).astype(jnp.int32), axis=1)
    tile_expert = jnp.minimum(tile_expert, N_EXPERTS - 1)
    rows_before = (tile_ids - (tile_end - region // TE)[tile_expert]) * TE
    tile_rows = jnp.where(tile_ids < tile_end[-1], jnp.clip(expert_rows[tile_expert] - rows_before, 0, TE), 0)
    seg_start = seg_start.reshape(-1).astype(jnp.int32)
    seg_rows = seg_rows.reshape(-1).astype(jnp.int32)

    xs = _dispatch(seg_start, seg_rows, mask_t, h, n_tiles * TE)
    ys = _expert_ffn(tile_expert, tile_rows.astype(jnp.int32), xs, w_gate_up, w_down, layer, tf)
    return _combine(seg_start, seg_rows, x, gate, mask, ys, out_gain, out_norm)


def kernel(x, attn_norm, ffn_norm, final_norm, t5_rel_bias, diff_w_qkv, diff_w_o, diff_lambda_q1, diff_lambda_k1, diff_lambda_q2, diff_lambda_k2, diff_subln, chunk_w_qkv, chunk_w_o, chunk_rel_bias, dense_w_gate_up, dense_w_down, moe_router, moe_w_gate_up, moe_w_down):
    b, s, d = x.shape
    n = b * s
    xf = x.reshape(n, d)
    t5_bias = _t5_bias_t(t5_rel_bias, s)
    for i in range(DEPTH):
        j = i // 2
        if i % 2 == 0:
            qkv = _norm_matmul(xf, attn_norm[i], diff_w_qkv[j].astype(BF16))
            lam_init = 0.8 - 0.6 * math.exp(-0.3 * i)
            lam_vecs = jnp.stack([diff_lambda_q1[j], diff_lambda_k1[j], diff_lambda_q2[j], diff_lambda_k2[j]])
            a = _diff_attention(qkv.reshape(b, s, 3 * d), t5_bias, lam_vecs, diff_subln[j], lam_init)
            xf = _matmul_residual(a.reshape(n, d), diff_w_o[j].astype(BF16), xf)
            xf = _dense_ffn(xf, ffn_norm[i], dense_w_gate_up[j].astype(BF16), dense_w_down[j].astype(BF16),
                            tf=dense_w_down.shape[1] // 2)
        else:
            qkv = _norm_matmul(xf, attn_norm[i], chunk_w_qkv[j].astype(BF16))
            a = _chunk_attention(qkv.reshape(b, s, 3 * d), _band_bias_t(chunk_rel_bias[j]))
            xf = _matmul_residual(a.reshape(n, d), chunk_w_o[j].astype(BF16), xf)
            xf = _moe_ffn(xf, ffn_norm[i], moe_router[j], moe_w_gate_up, moe_w_down, layer=j,
                          tf=moe_w_down.shape[2] // 7,
                          out_gain=final_norm, out_norm=i == DEPTH - 1)
    return xf.reshape(b, s, d)
```

```python
import functools
import math

import numpy as np
import jax
import jax.numpy as jnp
from jax import lax
from jax.experimental import pallas as pl
from jax.experimental.pallas import tpu as pltpu

F32 = jnp.float32
BF16 = jnp.bfloat16

D_MODEL = 1024
DEPTH = 4
CHUNK = 64
DA_HEADS = 8
DA_HEAD_DIM = 64
T5_BUCKETS = 32
T5_MAX_DIST = 128
CA_HEADS = 16
CA_HEAD_DIM = 64
BAND_PREV = 8
REL_CLIP = 4 * CHUNK
N_EXPERTS = 8
RMS_EPS = 1e-6
SUBLN_EPS = 1e-5
NEG_INF = -1e30

LANES = 128
BF16_ROWS = 16
VMEM_LIMIT = 56 * 1024 * 1024
TM = 512
TE = 4 * TM
TQ = 256
TT = 256
KB = 256
N_SLOTS = 4
ONES_ROWS = BF16_ROWS
LOG2E = math.log2(math.e)
Q_SCALE = DA_HEAD_DIM ** -0.5 * LOG2E
PAIR = 2 * CA_HEAD_DIM


def _round_up(a, m):
    return (a + m - 1) // m * m


def _cparams(*sem):
    return pltpu.CompilerParams(dimension_semantics=sem, vmem_limit_bytes=VMEM_LIMIT)


def _rmsnorm(x32, g, eps):
    return x32 * lax.rsqrt(jnp.mean(x32 * x32, axis=-1, keepdims=True) + eps) * g


def _dot_nt(a, b):
    return lax.dot_general(a, b, (((1,), (1,)), ((), ())), preferred_element_type=F32)


def _norm_matmul_kernel(x_ref, g_ref, w_ref, o_ref):
    h = _rmsnorm(x_ref[...], g_ref[...], RMS_EPS).astype(BF16)
    d = x_ref.shape[1]
    q = jnp.dot(h, w_ref[:, :d], preferred_element_type=F32) * Q_SCALE
    o_ref[:, :d] = q.astype(o_ref.dtype)
    o_ref[:, d:] = jnp.dot(h, w_ref[:, d:], preferred_element_type=F32).astype(o_ref.dtype)


def _norm_matmul(x, g, w):
    n, d = x.shape
    n_out = w.shape[1]
    return pl.pallas_call(
        _norm_matmul_kernel,
        grid=(n // TM,),
        in_specs=[pl.BlockSpec((TM, d), lambda i: (i, 0)),
                  pl.BlockSpec((1, d), lambda i: (0, 0)),
                  pl.BlockSpec((d, n_out), lambda i: (0, 0))],
        out_specs=pl.BlockSpec((TM, n_out), lambda i: (i, 0)),
        out_shape=jax.ShapeDtypeStruct((n, n_out), BF16),
        compiler_params=_cparams("parallel"),
        name="norm_qkv",
    )(x, g.reshape(1, d), w)


def _matmul_residual_kernel(a_ref, w_ref, x_ref, o_ref):
    o_ref[...] = x_ref[...] + jnp.dot(a_ref[...], w_ref[...], preferred_element_type=F32)


def _matmul_residual(a, w, x):
    n, d = x.shape
    return pl.pallas_call(
        _matmul_residual_kernel,
        grid=(n // TM,),
        in_specs=[pl.BlockSpec((TM, a.shape[1]), lambda i: (i, 0)),
                  pl.BlockSpec(w.shape, lambda i: (0, 0)),
                  pl.BlockSpec((TM, d), lambda i: (i, 0))],
        out_specs=pl.BlockSpec((TM, d), lambda i: (i, 0)),
        out_shape=jax.ShapeDtypeStruct((n, d), F32),
        compiler_params=_cparams("parallel"),
        name="wo_residual",
    )(a, w, x)


def _t5_bucket_np(rel):
    nb = T5_BUCKETS // 2
    max_exact = nb // 2
    ret = (rel > 0).astype(np.int32) * nb
    n = np.abs(rel)
    n_f = np.maximum(n, 1).astype(np.float32)
    large = max_exact + (np.log(n_f / np.float32(max_exact)) / np.float32(math.log(T5_MAX_DIST / max_exact))
                         * np.float32(nb - max_exact)).astype(np.int32)
    large = np.minimum(large, nb - 1)
    return ret + np.where(n < max_exact, n, large)


def _lookup_kernel(idx_ref, t_ref, o_ref):
    rows = lax.broadcasted_iota(jnp.int32, (t_ref.shape[1], idx_ref.shape[1]), 0)
    onehot = jnp.where(rows == idx_ref[...], 1.0, 0.0)
    o_ref[...] = jnp.dot(t_ref[...], onehot, preferred_element_type=F32, precision=lax.Precision.HIGHEST)


def _table_lookup_t(table, idx):
    r, h = table.shape
    table_t = jnp.zeros((h, _round_up(r, LANES)), F32).at[:, :r].set(table.T)
    return pl.pallas_call(
        _lookup_kernel,
        out_shape=jax.ShapeDtypeStruct((h, len(idx)), F32),
        name="bias_lookup",
    )(jnp.asarray(idx, jnp.int32).reshape(1, -1), table_t)


def _toeplitz_kernel(win_ref, mask_ref, o_ref):
    for blk in range(o_ref.shape[0] // TQ):
        x = jnp.broadcast_to(win_ref[:, blk * 2 * TQ:(blk + 1) * 2 * TQ], (TQ, 2 * TQ))
        skew = pltpu.roll(x, 0, 1, stride=1, stride_axis=0)
        o_ref[blk * TQ:(blk + 1) * TQ, :] = skew[:, :TQ] + mask_ref[blk * TQ:(blk + 1) * TQ, :]


def _window_diagonals(n_keys):
    m = np.arange(2 * TQ)
    back = np.where(m < TQ, -m, 2 * TQ - m)
    d = np.arange(0, n_keys, TQ)[:, None] + back[None, :]
    return np.clip(d, 1 - TQ, n_keys - 1).reshape(-1)


def _toeplitz_bias_t(table, idx, allowed):
    win = _table_lookup_t(table, idx) * LOG2E
    h = win.shape[0]
    n_keys = allowed.shape[0]
    mask_add = jnp.asarray(np.where(allowed, 0.0, NEG_INF).astype(np.float32))
    return pl.pallas_call(
        _toeplitz_kernel,
        grid=(h,),
        in_specs=[pl.BlockSpec((None, 1, win.shape[1]), lambda i: (i, 0, 0)),
                  pl.BlockSpec((n_keys, TQ), lambda i: (0, 0))],
        out_specs=pl.BlockSpec((None, n_keys, TQ), lambda i: (i, 0, 0)),
        out_shape=jax.ShapeDtypeStruct((h, n_keys, TQ), F32),
        compiler_params=_cparams("parallel"),
        name="toeplitz_bias",
    )(win.reshape(h, 1, -1), mask_add)


def _t5_bias_t(table, seq):
    idx = _t5_bucket_np(_window_diagonals(seq) - (seq - TQ))
    i = np.arange(TQ)[None, :]
    j = np.arange(seq)[:, None] - (seq - TQ)
    allowed = (j < 0) | ((j // CHUNK) <= (i // CHUNK))
    far_bucket = _t5_bucket_np(np.asarray(-seq))
    is_far = np.all((_t5_bucket_np(j - i) == far_bucket) & allowed, axis=1)
    const_rows = int(np.argmin(is_far)) if not is_far.all() else seq
    return _toeplitz_bias_t(table, idx, allowed), const_rows


def _load_vt(v_ref, vt_ref):
    n_v = v_ref.shape[1]
    vt_ref[0:n_v, :] = v_ref[...].T
    vt_ref[n_v:, :] = jnp.ones((ONES_ROWS, v_ref.shape[0]), BF16)


def _attend_pair_steps(k_ref, k0, n_k, q, lane_split, bias_rows, bias_const, vt_ref, s_ref, p_ref, finish):
    zero = jnp.zeros_like(q)
    qq = jnp.concatenate([jnp.where(lane_split, q, zero), jnp.where(lane_split, zero, q)], axis=0)
    s_ref[0:n_k, :] = _dot_nt(k_ref[k0:k0 + n_k, :], qq)
    yield 1
    halves = (slice(0, TQ), slice(TQ, 2 * TQ))
    m = [None, None]
    for kb in range(0, n_k, KB):
        for h, cols in enumerate(halves):
            if bias_const(kb):
                block_max = jnp.max(s_ref[kb:kb + KB, cols], axis=0, keepdims=True) + bias_rows(h, kb)
            else:
                s = s_ref[kb:kb + KB, cols] + bias_rows(h, kb)
                s_ref[kb:kb + KB, cols] = s
                block_max = jnp.max(s, axis=0, keepdims=True)
            m[h] = block_max if m[h] is None else jnp.maximum(m[h], block_max)
            yield 1
    for kb in range(0, n_k, KB):
        for h, cols in enumerate(halves):
            shift = m[h] - bias_rows(h, kb) if bias_const(kb) else m[h]
            p_ref[kb:kb + KB, cols] = jnp.exp2(s_ref[kb:kb + KB, cols] - shift).astype(BF16)
            yield 2
    o_t = jnp.dot(vt_ref[:, k0:k0 + n_k], p_ref[0:n_k, :], preferred_element_type=F32)
    n_v = vt_ref.shape[0] - ONES_ROWS
    finish([o_t[0:n_v, cols] * (1.0 / o_t[n_v:n_v + 1, cols]) for cols in halves])
    yield 2


def _interleave(tiles):
    pending = list(tiles)
    active = []
    while pending or active:
        if pending and len(active) < 2 and all(stage == 2 for _, stage in active):
            active.append([pending.pop(0), 1])
        for item in list(active):
            try:
                item[1] = next(item[0])
            except StopIteration:
                active.remove(item)


def _diff_attn_kernel(q_ref, k_ref, v_ref, bias_ref, lam_ref, g_ref, o_ref, vt_ref, s_ref, p_ref, *,
                      lam_init, const_rows):
    lv = lam_ref[...]
    lam = (jnp.exp(jnp.sum(lv[0:1] * lv[1:2], axis=-1, keepdims=True))
           - jnp.exp(jnp.sum(lv[2:3] * lv[3:4], axis=-1, keepdims=True)) + lam_init)
    first_map = lax.broadcasted_iota(jnp.int32, (1, 2 * DA_HEAD_DIM), 1) < DA_HEAD_DIM
    seq = q_ref.shape[0]
    _load_vt(v_ref, vt_ref)

    def tile(qi):
        n_k = (qi + 1) * TQ
        row0 = seq - n_k

        def bias_const(kb):
            return row0 + kb + KB <= const_rows

        def bias_rows(h, kb):
            n = 1 if bias_const(kb) else KB
            return bias_ref[row0 + kb:row0 + kb + n, :]

        def finish(outs):
            o_t = outs[0] - lam * outs[1]
            o_t = o_t * lax.rsqrt(jnp.mean(o_t * o_t, axis=0, keepdims=True) + SUBLN_EPS) * g_ref[...]
            o_ref[qi * TQ:(qi + 1) * TQ, :] = (o_t * (1.0 - lam_init)).T.astype(o_ref.dtype)

        slot = qi % N_SLOTS
        return _attend_pair_steps(k_ref, 0, n_k, q_ref[qi * TQ:(qi + 1) * TQ, :], first_map, bias_rows, bias_const,
                                  vt_ref, s_ref.at[slot], p_ref.at[slot], finish)

    _interleave(tile(qi) for qi in range(seq // TQ))


def _diff_attention(qkv, bias_t, const_rows, lam_vecs, subln_g, lam_init):
    b, s, _ = qkv.shape
    hd = 2 * DA_HEAD_DIM
    return pl.pallas_call(
        functools.partial(_diff_attn_kernel, lam_init=lam_init, const_rows=const_rows),
        grid=(b, DA_HEADS),
        in_specs=[pl.BlockSpec((None, s, hd), lambda i, h: (i, 0, h)),
                  pl.BlockSpec((None, s, hd), lambda i, h: (i, 0, DA_HEADS + h)),
                  pl.BlockSpec((None, s, hd), lambda i, h: (i, 0, 2 * DA_HEADS + h)),
                  pl.BlockSpec((None, s, TQ), lambda i, h: (h, 0, 0)),
                  pl.BlockSpec((4, DA_HEAD_DIM), lambda i, h: (0, 0)),
                  pl.BlockSpec((hd, 1), lambda i, h: (0, 0))],
        out_specs=pl.BlockSpec((None, s, hd), lambda i, h: (i, 0, h)),
        out_shape=jax.ShapeDtypeStruct((b, s, D_MODEL), BF16),
        scratch_shapes=[pltpu.VMEM((hd + ONES_ROWS, s), BF16), pltpu.VMEM((N_SLOTS, s, 2 * TQ), F32),
                        pltpu.VMEM((N_SLOTS, s, 2 * TQ), BF16)],
        compiler_params=_cparams("parallel", "parallel"),
        name="diff_attn",
    )(qkv, qkv, qkv, bias_t, lam_vecs, subln_g.reshape(hd, 1))


BAND_KEYS = BAND_PREV * CHUNK + TQ


def _band_bias_t(rel_table):
    dist = BAND_PREV * CHUNK - _window_diagonals(BAND_KEYS)
    q_chunk = np.arange(TQ)[None, :] // CHUNK
    k_chunk = np.arange(BAND_KEYS)[:, None] // CHUNK
    allowed = (k_chunk >= q_chunk) & (k_chunk <= q_chunk + BAND_PREV)
    return _toeplitz_bias_t(rel_table, np.clip(dist, -REL_CLIP, REL_CLIP) + REL_CLIP, allowed)


def _chunk_attn_kernel(q_ref, k_ref, v_ref, bias_ref, o_ref, vt_ref, s_ref, p_ref):
    first_head = lax.broadcasted_iota(jnp.int32, (1, PAIR), 1) < CA_HEAD_DIM
    first_head_rows = lax.broadcasted_iota(jnp.int32, (PAIR, 1), 0) < CA_HEAD_DIM
    seq = q_ref.shape[0]
    _load_vt(v_ref, vt_ref)

    def tile(qi):
        q0 = qi * TQ
        k0 = max(0, q0 - BAND_PREV * CHUNK)
        n_k = q0 + TQ - k0
        row0 = BAND_KEYS - n_k

        def bias_rows(h, kb):
            return bias_ref[h, row0 + kb:row0 + kb + KB, :]

        def finish(outs):
            o_ref[q0:q0 + TQ, :] = jnp.where(first_head_rows, outs[0], outs[1]).T.astype(o_ref.dtype)

        slot = qi % N_SLOTS
        return _attend_pair_steps(k_ref, k0, n_k, q_ref[q0:q0 + TQ, :], first_head, bias_rows, lambda kb: False,
                                  vt_ref, s_ref.at[slot], p_ref.at[slot], finish)

    _interleave(tile(qi) for qi in range(seq // TQ))


def _chunk_attention(qkv, band_bias_t):
    b, s, _ = qkv.shape
    n_pairs = CA_HEADS // 2
    return pl.pallas_call(
        _chunk_attn_kernel,
        grid=(b, n_pairs),
        in_specs=[pl.BlockSpec((None, s, PAIR), lambda i, p: (i, 0, p)),
                  pl.BlockSpec((None, s, PAIR), lambda i, p: (i, 0, n_pairs + p)),
                  pl.BlockSpec((None, s, PAIR), lambda i, p: (i, 0, 2 * n_pairs + p)),
                  pl.BlockSpec((2, BAND_KEYS, TQ), lambda i, p: (p, 0, 0))],
        out_specs=pl.BlockSpec((None, s, PAIR), lambda i, p: (i, 0, p)),
        out_shape=jax.ShapeDtypeStruct((b, s, D_MODEL), BF16),
        scratch_shapes=[pltpu.VMEM((PAIR + ONES_ROWS, s), BF16), pltpu.VMEM((N_SLOTS, BAND_KEYS, 2 * TQ), F32),
                        pltpu.VMEM((N_SLOTS, BAND_KEYS, 2 * TQ), BF16)],
        compiler_params=_cparams("parallel", "parallel"),
        name="chunk_attn",
    )(qkv, qkv, qkv, band_bias_t)


def _swiglu_partial(h, wg, wu, wd):
    g = jnp.dot(h, wg, preferred_element_type=F32)
    u = jnp.dot(h, wu, preferred_element_type=F32)
    a = (g * (1.0 / (1.0 + jnp.exp(-g))) * u).astype(BF16)
    return jnp.dot(a, wd, preferred_element_type=F32)


def _dense_ffn_kernel(x_ref, g_ref, wg_ref, wu_ref, wd_ref, o_ref, h_ref, acc_ref):
    f = pl.program_id(1)

    @pl.when(f == 0)
    def _():
        h_ref[...] = _rmsnorm(x_ref[...], g_ref[...], RMS_EPS).astype(BF16)
        acc_ref[...] = jnp.zeros_like(acc_ref)

    acc_ref[...] += _swiglu_partial(h_ref[...], wg_ref[...], wu_ref[...], wd_ref[...])

    @pl.when(f == pl.num_programs(1) - 1)
    def _():
        o_ref[...] = x_ref[...] + acc_ref[...]


def _dense_ffn(x, g, w_gate_up, w_down, tf):
    n, d = x.shape
    ff = w_down.shape[0]
    nf = ff // tf
    return pl.pallas_call(
        _dense_ffn_kernel,
        grid=(n // TM, nf),
        in_specs=[pl.BlockSpec((TM, d), lambda i, f: (i, 0)),
                  pl.BlockSpec((1, d), lambda i, f: (0, 0)),
                  pl.BlockSpec((d, tf), lambda i, f: (0, f)),
                  pl.BlockSpec((d, tf), lambda i, f: (0, nf + f)),
                  pl.BlockSpec((tf, d), lambda i, f: (f, 0))],
        out_specs=pl.BlockSpec((TM, d), lambda i, f: (i, 0)),
        out_shape=jax.ShapeDtypeStruct((n, d), F32),
        scratch_shapes=[pltpu.VMEM((TM, d), BF16), pltpu.VMEM((TM, d), F32)],
        compiler_params=_cparams("parallel", "arbitrary"),
        name="dense_ffn",
    )(x, g.reshape(1, d), w_gate_up, w_gate_up, w_down)


def _router_kernel(x_ref, g_ref, r_ref, h_ref, gate_ref, mask_ref, mask_t_ref, cnt_ref):
    h = _rmsnorm(x_ref[...], g_ref[...], RMS_EPS)
    h_ref[...] = h.astype(BF16)
    logits = jnp.dot(h, r_ref[...], preferred_element_type=F32, precision=lax.Precision.HIGHEST)
    lane = lax.broadcasted_iota(jnp.int32, logits.shape, 1)
    logits = jnp.where(lane < N_EXPERTS, logits, -jnp.inf)
    v1 = jnp.max(logits, axis=-1, keepdims=True)
    i1 = jnp.min(jnp.where(logits == v1, lane, LANES), axis=-1, keepdims=True)
    m1 = lane == i1
    rest = jnp.where(m1, -jnp.inf, logits)
    v2 = jnp.max(rest, axis=-1, keepdims=True)
    i2 = jnp.min(jnp.where(rest == v2, lane, LANES), axis=-1, keepdims=True)
    m2 = lane == i2
    e2 = jnp.exp(v2 - v1)
    g1 = 1.0 / (1.0 + e2)
    g2 = e2 / (1.0 + e2)
    gate_ref[...] = jnp.where(m1, g1, 0.0) + jnp.where(m2, g2, 0.0)
    mask = jnp.where(m1 | m2, 1.0, 0.0)
    mask_ref[...] = mask
    mask_t_ref[...] = mask.T[0:N_EXPERTS, :]
    cnt_ref[...] = jnp.broadcast_to(jnp.sum(mask, axis=0, keepdims=True), cnt_ref.shape)


def _router(x, g, router):
    n, d = x.shape
    t = n // TT
    r_pad = jnp.zeros((d, LANES), F32).at[:, :N_EXPERTS].set(router)
    return pl.pallas_call(
        _router_kernel,
        grid=(t,),
        in_specs=[pl.BlockSpec((TT, d), lambda i: (i, 0)),
                  pl.BlockSpec((1, d), lambda i: (0, 0)),
                  pl.BlockSpec((d, LANES), lambda i: (0, 0))],
        out_specs=[pl.BlockSpec((TT, d), lambda i: (i, 0)),
                   pl.BlockSpec((TT, LANES), lambda i: (i, 0)),
                   pl.BlockSpec((TT, LANES), lambda i: (i, 0)),
                   pl.BlockSpec((N_EXPERTS, TT), lambda i: (0, i)),
                   pl.BlockSpec((None, 8, LANES), lambda i: (i, 0, 0))],
        out_shape=[jax.ShapeDtypeStruct((n, d), BF16),
                   jax.ShapeDtypeStruct((n, LANES), F32),
                   jax.ShapeDtypeStruct((n, LANES), F32),
                   jax.ShapeDtypeStruct((N_EXPERTS, n), F32),
                   jax.ShapeDtypeStruct((t, 8, LANES), F32)],
        compiler_params=_cparams("parallel"),
        name="router",
    )(x, g.reshape(1, d), r_pad)


def _segment_copies(n_rows, make_copy):
    off = jnp.int32(0)
    size = TT
    while size >= BF16_ROWS:
        take = (n_rows & size) != 0

        @pl.when(take)
        def _(off=off, size=size):
            make_copy(off, size)

        off = off + jnp.where(take, size, 0)
        size //= 2


def _dispatch_kernel(start_ref, rows_ref, mask_t_ref, h_ref, xs_in_ref, xs_ref, blk_ref, sem_ref):
    del xs_in_ref
    t = pl.program_id(0)
    slot = t % 2
    mask_t = mask_t_ref[...]
    row = lax.broadcasted_iota(jnp.int32, (TT, TT), 0)
    col = lax.broadcasted_iota(jnp.int32, (TT, TT), 1)
    before = jnp.where(row < col, 1.0, 0.0).astype(BF16)
    rank = jnp.dot(mask_t.astype(BF16), before, preferred_element_type=F32)
    row_f = row.astype(F32)
    h = h_ref[...]
    for e in range(N_EXPERTS):
        pick = (row_f == rank[e:e + 1, :]) & (mask_t[e:e + 1, :] > 0.0)
        sel = jnp.where(pick, 1.0, 0.0).astype(BF16)
        blk_ref[slot, e] = jnp.dot(sel, h, preferred_element_type=F32).astype(BF16)

    def copies(step, step_slot, wait):
        for e in range(N_EXPERTS):
            start = pl.multiple_of(start_ref[step * N_EXPERTS + e], BF16_ROWS)

            def make_copy(off, size, e=e, start=start):
                src = blk_ref.at[step_slot, e, pl.ds(pl.multiple_of(off, BF16_ROWS), size)]
                dst = xs_ref.at[pl.ds(pl.multiple_of(start + off, BF16_ROWS), size)]
                cp = pltpu.make_async_copy(src, dst, sem_ref.at[step_slot])
                cp.wait() if wait else cp.start()

            _segment_copies(rows_ref[step * N_EXPERTS + e], make_copy)

    copies(t, slot, wait=False)

    @pl.when(t > 0)
    def _():
        copies(t - 1, 1 - slot, wait=True)

    @pl.when(t == pl.num_programs(0) - 1)
    def _():
        copies(t, slot, wait=True)


def _dispatch(seg_start, seg_rows, mask_t, h, n_rows):
    n, d = h.shape
    t = n // TT
    xs0 = jnp.zeros((n_rows, d), BF16)
    grid_spec = pltpu.PrefetchScalarGridSpec(
        num_scalar_prefetch=2,
        grid=(t,),
        in_specs=[pl.BlockSpec((N_EXPERTS, TT), lambda i, *_: (0, i)),
                  pl.BlockSpec((TT, d), lambda i, *_: (i, 0)),
                  pl.BlockSpec(memory_space=pl.ANY)],
        out_specs=pl.BlockSpec(memory_space=pl.ANY),
        scratch_shapes=[pltpu.VMEM((2, N_EXPERTS, TT, d), BF16), pltpu.SemaphoreType.DMA((2,))],
    )
    return pl.pallas_call(
        _dispatch_kernel,
        grid_spec=grid_spec,
        out_shape=jax.ShapeDtypeStruct((n_rows, d), BF16),
        input_output_aliases={4: 0},
        compiler_params=_cparams("arbitrary"),
        name="moe_dispatch",
    )(seg_start, seg_rows, mask_t, h, xs0)


def _expert_kernel(tile_expert_ref, tile_rows_ref, xs_ref, wg_ref, wu_ref, wd_ref, o_ref,
                   wg_bf_ref, wu_bf_ref, wd_bf_ref, acc_ref):
    del tile_expert_ref
    j = pl.program_id(0)
    f = pl.program_id(1)
    last = pl.num_programs(1) - 1
    rows = tile_rows_ref[j]

    @pl.when(rows > 0)
    def _():
        wg_bf_ref[...] = wg_ref[...].astype(BF16)
        wu_bf_ref[...] = wu_ref[...].astype(BF16)
        wd_bf_ref[...] = wd_ref[...].astype(BF16)

    @pl.when(f == 0)
    def _():
        acc_ref[...] = jnp.zeros_like(acc_ref)

    @pl.when(rows == TE)
    def _():
        acc_ref[...] += _swiglu_partial(xs_ref[...], wg_bf_ref[...], wu_bf_ref[...], wd_bf_ref[...])

    @pl.when((rows > 0) & (rows < TE))
    def _():
        for lo in range(0, TE, TM):
            @pl.when(rows > lo)
            def _(lo=lo):
                acc_ref[lo:lo + TM, :] += _swiglu_partial(xs_ref[lo:lo + TM, :], wg_bf_ref[...], wu_bf_ref[...],
                                                          wd_bf_ref[...])

    @pl.when(f == last)
    def _():
        o_ref[...] = acc_ref[...].astype(o_ref.dtype)


def _expert_ffn(tile_expert, tile_rows, xs, w_gate_up, w_down, layer, tf):
    n_rows, d = xs.shape
    ff = w_down.shape[2]
    nf = ff // tf

    def f_eff(j, f, tr):
        return jnp.where(tr[j] > 0, f, nf - 1)

    grid_spec = pltpu.PrefetchScalarGridSpec(
        num_scalar_prefetch=2,
        grid=(n_rows // TE, nf),
        in_specs=[pl.BlockSpec((TE, d), lambda j, f, te, tr: (j, 0)),
                  pl.BlockSpec((None, None, d, tf), lambda j, f, te, tr: (layer, te[j], 0, f_eff(j, f, tr))),
                  pl.BlockSpec((None, None, d, tf), lambda j, f, te, tr: (layer, te[j], 0, nf + f_eff(j, f, tr))),
                  pl.BlockSpec((None, None, tf, d), lambda j, f, te, tr: (layer, te[j], f_eff(j, f, tr), 0))],
        out_specs=pl.BlockSpec((TE, d), lambda j, f, te, tr: (j, 0)),
        scratch_shapes=[pltpu.VMEM((d, tf), BF16), pltpu.VMEM((d, tf), BF16), pltpu.VMEM((tf, d), BF16),
                        pltpu.VMEM((TE, d), F32)],
    )
    return pl.pallas_call(
        _expert_kernel,
        grid_spec=grid_spec,
        out_shape=jax.ShapeDtypeStruct((n_rows, d), BF16),
        compiler_params=_cparams("parallel", "arbitrary"),
        name="moe_experts",
    )(tile_expert, tile_rows, xs, w_gate_up, w_gate_up, w_down)


def _combine_kernel(start_ref, rows_ref, x_ref, gate_ref, mask_ref, out_gain_ref, ys_ref, o_ref, win_ref, sem_ref,
                    *, out_norm):
    t = pl.program_id(0)
    n_t = pl.num_programs(0)
    slot = t % 2

    def copies(step, step_slot, wait):
        for e in range(N_EXPERTS):
            start = pl.multiple_of(start_ref[step * N_EXPERTS + e], BF16_ROWS)

            def make_copy(off, size, e=e, start=start):
                src = ys_ref.at[pl.ds(pl.multiple_of(start + off, BF16_ROWS), size)]
                dst = win_ref.at[step_slot, e, pl.ds(pl.multiple_of(off, BF16_ROWS), size)]
                cp = pltpu.make_async_copy(src, dst, sem_ref.at[step_slot])
                cp.wait() if wait else cp.start()

            _segment_copies(rows_ref[step * N_EXPERTS + e], make_copy)

    @pl.when(t == 0)
    def _():
        win_ref[...] = jnp.zeros_like(win_ref)
        copies(0, 0, wait=False)

    @pl.when(t + 1 < n_t)
    def _():
        copies(t + 1, 1 - slot, wait=False)

    copies(t, slot, wait=True)

    mask = mask_ref[...]
    gate = gate_ref[...]
    row = lax.broadcasted_iota(jnp.int32, (TT, TT), 0)
    col = lax.broadcasted_iota(jnp.int32, (TT, TT), 1)
    before = jnp.where(col < row, 1.0, 0.0).astype(BF16)
    rank = jnp.dot(before, mask.astype(BF16), preferred_element_type=F32)
    col_f = col.astype(F32)
    y = x_ref[...]
    for e in range(N_EXPERTS):
        pick = (col_f == rank[:, e:e + 1]) & (mask[:, e:e + 1] > 0.0)
        sel = jnp.where(pick, 1.0, 0.0).astype(BF16)
        y = y + gate[:, e:e + 1] * jnp.dot(sel, win_ref[slot, e], preferred_element_type=F32)
    o_ref[...] = _rmsnorm(y, out_gain_ref[...], RMS_EPS) if out_norm else y


def _combine(seg_start, seg_rows, x, gate, mask, ys, out_gain, out_norm):
    n, d = x.shape
    t = n // TT
    grid_spec = pltpu.PrefetchScalarGridSpec(
        num_scalar_prefetch=2,
        grid=(t,),
        in_specs=[pl.BlockSpec((TT, d), lambda i, *_: (i, 0)),
                  pl.BlockSpec((TT, LANES), lambda i, *_: (i, 0)),
                  pl.BlockSpec((TT, LANES), lambda i, *_: (i, 0)),
                  pl.BlockSpec((1, d), lambda i, *_: (0, 0)),
                  pl.BlockSpec(memory_space=pl.ANY)],
        out_specs=pl.BlockSpec((TT, d), lambda i, *_: (i, 0)),
        scratch_shapes=[pltpu.VMEM((2, N_EXPERTS, TT, d), BF16), pltpu.SemaphoreType.DMA((2,))],
    )
    return pl.pallas_call(
        functools.partial(_combine_kernel, out_norm=out_norm),
        grid_spec=grid_spec,
        out_shape=jax.ShapeDtypeStruct((n, d), F32),
        compiler_params=_cparams("arbitrary"),
        name="moe_combine",
    )(seg_start, seg_rows, x, gate, mask, out_gain.reshape(1, d), ys)


def _moe_ffn(x, g, router, w_gate_up, w_down, layer, tf, out_gain, out_norm):
    n, d = x.shape
    t = n // TT
    h, gate, mask, mask_t, cnt = _router(x, g, router)
    cnt = cnt[:, 0, :N_EXPERTS].astype(jnp.int32)
    seg_rows = _round_up(cnt, BF16_ROWS)
    expert_rows = jnp.sum(seg_rows, axis=0)
    region = _round_up(expert_rows, TE)
    region_end = jnp.cumsum(region)
    seg_start = (region_end - region)[None, :] + jnp.cumsum(seg_rows, axis=0) - seg_rows
    n_tiles = (_round_up(2 * n + t * N_EXPERTS * (BF16_ROWS - 1), TE) + N_EXPERTS * TE) // TE
    tile_end = region_end // TE
    tile_ids = jnp.arange(n_tiles, dtype=jnp.int32)
    last_tile = jnp.minimum(tile_ids, tile_end[-1] - 1)
    tile_expert = jnp.sum((last_tile[:, None] >= tile_end[None, :]).astype(jnp.int32), axis=1)
    tile_expert = jnp.minimum(tile_expert, N_EXPERTS - 1)
    rows_before = (tile_ids - (tile_end - region // TE)[tile_expert]) * TE
    tile_rows = jnp.where(tile_ids < tile_end[-1], jnp.clip(expert_rows[tile_expert] - rows_before, 0, TE), 0)
    seg_start = seg_start.reshape(-1).astype(jnp.int32)
    seg_rows = seg_rows.reshape(-1).astype(jnp.int32)

    xs = _dispatch(seg_start, seg_rows, mask_t, h, n_tiles * TE)
    ys = _expert_ffn(tile_expert, tile_rows.astype(jnp.int32), xs, w_gate_up, w_down, layer, tf)
    return _combine(seg_start, seg_rows, x, gate, mask, ys, out_gain, out_norm)


def kernel(x, attn_norm, ffn_norm, final_norm, t5_rel_bias, diff_w_qkv, diff_w_o, diff_lambda_q1, diff_lambda_k1, diff_lambda_q2, diff_lambda_k2, diff_subln, chunk_w_qkv, chunk_w_o, chunk_rel_bias, dense_w_gate_up, dense_w_down, moe_router, moe_w_gate_up, moe_w_down):
    b, s, d = x.shape
    n = b * s
    xf = x.reshape(n, d)
    t5_bias, t5_const_rows = _t5_bias_t(t5_rel_bias, s)
    for i in range(DEPTH):
        j = i // 2
        if i % 2 == 0:
            qkv = _norm_matmul(xf, attn_norm[i], diff_w_qkv[j].astype(BF16))
            lam_init = 0.8 - 0.6 * math.exp(-0.3 * i)
            lam_vecs = jnp.stack([diff_lambda_q1[j], diff_lambda_k1[j], diff_lambda_q2[j], diff_lambda_k2[j]])
            a = _diff_attention(qkv.reshape(b, s, 3 * d), t5_bias, t5_const_rows, lam_vecs, diff_subln[j], lam_init)
            xf = _matmul_residual(a.reshape(n, d), diff_w_o[j].astype(BF16), xf)
            xf = _dense_ffn(xf, ffn_norm[i], dense_w_gate_up[j].astype(BF16), dense_w_down[j].astype(BF16),
                            tf=dense_w_down.shape[1] // 2)
        else:
            qkv = _norm_matmul(xf, attn_norm[i], chunk_w_qkv[j].astype(BF16))
            a = _chunk_attention(qkv.reshape(b, s, 3 * d), _band_bias_t(chunk_rel_bias[j]))
            xf = _matmul_residual(a.reshape(n, d), chunk_w_o[j].astype(BF16), xf)
            xf = _moe_ffn(xf, ffn_norm[i], moe_router[j], moe_w_gate_up, moe_w_down, layer=j,
                          tf=moe_w_down.shape[2] // 7,
                          out_gain=final_norm, out_norm=i == DEPTH - 1)
    return xf.reshape(b, s, d)
```

```python
import functools
import math

import numpy as np
import jax
import jax.numpy as jnp
from jax import lax
from jax.experimental import pallas as pl
from jax.experimental.pallas import tpu as pltpu

F32 = jnp.float32
BF16 = jnp.bfloat16

D_MODEL = 1024
DEPTH = 4
CHUNK = 64
DA_HEADS = 8
DA_HEAD_DIM = 64
T5_BUCKETS = 32
T5_MAX_DIST = 128
CA_HEADS = 16
CA_HEAD_DIM = 64
BAND_PREV = 8
REL_CLIP = 4 * CHUNK
N_EXPERTS = 8
RMS_EPS = 1e-6
SUBLN_EPS = 1e-5
NEG_INF = -1e30

LANES = 128
BF16_ROWS = 16
VMEM_LIMIT = 56 * 1024 * 1024
TM = 512
TE = 4 * TM
TQ = 256
TT = 256
KB_T5 = 128
KB_BAND = 256
N_SLOTS = 4
ONES_ROWS = BF16_ROWS
LOG2E = math.log2(math.e)
Q_SCALE = DA_HEAD_DIM ** -0.5 * LOG2E
PAIR = 2 * CA_HEAD_DIM


def _round_up(a, m):
    return (a + m - 1) // m * m


def _cparams(*sem):
    return pltpu.CompilerParams(dimension_semantics=sem, vmem_limit_bytes=VMEM_LIMIT)


def _rmsnorm(x32, g, eps):
    return x32 * lax.rsqrt(jnp.mean(x32 * x32, axis=-1, keepdims=True) + eps) * g


def _dot_nt(a, b):
    return lax.dot_general(a, b, (((1,), (1,)), ((), ())), preferred_element_type=F32)


def _norm_matmul_kernel(x_ref, g_ref, w_ref, o_ref):
    h = _rmsnorm(x_ref[...], g_ref[...], RMS_EPS).astype(BF16)
    d = x_ref.shape[1]
    q = jnp.dot(h, w_ref[:, :d], preferred_element_type=F32) * Q_SCALE
    o_ref[:, :d] = q.astype(o_ref.dtype)
    o_ref[:, d:] = jnp.dot(h, w_ref[:, d:], preferred_element_type=F32).astype(o_ref.dtype)


def _norm_matmul(x, g, w):
    n, d = x.shape
    n_out = w.shape[1]
    return pl.pallas_call(
        _norm_matmul_kernel,
        grid=(n // TM,),
        in_specs=[pl.BlockSpec((TM, d), lambda i: (i, 0)),
                  pl.BlockSpec((1, d), lambda i: (0, 0)),
                  pl.BlockSpec((d, n_out), lambda i: (0, 0))],
        out_specs=pl.BlockSpec((TM, n_out), lambda i: (i, 0)),
        out_shape=jax.ShapeDtypeStruct((n, n_out), BF16),
        compiler_params=_cparams("parallel"),
        name="norm_qkv",
    )(x, g.reshape(1, d), w)


def _matmul_residual_kernel(a_ref, w_ref, x_ref, o_ref):
    o_ref[...] = x_ref[...] + jnp.dot(a_ref[...], w_ref[...], preferred_element_type=F32)


def _matmul_residual(a, w, x):
    n, d = x.shape
    return pl.pallas_call(
        _matmul_residual_kernel,
        grid=(n // TM,),
        in_specs=[pl.BlockSpec((TM, a.shape[1]), lambda i: (i, 0)),
                  pl.BlockSpec(w.shape, lambda i: (0, 0)),
                  pl.BlockSpec((TM, d), lambda i: (i, 0))],
        out_specs=pl.BlockSpec((TM, d), lambda i: (i, 0)),
        out_shape=jax.ShapeDtypeStruct((n, d), F32),
        compiler_params=_cparams("parallel"),
        name="wo_residual",
    )(a, w, x)


def _t5_bucket_np(rel):
    nb = T5_BUCKETS // 2
    max_exact = nb // 2
    ret = (rel > 0).astype(np.int32) * nb
    n = np.abs(rel)
    n_f = np.maximum(n, 1).astype(np.float32)
    large = max_exact + (np.log(n_f / np.float32(max_exact)) / np.float32(math.log(T5_MAX_DIST / max_exact))
                         * np.float32(nb - max_exact)).astype(np.int32)
    large = np.minimum(large, nb - 1)
    return ret + np.where(n < max_exact, n, large)


def _lookup_kernel(idx_ref, t_ref, o_ref):
    rows = lax.broadcasted_iota(jnp.int32, (t_ref.shape[1], idx_ref.shape[1]), 0)
    onehot = jnp.where(rows == idx_ref[...], 1.0, 0.0)
    o_ref[...] = jnp.dot(t_ref[...], onehot, preferred_element_type=F32, precision=lax.Precision.HIGHEST)


def _table_lookup_t(table, idx):
    r, h = table.shape
    table_t = jnp.zeros((h, _round_up(r, LANES)), F32).at[:, :r].set(table.T)
    return pl.pallas_call(
        _lookup_kernel,
        out_shape=jax.ShapeDtypeStruct((h, len(idx)), F32),
        name="bias_lookup",
    )(jnp.asarray(idx, jnp.int32).reshape(1, -1), table_t)


def _toeplitz_kernel(win_ref, mask_ref, o_ref):
    for blk in range(o_ref.shape[0] // TQ):
        x = jnp.broadcast_to(win_ref[:, blk * 2 * TQ:(blk + 1) * 2 * TQ], (TQ, 2 * TQ))
        skew = pltpu.roll(x, 0, 1, stride=1, stride_axis=0)
        o_ref[blk * TQ:(blk + 1) * TQ, :] = skew[:, :TQ] + mask_ref[blk * TQ:(blk + 1) * TQ, :]


def _window_diagonals(n_keys):
    m = np.arange(2 * TQ)
    back = np.where(m < TQ, -m, 2 * TQ - m)
    d = np.arange(0, n_keys, TQ)[:, None] + back[None, :]
    return np.clip(d, 1 - TQ, n_keys - 1).reshape(-1)


def _toeplitz_bias_t(table, idx, allowed):
    win = _table_lookup_t(table, idx) * LOG2E
    h = win.shape[0]
    n_keys = allowed.shape[0]
    mask_add = jnp.asarray(np.where(allowed, 0.0, NEG_INF).astype(np.float32))
    return pl.pallas_call(
        _toeplitz_kernel,
        grid=(h,),
        in_specs=[pl.BlockSpec((None, 1, win.shape[1]), lambda i: (i, 0, 0)),
                  pl.BlockSpec((n_keys, TQ), lambda i: (0, 0))],
        out_specs=pl.BlockSpec((None, n_keys, TQ), lambda i: (i, 0, 0)),
        out_shape=jax.ShapeDtypeStruct((h, n_keys, TQ), F32),
        compiler_params=_cparams("parallel"),
        name="toeplitz_bias",
    )(win.reshape(h, 1, -1), mask_add)


def _t5_bias_t(table, seq):
    idx = _t5_bucket_np(_window_diagonals(seq) - (seq - TQ))
    i = np.arange(TQ)[None, :]
    j = np.arange(seq)[:, None] - (seq - TQ)
    allowed = (j < 0) | ((j // CHUNK) <= (i // CHUNK))
    far_bucket = _t5_bucket_np(np.asarray(-seq))
    is_far = np.all((_t5_bucket_np(j - i) == far_bucket) & allowed, axis=1)
    const_rows = int(np.argmin(is_far)) if not is_far.all() else seq
    return _toeplitz_bias_t(table, idx, allowed), const_rows


def _load_vt(v_ref, vt_ref):
    n_v = v_ref.shape[1]
    vt_ref[0:n_v, :] = v_ref[...].T
    vt_ref[n_v:, :] = jnp.ones((ONES_ROWS, v_ref.shape[0]), BF16)


def _attend_pair_steps(k_ref, k0, n_k, kb_rows, q, lane_split, bias_rows, bias_const, vt_ref, s_ref, p_ref, finish):
    zero = jnp.zeros_like(q)
    qq = jnp.concatenate([jnp.where(lane_split, q, zero), jnp.where(lane_split, zero, q)], axis=0)
    s_ref[0:n_k, :] = _dot_nt(k_ref[k0:k0 + n_k, :], qq)
    yield 1
    halves = (slice(0, TQ), slice(TQ, 2 * TQ))
    m = [None, None]
    for kb in range(0, n_k, kb_rows):
        for h, cols in enumerate(halves):
            if bias_const(kb):
                block_max = jnp.max(s_ref[kb:kb + kb_rows, cols], axis=0, keepdims=True) + bias_rows(h, kb)
            else:
                s = s_ref[kb:kb + kb_rows, cols] + bias_rows(h, kb)
                s_ref[kb:kb + kb_rows, cols] = s
                block_max = jnp.max(s, axis=0, keepdims=True)
            m[h] = block_max if m[h] is None else jnp.maximum(m[h], block_max)
            yield 1
    for kb in range(0, n_k, kb_rows):
        for h, cols in enumerate(halves):
            shift = m[h] - bias_rows(h, kb) if bias_const(kb) else m[h]
            p_ref[kb:kb + kb_rows, cols] = jnp.exp2(s_ref[kb:kb + kb_rows, cols] - shift).astype(BF16)
            yield 2
    o_t = jnp.dot(vt_ref[:, k0:k0 + n_k], p_ref[0:n_k, :], preferred_element_type=F32)
    n_v = vt_ref.shape[0] - ONES_ROWS
    finish([o_t[0:n_v, cols] * (1.0 / o_t[n_v:n_v + 1, cols]) for cols in halves])
    yield 2


def _interleave(tiles):
    pending = list(tiles)
    active = []
    while pending or active:
        if pending and len(active) < 2 and all(stage == 2 for _, stage in active):
            active.append([pending.pop(0), 1])
        for item in list(active):
            try:
                item[1] = next(item[0])
            except StopIteration:
                active.remove(item)


def _diff_attn_kernel(q_ref, k_ref, v_ref, bias_ref, lam_ref, g_ref, o_ref, vt_ref, s_ref, p_ref, *,
                      lam_init, const_rows):
    lv = lam_ref[...]
    lam = (jnp.exp(jnp.sum(lv[0:1] * lv[1:2], axis=-1, keepdims=True))
           - jnp.exp(jnp.sum(lv[2:3] * lv[3:4], axis=-1, keepdims=True)) + lam_init)
    first_map = lax.broadcasted_iota(jnp.int32, (1, 2 * DA_HEAD_DIM), 1) < DA_HEAD_DIM
    seq = q_ref.shape[0]
    _load_vt(v_ref, vt_ref)

    def tile(qi):
        n_k = (qi + 1) * TQ
        row0 = seq - n_k

        def bias_const(kb):
            return row0 + kb + KB_T5 <= const_rows

        def bias_rows(h, kb):
            n = 1 if bias_const(kb) else KB_T5
            return bias_ref[row0 + kb:row0 + kb + n, :]

        def finish(outs):
            o_t = outs[0] - lam * outs[1]
            o_t = o_t * lax.rsqrt(jnp.mean(o_t * o_t, axis=0, keepdims=True) + SUBLN_EPS) * g_ref[...]
            o_ref[qi * TQ:(qi + 1) * TQ, :] = (o_t * (1.0 - lam_init)).T.astype(o_ref.dtype)

        slot = qi % N_SLOTS
        return _attend_pair_steps(k_ref, 0, n_k, KB_T5, q_ref[qi * TQ:(qi + 1) * TQ, :], first_map, bias_rows, bias_const,
                                  vt_ref, s_ref.at[slot], p_ref.at[slot], finish)

    _interleave(tile(qi) for qi in range(seq // TQ))


def _diff_attention(qkv, bias_t, const_rows, lam_vecs, subln_g, lam_init):
    b, s, _ = qkv.shape
    hd = 2 * DA_HEAD_DIM
    return pl.pallas_call(
        functools.partial(_diff_attn_kernel, lam_init=lam_init, const_rows=const_rows),
        grid=(b, DA_HEADS),
        in_specs=[pl.BlockSpec((None, s, hd), lambda i, h: (i, 0, h)),
                  pl.BlockSpec((None, s, hd), lambda i, h: (i, 0, DA_HEADS + h)),
                  pl.BlockSpec((None, s, hd), lambda i, h: (i, 0, 2 * DA_HEADS + h)),
                  pl.BlockSpec((None, s, TQ), lambda i, h: (h, 0, 0)),
                  pl.BlockSpec((4, DA_HEAD_DIM), lambda i, h: (0, 0)),
                  pl.BlockSpec((hd, 1), lambda i, h: (0, 0))],
        out_specs=pl.BlockSpec((None, s, hd), lambda i, h: (i, 0, h)),
        out_shape=jax.ShapeDtypeStruct((b, s, D_MODEL), BF16),
        scratch_shapes=[pltpu.VMEM((hd + ONES_ROWS, s), BF16), pltpu.VMEM((N_SLOTS, s, 2 * TQ), F32),
                        pltpu.VMEM((N_SLOTS, s, 2 * TQ), BF16)],
        compiler_params=_cparams("parallel", "parallel"),
        name="diff_attn",
    )(qkv, qkv, qkv, bias_t, lam_vecs, subln_g.reshape(hd, 1))


BAND_KEYS = BAND_PREV * CHUNK + TQ


def _band_bias_t(rel_table):
    dist = BAND_PREV * CHUNK - _window_diagonals(BAND_KEYS)
    q_chunk = np.arange(TQ)[None, :] // CHUNK
    k_chunk = np.arange(BAND_KEYS)[:, None] // CHUNK
    allowed = (k_chunk >= q_chunk) & (k_chunk <= q_chunk + BAND_PREV)
    return _toeplitz_bias_t(rel_table, np.clip(dist, -REL_CLIP, REL_CLIP) + REL_CLIP, allowed)


def _chunk_attn_kernel(q_ref, k_ref, v_ref, bias_ref, o_ref, vt_ref, s_ref, p_ref):
    first_head = lax.broadcasted_iota(jnp.int32, (1, PAIR), 1) < CA_HEAD_DIM
    first_head_rows = lax.broadcasted_iota(jnp.int32, (PAIR, 1), 0) < CA_HEAD_DIM
    seq = q_ref.shape[0]
    _load_vt(v_ref, vt_ref)

    def tile(qi):
        q0 = qi * TQ
        k0 = max(0, q0 - BAND_PREV * CHUNK)
        n_k = q0 + TQ - k0
        row0 = BAND_KEYS - n_k

        def bias_rows(h, kb):
            return bias_ref[h, row0 + kb:row0 + kb + KB_BAND, :]

        def finish(outs):
            o_ref[q0:q0 + TQ, :] = jnp.where(first_head_rows, outs[0], outs[1]).T.astype(o_ref.dtype)

        slot = qi % N_SLOTS
        return _attend_pair_steps(k_ref, k0, n_k, KB_BAND, q_ref[q0:q0 + TQ, :], first_head, bias_rows, lambda kb: False,
                                  vt_ref, s_ref.at[slot], p_ref.at[slot], finish)

    _interleave(tile(qi) for qi in range(seq // TQ))


def _chunk_attention(qkv, band_bias_t):
    b, s, _ = qkv.shape
    n_pairs = CA_HEADS // 2
    return pl.pallas_call(
        _chunk_attn_kernel,
        grid=(b, n_pairs),
        in_specs=[pl.BlockSpec((None, s, PAIR), lambda i, p: (i, 0, p)),
                  pl.BlockSpec((None, s, PAIR), lambda i, p: (i, 0, n_pairs + p)),
                  pl.BlockSpec((None, s, PAIR), lambda i, p: (i, 0, 2 * n_pairs + p)),
                  pl.BlockSpec((2, BAND_KEYS, TQ), lambda i, p: (p, 0, 0))],
        out_specs=pl.BlockSpec((None, s, PAIR), lambda i, p: (i, 0, p)),
        out_shape=jax.ShapeDtypeStruct((b, s, D_MODEL), BF16),
        scratch_shapes=[pltpu.VMEM((PAIR + ONES_ROWS, s), BF16), pltpu.VMEM((N_SLOTS, BAND_KEYS, 2 * TQ), F32),
                        pltpu.VMEM((N_SLOTS, BAND_KEYS, 2 * TQ), BF16)],
        compiler_params=_cparams("parallel", "parallel"),
        name="chunk_attn",
    )(qkv, qkv, qkv, band_bias_t)


def _swiglu_partial(h, wg, wu, wd):
    g = jnp.dot(h, wg, preferred_element_type=F32)
    u = jnp.dot(h, wu, preferred_element_type=F32)
    a = (g * (1.0 / (1.0 + jnp.exp(-g))) * u).astype(BF16)
    return jnp.dot(a, wd, preferred_element_type=F32)


def _dense_ffn_kernel(x_ref, g_ref, wg_ref, wu_ref, wd_ref, o_ref, h_ref, acc_ref):
    f = pl.program_id(1)

    @pl.when(f == 0)
    def _():
        h_ref[...] = _rmsnorm(x_ref[...], g_ref[...], RMS_EPS).astype(BF16)
        acc_ref[...] = jnp.zeros_like(acc_ref)

    acc_ref[...] += _swiglu_partial(h_ref[...], wg_ref[...], wu_ref[...], wd_ref[...])

    @pl.when(f == pl.num_programs(1) - 1)
    def _():
        o_ref[...] = x_ref[...] + acc_ref[...]


def _dense_ffn(x, g, w_gate_up, w_down, tf):
    n, d = x.shape
    ff = w_down.shape[0]
    nf = ff // tf
    return pl.pallas_call(
        _dense_ffn_kernel,
        grid=(n // TM, nf),
        in_specs=[pl.BlockSpec((TM, d), lambda i, f: (i, 0)),
                  pl.BlockSpec((1, d), lambda i, f: (0, 0)),
                  pl.BlockSpec((d, tf), lambda i, f: (0, f)),
                  pl.BlockSpec((d, tf), lambda i, f: (0, nf + f)),
                  pl.BlockSpec((tf, d), lambda i, f: (f, 0))],
        out_specs=pl.BlockSpec((TM, d), lambda i, f: (i, 0)),
        out_shape=jax.ShapeDtypeStruct((n, d), F32),
        scratch_shapes=[pltpu.VMEM((TM, d), BF16), pltpu.VMEM((TM, d), F32)],
        compiler_params=_cparams("parallel", "arbitrary"),
        name="dense_ffn",
    )(x, g.reshape(1, d), w_gate_up, w_gate_up, w_down)


def _router_kernel(x_ref, g_ref, r_ref, h_ref, gate_ref, mask_ref, mask_t_ref, cnt_ref):
    h = _rmsnorm(x_ref[...], g_ref[...], RMS_EPS)
    h_hi = h.astype(BF16)
    h_ref[...] = h_hi
    r = r_ref[...]
    r_hi = r.astype(BF16)
    r_lo = (r - r_hi.astype(F32)).astype(BF16)
    h_lo = (h - h_hi.astype(F32)).astype(BF16)
    logits = (jnp.dot(h_hi, r_hi, preferred_element_type=F32)
              + (jnp.dot(h_hi, r_lo, preferred_element_type=F32) + jnp.dot(h_lo, r_hi, preferred_element_type=F32)))
    lane = lax.broadcasted_iota(jnp.int32, logits.shape, 1)
    logits = jnp.where(lane < N_EXPERTS, logits, -jnp.inf)
    v1 = jnp.max(logits, axis=-1, keepdims=True)
    i1 = jnp.min(jnp.where(logits == v1, lane, LANES), axis=-1, keepdims=True)
    m1 = lane == i1
    rest = jnp.where(m1, -jnp.inf, logits)
    v2 = jnp.max(rest, axis=-1, keepdims=True)
    i2 = jnp.min(jnp.where(rest == v2, lane, LANES), axis=-1, keepdims=True)
    m2 = lane == i2
    e2 = jnp.exp(v2 - v1)
    g1 = 1.0 / (1.0 + e2)
    g2 = e2 / (1.0 + e2)
    gate_ref[...] = jnp.where(m1, g1, 0.0) + jnp.where(m2, g2, 0.0)
    mask = jnp.where(m1 | m2, 1.0, 0.0)
    mask_ref[...] = mask
    mask_t_ref[...] = mask.T[0:N_EXPERTS, :]
    cnt_ref[...] = jnp.broadcast_to(jnp.sum(mask, axis=0, keepdims=True), cnt_ref.shape)


def _router(x, g, router):
    n, d = x.shape
    t = n // TT
    r_pad = jnp.zeros((d, LANES), F32).at[:, :N_EXPERTS].set(router)
    return pl.pallas_call(
        _router_kernel,
        grid=(t,),
        in_specs=[pl.BlockSpec((TT, d), lambda i: (i, 0)),
                  pl.BlockSpec((1, d), lambda i: (0, 0)),
                  pl.BlockSpec((d, LANES), lambda i: (0, 0))],
        out_specs=[pl.BlockSpec((TT, d), lambda i: (i, 0)),
                   pl.BlockSpec((TT, LANES), lambda i: (i, 0)),
                   pl.BlockSpec((TT, LANES), lambda i: (i, 0)),
                   pl.BlockSpec((N_EXPERTS, TT), lambda i: (0, i)),
                   pl.BlockSpec((None, 8, LANES), lambda i: (i, 0, 0))],
        out_shape=[jax.ShapeDtypeStruct((n, d), BF16),
                   jax.ShapeDtypeStruct((n, LANES), F32),
                   jax.ShapeDtypeStruct((n, LANES), F32),
                   jax.ShapeDtypeStruct((N_EXPERTS, n), F32),
                   jax.ShapeDtypeStruct((t, 8, LANES), F32)],
        compiler_params=_cparams("parallel"),
        name="router",
    )(x, g.reshape(1, d), r_pad)


def _segment_copies(n_rows, make_copy):
    off = jnp.int32(0)
    size = TT
    while size >= BF16_ROWS:
        take = (n_rows & size) != 0

        @pl.when(take)
        def _(off=off, size=size):
            make_copy(off, size)

        off = off + jnp.where(take, size, 0)
        size //= 2


def _dispatch_kernel(start_ref, rows_ref, mask_t_ref, h_ref, xs_in_ref, xs_ref, blk_ref, sem_ref):
    del xs_in_ref
    t = pl.program_id(0)
    slot = t % 2
    mask_t = mask_t_ref[...]
    row = lax.broadcasted_iota(jnp.int32, (TT, TT), 0)
    col = lax.broadcasted_iota(jnp.int32, (TT, TT), 1)
    before = jnp.where(row < col, 1.0, 0.0).astype(BF16)
    rank = jnp.dot(mask_t.astype(BF16), before, preferred_element_type=F32)
    row_f = row.astype(F32)
    sels = []
    for e in range(N_EXPERTS):
        pick = (row_f == rank[e:e + 1, :]) & (mask_t[e:e + 1, :] > 0.0)
        sels.append(jnp.where(pick, 1.0, 0.0).astype(BF16))
    packed = jnp.dot(jnp.concatenate(sels, axis=0), h_ref[...], preferred_element_type=F32)
    blk_ref[slot] = packed.astype(BF16).reshape(blk_ref.shape[1:])

    def copies(step, step_slot, wait):
        for e in range(N_EXPERTS):
            start = pl.multiple_of(start_ref[step * N_EXPERTS + e], BF16_ROWS)

            def make_copy(off, size, e=e, start=start):
                src = blk_ref.at[step_slot, e, pl.ds(pl.multiple_of(off, BF16_ROWS), size)]
                dst = xs_ref.at[pl.ds(pl.multiple_of(start + off, BF16_ROWS), size)]
                cp = pltpu.make_async_copy(src, dst, sem_ref.at[step_slot])
                cp.wait() if wait else cp.start()

            _segment_copies(rows_ref[step * N_EXPERTS + e], make_copy)

    copies(t, slot, wait=False)

    @pl.when(t > 0)
    def _():
        copies(t - 1, 1 - slot, wait=True)

    @pl.when(t == pl.num_programs(0) - 1)
    def _():
        copies(t, slot, wait=True)


def _dispatch(seg_start, seg_rows, mask_t, h, n_rows):
    n, d = h.shape
    t = n // TT
    xs0 = jnp.zeros((n_rows, d), BF16)
    grid_spec = pltpu.PrefetchScalarGridSpec(
        num_scalar_prefetch=2,
        grid=(t,),
        in_specs=[pl.BlockSpec((N_EXPERTS, TT), lambda i, *_: (0, i)),
                  pl.BlockSpec((TT, d), lambda i, *_: (i, 0)),
                  pl.BlockSpec(memory_space=pl.ANY)],
        out_specs=pl.BlockSpec(memory_space=pl.ANY),
        scratch_shapes=[pltpu.VMEM((2, N_EXPERTS, TT, d), BF16), pltpu.SemaphoreType.DMA((2,))],
    )
    return pl.pallas_call(
        _dispatch_kernel,
        grid_spec=grid_spec,
        out_shape=jax.ShapeDtypeStruct((n_rows, d), BF16),
        input_output_aliases={4: 0},
        compiler_params=_cparams("arbitrary"),
        name="moe_dispatch",
    )(seg_start, seg_rows, mask_t, h, xs0)


def _expert_kernel(tile_expert_ref, tile_rows_ref, xs_ref, wg_ref, wu_ref, wd_ref, o_ref,
                   wg_bf_ref, wu_bf_ref, wd_bf_ref, acc_ref):
    del tile_expert_ref
    j = pl.program_id(0)
    f = pl.program_id(1)
    last = pl.num_programs(1) - 1
    rows = tile_rows_ref[j]

    @pl.when(rows > 0)
    def _():
        wg_bf_ref[...] = wg_ref[...].astype(BF16)
        wu_bf_ref[...] = wu_ref[...].astype(BF16)
        wd_bf_ref[...] = wd_ref[...].astype(BF16)

    @pl.when(f == 0)
    def _():
        acc_ref[...] = jnp.zeros_like(acc_ref)

    @pl.when(rows == TE)
    def _():
        acc_ref[...] += _swiglu_partial(xs_ref[...], wg_bf_ref[...], wu_bf_ref[...], wd_bf_ref[...])

    @pl.when((rows > 0) & (rows < TE))
    def _():
        for lo in range(0, TE, TM):
            @pl.when(rows > lo)
            def _(lo=lo):
                acc_ref[lo:lo + TM, :] += _swiglu_partial(xs_ref[lo:lo + TM, :], wg_bf_ref[...], wu_bf_ref[...],
                                                          wd_bf_ref[...])

    @pl.when(f == last)
    def _():
        o_ref[...] = acc_ref[...].astype(o_ref.dtype)


def _expert_ffn(tile_expert, tile_rows, xs, w_gate_up, w_down, layer, tf):
    n_rows, d = xs.shape
    ff = w_down.shape[2]
    nf = ff // tf

    def f_eff(j, f, tr):
        return jnp.where(tr[j] > 0, f, nf - 1)

    grid_spec = pltpu.PrefetchScalarGridSpec(
        num_scalar_prefetch=2,
        grid=(n_rows // TE, nf),
        in_specs=[pl.BlockSpec((TE, d), lambda j, f, te, tr: (j, 0)),
                  pl.BlockSpec((None, None, d, tf), lambda j, f, te, tr: (layer, te[j], 0, f_eff(j, f, tr))),
                  pl.BlockSpec((None, None, d, tf), lambda j, f, te, tr: (layer, te[j], 0, nf + f_eff(j, f, tr))),
                  pl.BlockSpec((None, None, tf, d), lambda j, f, te, tr: (layer, te[j], f_eff(j, f, tr), 0))],
        out_specs=pl.BlockSpec((TE, d), lambda j, f, te, tr: (j, 0)),
        scratch_shapes=[pltpu.VMEM((d, tf), BF16), pltpu.VMEM((d, tf), BF16), pltpu.VMEM((tf, d), BF16),
                        pltpu.VMEM((TE, d), F32)],
    )
    return pl.pallas_call(
        _expert_kernel,
        grid_spec=grid_spec,
        out_shape=jax.ShapeDtypeStruct((n_rows, d), BF16),
        compiler_params=_cparams("parallel", "arbitrary"),
        name="moe_experts",
    )(tile_expert, tile_rows, xs, w_gate_up, w_gate_up, w_down)


def _combine_kernel(start_ref, rows_ref, x_ref, gate_ref, mask_ref, out_gain_ref, ys_ref, o_ref, win_ref, sem_ref,
                    *, out_norm):
    t = pl.program_id(0)
    n_t = pl.num_programs(0)
    slot = t % 2

    def copies(step, step_slot, wait):
        for e in range(N_EXPERTS):
            start = pl.multiple_of(start_ref[step * N_EXPERTS + e], BF16_ROWS)

            def make_copy(off, size, e=e, start=start):
                src = ys_ref.at[pl.ds(pl.multiple_of(start + off, BF16_ROWS), size)]
                dst = win_ref.at[step_slot, e, pl.ds(pl.multiple_of(off, BF16_ROWS), size)]
                cp = pltpu.make_async_copy(src, dst, sem_ref.at[step_slot])
                cp.wait() if wait else cp.start()

            _segment_copies(rows_ref[step * N_EXPERTS + e], make_copy)

    @pl.when(t == 0)
    def _():
        win_ref[...] = jnp.zeros_like(win_ref)
        copies(0, 0, wait=False)

    @pl.when(t + 1 < n_t)
    def _():
        copies(t + 1, 1 - slot, wait=False)

    copies(t, slot, wait=True)

    mask = mask_ref[...]
    gate = gate_ref[...]
    row = lax.broadcasted_iota(jnp.int32, (TT, TT), 0)
    col = lax.broadcasted_iota(jnp.int32, (TT, TT), 1)
    before = jnp.where(col < row, 1.0, 0.0).astype(BF16)
    rank = jnp.dot(before, mask.astype(BF16), preferred_element_type=F32)
    col_f = col.astype(F32)
    y = x_ref[...]
    for e in range(N_EXPERTS):
        pick = (col_f == rank[:, e:e + 1]) & (mask[:, e:e + 1] > 0.0)
        sel = jnp.where(pick, 1.0, 0.0).astype(BF16)
        y = y + gate[:, e:e + 1] * jnp.dot(sel, win_ref[slot, e], preferred_element_type=F32)
    o_ref[...] = _rmsnorm(y, out_gain_ref[...], RMS_EPS) if out_norm else y


def _combine(seg_start, seg_rows, x, gate, mask, ys, out_gain, out_norm):
    n, d = x.shape
    t = n // TT
    grid_spec = pltpu.PrefetchScalarGridSpec(
        num_scalar_prefetch=2,
        grid=(t,),
        in_specs=[pl.BlockSpec((TT, d), lambda i, *_: (i, 0)),
                  pl.BlockSpec((TT, LANES), lambda i, *_: (i, 0)),
                  pl.BlockSpec((TT, LANES), lambda i, *_: (i, 0)),
                  pl.BlockSpec((1, d), lambda i, *_: (0, 0)),
                  pl.BlockSpec(memory_space=pl.ANY)],
        out_specs=pl.BlockSpec((TT, d), lambda i, *_: (i, 0)),
        scratch_shapes=[pltpu.VMEM((2, N_EXPERTS, TT, d), BF16), pltpu.SemaphoreType.DMA((2,))],
    )
    return pl.pallas_call(
        functools.partial(_combine_kernel, out_norm=out_norm),
        grid_spec=grid_spec,
        out_shape=jax.ShapeDtypeStruct((n, d), F32),
        compiler_params=_cparams("arbitrary"),
        name="moe_combine",
    )(seg_start, seg_rows, x, gate, mask, out_gain.reshape(1, d), ys)


def _moe_ffn(x, g, router, w_gate_up, w_down, layer, tf, out_gain, out_norm):
    n, d = x.shape
    t = n // TT
    h, gate, mask, mask_t, cnt = _router(x, g, router)
    cnt = cnt[:, 0, :N_EXPERTS].astype(jnp.int32)
    seg_rows = _round_up(cnt, BF16_ROWS)
    expert_rows = jnp.sum(seg_rows, axis=0)
    region = _round_up(expert_rows, TE)
    region_end = jnp.cumsum(region)
    seg_start = (region_end - region)[None, :] + jnp.cumsum(seg_rows, axis=0) - seg_rows
    n_tiles = (_round_up(2 * n + t * N_EXPERTS * (BF16_ROWS - 1), TE) + N_EXPERTS * TE) // TE
    tile_end = region_end // TE
    tile_ids = jnp.arange(n_tiles, dtype=jnp.int32)
    last_tile = jnp.minimum(tile_ids, tile_end[-1] - 1)
    tile_expert = jnp.sum((last_tile[:, None] >= tile_end[None, :]).astype(jnp.int32), axis=1)
    tile_expert = jnp.minimum(tile_expert, N_EXPERTS - 1)
    rows_before = (tile_ids - (tile_end - region // TE)[tile_expert]) * TE
    tile_rows = jnp.where(tile_ids < tile_end[-1], jnp.clip(expert_rows[tile_expert] - rows_before, 0, TE), 0)
    seg_start = seg_start.reshape(-1).astype(jnp.int32)
    seg_rows = seg_rows.reshape(-1).astype(jnp.int32)

    xs = _dispatch(seg_start, seg_rows, mask_t, h, n_tiles * TE)
    ys = _expert_ffn(tile_expert, tile_rows.astype(jnp.int32), xs, w_gate_up, w_down, layer, tf)
    return _combine(seg_start, seg_rows, x, gate, mask, ys, out_gain, out_norm)


def kernel(x, attn_norm, ffn_norm, final_norm, t5_rel_bias, diff_w_qkv, diff_w_o, diff_lambda_q1, diff_lambda_k1, diff_lambda_q2, diff_lambda_k2, diff_subln, chunk_w_qkv, chunk_w_o, chunk_rel_bias, dense_w_gate_up, dense_w_down, moe_router, moe_w_gate_up, moe_w_down):
    b, s, d = x.shape
    n = b * s
    xf = x.reshape(n, d)
    t5_bias, t5_const_rows = _t5_bias_t(t5_rel_bias, s)
    for i in range(DEPTH):
        j = i // 2
        if i % 2 == 0:
            qkv = _norm_matmul(xf, attn_norm[i], diff_w_qkv[j].astype(BF16))
            lam_init = 0.8 - 0.6 * math.exp(-0.3 * i)
            lam_vecs = jnp.stack([diff_lambda_q1[j], diff_lambda_k1[j], diff_lambda_q2[j], diff_lambda_k2[j]])
            a = _diff_attention(qkv.reshape(b, s, 3 * d), t5_bias, t5_const_rows, lam_vecs, diff_subln[j], lam_init)
            xf = _matmul_residual(a.reshape(n, d), diff_w_o[j].astype(BF16), xf)
            xf = _dense_ffn(xf, ffn_norm[i], dense_w_gate_up[j].astype(BF16), dense_w_down[j].astype(BF16),
                            tf=dense_w_down.shape[1] // 2)
        else:
            qkv = _norm_matmul(xf, attn_norm[i], chunk_w_qkv[j].astype(BF16))
            a = _chunk_attention(qkv.reshape(b, s, 3 * d), _band_bias_t(chunk_rel_bias[j]))
            xf = _matmul_residual(a.reshape(n, d), chunk_w_o[j].astype(BF16), xf)
            xf = _moe_ffn(xf, ffn_norm[i], moe_router[j], moe_w_gate_up, moe_w_down, layer=j,
                          tf=moe_w_down.shape[2] // 7,
                          out_gain=final_norm, out_norm=i == DEPTH - 1)
    return xf.reshape(b, s, d)
```

```python
import functools
import math

import numpy as np
import jax
import jax.numpy as jnp
from jax import lax
from jax.experimental import pallas as pl
from jax.experimental.pallas import tpu as pltpu

F32 = jnp.float32
BF16 = jnp.bfloat16

D_MODEL = 1024
DEPTH = 4
CHUNK = 64
DA_HEADS = 8
DA_HEAD_DIM = 64
T5_BUCKETS = 32
T5_MAX_DIST = 128
CA_HEADS = 16
CA_HEAD_DIM = 64
BAND_PREV = 8
REL_CLIP = 4 * CHUNK
N_EXPERTS = 8
RMS_EPS = 1e-6
SUBLN_EPS = 1e-5
NEG_INF = -1e30

LANES = 128
BF16_ROWS = 16
VMEM_LIMIT = 56 * 1024 * 1024
TM = 512
TE = 4 * TM
TQ = 256
TT = 256
KB_T5 = 128
KB_BAND = 256
N_SLOTS = 4
ONES_ROWS = BF16_ROWS
LOG2E = math.log2(math.e)
Q_SCALE = DA_HEAD_DIM ** -0.5 * LOG2E
PAIR = 2 * CA_HEAD_DIM


def _round_up(a, m):
    return (a + m - 1) // m * m


def _cparams(*sem):
    return pltpu.CompilerParams(dimension_semantics=sem, vmem_limit_bytes=VMEM_LIMIT)


def _rmsnorm(x32, g, eps):
    return x32 * lax.rsqrt(jnp.mean(x32 * x32, axis=-1, keepdims=True) + eps) * g


def _dot_nt(a, b):
    return lax.dot_general(a, b, (((1,), (1,)), ((), ())), preferred_element_type=F32)


def _norm_matmul_kernel(x_ref, g_ref, w_ref, o_ref):
    h = _rmsnorm(x_ref[...], g_ref[...], RMS_EPS).astype(BF16)
    d = x_ref.shape[1]
    q = jnp.dot(h, w_ref[:, :d], preferred_element_type=F32) * Q_SCALE
    o_ref[:, :d] = q.astype(o_ref.dtype)
    o_ref[:, d:] = jnp.dot(h, w_ref[:, d:], preferred_element_type=F32).astype(o_ref.dtype)


def _norm_matmul(x, g, w):
    n, d = x.shape
    n_out = w.shape[1]
    return pl.pallas_call(
        _norm_matmul_kernel,
        grid=(n // TM,),
        in_specs=[pl.BlockSpec((TM, d), lambda i: (i, 0)),
                  pl.BlockSpec((1, d), lambda i: (0, 0)),
                  pl.BlockSpec((d, n_out), lambda i: (0, 0))],
        out_specs=pl.BlockSpec((TM, n_out), lambda i: (i, 0)),
        out_shape=jax.ShapeDtypeStruct((n, n_out), BF16),
        compiler_params=_cparams("parallel"),
        name="norm_qkv",
    )(x, g.reshape(1, d), w)


def _t5_bucket_np(rel):
    nb = T5_BUCKETS // 2
    max_exact = nb // 2
    ret = (rel > 0).astype(np.int32) * nb
    n = np.abs(rel)
    n_f = np.maximum(n, 1).astype(np.float32)
    large = max_exact + (np.log(n_f / np.float32(max_exact)) / np.float32(math.log(T5_MAX_DIST / max_exact))
                         * np.float32(nb - max_exact)).astype(np.int32)
    large = np.minimum(large, nb - 1)
    return ret + np.where(n < max_exact, n, large)


def _lookup_kernel(idx_ref, t_ref, o_ref):
    rows = lax.broadcasted_iota(jnp.int32, (t_ref.shape[1], idx_ref.shape[1]), 0)
    onehot = jnp.where(rows == idx_ref[...], 1.0, 0.0)
    o_ref[...] = jnp.dot(t_ref[...], onehot, preferred_element_type=F32, precision=lax.Precision.HIGHEST)


def _table_lookup_t(table, idx):
    r, h = table.shape
    table_t = jnp.zeros((h, _round_up(r, LANES)), F32).at[:, :r].set(table.T)
    return pl.pallas_call(
        _lookup_kernel,
        out_shape=jax.ShapeDtypeStruct((h, len(idx)), F32),
        name="bias_lookup",
    )(jnp.asarray(idx, jnp.int32).reshape(1, -1), table_t)


def _toeplitz_kernel(win_ref, mask_ref, o_ref):
    for blk in range(o_ref.shape[0] // TQ):
        x = jnp.broadcast_to(win_ref[:, blk * 2 * TQ:(blk + 1) * 2 * TQ], (TQ, 2 * TQ))
        skew = pltpu.roll(x, 0, 1, stride=1, stride_axis=0)
        o_ref[blk * TQ:(blk + 1) * TQ, :] = skew[:, :TQ] + mask_ref[blk * TQ:(blk + 1) * TQ, :]


def _window_diagonals(n_keys):
    m = np.arange(2 * TQ)
    back = np.where(m < TQ, -m, 2 * TQ - m)
    d = np.arange(0, n_keys, TQ)[:, None] + back[None, :]
    return np.clip(d, 1 - TQ, n_keys - 1).reshape(-1)


def _toeplitz_bias_t(table, idx, allowed):
    win = _table_lookup_t(table, idx) * LOG2E
    h = win.shape[0]
    n_keys = allowed.shape[0]
    mask_add = jnp.asarray(np.where(allowed, 0.0, NEG_INF).astype(np.float32))
    return pl.pallas_call(
        _toeplitz_kernel,
        grid=(h,),
        in_specs=[pl.BlockSpec((None, 1, win.shape[1]), lambda i: (i, 0, 0)),
                  pl.BlockSpec((n_keys, TQ), lambda i: (0, 0))],
        out_specs=pl.BlockSpec((None, n_keys, TQ), lambda i: (i, 0, 0)),
        out_shape=jax.ShapeDtypeStruct((h, n_keys, TQ), F32),
        compiler_params=_cparams("parallel"),
        name="toeplitz_bias",
    )(win.reshape(h, 1, -1), mask_add)


def _t5_bias_t(table, seq):
    idx = _t5_bucket_np(_window_diagonals(seq) - (seq - TQ))
    i = np.arange(TQ)[None, :]
    j = np.arange(seq)[:, None] - (seq - TQ)
    allowed = (j < 0) | ((j // CHUNK) <= (i // CHUNK))
    far_bucket = _t5_bucket_np(np.asarray(-seq))
    is_far = np.all((_t5_bucket_np(j - i) == far_bucket) & allowed, axis=1)
    const_rows = int(np.argmin(is_far)) if not is_far.all() else seq
    return _toeplitz_bias_t(table, idx, allowed), const_rows


def _load_vt(v_ref, vt_ref):
    n_v = v_ref.shape[1]
    vt_ref[0:n_v, :] = v_ref[...].T
    vt_ref[n_v:, :] = jnp.ones((ONES_ROWS, v_ref.shape[0]), BF16)


def _attend_pair_steps(k_ref, k0, n_k, kb_rows, q, lane_split, bias_rows, bias_const, vt_ref, s_ref, p_ref, finish):
    zero = jnp.zeros_like(q)
    qq = jnp.concatenate([jnp.where(lane_split, q, zero), jnp.where(lane_split, zero, q)], axis=0)
    s_ref[0:n_k, :] = _dot_nt(k_ref[k0:k0 + n_k, :], qq)
    yield 1
    halves = (slice(0, TQ), slice(TQ, 2 * TQ))
    m = [None, None]
    for kb in range(0, n_k, kb_rows):
        for h, cols in enumerate(halves):
            if bias_const(kb):
                block_max = jnp.max(s_ref[kb:kb + kb_rows, cols], axis=0, keepdims=True) + bias_rows(h, kb)
            else:
                s = s_ref[kb:kb + kb_rows, cols] + bias_rows(h, kb)
                s_ref[kb:kb + kb_rows, cols] = s
                block_max = jnp.max(s, axis=0, keepdims=True)
            m[h] = block_max if m[h] is None else jnp.maximum(m[h], block_max)
            yield 1
    for kb in range(0, n_k, kb_rows):
        for h, cols in enumerate(halves):
            shift = m[h] - bias_rows(h, kb) if bias_const(kb) else m[h]
            p_ref[kb:kb + kb_rows, cols] = jnp.exp2(s_ref[kb:kb + kb_rows, cols] - shift).astype(BF16)
            yield 2
    o_t = jnp.dot(vt_ref[:, k0:k0 + n_k], p_ref[0:n_k, :], preferred_element_type=F32)
    n_v = vt_ref.shape[0] - ONES_ROWS
    finish([o_t[0:n_v, cols] * (1.0 / o_t[n_v:n_v + 1, cols]) for cols in halves])
    yield 2


def _interleave(tiles):
    pending = list(tiles)
    active = []
    while pending or active:
        if pending and len(active) < 2 and all(stage == 2 for _, stage in active):
            active.append([pending.pop(0), 1])
        for item in list(active):
            try:
                item[1] = next(item[0])
            except StopIteration:
                active.remove(item)


def _diff_attn_kernel(q_ref, k_ref, v_ref, bias_ref, lam_ref, g_ref, o_ref, vt_ref, s_ref, p_ref, *,
                      lam_init, const_rows):
    lv = lam_ref[...]
    lam = (jnp.exp(jnp.sum(lv[0:1] * lv[1:2], axis=-1, keepdims=True))
           - jnp.exp(jnp.sum(lv[2:3] * lv[3:4], axis=-1, keepdims=True)) + lam_init)
    first_map = lax.broadcasted_iota(jnp.int32, (1, 2 * DA_HEAD_DIM), 1) < DA_HEAD_DIM
    seq = q_ref.shape[0]
    _load_vt(v_ref, vt_ref)

    def tile(qi):
        n_k = (qi + 1) * TQ
        row0 = seq - n_k

        def bias_const(kb):
            return row0 + kb + KB_T5 <= const_rows

        def bias_rows(h, kb):
            n = 1 if bias_const(kb) else KB_T5
            return bias_ref[row0 + kb:row0 + kb + n, :]

        def finish(outs):
            o_t = outs[0] - lam * outs[1]
            o_t = o_t * lax.rsqrt(jnp.mean(o_t * o_t, axis=0, keepdims=True) + SUBLN_EPS) * g_ref[...]
            o_ref[qi * TQ:(qi + 1) * TQ, :] = (o_t * (1.0 - lam_init)).T.astype(o_ref.dtype)

        slot = qi % N_SLOTS
        return _attend_pair_steps(k_ref, 0, n_k, KB_T5, q_ref[qi * TQ:(qi + 1) * TQ, :], first_map, bias_rows, bias_const,
                                  vt_ref, s_ref.at[slot], p_ref.at[slot], finish)

    _interleave(tile(qi) for qi in range(seq // TQ))


def _diff_attention(qkv, bias_t, const_rows, lam_vecs, subln_g, lam_init):
    b, s, _ = qkv.shape
    hd = 2 * DA_HEAD_DIM
    return pl.pallas_call(
        functools.partial(_diff_attn_kernel, lam_init=lam_init, const_rows=const_rows),
        grid=(b, DA_HEADS),
        in_specs=[pl.BlockSpec((None, s, hd), lambda i, h: (i, 0, h)),
                  pl.BlockSpec((None, s, hd), lambda i, h: (i, 0, DA_HEADS + h)),
                  pl.BlockSpec((None, s, hd), lambda i, h: (i, 0, 2 * DA_HEADS + h)),
                  pl.BlockSpec((None, s, TQ), lambda i, h: (h, 0, 0)),
                  pl.BlockSpec((4, DA_HEAD_DIM), lambda i, h: (0, 0)),
                  pl.BlockSpec((hd, 1), lambda i, h: (0, 0))],
        out_specs=pl.BlockSpec((None, s, hd), lambda i, h: (i, 0, h)),
        out_shape=jax.ShapeDtypeStruct((b, s, D_MODEL), BF16),
        scratch_shapes=[pltpu.VMEM((hd + ONES_ROWS, s), BF16), pltpu.VMEM((N_SLOTS, s, 2 * TQ), F32),
                        pltpu.VMEM((N_SLOTS, s, 2 * TQ), BF16)],
        compiler_params=_cparams("parallel", "parallel"),
        name="diff_attn",
    )(qkv, qkv, qkv, bias_t, lam_vecs, subln_g.reshape(hd, 1))


BAND_KEYS = BAND_PREV * CHUNK + TQ


def _band_bias_t(rel_table):
    dist = BAND_PREV * CHUNK - _window_diagonals(BAND_KEYS)
    q_chunk = np.arange(TQ)[None, :] // CHUNK
    k_chunk = np.arange(BAND_KEYS)[:, None] // CHUNK
    allowed = (k_chunk >= q_chunk) & (k_chunk <= q_chunk + BAND_PREV)
    return _toeplitz_bias_t(rel_table, np.clip(dist, -REL_CLIP, REL_CLIP) + REL_CLIP, allowed)


def _chunk_attn_kernel(q_ref, k_ref, v_ref, bias_ref, o_ref, vt_ref, s_ref, p_ref):
    first_head = lax.broadcasted_iota(jnp.int32, (1, PAIR), 1) < CA_HEAD_DIM
    first_head_rows = lax.broadcasted_iota(jnp.int32, (PAIR, 1), 0) < CA_HEAD_DIM
    seq = q_ref.shape[0]
    _load_vt(v_ref, vt_ref)

    def tile(qi):
        q0 = qi * TQ
        k0 = max(0, q0 - BAND_PREV * CHUNK)
        n_k = q0 + TQ - k0
        row0 = BAND_KEYS - n_k

        def bias_rows(h, kb):
            return bias_ref[h, row0 + kb:row0 + kb + KB_BAND, :]

        def finish(outs):
            o_ref[q0:q0 + TQ, :] = jnp.where(first_head_rows, outs[0], outs[1]).T.astype(o_ref.dtype)

        slot = qi % N_SLOTS
        return _attend_pair_steps(k_ref, k0, n_k, KB_BAND, q_ref[q0:q0 + TQ, :], first_head, bias_rows, lambda kb: False,
                                  vt_ref, s_ref.at[slot], p_ref.at[slot], finish)

    _interleave(tile(qi) for qi in range(seq // TQ))


def _chunk_attention(qkv, band_bias_t):
    b, s, _ = qkv.shape
    n_pairs = CA_HEADS // 2
    return pl.pallas_call(
        _chunk_attn_kernel,
        grid=(b, n_pairs),
        in_specs=[pl.BlockSpec((None, s, PAIR), lambda i, p: (i, 0, p)),
                  pl.BlockSpec((None, s, PAIR), lambda i, p: (i, 0, n_pairs + p)),
                  pl.BlockSpec((None, s, PAIR), lambda i, p: (i, 0, 2 * n_pairs + p)),
                  pl.BlockSpec((2, BAND_KEYS, TQ), lambda i, p: (p, 0, 0))],
        out_specs=pl.BlockSpec((None, s, PAIR), lambda i, p: (i, 0, p)),
        out_shape=jax.ShapeDtypeStruct((b, s, D_MODEL), BF16),
        scratch_shapes=[pltpu.VMEM((PAIR + ONES_ROWS, s), BF16), pltpu.VMEM((N_SLOTS, BAND_KEYS, 2 * TQ), F32),
                        pltpu.VMEM((N_SLOTS, BAND_KEYS, 2 * TQ), BF16)],
        compiler_params=_cparams("parallel", "parallel"),
        name="chunk_attn",
    )(qkv, qkv, qkv, band_bias_t)


def _swiglu_partial(h, wg, wu, wd):
    g = jnp.dot(h, wg, preferred_element_type=F32)
    u = jnp.dot(h, wu, preferred_element_type=F32)
    a = (g * (1.0 / (1.0 + jnp.exp(-g))) * u).astype(BF16)
    return jnp.dot(a, wd, preferred_element_type=F32)


def _dense_ffn_kernel(a_ref, wo_ref, x_ref, g_ref, wg_ref, wu_ref, wd_ref, o_ref, x1_ref, h_ref, acc_ref):
    f = pl.program_id(1)

    @pl.when(f == 0)
    def _():
        x1 = x_ref[...] + jnp.dot(a_ref[...], wo_ref[...], preferred_element_type=F32)
        x1_ref[...] = x1
        h_ref[...] = _rmsnorm(x1, g_ref[...], RMS_EPS).astype(BF16)
        acc_ref[...] = jnp.zeros_like(acc_ref)

    acc_ref[...] += _swiglu_partial(h_ref[...], wg_ref[...], wu_ref[...], wd_ref[...])

    @pl.when(f == pl.num_programs(1) - 1)
    def _():
        o_ref[...] = x1_ref[...] + acc_ref[...]


def _attn_out_dense_ffn(a, w_o, x, g, w_gate_up, w_down, tf):
    n, d = x.shape
    ff = w_down.shape[0]
    nf = ff // tf
    return pl.pallas_call(
        _dense_ffn_kernel,
        grid=(n // TM, nf),
        in_specs=[pl.BlockSpec((TM, d), lambda i, f: (i, 0)),
                  pl.BlockSpec((d, d), lambda i, f: (0, 0)),
                  pl.BlockSpec((TM, d), lambda i, f: (i, 0)),
                  pl.BlockSpec((1, d), lambda i, f: (0, 0)),
                  pl.BlockSpec((d, tf), lambda i, f: (0, f)),
                  pl.BlockSpec((d, tf), lambda i, f: (0, nf + f)),
                  pl.BlockSpec((tf, d), lambda i, f: (f, 0))],
        out_specs=pl.BlockSpec((TM, d), lambda i, f: (i, 0)),
        out_shape=jax.ShapeDtypeStruct((n, d), F32),
        scratch_shapes=[pltpu.VMEM((TM, d), F32), pltpu.VMEM((TM, d), BF16), pltpu.VMEM((TM, d), F32)],
        compiler_params=_cparams("parallel", "arbitrary"),
        name="dense_ffn",
    )(a, w_o, x, g.reshape(1, d), w_gate_up, w_gate_up, w_down)


def _router_kernel(a_ref, wo_ref, x_ref, g_ref, r_ref, x1_ref, h_ref, gate_ref, mask_ref, mask_t_ref, cnt_ref):
    x1 = x_ref[...] + jnp.dot(a_ref[...], wo_ref[...], preferred_element_type=F32)
    x1_ref[...] = x1
    h = _rmsnorm(x1, g_ref[...], RMS_EPS)
    h_hi = h.astype(BF16)
    h_ref[...] = h_hi
    r = r_ref[...]
    r_hi = r.astype(BF16)
    r_lo = (r - r_hi.astype(F32)).astype(BF16)
    h_lo = (h - h_hi.astype(F32)).astype(BF16)
    logits = (jnp.dot(h_hi, r_hi, preferred_element_type=F32)
              + (jnp.dot(h_hi, r_lo, preferred_element_type=F32) + jnp.dot(h_lo, r_hi, preferred_element_type=F32)))
    lane = lax.broadcasted_iota(jnp.int32, logits.shape, 1)
    logits = jnp.where(lane < N_EXPERTS, logits, -jnp.inf)
    v1 = jnp.max(logits, axis=-1, keepdims=True)
    i1 = jnp.min(jnp.where(logits == v1, lane, LANES), axis=-1, keepdims=True)
    m1 = lane == i1
    rest = jnp.where(m1, -jnp.inf, logits)
    v2 = jnp.max(rest, axis=-1, keepdims=True)
    i2 = jnp.min(jnp.where(rest == v2, lane, LANES), axis=-1, keepdims=True)
    m2 = lane == i2
    e2 = jnp.exp(v2 - v1)
    g1 = 1.0 / (1.0 + e2)
    g2 = e2 / (1.0 + e2)
    gate_ref[...] = jnp.where(m1, g1, 0.0) + jnp.where(m2, g2, 0.0)
    mask = jnp.where(m1 | m2, 1.0, 0.0)
    mask_ref[...] = mask
    mask_t_ref[...] = mask.T[0:N_EXPERTS, :]
    cnt_ref[...] = jnp.broadcast_to(jnp.sum(mask, axis=0, keepdims=True), cnt_ref.shape)


def _attn_out_router(a, w_o, x, g, router):
    n, d = x.shape
    t = n // TT
    r_pad = jnp.zeros((d, LANES), F32).at[:, :N_EXPERTS].set(router)
    return pl.pallas_call(
        _router_kernel,
        grid=(t,),
        in_specs=[pl.BlockSpec((TT, d), lambda i: (i, 0)),
                  pl.BlockSpec((d, d), lambda i: (0, 0)),
                  pl.BlockSpec((TT, d), lambda i: (i, 0)),
                  pl.BlockSpec((1, d), lambda i: (0, 0)),
                  pl.BlockSpec((d, LANES), lambda i: (0, 0))],
        out_specs=[pl.BlockSpec((TT, d), lambda i: (i, 0)),
                   pl.BlockSpec((TT, d), lambda i: (i, 0)),
                   pl.BlockSpec((TT, LANES), lambda i: (i, 0)),
                   pl.BlockSpec((TT, LANES), lambda i: (i, 0)),
                   pl.BlockSpec((N_EXPERTS, TT), lambda i: (0, i)),
                   pl.BlockSpec((None, 8, LANES), lambda i: (i, 0, 0))],
        out_shape=[jax.ShapeDtypeStruct((n, d), F32),
                   jax.ShapeDtypeStruct((n, d), BF16),
                   jax.ShapeDtypeStruct((n, LANES), F32),
                   jax.ShapeDtypeStruct((n, LANES), F32),
                   jax.ShapeDtypeStruct((N_EXPERTS, n), F32),
                   jax.ShapeDtypeStruct((t, 8, LANES), F32)],
        compiler_params=_cparams("parallel"),
        name="router",
    )(a, w_o, x, g.reshape(1, d), r_pad)


def _segment_copies(n_rows, make_copy):
    off = jnp.int32(0)
    size = TT
    while size >= BF16_ROWS:
        take = (n_rows & size) != 0

        @pl.when(take)
        def _(off=off, size=size):
            make_copy(off, size)

        off = off + jnp.where(take, size, 0)
        size //= 2


def _dispatch_kernel(start_ref, rows_ref, mask_t_ref, h_ref, xs_in_ref, xs_ref, blk_ref, sem_ref):
    del xs_in_ref
    t = pl.program_id(0)
    slot = t % 2
    mask_t = mask_t_ref[...]
    row = lax.broadcasted_iota(jnp.int32, (TT, TT), 0)
    col = lax.broadcasted_iota(jnp.int32, (TT, TT), 1)
    before = jnp.where(row < col, 1.0, 0.0).astype(BF16)
    rank = jnp.dot(mask_t.astype(BF16), before, preferred_element_type=F32)
    row_f = row.astype(F32)
    sels = []
    for e in range(N_EXPERTS):
        pick = (row_f == rank[e:e + 1, :]) & (mask_t[e:e + 1, :] > 0.0)
        sels.append(jnp.where(pick, 1.0, 0.0).astype(BF16))
    packed = jnp.dot(jnp.concatenate(sels, axis=0), h_ref[...], preferred_element_type=F32)
    blk_ref[slot] = packed.astype(BF16).reshape(blk_ref.shape[1:])

    def copies(step, step_slot, wait):
        for e in range(N_EXPERTS):
            start = pl.multiple_of(start_ref[step * N_EXPERTS + e], BF16_ROWS)

            def make_copy(off, size, e=e, start=start):
                src = blk_ref.at[step_slot, e, pl.ds(pl.multiple_of(off, BF16_ROWS), size)]
                dst = xs_ref.at[pl.ds(pl.multiple_of(start + off, BF16_ROWS), size)]
                cp = pltpu.make_async_copy(src, dst, sem_ref.at[step_slot])
                cp.wait() if wait else cp.start()

            _segment_copies(rows_ref[step * N_EXPERTS + e], make_copy)

    copies(t, slot, wait=False)

    @pl.when(t > 0)
    def _():
        copies(t - 1, 1 - slot, wait=True)

    @pl.when(t == pl.num_programs(0) - 1)
    def _():
        copies(t, slot, wait=True)


def _dispatch(seg_start, seg_rows, mask_t, h, n_rows):
    n, d = h.shape
    t = n // TT
    xs0 = jnp.zeros((n_rows, d), BF16)
    grid_spec = pltpu.PrefetchScalarGridSpec(
        num_scalar_prefetch=2,
        grid=(t,),
        in_specs=[pl.BlockSpec((N_EXPERTS, TT), lambda i, *_: (0, i)),
                  pl.BlockSpec((TT, d), lambda i, *_: (i, 0)),
                  pl.BlockSpec(memory_space=pl.ANY)],
        out_specs=pl.BlockSpec(memory_space=pl.ANY),
        scratch_shapes=[pltpu.VMEM((2, N_EXPERTS, TT, d), BF16), pltpu.SemaphoreType.DMA((2,))],
    )
    return pl.pallas_call(
        _dispatch_kernel,
        grid_spec=grid_spec,
        out_shape=jax.ShapeDtypeStruct((n_rows, d), BF16),
        input_output_aliases={4: 0},
        compiler_params=_cparams("arbitrary"),
        name="moe_dispatch",
    )(seg_start, seg_rows, mask_t, h, xs0)


def _expert_kernel(tile_expert_ref, tile_rows_ref, xs_ref, wg_ref, wu_ref, wd_ref, o_ref,
                   wg_bf_ref, wu_bf_ref, wd_bf_ref, acc_ref):
    del tile_expert_ref
    j = pl.program_id(0)
    f = pl.program_id(1)
    last = pl.num_programs(1) - 1
    rows = tile_rows_ref[j]

    @pl.when(rows > 0)
    def _():
        wg_bf_ref[...] = wg_ref[...].astype(BF16)
        wu_bf_ref[...] = wu_ref[...].astype(BF16)
        wd_bf_ref[...] = wd_ref[...].astype(BF16)

    @pl.when(f == 0)
    def _():
        acc_ref[...] = jnp.zeros_like(acc_ref)

    @pl.when(rows == TE)
    def _():
        acc_ref[...] += _swiglu_partial(xs_ref[...], wg_bf_ref[...], wu_bf_ref[...], wd_bf_ref[...])

    @pl.when((rows > 0) & (rows < TE))
    def _():
        for lo in range(0, TE, TM):
            @pl.when(rows > lo)
            def _(lo=lo):
                acc_ref[lo:lo + TM, :] += _swiglu_partial(xs_ref[lo:lo + TM, :], wg_bf_ref[...], wu_bf_ref[...],
                                                          wd_bf_ref[...])

    @pl.when(f == last)
    def _():
        o_ref[...] = acc_ref[...].astype(o_ref.dtype)


def _expert_ffn(tile_expert, tile_rows, xs, w_gate_up, w_down, layer, tf):
    n_rows, d = xs.shape
    ff = w_down.shape[2]
    nf = ff // tf

    def f_eff(j, f, tr):
        return jnp.where(tr[j] > 0, f, nf - 1)

    grid_spec = pltpu.PrefetchScalarGridSpec(
        num_scalar_prefetch=2,
        grid=(n_rows // TE, nf),
        in_specs=[pl.BlockSpec((TE, d), lambda j, f, te, tr: (j, 0)),
                  pl.BlockSpec((None, None, d, tf), lambda j, f, te, tr: (layer, te[j], 0, f_eff(j, f, tr))),
                  pl.BlockSpec((None, None, d, tf), lambda j, f, te, tr: (layer, te[j], 0, nf + f_eff(j, f, tr))),
                  pl.BlockSpec((None, None, tf, d), lambda j, f, te, tr: (layer, te[j], f_eff(j, f, tr), 0))],
        out_specs=pl.BlockSpec((TE, d), lambda j, f, te, tr: (j, 0)),
        scratch_shapes=[pltpu.VMEM((d, tf), BF16), pltpu.VMEM((d, tf), BF16), pltpu.VMEM((tf, d), BF16),
                        pltpu.VMEM((TE, d), F32)],
    )
    return pl.pallas_call(
        _expert_kernel,
        grid_spec=grid_spec,
        out_shape=jax.ShapeDtypeStruct((n_rows, d), BF16),
        compiler_params=_cparams("parallel", "arbitrary"),
        name="moe_experts",
    )(tile_expert, tile_rows, xs, w_gate_up, w_gate_up, w_down)


def _combine_kernel(start_ref, rows_ref, x_ref, gate_ref, mask_ref, out_gain_ref, ys_ref, o_ref, win_ref, sem_ref,
                    *, out_norm):
    t = pl.program_id(0)
    n_t = pl.num_programs(0)
    slot = t % 2

    def copies(step, step_slot, wait):
        for e in range(N_EXPERTS):
            start = pl.multiple_of(start_ref[step * N_EXPERTS + e], BF16_ROWS)

            def make_copy(off, size, e=e, start=start):
                src = ys_ref.at[pl.ds(pl.multiple_of(start + off, BF16_ROWS), size)]
                dst = win_ref.at[step_slot, e, pl.ds(pl.multiple_of(off, BF16_ROWS), size)]
                cp = pltpu.make_async_copy(src, dst, sem_ref.at[step_slot])
                cp.wait() if wait else cp.start()

            _segment_copies(rows_ref[step * N_EXPERTS + e], make_copy)

    @pl.when(t == 0)
    def _():
        win_ref[...] = jnp.zeros_like(win_ref)
        copies(0, 0, wait=False)

    @pl.when(t + 1 < n_t)
    def _():
        copies(t + 1, 1 - slot, wait=False)

    copies(t, slot, wait=True)

    mask = mask_ref[...]
    gate = gate_ref[...]
    row = lax.broadcasted_iota(jnp.int32, (TT, TT), 0)
    col = lax.broadcasted_iota(jnp.int32, (TT, TT), 1)
    before = jnp.where(col < row, 1.0, 0.0).astype(BF16)
    rank = jnp.dot(before, mask.astype(BF16), preferred_element_type=F32)
    col_f = col.astype(F32)
    y = x_ref[...]
    for e in range(N_EXPERTS):
        pick = (col_f == rank[:, e:e + 1]) & (mask[:, e:e + 1] > 0.0)
        sel = jnp.where(pick, 1.0, 0.0).astype(BF16)
        y = y + gate[:, e:e + 1] * jnp.dot(sel, win_ref[slot, e], preferred_element_type=F32)
    o_ref[...] = _rmsnorm(y, out_gain_ref[...], RMS_EPS) if out_norm else y


def _combine(seg_start, seg_rows, x, gate, mask, ys, out_gain, out_norm):
    n, d = x.shape
    t = n // TT
    grid_spec = pltpu.PrefetchScalarGridSpec(
        num_scalar_prefetch=2,
        grid=(t,),
        in_specs=[pl.BlockSpec((TT, d), lambda i, *_: (i, 0)),
                  pl.BlockSpec((TT, LANES), lambda i, *_: (i, 0)),
                  pl.BlockSpec((TT, LANES), lambda i, *_: (i, 0)),
                  pl.BlockSpec((1, d), lambda i, *_: (0, 0)),
                  pl.BlockSpec(memory_space=pl.ANY)],
        out_specs=pl.BlockSpec((TT, d), lambda i, *_: (i, 0)),
        scratch_shapes=[pltpu.VMEM((2, N_EXPERTS, TT, d), BF16), pltpu.SemaphoreType.DMA((2,))],
    )
    return pl.pallas_call(
        functools.partial(_combine_kernel, out_norm=out_norm),
        grid_spec=grid_spec,
        out_shape=jax.ShapeDtypeStruct((n, d), F32),
        compiler_params=_cparams("arbitrary"),
        name="moe_combine",
    )(seg_start, seg_rows, x, gate, mask, out_gain.reshape(1, d), ys)


def _attn_out_moe_ffn(a, w_o, x, g, router, w_gate_up, w_down, layer, tf, out_gain, out_norm):
    n, d = x.shape
    t = n // TT
    x, h, gate, mask, mask_t, cnt = _attn_out_router(a, w_o, x, g, router)
    cnt = cnt[:, 0, :N_EXPERTS].astype(jnp.int32)
    seg_rows = _round_up(cnt, BF16_ROWS)
    expert_rows = jnp.sum(seg_rows, axis=0)
    region = _round_up(expert_rows, TE)
    region_end = jnp.cumsum(region)
    seg_start = (region_end - region)[None, :] + jnp.cumsum(seg_rows, axis=0) - seg_rows
    n_tiles = (_round_up(2 * n + t * N_EXPERTS * (BF16_ROWS - 1), TE) + N_EXPERTS * TE) // TE
    tile_end = region_end // TE
    tile_ids = jnp.arange(n_tiles, dtype=jnp.int32)
    last_tile = jnp.minimum(tile_ids, tile_end[-1] - 1)
    tile_expert = jnp.sum((last_tile[:, None] >= tile_end[None, :]).astype(jnp.int32), axis=1)
    tile_expert = jnp.minimum(tile_expert, N_EXPERTS - 1)
    rows_before = (tile_ids - (tile_end - region // TE)[tile_expert]) * TE
    tile_rows = jnp.where(tile_ids < tile_end[-1], jnp.clip(expert_rows[tile_expert] - rows_before, 0, TE), 0)
    seg_start = seg_start.reshape(-1).astype(jnp.int32)
    seg_rows = seg_rows.reshape(-1).astype(jnp.int32)

    xs = _dispatch(seg_start, seg_rows, mask_t, h, n_tiles * TE)
    ys = _expert_ffn(tile_expert, tile_rows.astype(jnp.int32), xs, w_gate_up, w_down, layer, tf)
    return _combine(seg_start, seg_rows, x, gate, mask, ys, out_gain, out_norm)


def kernel(x, attn_norm, ffn_norm, final_norm, t5_rel_bias, diff_w_qkv, diff_w_o, diff_lambda_q1, diff_lambda_k1, diff_lambda_q2, diff_lambda_k2, diff_subln, chunk_w_qkv, chunk_w_o, chunk_rel_bias, dense_w_gate_up, dense_w_down, moe_router, moe_w_gate_up, moe_w_down):
    b, s, d = x.shape
    n = b * s
    xf = x.reshape(n, d)
    t5_bias, t5_const_rows = _t5_bias_t(t5_rel_bias, s)
    for i in range(DEPTH):
        j = i // 2
        if i % 2 == 0:
            qkv = _norm_matmul(xf, attn_norm[i], diff_w_qkv[j].astype(BF16))
            lam_init = 0.8 - 0.6 * math.exp(-0.3 * i)
            lam_vecs = jnp.stack([diff_lambda_q1[j], diff_lambda_k1[j], diff_lambda_q2[j], diff_lambda_k2[j]])
            a = _diff_attention(qkv.reshape(b, s, 3 * d), t5_bias, t5_const_rows, lam_vecs, diff_subln[j], lam_init)
            xf = _attn_out_dense_ffn(a.reshape(n, d), diff_w_o[j].astype(BF16), xf, ffn_norm[i],
                                     dense_w_gate_up[j].astype(BF16), dense_w_down[j].astype(BF16),
                                     tf=dense_w_down.shape[1] // 2)
        else:
            qkv = _norm_matmul(xf, attn_norm[i], chunk_w_qkv[j].astype(BF16))
            a = _chunk_attention(qkv.reshape(b, s, 3 * d), _band_bias_t(chunk_rel_bias[j]))
            xf = _attn_out_moe_ffn(a.reshape(n, d), chunk_w_o[j].astype(BF16), xf, ffn_norm[i], moe_router[j],
                                   moe_w_gate_up, moe_w_down, layer=j, tf=moe_w_down.shape[2] // 7,
                                   out_gain=final_norm, out_norm=i == DEPTH - 1)
    return xf.reshape(b, s, d)
```

```python
import functools
import math

import numpy as np
import jax
import jax.numpy as jnp
from jax import lax
from jax.experimental import pallas as pl
from jax.experimental.pallas import tpu as pltpu

F32 = jnp.float32
BF16 = jnp.bfloat16

D_MODEL = 1024
DEPTH = 4
CHUNK = 64
DA_HEADS = 8
DA_HEAD_DIM = 64
T5_BUCKETS = 32
T5_MAX_DIST = 128
CA_HEADS = 16
CA_HEAD_DIM = 64
BAND_PREV = 8
REL_CLIP = 4 * CHUNK
N_EXPERTS = 8
RMS_EPS = 1e-6
SUBLN_EPS = 1e-5
NEG_INF = -1e30

LANES = 128
BF16_ROWS = 16
VMEM_LIMIT = 56 * 1024 * 1024
TM = 512
TE = 4 * TM
TQ = 256
TT = 256
KB_T5 = 128
KB_BAND = 256
MM_ROWS = 2048
N_SLOTS = 4
ONES_ROWS = BF16_ROWS
LOG2E = math.log2(math.e)
Q_SCALE = DA_HEAD_DIM ** -0.5 * LOG2E
PAIR = 2 * CA_HEAD_DIM


def _round_up(a, m):
    return (a + m - 1) // m * m


def _cparams(*sem):
    return pltpu.CompilerParams(dimension_semantics=sem, vmem_limit_bytes=VMEM_LIMIT)


def _rmsnorm(x32, g, eps):
    return x32 * lax.rsqrt(jnp.mean(x32 * x32, axis=-1, keepdims=True) + eps) * g


def _dot_nt(a, b):
    return lax.dot_general(a, b, (((1,), (1,)), ((), ())), preferred_element_type=F32)


def _norm_matmul_kernel(x_ref, g_ref, w_ref, o_ref):
    h = _rmsnorm(x_ref[...], g_ref[...], RMS_EPS).astype(BF16)
    d = x_ref.shape[1]
    q = jnp.dot(h, w_ref[:, :d], preferred_element_type=F32) * Q_SCALE
    o_ref[:, :d] = q.astype(o_ref.dtype)
    o_ref[:, d:] = jnp.dot(h, w_ref[:, d:], preferred_element_type=F32).astype(o_ref.dtype)


def _norm_matmul(x, g, w):
    n, d = x.shape
    n_out = w.shape[1]
    return pl.pallas_call(
        _norm_matmul_kernel,
        grid=(n // TM,),
        in_specs=[pl.BlockSpec((TM, d), lambda i: (i, 0)),
                  pl.BlockSpec((1, d), lambda i: (0, 0)),
                  pl.BlockSpec((d, n_out), lambda i: (0, 0))],
        out_specs=pl.BlockSpec((TM, n_out), lambda i: (i, 0)),
        out_shape=jax.ShapeDtypeStruct((n, n_out), BF16),
        compiler_params=_cparams("parallel"),
        name="norm_qkv",
    )(x, g.reshape(1, d), w)


def _t5_bucket_np(rel):
    nb = T5_BUCKETS // 2
    max_exact = nb // 2
    ret = (rel > 0).astype(np.int32) * nb
    n = np.abs(rel)
    n_f = np.maximum(n, 1).astype(np.float32)
    large = max_exact + (np.log(n_f / np.float32(max_exact)) / np.float32(math.log(T5_MAX_DIST / max_exact))
                         * np.float32(nb - max_exact)).astype(np.int32)
    large = np.minimum(large, nb - 1)
    return ret + np.where(n < max_exact, n, large)


def _lookup_kernel(idx_ref, t_ref, o_ref):
    rows = lax.broadcasted_iota(jnp.int32, (t_ref.shape[1], idx_ref.shape[1]), 0)
    onehot = jnp.where(rows == idx_ref[...], 1.0, 0.0)
    o_ref[...] = jnp.dot(t_ref[...], onehot, preferred_element_type=F32, precision=lax.Precision.HIGHEST)


def _table_lookup_t(table, idx):
    r, h = table.shape
    table_t = jnp.zeros((h, _round_up(r, LANES)), F32).at[:, :r].set(table.T)
    return pl.pallas_call(
        _lookup_kernel,
        out_shape=jax.ShapeDtypeStruct((h, len(idx)), F32),
        name="bias_lookup",
    )(jnp.asarray(idx, jnp.int32).reshape(1, -1), table_t)


def _toeplitz_kernel(win_ref, mask_ref, o_ref):
    for blk in range(o_ref.shape[0] // TQ):
        x = jnp.broadcast_to(win_ref[:, blk * 2 * TQ:(blk + 1) * 2 * TQ], (TQ, 2 * TQ))
        skew = pltpu.roll(x, 0, 1, stride=1, stride_axis=0)
        o_ref[blk * TQ:(blk + 1) * TQ, :] = skew[:, :TQ] + mask_ref[blk * TQ:(blk + 1) * TQ, :]


def _window_diagonals(n_keys):
    m = np.arange(2 * TQ)
    back = np.where(m < TQ, -m, 2 * TQ - m)
    d = np.arange(0, n_keys, TQ)[:, None] + back[None, :]
    return np.clip(d, 1 - TQ, n_keys - 1).reshape(-1)


def _toeplitz_bias_t(table, idx, allowed):
    win = _table_lookup_t(table, idx) * LOG2E
    h = win.shape[0]
    n_keys = allowed.shape[0]
    mask_add = jnp.asarray(np.where(allowed, 0.0, NEG_INF).astype(np.float32))
    return pl.pallas_call(
        _toeplitz_kernel,
        grid=(h,),
        in_specs=[pl.BlockSpec((None, 1, win.shape[1]), lambda i: (i, 0, 0)),
                  pl.BlockSpec((n_keys, TQ), lambda i: (0, 0))],
        out_specs=pl.BlockSpec((None, n_keys, TQ), lambda i: (i, 0, 0)),
        out_shape=jax.ShapeDtypeStruct((h, n_keys, TQ), F32),
        compiler_params=_cparams("parallel"),
        name="toeplitz_bias",
    )(win.reshape(h, 1, -1), mask_add)


def _t5_bias_t(table, seq):
    idx = _t5_bucket_np(_window_diagonals(seq) - (seq - TQ))
    i = np.arange(TQ)[None, :]
    j = np.arange(seq)[:, None] - (seq - TQ)
    allowed = (j < 0) | ((j // CHUNK) <= (i // CHUNK))
    far_bucket = _t5_bucket_np(np.asarray(-seq))
    is_far = np.all((_t5_bucket_np(j - i) == far_bucket) & allowed, axis=1)
    const_rows = int(np.argmin(is_far)) if not is_far.all() else seq
    return _toeplitz_bias_t(table, idx, allowed), const_rows


def _load_vt(v_ref, vt_ref):
    n_v = v_ref.shape[1]
    vt_ref[0:n_v, :] = v_ref[...].T
    vt_ref[n_v:, :] = jnp.ones((ONES_ROWS, v_ref.shape[0]), BF16)


def _attend_pair_phases(k_ref, k0, n_k, kb_rows, q, lane_split, bias_rows, bias_const, vt_ref, s_ref, p_ref, finish):
    halves = (slice(0, TQ), slice(TQ, 2 * TQ))
    n_v = vt_ref.shape[0] - ONES_ROWS
    state = {"m": [None, None], "o": None}

    def score(kr):
        def run():
            if kr == 0:
                zero = jnp.zeros_like(q)
                state["qq"] = jnp.concatenate([jnp.where(lane_split, q, zero), jnp.where(lane_split, zero, q)], axis=0)
            rows = min(MM_ROWS, n_k - kr)
            s_ref[kr:kr + rows, :] = _dot_nt(k_ref[k0 + kr:k0 + kr + rows, :], state["qq"])
        return run

    def running_max(kb, h):
        def run():
            cols = halves[h]
            if bias_const(kb):
                block_max = jnp.max(s_ref[kb:kb + kb_rows, cols], axis=0, keepdims=True) + bias_rows(h, kb)
            else:
                s = s_ref[kb:kb + kb_rows, cols] + bias_rows(h, kb)
                s_ref[kb:kb + kb_rows, cols] = s
                block_max = jnp.max(s, axis=0, keepdims=True)
            m = state["m"]
            m[h] = block_max if m[h] is None else jnp.maximum(m[h], block_max)
        return run

    def exponentials(kb, h):
        def run():
            cols = halves[h]
            m = state["m"][h]
            shift = m - bias_rows(h, kb) if bias_const(kb) else m
            p_ref[kb:kb + kb_rows, cols] = jnp.exp2(s_ref[kb:kb + kb_rows, cols] - shift).astype(BF16)
        return run

    def pv(kr):
        def run():
            rows = min(MM_ROWS, n_k - kr)
            part = jnp.dot(vt_ref[:, k0 + kr:k0 + kr + rows], p_ref[kr:kr + rows, :], preferred_element_type=F32)
            state["o"] = part if state["o"] is None else state["o"] + part
            if kr + rows == n_k:
                o_t = state["o"]
                finish([o_t[0:n_v, cols] * (1.0 / o_t[n_v:n_v + 1, cols]) for cols in halves])
        return run

    blocks = [(kb, h) for kb in range(0, n_k, kb_rows) for h in range(2)]
    return ([score(kr) for kr in range(0, n_k, MM_ROWS)], [running_max(kb, h) for kb, h in blocks],
            [exponentials(kb, h) for kb, h in blocks], [pv(kr) for kr in range(0, n_k, MM_ROWS)])


def _interleave(tiles):
    def merged(a, b):
        order = sorted([((i + 0.5) / len(a), 0, f) for i, f in enumerate(a)]
                       + [((i + 0.5) / len(b), 1, f) for i, f in enumerate(b)], key=lambda t: t[:2])
        for _, _, thunk in order:
            thunk()

    empty = ([], [], [], [])
    prev = empty
    for cur in list(tiles) + [empty]:
        merged(cur[0], prev[2])
        merged(cur[1], prev[3])
        prev = cur


def _diff_attn_kernel(q_ref, k_ref, v_ref, bias_ref, lam_ref, g_ref, o_ref, vt_ref, s_ref, p_ref, *,
                      lam_init, const_rows):
    lv = lam_ref[...]
    lam = (jnp.exp(jnp.sum(lv[0:1] * lv[1:2], axis=-1, keepdims=True))
           - jnp.exp(jnp.sum(lv[2:3] * lv[3:4], axis=-1, keepdims=True)) + lam_init)
    first_map = lax.broadcasted_iota(jnp.int32, (1, 2 * DA_HEAD_DIM), 1) < DA_HEAD_DIM
    seq = q_ref.shape[0]
    _load_vt(v_ref, vt_ref)

    def tile(qi):
        n_k = (qi + 1) * TQ
        row0 = seq - n_k

        def bias_const(kb):
            return row0 + kb + KB_T5 <= const_rows

        def bias_rows(h, kb):
            n = 1 if bias_const(kb) else KB_T5
            return bias_ref[row0 + kb:row0 + kb + n, :]

        def finish(outs):
            o_t = outs[0] - lam * outs[1]
            o_t = o_t * lax.rsqrt(jnp.mean(o_t * o_t, axis=0, keepdims=True) + SUBLN_EPS) * g_ref[...]
            o_ref[qi * TQ:(qi + 1) * TQ, :] = (o_t * (1.0 - lam_init)).T.astype(o_ref.dtype)

        slot = qi % N_SLOTS
        return _attend_pair_phases(k_ref, 0, n_k, KB_T5, q_ref[qi * TQ:(qi + 1) * TQ, :], first_map, bias_rows, bias_const,
                                  vt_ref, s_ref.at[slot], p_ref.at[slot], finish)

    _interleave(tile(qi) for qi in range(seq // TQ))


def _diff_attention(qkv, bias_t, const_rows, lam_vecs, subln_g, lam_init):
    b, s, _ = qkv.shape
    hd = 2 * DA_HEAD_DIM
    return pl.pallas_call(
        functools.partial(_diff_attn_kernel, lam_init=lam_init, const_rows=const_rows),
        grid=(b, DA_HEADS),
        in_specs=[pl.BlockSpec((None, s, hd), lambda i, h: (i, 0, h)),
                  pl.BlockSpec((None, s, hd), lambda i, h: (i, 0, DA_HEADS + h)),
                  pl.BlockSpec((None, s, hd), lambda i, h: (i, 0, 2 * DA_HEADS + h)),
                  pl.BlockSpec((None, s, TQ), lambda i, h: (h, 0, 0)),
                  pl.BlockSpec((4, DA_HEAD_DIM), lambda i, h: (0, 0)),
                  pl.BlockSpec((hd, 1), lambda i, h: (0, 0))],
        out_specs=pl.BlockSpec((None, s, hd), lambda i, h: (i, 0, h)),
        out_shape=jax.ShapeDtypeStruct((b, s, D_MODEL), BF16),
        scratch_shapes=[pltpu.VMEM((hd + ONES_ROWS, s), BF16), pltpu.VMEM((N_SLOTS, s, 2 * TQ), F32),
                        pltpu.VMEM((N_SLOTS, s, 2 * TQ), BF16)],
        compiler_params=_cparams("parallel", "parallel"),
        name="diff_attn",
    )(qkv, qkv, qkv, bias_t, lam_vecs, subln_g.reshape(hd, 1))


BAND_KEYS = BAND_PREV * CHUNK + TQ


def _band_bias_t(rel_table):
    dist = BAND_PREV * CHUNK - _window_diagonals(BAND_KEYS)
    q_chunk = np.arange(TQ)[None, :] // CHUNK
    k_chunk = np.arange(BAND_KEYS)[:, None] // CHUNK
    allowed = (k_chunk >= q_chunk) & (k_chunk <= q_chunk + BAND_PREV)
    return _toeplitz_bias_t(rel_table, np.clip(dist, -REL_CLIP, REL_CLIP) + REL_CLIP, allowed)


def _chunk_attn_kernel(q_ref, k_ref, v_ref, bias_ref, o_ref, vt_ref, s_ref, p_ref):
    first_head = lax.broadcasted_iota(jnp.int32, (1, PAIR), 1) < CA_HEAD_DIM
    first_head_rows = lax.broadcasted_iota(jnp.int32, (PAIR, 1), 0) < CA_HEAD_DIM
    seq = q_ref.shape[0]
    _load_vt(v_ref, vt_ref)

    def tile(qi):
        q0 = qi * TQ
        k0 = max(0, q0 - BAND_PREV * CHUNK)
        n_k = q0 + TQ - k0
        row0 = BAND_KEYS - n_k

        def bias_rows(h, kb):
            return bias_ref[h, row0 + kb:row0 + kb + KB_BAND, :]

        def finish(outs):
            o_ref[q0:q0 + TQ, :] = jnp.where(first_head_rows, outs[0], outs[1]).T.astype(o_ref.dtype)

        slot = qi % N_SLOTS
        return _attend_pair_phases(k_ref, k0, n_k, KB_BAND, q_ref[q0:q0 + TQ, :], first_head, bias_rows, lambda kb: False,
                                  vt_ref, s_ref.at[slot], p_ref.at[slot], finish)

    _interleave(tile(qi) for qi in range(seq // TQ))


def _chunk_attention(qkv, band_bias_t):
    b, s, _ = qkv.shape
    n_pairs = CA_HEADS // 2
    return pl.pallas_call(
        _chunk_attn_kernel,
        grid=(b, n_pairs),
        in_specs=[pl.BlockSpec((None, s, PAIR), lambda i, p: (i, 0, p)),
                  pl.BlockSpec((None, s, PAIR), lambda i, p: (i, 0, n_pairs + p)),
                  pl.BlockSpec((None, s, PAIR), lambda i, p: (i, 0, 2 * n_pairs + p)),
                  pl.BlockSpec((2, BAND_KEYS, TQ), lambda i, p: (p, 0, 0))],
        out_specs=pl.BlockSpec((None, s, PAIR), lambda i, p: (i, 0, p)),
        out_shape=jax.ShapeDtypeStruct((b, s, D_MODEL), BF16),
        scratch_shapes=[pltpu.VMEM((PAIR + ONES_ROWS, s), BF16), pltpu.VMEM((N_SLOTS, BAND_KEYS, 2 * TQ), F32),
                        pltpu.VMEM((N_SLOTS, BAND_KEYS, 2 * TQ), BF16)],
        compiler_params=_cparams("parallel", "parallel"),
        name="chunk_attn",
    )(qkv, qkv, qkv, band_bias_t)


def _swiglu_partial(h, wg, wu, wd):
    g = jnp.dot(h, wg, preferred_element_type=F32)
    u = jnp.dot(h, wu, preferred_element_type=F32)
    a = (g * (1.0 / (1.0 + jnp.exp(-g))) * u).astype(BF16)
    return jnp.dot(a, wd, preferred_element_type=F32)


def _dense_ffn_kernel(a_ref, wo_ref, x_ref, g_ref, wg_ref, wu_ref, wd_ref, o_ref, x1_ref, h_ref, acc_ref):
    f = pl.program_id(1)

    @pl.when(f == 0)
    def _():
        x1 = x_ref[...] + jnp.dot(a_ref[...], wo_ref[...], preferred_element_type=F32)
        x1_ref[...] = x1
        h_ref[...] = _rmsnorm(x1, g_ref[...], RMS_EPS).astype(BF16)
        acc_ref[...] = jnp.zeros_like(acc_ref)

    acc_ref[...] += _swiglu_partial(h_ref[...], wg_ref[...], wu_ref[...], wd_ref[...])

    @pl.when(f == pl.num_programs(1) - 1)
    def _():
        o_ref[...] = x1_ref[...] + acc_ref[...]


def _attn_out_dense_ffn(a, w_o, x, g, w_gate_up, w_down, tf):
    n, d = x.shape
    ff = w_down.shape[0]
    nf = ff // tf
    return pl.pallas_call(
        _dense_ffn_kernel,
        grid=(n // TM, nf),
        in_specs=[pl.BlockSpec((TM, d), lambda i, f: (i, 0)),
                  pl.BlockSpec((d, d), lambda i, f: (0, 0)),
                  pl.BlockSpec((TM, d), lambda i, f: (i, 0)),
                  pl.BlockSpec((1, d), lambda i, f: (0, 0)),
                  pl.BlockSpec((d, tf), lambda i, f: (0, f)),
                  pl.BlockSpec((d, tf), lambda i, f: (0, nf + f)),
                  pl.BlockSpec((tf, d), lambda i, f: (f, 0))],
        out_specs=pl.BlockSpec((TM, d), lambda i, f: (i, 0)),
        out_shape=jax.ShapeDtypeStruct((n, d), F32),
        scratch_shapes=[pltpu.VMEM((TM, d), F32), pltpu.VMEM((TM, d), BF16), pltpu.VMEM((TM, d), F32)],
        compiler_params=_cparams("parallel", "arbitrary"),
        name="dense_ffn",
    )(a, w_o, x, g.reshape(1, d), w_gate_up, w_gate_up, w_down)


def _router_kernel(a_ref, wo_ref, x_ref, g_ref, r_ref, x1_ref, h_ref, gate_ref, mask_ref, mask_t_ref, cnt_ref):
    x1 = x_ref[...] + jnp.dot(a_ref[...], wo_ref[...], preferred_element_type=F32)
    x1_ref[...] = x1
    h = _rmsnorm(x1, g_ref[...], RMS_EPS)
    h_hi = h.astype(BF16)
    h_ref[...] = h_hi
    r = r_ref[...]
    r_hi = r.astype(BF16)
    r_lo = (r - r_hi.astype(F32)).astype(BF16)
    h_lo = (h - h_hi.astype(F32)).astype(BF16)
    logits = (jnp.dot(h_hi, r_hi, preferred_element_type=F32)
              + (jnp.dot(h_hi, r_lo, preferred_element_type=F32) + jnp.dot(h_lo, r_hi, preferred_element_type=F32)))
    lane = lax.broadcasted_iota(jnp.int32, logits.shape, 1)
    logits = jnp.where(lane < N_EXPERTS, logits, -jnp.inf)
    v1 = jnp.max(logits, axis=-1, keepdims=True)
    i1 = jnp.min(jnp.where(logits == v1, lane, LANES), axis=-1, keepdims=True)
    m1 = lane == i1
    rest = jnp.where(m1, -jnp.inf, logits)
    v2 = jnp.max(rest, axis=-1, keepdims=True)
    i2 = jnp.min(jnp.where(rest == v2, lane, LANES), axis=-1, keepdims=True)
    m2 = lane == i2
    e2 = jnp.exp(v2 - v1)
    g1 = 1.0 / (1.0 + e2)
    g2 = e2 / (1.0 + e2)
    gate_ref[...] = jnp.where(m1, g1, 0.0) + jnp.where(m2, g2, 0.0)
    mask = jnp.where(m1 | m2, 1.0, 0.0)
    mask_ref[...] = mask
    mask_t_ref[...] = mask.T[0:N_EXPERTS, :]
    cnt_ref[...] = jnp.broadcast_to(jnp.sum(mask, axis=0, keepdims=True), cnt_ref.shape)


def _attn_out_router(a, w_o, x, g, router):
    n, d = x.shape
    t = n // TT
    r_pad = jnp.zeros((d, LANES), F32).at[:, :N_EXPERTS].set(router)
    return pl.pallas_call(
        _router_kernel,
        grid=(t,),
        in_specs=[pl.BlockSpec((TT, d), lambda i: (i, 0)),
                  pl.BlockSpec((d, d), lambda i: (0, 0)),
                  pl.BlockSpec((TT, d), lambda i: (i, 0)),
                  pl.BlockSpec((1, d), lambda i: (0, 0)),
                  pl.BlockSpec((d, LANES), lambda i: (0, 0))],
        out_specs=[pl.BlockSpec((TT, d), lambda i: (i, 0)),
                   pl.BlockSpec((TT, d), lambda i: (i, 0)),
                   pl.BlockSpec((TT, LANES), lambda i: (i, 0)),
                   pl.BlockSpec((TT, LANES), lambda i: (i, 0)),
                   pl.BlockSpec((N_EXPERTS, TT), lambda i: (0, i)),
                   pl.BlockSpec((None, 8, LANES), lambda i: (i, 0, 0))],
        out_shape=[jax.ShapeDtypeStruct((n, d), F32),
                   jax.ShapeDtypeStruct((n, d), BF16),
                   jax.ShapeDtypeStruct((n, LANES), F32),
                   jax.ShapeDtypeStruct((n, LANES), F32),
                   jax.ShapeDtypeStruct((N_EXPERTS, n), F32),
                   jax.ShapeDtypeStruct((t, 8, LANES), F32)],
        compiler_params=_cparams("parallel"),
        name="router",
    )(a, w_o, x, g.reshape(1, d), r_pad)


def _segment_copies(n_rows, make_copy):
    off = jnp.int32(0)
    size = TT
    while size >= BF16_ROWS:
        take = (n_rows & size) != 0

        @pl.when(take)
        def _(off=off, size=size):
            make_copy(off, size)

        off = off + jnp.where(take, size, 0)
        size //= 2


def _dispatch_kernel(start_ref, rows_ref, mask_t_ref, h_ref, xs_in_ref, xs_ref, blk_ref, sem_ref):
    del xs_in_ref
    t = pl.program_id(0)
    slot = t % 2
    mask_t = mask_t_ref[...]
    row = lax.broadcasted_iota(jnp.int32, (TT, TT), 0)
    col = lax.broadcasted_iota(jnp.int32, (TT, TT), 1)
    before = jnp.where(row < col, 1.0, 0.0).astype(BF16)
    rank = jnp.dot(mask_t.astype(BF16), before, preferred_element_type=F32)
    row_f = row.astype(F32)
    sels = []
    for e in range(N_EXPERTS):
        pick = (row_f == rank[e:e + 1, :]) & (mask_t[e:e + 1, :] > 0.0)
        sels.append(jnp.where(pick, 1.0, 0.0).astype(BF16))
    packed = jnp.dot(jnp.concatenate(sels, axis=0), h_ref[...], preferred_element_type=F32)
    blk_ref[slot] = packed.astype(BF16).reshape(blk_ref.shape[1:])

    def copies(step, step_slot, wait):
        for e in range(N_EXPERTS):
            start = pl.multiple_of(start_ref[step * N_EXPERTS + e], BF16_ROWS)

            def make_copy(off, size, e=e, start=start):
                src = blk_ref.at[step_slot, e, pl.ds(pl.multiple_of(off, BF16_ROWS), size)]
                dst = xs_ref.at[pl.ds(pl.multiple_of(start + off, BF16_ROWS), size)]
                cp = pltpu.make_async_copy(src, dst, sem_ref.at[step_slot])
                cp.wait() if wait else cp.start()

            _segment_copies(rows_ref[step * N_EXPERTS + e], make_copy)

    copies(t, slot, wait=False)

    @pl.when(t > 0)
    def _():
        copies(t - 1, 1 - slot, wait=True)

    @pl.when(t == pl.num_programs(0) - 1)
    def _():
        copies(t, slot, wait=True)


def _dispatch(seg_start, seg_rows, mask_t, h, n_rows):
    n, d = h.shape
    t = n // TT
    xs0 = jnp.zeros((n_rows, d), BF16)
    grid_spec = pltpu.PrefetchScalarGridSpec(
        num_scalar_prefetch=2,
        grid=(t,),
        in_specs=[pl.BlockSpec((N_EXPERTS, TT), lambda i, *_: (0, i)),
                  pl.BlockSpec((TT, d), lambda i, *_: (i, 0)),
                  pl.BlockSpec(memory_space=pl.ANY)],
        out_specs=pl.BlockSpec(memory_space=pl.ANY),
        scratch_shapes=[pltpu.VMEM((2, N_EXPERTS, TT, d), BF16), pltpu.SemaphoreType.DMA((2,))],
    )
    return pl.pallas_call(
        _dispatch_kernel,
        grid_spec=grid_spec,
        out_shape=jax.ShapeDtypeStruct((n_rows, d), BF16),
        input_output_aliases={4: 0},
        compiler_params=_cparams("arbitrary"),
        name="moe_dispatch",
    )(seg_start, seg_rows, mask_t, h, xs0)


def _expert_kernel(tile_expert_ref, tile_rows_ref, xs_ref, wg_ref, wu_ref, wd_ref, o_ref,
                   wg_bf_ref, wu_bf_ref, wd_bf_ref, acc_ref):
    del tile_expert_ref
    j = pl.program_id(0)
    f = pl.program_id(1)
    last = pl.num_programs(1) - 1
    rows = tile_rows_ref[j]

    @pl.when(f == 0)
    def _():
        acc_ref[...] = jnp.zeros_like(acc_ref)

    @pl.when(rows == TE)
    def _():
        acc_ref[...] += _swiglu_partial(xs_ref[...], wg_ref[...].astype(BF16), wu_ref[...].astype(BF16),
                                        wd_ref[...].astype(BF16))

    @pl.when((rows > 0) & (rows < TE))
    def _():
        wg_bf_ref[...] = wg_ref[...].astype(BF16)
        wu_bf_ref[...] = wu_ref[...].astype(BF16)
        wd_bf_ref[...] = wd_ref[...].astype(BF16)
        for lo in range(0, TE, TM):
            @pl.when(rows > lo)
            def _(lo=lo):
                acc_ref[lo:lo + TM, :] += _swiglu_partial(xs_ref[lo:lo + TM, :], wg_bf_ref[...], wu_bf_ref[...],
                                                          wd_bf_ref[...])

    @pl.when(f == last)
    def _():
        o_ref[...] = acc_ref[...].astype(o_ref.dtype)


def _expert_ffn(tile_expert, tile_rows, xs, w_gate_up, w_down, layer, tf):
    n_rows, d = xs.shape
    ff = w_down.shape[2]
    nf = ff // tf

    def f_eff(j, f, tr):
        return jnp.where(tr[j] > 0, f, nf - 1)

    grid_spec = pltpu.PrefetchScalarGridSpec(
        num_scalar_prefetch=2,
        grid=(n_rows // TE, nf),
        in_specs=[pl.BlockSpec((TE, d), lambda j, f, te, tr: (j, 0)),
                  pl.BlockSpec((None, None, d, tf), lambda j, f, te, tr: (layer, te[j], 0, f_eff(j, f, tr))),
                  pl.BlockSpec((None, None, d, tf), lambda j, f, te, tr: (layer, te[j], 0, nf + f_eff(j, f, tr))),
                  pl.BlockSpec((None, None, tf, d), lambda j, f, te, tr: (layer, te[j], f_eff(j, f, tr), 0))],
        out_specs=pl.BlockSpec((TE, d), lambda j, f, te, tr: (j, 0)),
        scratch_shapes=[pltpu.VMEM((d, tf), BF16), pltpu.VMEM((d, tf), BF16), pltpu.VMEM((tf, d), BF16),
                        pltpu.VMEM((TE, d), F32)],
    )
    return pl.pallas_call(
        _expert_kernel,
        grid_spec=grid_spec,
        out_shape=jax.ShapeDtypeStruct((n_rows, d), BF16),
        compiler_params=_cparams("parallel", "arbitrary"),
        name="moe_experts",
    )(tile_expert, tile_rows, xs, w_gate_up, w_gate_up, w_down)


def _combine_kernel(start_ref, rows_ref, x_ref, gate_ref, mask_ref, out_gain_ref, ys_ref, o_ref, win_ref, sem_ref,
                    *, out_norm):
    t = pl.program_id(0)
    n_t = pl.num_programs(0)
    slot = t % 2

    def copies(step, step_slot, wait):
        for e in range(N_EXPERTS):
            start = pl.multiple_of(start_ref[step * N_EXPERTS + e], BF16_ROWS)

            def make_copy(off, size, e=e, start=start):
                src = ys_ref.at[pl.ds(pl.multiple_of(start + off, BF16_ROWS), size)]
                dst = win_ref.at[step_slot, e, pl.ds(pl.multiple_of(off, BF16_ROWS), size)]
                cp = pltpu.make_async_copy(src, dst, sem_ref.at[step_slot])
                cp.wait() if wait else cp.start()

            _segment_copies(rows_ref[step * N_EXPERTS + e], make_copy)

    @pl.when(t == 0)
    def _():
        win_ref[...] = jnp.zeros_like(win_ref)
        copies(0, 0, wait=False)

    @pl.when(t + 1 < n_t)
    def _():
        copies(t + 1, 1 - slot, wait=False)

    copies(t, slot, wait=True)

    mask = mask_ref[...]
    gate = gate_ref[...]
    row = lax.broadcasted_iota(jnp.int32, (TT, TT), 0)
    col = lax.broadcasted_iota(jnp.int32, (TT, TT), 1)
    before = jnp.where(col < row, 1.0, 0.0).astype(BF16)
    rank = jnp.dot(before, mask.astype(BF16), preferred_element_type=F32)
    col_f = col.astype(F32)
    y = x_ref[...]
    for e in range(N_EXPERTS):
        pick = (col_f == rank[:, e:e + 1]) & (mask[:, e:e + 1] > 0.0)
        sel = jnp.where(pick, 1.0, 0.0).astype(BF16)
        y = y + gate[:, e:e + 1] * jnp.dot(sel, win_ref[slot, e], preferred_element_type=F32)
    o_ref[...] = _rmsnorm(y, out_gain_ref[...], RMS_EPS) if out_norm else y


def _combine(seg_start, seg_rows, x, gate, mask, ys, out_gain, out_norm):
    n, d = x.shape
    t = n // TT
    grid_spec = pltpu.PrefetchScalarGridSpec(
        num_scalar_prefetch=2,
        grid=(t,),
        in_specs=[pl.BlockSpec((TT, d), lambda i, *_: (i, 0)),
                  pl.BlockSpec((TT, LANES), lambda i, *_: (i, 0)),
                  pl.BlockSpec((TT, LANES), lambda i, *_: (i, 0)),
                  pl.BlockSpec((1, d), lambda i, *_: (0, 0)),
                  pl.BlockSpec(memory_space=pl.ANY)],
        out_specs=pl.BlockSpec((TT, d), lambda i, *_: (i, 0)),
        scratch_shapes=[pltpu.VMEM((2, N_EXPERTS, TT, d), BF16), pltpu.SemaphoreType.DMA((2,))],
    )
    return pl.pallas_call(
        functools.partial(_combine_kernel, out_norm=out_norm),
        grid_spec=grid_spec,
        out_shape=jax.ShapeDtypeStruct((n, d), F32),
        compiler_params=_cparams("arbitrary"),
        name="moe_combine",
    )(seg_start, seg_rows, x, gate, mask, out_gain.reshape(1, d), ys)


def _attn_out_moe_ffn(a, w_o, x, g, router, w_gate_up, w_down, layer, tf, out_gain, out_norm):
    n, d = x.shape
    t = n // TT
    x, h, gate, mask, mask_t, cnt = _attn_out_router(a, w_o, x, g, router)
    cnt = cnt[:, 0, :N_EXPERTS].astype(jnp.int32)
    seg_rows = _round_up(cnt, BF16_ROWS)
    expert_rows = jnp.sum(seg_rows, axis=0)
    region = _round_up(expert_rows, TE)
    region_end = jnp.cumsum(region)
    seg_start = (region_end - region)[None, :] + jnp.cumsum(seg_rows, axis=0) - seg_rows
    n_tiles = (_round_up(2 * n + t * N_EXPERTS * (BF16_ROWS - 1), TE) + N_EXPERTS * TE) // TE
    tile_end = region_end // TE
    tile_ids = jnp.arange(n_tiles, dtype=jnp.int32)
    last_tile = jnp.minimum(tile_ids, tile_end[-1] - 1)
    tile_expert = jnp.sum((last_tile[:, None] >= tile_end[None, :]).astype(jnp.int32), axis=1)
    tile_expert = jnp.minimum(tile_expert, N_EXPERTS - 1)
    rows_before = (tile_ids - (tile_end - region // TE)[tile_expert]) * TE
    tile_rows = jnp.where(tile_ids < tile_end[-1], jnp.clip(expert_rows[tile_expert] - rows_before, 0, TE), 0)
    seg_start = seg_start.reshape(-1).astype(jnp.int32)
    seg_rows = seg_rows.reshape(-1).astype(jnp.int32)

    xs = _dispatch(seg_start, seg_rows, mask_t, h, n_tiles * TE)
    ys = _expert_ffn(tile_expert, tile_rows.astype(jnp.int32), xs, w_gate_up, w_down, layer, tf)
    return _combine(seg_start, seg_rows, x, gate, mask, ys, out_gain, out_norm)


def kernel(x, attn_norm, ffn_norm, final_norm, t5_rel_bias, diff_w_qkv, diff_w_o, diff_lambda_q1, diff_lambda_k1, diff_lambda_q2, diff_lambda_k2, diff_subln, chunk_w_qkv, chunk_w_o, chunk_rel_bias, dense_w_gate_up, dense_w_down, moe_router, moe_w_gate_up, moe_w_down):
    b, s, d = x.shape
    n = b * s
    xf = x.reshape(n, d)
    t5_bias, t5_const_rows = _t5_bias_t(t5_rel_bias, s)
    for i in range(DEPTH):
        j = i // 2
        if i % 2 == 0:
            qkv = _norm_matmul(xf, attn_norm[i], diff_w_qkv[j].astype(BF16))
            lam_init = 0.8 - 0.6 * math.exp(-0.3 * i)
            lam_vecs = jnp.stack([diff_lambda_q1[j], diff_lambda_k1[j], diff_lambda_q2[j], diff_lambda_k2[j]])
            a = _diff_attention(qkv.reshape(b, s, 3 * d), t5_bias, t5_const_rows, lam_vecs, diff_subln[j], lam_init)
            xf = _attn_out_dense_ffn(a.reshape(n, d), diff_w_o[j].astype(BF16), xf, ffn_norm[i],
                                     dense_w_gate_up[j].astype(BF16), dense_w_down[j].astype(BF16),
                                     tf=dense_w_down.shape[1] // 2)
        else:
            qkv = _norm_matmul(xf, attn_norm[i], chunk_w_qkv[j].astype(BF16))
            a = _chunk_attention(qkv.reshape(b, s, 3 * d), _band_bias_t(chunk_rel_bias[j]))
            xf = _attn_out_moe_ffn(a.reshape(n, d), chunk_w_o[j].astype(BF16), xf, ffn_norm[i], moe_router[j],
                                   moe_w_gate_up, moe_w_down, layer=j, tf=moe_w_down.shape[2] // 7,
                                   out_gain=final_norm, out_norm=i == DEPTH - 1)
    return xf.reshape(b, s, d)
```

```python
import functools
import math

import numpy as np
import jax
import jax.numpy as jnp
from jax import lax
from jax.experimental import pallas as pl
from jax.experimental.pallas import tpu as pltpu

F32 = jnp.float32
BF16 = jnp.bfloat16

D_MODEL = 1024
DEPTH = 4
CHUNK = 64
DA_HEADS = 8
DA_HEAD_DIM = 64
T5_BUCKETS = 32
T5_MAX_DIST = 128
CA_HEADS = 16
CA_HEAD_DIM = 64
BAND_PREV = 8
REL_CLIP = 4 * CHUNK
N_EXPERTS = 8
RMS_EPS = 1e-6
SUBLN_EPS = 1e-5
NEG_INF = -1e30

LANES = 128
BF16_ROWS = 16
VMEM_LIMIT = 56 * 1024 * 1024
TM = 512
TE = 4 * TM
TQ = 256
TQ_BAND = 256
TT = 256
KB_T5 = 128
KB_BAND = 256
MM_ROWS = 2048
N_SLOTS = 4
ONES_ROWS = BF16_ROWS
LOG2E = math.log2(math.e)
Q_SCALE = DA_HEAD_DIM ** -0.5 * LOG2E
PAIR = 2 * CA_HEAD_DIM


def _round_up(a, m):
    return (a + m - 1) // m * m


def _cparams(*sem):
    return pltpu.CompilerParams(dimension_semantics=sem, vmem_limit_bytes=VMEM_LIMIT)


def _rmsnorm(x32, g, eps):
    return x32 * lax.rsqrt(jnp.mean(x32 * x32, axis=-1, keepdims=True) + eps) * g


def _dot_nt(a, b):
    return lax.dot_general(a, b, (((1,), (1,)), ((), ())), preferred_element_type=F32)


def _norm_matmul_kernel(x_ref, g_ref, w_ref, o_ref):
    d = x_ref.shape[1]
    half = x_ref.shape[0] // 2
    for rows in (slice(0, half), slice(half, 2 * half)):
        h = _rmsnorm(x_ref[rows, :], g_ref[...], RMS_EPS).astype(BF16)
        q = jnp.dot(h, w_ref[:, :d], preferred_element_type=F32) * Q_SCALE
        o_ref[rows, :d] = q.astype(o_ref.dtype)
        o_ref[rows, d:] = jnp.dot(h, w_ref[:, d:], preferred_element_type=F32).astype(o_ref.dtype)


def _norm_matmul(x, g, w):
    n, d = x.shape
    n_out = w.shape[1]
    return pl.pallas_call(
        _norm_matmul_kernel,
        grid=(n // TM,),
        in_specs=[pl.BlockSpec((TM, d), lambda i: (i, 0)),
                  pl.BlockSpec((1, d), lambda i: (0, 0)),
                  pl.BlockSpec((d, n_out), lambda i: (0, 0))],
        out_specs=pl.BlockSpec((TM, n_out), lambda i: (i, 0)),
        out_shape=jax.ShapeDtypeStruct((n, n_out), BF16),
        compiler_params=_cparams("parallel"),
        name="norm_qkv",
    )(x, g.reshape(1, d), w)


def _t5_bucket_np(rel):
    nb = T5_BUCKETS // 2
    max_exact = nb // 2
    ret = (rel > 0).astype(np.int32) * nb
    n = np.abs(rel)
    n_f = np.maximum(n, 1).astype(np.float32)
    large = max_exact + (np.log(n_f / np.float32(max_exact)) / np.float32(math.log(T5_MAX_DIST / max_exact))
                         * np.float32(nb - max_exact)).astype(np.int32)
    large = np.minimum(large, nb - 1)
    return ret + np.where(n < max_exact, n, large)


def _lookup_kernel(idx_ref, t_ref, o_ref):
    rows = lax.broadcasted_iota(jnp.int32, (t_ref.shape[1], idx_ref.shape[1]), 0)
    onehot = jnp.where(rows == idx_ref[...], 1.0, 0.0)
    o_ref[...] = jnp.dot(t_ref[...], onehot, preferred_element_type=F32, precision=lax.Precision.HIGHEST)


def _table_lookup_t(table, idx):
    r, h = table.shape
    table_t = jnp.zeros((h, _round_up(r, LANES)), F32).at[:, :r].set(table.T)
    return pl.pallas_call(
        _lookup_kernel,
        out_shape=jax.ShapeDtypeStruct((h, len(idx)), F32),
        name="bias_lookup",
    )(jnp.asarray(idx, jnp.int32).reshape(1, -1), table_t)


def _toeplitz_kernel(win_ref, mask_ref, o_ref):
    tq = o_ref.shape[1]
    for blk in range(o_ref.shape[0] // tq):
        x = jnp.broadcast_to(win_ref[:, blk * 2 * tq:(blk + 1) * 2 * tq], (tq, 2 * tq))
        skew = pltpu.roll(x, 0, 1, stride=1, stride_axis=0)
        o_ref[blk * tq:(blk + 1) * tq, :] = skew[:, :tq] + mask_ref[blk * tq:(blk + 1) * tq, :]


def _window_diagonals(n_keys, tq):
    m = np.arange(2 * tq)
    back = np.where(m < tq, -m, 2 * tq - m)
    d = np.arange(0, n_keys, tq)[:, None] + back[None, :]
    return np.clip(d, 1 - tq, n_keys - 1).reshape(-1)


def _toeplitz_bias_t(table, idx, allowed):
    win = _table_lookup_t(table, idx) * LOG2E
    h = win.shape[0]
    n_keys, tq = allowed.shape
    mask_add = jnp.asarray(np.where(allowed, 0.0, NEG_INF).astype(np.float32))
    return pl.pallas_call(
        _toeplitz_kernel,
        grid=(h,),
        in_specs=[pl.BlockSpec((None, 1, win.shape[1]), lambda i: (i, 0, 0)),
                  pl.BlockSpec((n_keys, tq), lambda i: (0, 0))],
        out_specs=pl.BlockSpec((None, n_keys, tq), lambda i: (i, 0, 0)),
        out_shape=jax.ShapeDtypeStruct((h, n_keys, tq), F32),
        compiler_params=_cparams("parallel"),
        name="toeplitz_bias",
    )(win.reshape(h, 1, -1), mask_add)


def _t5_bias_t(table, seq):
    idx = _t5_bucket_np(_window_diagonals(seq, TQ) - (seq - TQ))
    i = np.arange(TQ)[None, :]
    j = np.arange(seq)[:, None] - (seq - TQ)
    allowed = (j < 0) | ((j // CHUNK) <= (i // CHUNK))
    far_bucket = _t5_bucket_np(np.asarray(-seq))
    is_far = np.all((_t5_bucket_np(j - i) == far_bucket) & allowed, axis=1)
    const_rows = int(np.argmin(is_far)) if not is_far.all() else seq
    return _toeplitz_bias_t(table, idx, allowed), const_rows


def _load_vt(v_ref, vt_ref):
    n_v = v_ref.shape[1]
    vt_ref[0:n_v, :] = v_ref[...].T
    vt_ref[n_v:, :] = jnp.ones((ONES_ROWS, v_ref.shape[0]), BF16)


def _attend_pair_phases(k_ref, k0, n_k, kb_rows, q, lane_split, bias_rows, bias_const, vt_ref, s_ref, p_ref, finish):
    tq = q.shape[0]
    halves = (slice(0, tq), slice(tq, 2 * tq))
    n_v = vt_ref.shape[0] - ONES_ROWS
    state = {"m": [None, None], "o": None}

    def score(kr):
        def run():
            if kr == 0:
                zero = jnp.zeros_like(q)
                state["qq"] = jnp.concatenate([jnp.where(lane_split, q, zero), jnp.where(lane_split, zero, q)], axis=0)
            rows = min(MM_ROWS, n_k - kr)
            s_ref[kr:kr + rows, :] = _dot_nt(k_ref[k0 + kr:k0 + kr + rows, :], state["qq"])
        return run

    def running_max(kb, h):
        def run():
            cols = halves[h]
            if bias_const(kb):
                block_max = jnp.max(s_ref[kb:kb + kb_rows, cols], axis=0, keepdims=True) + bias_rows(h, kb)
            else:
                s = s_ref[kb:kb + kb_rows, cols] + bias_rows(h, kb)
                s_ref[kb:kb + kb_rows, cols] = s
                block_max = jnp.max(s, axis=0, keepdims=True)
            m = state["m"]
            m[h] = block_max if m[h] is None else jnp.maximum(m[h], block_max)
        return run

    def exponentials(kb, h):
        def run():
            cols = halves[h]
            m = state["m"][h]
            shift = m - bias_rows(h, kb) if bias_const(kb) else m
            p_ref[kb:kb + kb_rows, cols] = jnp.exp2(s_ref[kb:kb + kb_rows, cols] - shift).astype(BF16)
        return run

    def pv(kr):
        def run():
            rows = min(MM_ROWS, n_k - kr)
            part = jnp.dot(vt_ref[:, k0 + kr:k0 + kr + rows], p_ref[kr:kr + rows, :], preferred_element_type=F32)
            state["o"] = part if state["o"] is None else state["o"] + part
            if kr + rows == n_k:
                o_t = state["o"]
                finish([o_t[0:n_v, cols] * (1.0 / o_t[n_v:n_v + 1, cols]) for cols in halves])
        return run

    blocks = [(kb, h) for kb in range(0, n_k, kb_rows) for h in range(2)]
    return ([score(kr) for kr in range(0, n_k, MM_ROWS)], [running_max(kb, h) for kb, h in blocks],
            [exponentials(kb, h) for kb, h in blocks], [pv(kr) for kr in range(0, n_k, MM_ROWS)])


def _interleave(tiles):
    def merged(a, b):
        order = sorted([((i + 0.5) / len(a), 0, f) for i, f in enumerate(a)]
                       + [((i + 0.5) / len(b), 1, f) for i, f in enumerate(b)], key=lambda t: t[:2])
        for _, _, thunk in order:
            thunk()

    empty = ([], [], [], [])
    prev = empty
    for cur in list(tiles) + [empty]:
        merged(cur[0], prev[2])
        merged(cur[1], prev[3])
        prev = cur


def _diff_attn_kernel(q_ref, k_ref, v_ref, bias_ref, lam_ref, g_ref, o_ref, vt_ref, s_ref, p_ref, *,
                      lam_init, const_rows):
    lv = lam_ref[...]
    lam = (jnp.exp(jnp.sum(lv[0:1] * lv[1:2], axis=-1, keepdims=True))
           - jnp.exp(jnp.sum(lv[2:3] * lv[3:4], axis=-1, keepdims=True)) + lam_init)
    first_map = lax.broadcasted_iota(jnp.int32, (1, 2 * DA_HEAD_DIM), 1) < DA_HEAD_DIM
    seq = q_ref.shape[0]
    _load_vt(v_ref, vt_ref)

    def tile(qi):
        n_k = (qi + 1) * TQ
        row0 = seq - n_k

        def bias_const(kb):
            return row0 + kb + KB_T5 <= const_rows

        def bias_rows(h, kb):
            n = 1 if bias_const(kb) else KB_T5
            return bias_ref[row0 + kb:row0 + kb + n, :]

        def finish(outs):
            o_t = outs[0] - lam * outs[1]
            o_t = o_t * lax.rsqrt(jnp.mean(o_t * o_t, axis=0, keepdims=True) + SUBLN_EPS) * g_ref[...]
            o_ref[qi * TQ:(qi + 1) * TQ, :] = (o_t * (1.0 - lam_init)).T.astype(o_ref.dtype)

        slot = qi % N_SLOTS
        return _attend_pair_phases(k_ref, 0, n_k, KB_T5, q_ref[qi * TQ:(qi + 1) * TQ, :], first_map, bias_rows, bias_const,
                                  vt_ref, s_ref.at[slot], p_ref.at[slot], finish)

    _interleave(tile(qi) for qi in range(seq // TQ))


def _diff_attention(qkv, bias_t, const_rows, lam_vecs, subln_g, lam_init):
    b, s, _ = qkv.shape
    hd = 2 * DA_HEAD_DIM
    return pl.pallas_call(
        functools.partial(_diff_attn_kernel, lam_init=lam_init, const_rows=const_rows),
        grid=(b, DA_HEADS),
        in_specs=[pl.BlockSpec((None, s, hd), lambda i, h: (i, 0, h)),
                  pl.BlockSpec((None, s, hd), lambda i, h: (i, 0, DA_HEADS + h)),
                  pl.BlockSpec((None, s, hd), lambda i, h: (i, 0, 2 * DA_HEADS + h)),
                  pl.BlockSpec((None, s, TQ), lambda i, h: (h, 0, 0)),
                  pl.BlockSpec((4, DA_HEAD_DIM), lambda i, h: (0, 0)),
                  pl.BlockSpec((hd, 1), lambda i, h: (0, 0))],
        out_specs=pl.BlockSpec((None, s, hd), lambda i, h: (i, 0, h)),
        out_shape=jax.ShapeDtypeStruct((b, s, D_MODEL), BF16),
        scratch_shapes=[pltpu.VMEM((hd + ONES_ROWS, s), BF16), pltpu.VMEM((N_SLOTS, s, 2 * TQ), F32),
                        pltpu.VMEM((N_SLOTS, s, 2 * TQ), BF16)],
        compiler_params=_cparams("parallel", "parallel"),
        name="diff_attn",
    )(qkv, qkv, qkv, bias_t, lam_vecs, subln_g.reshape(hd, 1))


BAND_KEYS = BAND_PREV * CHUNK + TQ_BAND


def _band_bias_t(rel_table):
    dist = BAND_PREV * CHUNK - _window_diagonals(BAND_KEYS, TQ_BAND)
    q_chunk = np.arange(TQ_BAND)[None, :] // CHUNK
    k_chunk = np.arange(BAND_KEYS)[:, None] // CHUNK
    allowed = (k_chunk >= q_chunk) & (k_chunk <= q_chunk + BAND_PREV)
    return _toeplitz_bias_t(rel_table, np.clip(dist, -REL_CLIP, REL_CLIP) + REL_CLIP, allowed)


def _chunk_attn_kernel(q_ref, k_ref, v_ref, bias_ref, o_ref, vt_ref, s_ref, p_ref):
    first_head = lax.broadcasted_iota(jnp.int32, (1, PAIR), 1) < CA_HEAD_DIM
    first_head_rows = lax.broadcasted_iota(jnp.int32, (PAIR, 1), 0) < CA_HEAD_DIM
    seq = q_ref.shape[0]
    _load_vt(v_ref, vt_ref)

    def tile(qi):
        q0 = qi * TQ_BAND
        k0 = max(0, q0 - BAND_PREV * CHUNK)
        n_k = q0 + TQ_BAND - k0
        row0 = BAND_KEYS - n_k

        def bias_rows(h, kb):
            return bias_ref[h, row0 + kb:row0 + kb + KB_BAND, :]

        def finish(outs):
            o_ref[q0:q0 + TQ_BAND, :] = jnp.where(first_head_rows, outs[0], outs[1]).T.astype(o_ref.dtype)

        slot = qi % N_SLOTS
        return _attend_pair_phases(k_ref, k0, n_k, KB_BAND, q_ref[q0:q0 + TQ_BAND, :], first_head, bias_rows, lambda kb: False,
                                  vt_ref, s_ref.at[slot], p_ref.at[slot], finish)

    _interleave(tile(qi) for qi in range(seq // TQ_BAND))


def _chunk_attention(qkv, band_bias_t):
    b, s, _ = qkv.shape
    n_pairs = CA_HEADS // 2
    return pl.pallas_call(
        _chunk_attn_kernel,
        grid=(b, n_pairs),
        in_specs=[pl.BlockSpec((None, s, PAIR), lambda i, p: (i, 0, p)),
                  pl.BlockSpec((None, s, PAIR), lambda i, p: (i, 0, n_pairs + p)),
                  pl.BlockSpec((None, s, PAIR), lambda i, p: (i, 0, 2 * n_pairs + p)),
                  pl.BlockSpec((2, BAND_KEYS, TQ_BAND), lambda i, p: (p, 0, 0))],
        out_specs=pl.BlockSpec((None, s, PAIR), lambda i, p: (i, 0, p)),
        out_shape=jax.ShapeDtypeStruct((b, s, D_MODEL), BF16),
        scratch_shapes=[pltpu.VMEM((PAIR + ONES_ROWS, s), BF16), pltpu.VMEM((N_SLOTS, BAND_KEYS, 2 * TQ_BAND), F32),
                        pltpu.VMEM((N_SLOTS, BAND_KEYS, 2 * TQ_BAND), BF16)],
        compiler_params=_cparams("parallel", "parallel"),
        name="chunk_attn",
    )(qkv, qkv, qkv, band_bias_t)


def _swiglu_partial(h, wg, wu, wd):
    g = jnp.dot(h, wg, preferred_element_type=F32)
    u = jnp.dot(h, wu, preferred_element_type=F32)
    a = (g * (0.5 * jnp.tanh(0.5 * g) + 0.5) * u).astype(BF16)
    return jnp.dot(a, wd, preferred_element_type=F32)


def _dense_ffn_kernel(a_ref, wo_ref, x_ref, g_ref, wg_ref, wu_ref, wd_ref, o_ref, x1_ref, h_ref, acc_ref):
    f = pl.program_id(1)

    @pl.when(f == 0)
    def _():
        x1 = x_ref[...] + jnp.dot(a_ref[...], wo_ref[...], preferred_element_type=F32)
        x1_ref[...] = x1
        h_ref[...] = _rmsnorm(x1, g_ref[...], RMS_EPS).astype(BF16)
        acc_ref[...] = jnp.zeros_like(acc_ref)

    acc_ref[...] += _swiglu_partial(h_ref[...], wg_ref[...], wu_ref[...], wd_ref[...])

    @pl.when(f == pl.num_programs(1) - 1)
    def _():
        o_ref[...] = x1_ref[...] + acc_ref[...]


def _attn_out_dense_ffn(a, w_o, x, g, w_gate_up, w_down, tf):
    n, d = x.shape
    ff = w_down.shape[0]
    nf = ff // tf
    return pl.pallas_call(
        _dense_ffn_kernel,
        grid=(n // TM, nf),
        in_specs=[pl.BlockSpec((TM, d), lambda i, f: (i, 0)),
                  pl.BlockSpec((d, d), lambda i, f: (0, 0)),
                  pl.BlockSpec((TM, d), lambda i, f: (i, 0)),
                  pl.BlockSpec((1, d), lambda i, f: (0, 0)),
                  pl.BlockSpec((d, tf), lambda i, f: (0, f)),
                  pl.BlockSpec((d, tf), lambda i, f: (0, nf + f)),
                  pl.BlockSpec((tf, d), lambda i, f: (f, 0))],
        out_specs=pl.BlockSpec((TM, d), lambda i, f: (i, 0)),
        out_shape=jax.ShapeDtypeStruct((n, d), F32),
        scratch_shapes=[pltpu.VMEM((TM, d), F32), pltpu.VMEM((TM, d), BF16), pltpu.VMEM((TM, d), F32)],
        compiler_params=_cparams("parallel", "arbitrary"),
        name="dense_ffn",
    )(a, w_o, x, g.reshape(1, d), w_gate_up, w_gate_up, w_down)


def _router_kernel(a_ref, wo_ref, x_ref, g_ref, r_ref, x1_ref, h_ref, gate_ref, mask_ref, mask_t_ref, cnt_ref):
    x1 = x_ref[...] + jnp.dot(a_ref[...], wo_ref[...], preferred_element_type=F32)
    x1_ref[...] = x1
    h = _rmsnorm(x1, g_ref[...], RMS_EPS)
    h_hi = h.astype(BF16)
    h_ref[...] = h_hi
    r = r_ref[...]
    r_hi = r.astype(BF16)
    r_lo = (r - r_hi.astype(F32)).astype(BF16)
    h_lo = (h - h_hi.astype(F32)).astype(BF16)
    logits = (jnp.dot(h_hi, r_hi, preferred_element_type=F32)
              + (jnp.dot(h_hi, r_lo, preferred_element_type=F32) + jnp.dot(h_lo, r_hi, preferred_element_type=F32)))
    lane = lax.broadcasted_iota(jnp.int32, logits.shape, 1)
    logits = jnp.where(lane < N_EXPERTS, logits, -jnp.inf)
    v1 = jnp.max(logits, axis=-1, keepdims=True)
    i1 = jnp.min(jnp.where(logits == v1, lane, LANES), axis=-1, keepdims=True)
    m1 = lane == i1
    rest = jnp.where(m1, -jnp.inf, logits)
    v2 = jnp.max(rest, axis=-1, keepdims=True)
    i2 = jnp.min(jnp.where(rest == v2, lane, LANES), axis=-1, keepdims=True)
    m2 = lane == i2
    e2 = jnp.exp(v2 - v1)
    g1 = 1.0 / (1.0 + e2)
    g2 = e2 / (1.0 + e2)
    gate_ref[...] = jnp.where(m1, g1, 0.0) + jnp.where(m2, g2, 0.0)
    mask = jnp.where(m1 | m2, 1.0, 0.0)
    mask_ref[...] = mask
    mask_t_ref[...] = mask.T[0:N_EXPERTS, :]
    cnt_ref[...] = jnp.broadcast_to(jnp.sum(mask, axis=0, keepdims=True), cnt_ref.shape)


def _attn_out_router(a, w_o, x, g, router):
    n, d = x.shape
    t = n // TT
    r_pad = jnp.zeros((d, LANES), F32).at[:, :N_EXPERTS].set(router)
    return pl.pallas_call(
        _router_kernel,
        grid=(t,),
        in_specs=[pl.BlockSpec((TT, d), lambda i: (i, 0)),
                  pl.BlockSpec((d, d), lambda i: (0, 0)),
                  pl.BlockSpec((TT, d), lambda i: (i, 0)),
                  pl.BlockSpec((1, d), lambda i: (0, 0)),
                  pl.BlockSpec((d, LANES), lambda i: (0, 0))],
        out_specs=[pl.BlockSpec((TT, d), lambda i: (i, 0)),
                   pl.BlockSpec((TT, d), lambda i: (i, 0)),
                   pl.BlockSpec((TT, LANES), lambda i: (i, 0)),
                   pl.BlockSpec((TT, LANES), lambda i: (i, 0)),
                   pl.BlockSpec((N_EXPERTS, TT), lambda i: (0, i)),
                   pl.BlockSpec((None, 8, LANES), lambda i: (i, 0, 0))],
        out_shape=[jax.ShapeDtypeStruct((n, d), F32),
                   jax.ShapeDtypeStruct((n, d), BF16),
                   jax.ShapeDtypeStruct((n, LANES), F32),
                   jax.ShapeDtypeStruct((n, LANES), F32),
                   jax.ShapeDtypeStruct((N_EXPERTS, n), F32),
                   jax.ShapeDtypeStruct((t, 8, LANES), F32)],
        compiler_params=_cparams("parallel"),
        name="router",
    )(a, w_o, x, g.reshape(1, d), r_pad)


def _segment_copies(n_rows, make_copy):
    off = jnp.int32(0)
    size = TT
    while size >= BF16_ROWS:
        take = (n_rows & size) != 0

        @pl.when(take)
        def _(off=off, size=size):
            make_copy(off, size)

        off = off + jnp.where(take, size, 0)
        size //= 2


def _dispatch_kernel(start_ref, rows_ref, mask_t_ref, h_ref, xs_in_ref, xs_ref, blk_ref, sem_ref):
    del xs_in_ref
    t = pl.program_id(0)
    slot = t % 2
    mask_t = mask_t_ref[...]
    row = lax.broadcasted_iota(jnp.int32, (TT, TT), 0)
    col = lax.broadcasted_iota(jnp.int32, (TT, TT), 1)
    before = jnp.where(row < col, 1.0, 0.0).astype(BF16)
    rank = jnp.dot(mask_t.astype(BF16), before, preferred_element_type=F32)
    row_f = row.astype(F32)
    sels = []
    for e in range(N_EXPERTS):
        pick = (row_f == rank[e:e + 1, :]) & (mask_t[e:e + 1, :] > 0.0)
        sels.append(jnp.where(pick, 1.0, 0.0).astype(BF16))
    packed = jnp.dot(jnp.concatenate(sels, axis=0), h_ref[...], preferred_element_type=F32)
    blk_ref[slot] = packed.astype(BF16).reshape(blk_ref.shape[1:])

    def copies(step, step_slot, wait):
        for e in range(N_EXPERTS):
            start = pl.multiple_of(start_ref[step * N_EXPERTS + e], BF16_ROWS)

            def make_copy(off, size, e=e, start=start):
                src = blk_ref.at[step_slot, e, pl.ds(pl.multiple_of(off, BF16_ROWS), size)]
                dst = xs_ref.at[pl.ds(pl.multiple_of(start + off, BF16_ROWS), size)]
                cp = pltpu.make_async_copy(src, dst, sem_ref.at[step_slot])
                cp.wait() if wait else cp.start()

            _segment_copies(rows_ref[step * N_EXPERTS + e], make_copy)

    copies(t, slot, wait=False)

    @pl.when(t > 0)
    def _():
        copies(t - 1, 1 - slot, wait=True)

    @pl.when(t == pl.num_programs(0) - 1)
    def _():
        copies(t, slot, wait=True)


def _dispatch(seg_start, seg_rows, mask_t, h, n_rows):
    n, d = h.shape
    t = n // TT
    xs0 = jnp.zeros((n_rows, d), BF16)
    grid_spec = pltpu.PrefetchScalarGridSpec(
        num_scalar_prefetch=2,
        grid=(t,),
        in_specs=[pl.BlockSpec((N_EXPERTS, TT), lambda i, *_: (0, i)),
                  pl.BlockSpec((TT, d), lambda i, *_: (i, 0)),
                  pl.BlockSpec(memory_space=pl.ANY)],
        out_specs=pl.BlockSpec(memory_space=pl.ANY),
        scratch_shapes=[pltpu.VMEM((2, N_EXPERTS, TT, d), BF16), pltpu.SemaphoreType.DMA((2,))],
    )
    return pl.pallas_call(
        _dispatch_kernel,
        grid_spec=grid_spec,
        out_shape=jax.ShapeDtypeStruct((n_rows, d), BF16),
        input_output_aliases={4: 0},
        compiler_params=_cparams("arbitrary"),
        name="moe_dispatch",
    )(seg_start, seg_rows, mask_t, h, xs0)


def _expert_kernel(tile_expert_ref, tile_rows_ref, xs_ref, wg_ref, wu_ref, wd_ref, o_ref,
                   wg_bf_ref, wu_bf_ref, wd_bf_ref, acc_ref):
    del tile_expert_ref
    j = pl.program_id(0)
    f = pl.program_id(1)
    last = pl.num_programs(1) - 1
    rows = tile_rows_ref[j]

    @pl.when(f == 0)
    def _():
        acc_ref[...] = jnp.zeros_like(acc_ref)

    @pl.when(rows == TE)
    def _():
        acc_ref[...] += _swiglu_partial(xs_ref[...], wg_ref[...].astype(BF16), wu_ref[...].astype(BF16),
                                        wd_ref[...].astype(BF16))

    @pl.when((rows > 0) & (rows < TE))
    def _():
        wg_bf_ref[...] = wg_ref[...].astype(BF16)
        wu_bf_ref[...] = wu_ref[...].astype(BF16)
        wd_bf_ref[...] = wd_ref[...].astype(BF16)
        for lo in range(0, TE, TM):
            @pl.when(rows > lo)
            def _(lo=lo):
                acc_ref[lo:lo + TM, :] += _swiglu_partial(xs_ref[lo:lo + TM, :], wg_bf_ref[...], wu_bf_ref[...],
                                                          wd_bf_ref[...])

    @pl.when(f == last)
    def _():
        o_ref[...] = acc_ref[...].astype(o_ref.dtype)


def _expert_ffn(tile_expert, tile_rows, xs, w_gate_up, w_down, layer, tf):
    n_rows, d = xs.shape
    ff = w_down.shape[2]
    nf = ff // tf

    def f_eff(j, f, tr):
        return jnp.where(tr[j] > 0, f, nf - 1)

    grid_spec = pltpu.PrefetchScalarGridSpec(
        num_scalar_prefetch=2,
        grid=(n_rows // TE, nf),
        in_specs=[pl.BlockSpec((TE, d), lambda j, f, te, tr: (j, 0)),
                  pl.BlockSpec((None, None, d, tf), lambda j, f, te, tr: (layer, te[j], 0, f_eff(j, f, tr))),
                  pl.BlockSpec((None, None, d, tf), lambda j, f, te, tr: (layer, te[j], 0, nf + f_eff(j, f, tr))),
                  pl.BlockSpec((None, None, tf, d), lambda j, f, te, tr: (layer, te[j], f_eff(j, f, tr), 0))],
        out_specs=pl.BlockSpec((TE, d), lambda j, f, te, tr: (j, 0)),
        scratch_shapes=[pltpu.VMEM((d, tf), BF16), pltpu.VMEM((d, tf), BF16), pltpu.VMEM((tf, d), BF16),
                        pltpu.VMEM((TE, d), F32)],
    )
    return pl.pallas_call(
        _expert_kernel,
        grid_spec=grid_spec,
        out_shape=jax.ShapeDtypeStruct((n_rows, d), BF16),
        compiler_params=_cparams("parallel", "arbitrary"),
        name="moe_experts",
    )(tile_expert, tile_rows, xs, w_gate_up, w_gate_up, w_down)


def _combine_kernel(start_ref, rows_ref, x_ref, gate_ref, mask_ref, out_gain_ref, ys_ref, o_ref, win_ref, sem_ref,
                    *, out_norm):
    t = pl.program_id(0)
    n_t = pl.num_programs(0)
    slot = t % 2

    def copies(step, step_slot, wait):
        for e in range(N_EXPERTS):
            start = pl.multiple_of(start_ref[step * N_EXPERTS + e], BF16_ROWS)

            def make_copy(off, size, e=e, start=start):
                src = ys_ref.at[pl.ds(pl.multiple_of(start + off, BF16_ROWS), size)]
                dst = win_ref.at[step_slot, e, pl.ds(pl.multiple_of(off, BF16_ROWS), size)]
                cp = pltpu.make_async_copy(src, dst, sem_ref.at[step_slot])
                cp.wait() if wait else cp.start()

            _segment_copies(rows_ref[step * N_EXPERTS + e], make_copy)

    @pl.when(t == 0)
    def _():
        win_ref[...] = jnp.zeros_like(win_ref)
        copies(0, 0, wait=False)

    @pl.when(t + 1 < n_t)
    def _():
        copies(t + 1, 1 - slot, wait=False)

    copies(t, slot, wait=True)

    mask = mask_ref[...]
    gate = gate_ref[...]
    row = lax.broadcasted_iota(jnp.int32, (TT, TT), 0)
    col = lax.broadcasted_iota(jnp.int32, (TT, TT), 1)
    before = jnp.where(col < row, 1.0, 0.0).astype(BF16)
    rank = jnp.dot(before, mask.astype(BF16), preferred_element_type=F32)
    col_f = col.astype(F32)
    y = x_ref[...]
    for e in range(N_EXPERTS):
        pick = (col_f == rank[:, e:e + 1]) & (mask[:, e:e + 1] > 0.0)
        sel = jnp.where(pick, 1.0, 0.0).astype(BF16)
        y = y + gate[:, e:e + 1] * jnp.dot(sel, win_ref[slot, e], preferred_element_type=F32)
    o_ref[...] = _rmsnorm(y, out_gain_ref[...], RMS_EPS) if out_norm else y


def _combine(seg_start, seg_rows, x, gate, mask, ys, out_gain, out_norm):
    n, d = x.shape
    t = n // TT
    grid_spec = pltpu.PrefetchScalarGridSpec(
        num_scalar_prefetch=2,
        grid=(t,),
        in_specs=[pl.BlockSpec((TT, d), lambda i, *_: (i, 0)),
                  pl.BlockSpec((TT, LANES), lambda i, *_: (i, 0)),
                  pl.BlockSpec((TT, LANES), lambda i, *_: (i, 0)),
                  pl.BlockSpec((1, d), lambda i, *_: (0, 0)),
                  pl.BlockSpec(memory_space=pl.ANY)],
        out_specs=pl.BlockSpec((TT, d), lambda i, *_: (i, 0)),
        scratch_shapes=[pltpu.VMEM((2, N_EXPERTS, TT, d), BF16), pltpu.SemaphoreType.DMA((2,))],
    )
    return pl.pallas_call(
        functools.partial(_combine_kernel, out_norm=out_norm),
        grid_spec=grid_spec,
        out_shape=jax.ShapeDtypeStruct((n, d), F32),
        compiler_params=_cparams("arbitrary"),
        name="moe_combine",
    )(seg_start, seg_rows, x, gate, mask, out_gain.reshape(1, d), ys)


def _attn_out_moe_ffn(a, w_o, x, g, router, w_gate_up, w_down, layer, tf, out_gain, out_norm):
    n, d = x.shape
    t = n // TT
    x, h, gate, mask, mask_t, cnt = _attn_out_router(a, w_o, x, g, router)
    cnt = cnt[:, 0, :N_EXPERTS].astype(jnp.int32)
    seg_rows = _round_up(cnt, BF16_ROWS)
    expert_rows = jnp.sum(seg_rows, axis=0)
    region = _round_up(expert_rows, TE)
    region_end = jnp.cumsum(region)
    seg_start = (region_end - region)[None, :] + jnp.cumsum(seg_rows, axis=0) - seg_rows
    n_tiles = (_round_up(2 * n + t * N_EXPERTS * (BF16_ROWS - 1), TE) + N_EXPERTS * TE) // TE
    tile_end = region_end // TE
    tile_ids = jnp.arange(n_tiles, dtype=jnp.int32)
    last_tile = jnp.minimum(tile_ids, tile_end[-1] - 1)
    tile_expert = jnp.sum((last_tile[:, None] >= tile_end[None, :]).astype(jnp.int32), axis=1)
    tile_expert = jnp.minimum(tile_expert, N_EXPERTS - 1)
    rows_before = (tile_ids - (tile_end - region // TE)[tile_expert]) * TE
    tile_rows = jnp.where(tile_ids < tile_end[-1], jnp.clip(expert_rows[tile_expert] - rows_before, 0, TE), 0)
    seg_start = seg_start.reshape(-1).astype(jnp.int32)
    seg_rows = seg_rows.reshape(-1).astype(jnp.int32)

    xs = _dispatch(seg_start, seg_rows, mask_t, h, n_tiles * TE)
    ys = _expert_ffn(tile_expert, tile_rows.astype(jnp.int32), xs, w_gate_up, w_down, layer, tf)
    return _combine(seg_start, seg_rows, x, gate, mask, ys, out_gain, out_norm)


def kernel(x, attn_norm, ffn_norm, final_norm, t5_rel_bias, diff_w_qkv, diff_w_o, diff_lambda_q1, diff_lambda_k1, diff_lambda_q2, diff_lambda_k2, diff_subln, chunk_w_qkv, chunk_w_o, chunk_rel_bias, dense_w_gate_up, dense_w_down, moe_router, moe_w_gate_up, moe_w_down):
    b, s, d = x.shape
    n = b * s
    xf = x.reshape(n, d)
    t5_bias, t5_const_rows = _t5_bias_t(t5_rel_bias, s)
    for i in range(DEPTH):
        j = i // 2
        if i % 2 == 0:
            qkv = _norm_matmul(xf, attn_norm[i], diff_w_qkv[j].astype(BF16))
            lam_init = 0.8 - 0.6 * math.exp(-0.3 * i)
            lam_vecs = jnp.stack([diff_lambda_q1[j], diff_lambda_k1[j], diff_lambda_q2[j], diff_lambda_k2[j]])
            a = _diff_attention(qkv.reshape(b, s, 3 * d), t5_bias, t5_const_rows, lam_vecs, diff_subln[j], lam_init)
            xf = _attn_out_dense_ffn(a.reshape(n, d), diff_w_o[j].astype(BF16), xf, ffn_norm[i],
                                     dense_w_gate_up[j].astype(BF16), dense_w_down[j].astype(BF16),
                                     tf=dense_w_down.shape[1] // 2)
        else:
            qkv = _norm_matmul(xf, attn_norm[i], chunk_w_qkv[j].astype(BF16))
            a = _chunk_attention(qkv.reshape(b, s, 3 * d), _band_bias_t(chunk_rel_bias[j]))
            xf = _attn_out_moe_ffn(a.reshape(n, d), chunk_w_o[j].astype(BF16), xf, ffn_norm[i], moe_router[j],
                                   moe_w_gate_up, moe_w_down, layer=j, tf=moe_w_down.shape[2] // 7,
                                   out_gain=final_norm, out_norm=i == DEPTH - 1)
    return xf.reshape(b, s, d)
```

```python
import functools
import math

import numpy as np
import jax
import jax.numpy as jnp
from jax import lax
from jax.experimental import pallas as pl
from jax.experimental.pallas import tpu as pltpu

F32 = jnp.float32
BF16 = jnp.bfloat16

D_MODEL = 1024
DEPTH = 4
CHUNK = 64
DA_HEADS = 8
DA_HEAD_DIM = 64
T5_BUCKETS = 32
T5_MAX_DIST = 128
CA_HEADS = 16
CA_HEAD_DIM = 64
BAND_PREV = 8
REL_CLIP = 4 * CHUNK
N_EXPERTS = 8
RMS_EPS = 1e-6
SUBLN_EPS = 1e-5
NEG_INF = -1e30

LANES = 128
BF16_ROWS = 16
VMEM_LIMIT = 56 * 1024 * 1024
TM = 512
TE = 4 * TM
TQ = 256
TQ_BAND = 256
TT = 256
KB_T5 = 128
KB_BAND = 256
MM_ROWS = 2048
N_SLOTS = 4
ONES_ROWS = BF16_ROWS
LOG2E = math.log2(math.e)
Q_SCALE = DA_HEAD_DIM ** -0.5 * LOG2E
PAIR = 2 * CA_HEAD_DIM


def _round_up(a, m):
    return (a + m - 1) // m * m


def _cparams(*sem):
    return pltpu.CompilerParams(dimension_semantics=sem, vmem_limit_bytes=VMEM_LIMIT)


def _rmsnorm(x32, g, eps):
    return x32 * lax.rsqrt(jnp.mean(x32 * x32, axis=-1, keepdims=True) + eps) * g


def _dot_nt(a, b):
    return lax.dot_general(a, b, (((1,), (1,)), ((), ())), preferred_element_type=F32)


def _norm_matmul_kernel(x_ref, g_ref, w_ref, o_ref):
    d = x_ref.shape[1]
    half = x_ref.shape[0] // 2
    for rows in (slice(0, half), slice(half, 2 * half)):
        h = _rmsnorm(x_ref[rows, :], g_ref[...], RMS_EPS).astype(BF16)
        q = jnp.dot(h, w_ref[:, :d], preferred_element_type=F32) * Q_SCALE
        o_ref[rows, :d] = q.astype(o_ref.dtype)
        o_ref[rows, d:] = jnp.dot(h, w_ref[:, d:], preferred_element_type=F32).astype(o_ref.dtype)


def _norm_matmul(x, g, w):
    n, d = x.shape
    n_out = w.shape[1]
    return pl.pallas_call(
        _norm_matmul_kernel,
        grid=(n // TM,),
        in_specs=[pl.BlockSpec((TM, d), lambda i: (i, 0)),
                  pl.BlockSpec((1, d), lambda i: (0, 0)),
                  pl.BlockSpec((d, n_out), lambda i: (0, 0))],
        out_specs=pl.BlockSpec((TM, n_out), lambda i: (i, 0)),
        out_shape=jax.ShapeDtypeStruct((n, n_out), BF16),
        compiler_params=_cparams("parallel"),
        name="norm_qkv",
    )(x, g.reshape(1, d), w)


def _t5_bucket_np(rel):
    nb = T5_BUCKETS // 2
    max_exact = nb // 2
    ret = (rel > 0).astype(np.int32) * nb
    n = np.abs(rel)
    n_f = np.maximum(n, 1).astype(np.float32)
    large = max_exact + (np.log(n_f / np.float32(max_exact)) / np.float32(math.log(T5_MAX_DIST / max_exact))
                         * np.float32(nb - max_exact)).astype(np.int32)
    large = np.minimum(large, nb - 1)
    return ret + np.where(n < max_exact, n, large)


def _lookup_kernel(idx_ref, t_ref, o_ref):
    rows = lax.broadcasted_iota(jnp.int32, (t_ref.shape[1], idx_ref.shape[1]), 0)
    onehot = jnp.where(rows == idx_ref[...], 1.0, 0.0)
    o_ref[...] = jnp.dot(t_ref[...], onehot, preferred_element_type=F32, precision=lax.Precision.HIGHEST)


def _table_lookup_t(table, idx):
    r, h = table.shape
    table_t = jnp.zeros((h, _round_up(r, LANES)), F32).at[:, :r].set(table.T)
    return pl.pallas_call(
        _lookup_kernel,
        out_shape=jax.ShapeDtypeStruct((h, len(idx)), F32),
        name="bias_lookup",
    )(jnp.asarray(idx, jnp.int32).reshape(1, -1), table_t)


def _toeplitz_kernel(win_ref, mask_ref, o_ref):
    tq = o_ref.shape[1]
    for blk in range(o_ref.shape[0] // tq):
        x = jnp.broadcast_to(win_ref[:, blk * 2 * tq:(blk + 1) * 2 * tq], (tq, 2 * tq))
        skew = pltpu.roll(x, 0, 1, stride=1, stride_axis=0)
        o_ref[blk * tq:(blk + 1) * tq, :] = skew[:, :tq] + mask_ref[blk * tq:(blk + 1) * tq, :]


def _window_diagonals(n_keys, tq):
    m = np.arange(2 * tq)
    back = np.where(m < tq, -m, 2 * tq - m)
    d = np.arange(0, n_keys, tq)[:, None] + back[None, :]
    return np.clip(d, 1 - tq, n_keys - 1).reshape(-1)


def _toeplitz_bias_t(table, idx, allowed):
    win = _table_lookup_t(table, idx) * LOG2E
    h = win.shape[0]
    n_keys, tq = allowed.shape
    mask_add = jnp.asarray(np.where(allowed, 0.0, NEG_INF).astype(np.float32))
    return pl.pallas_call(
        _toeplitz_kernel,
        grid=(h,),
        in_specs=[pl.BlockSpec((None, 1, win.shape[1]), lambda i: (i, 0, 0)),
                  pl.BlockSpec((n_keys, tq), lambda i: (0, 0))],
        out_specs=pl.BlockSpec((None, n_keys, tq), lambda i: (i, 0, 0)),
        out_shape=jax.ShapeDtypeStruct((h, n_keys, tq), F32),
        compiler_params=_cparams("parallel"),
        name="toeplitz_bias",
    )(win.reshape(h, 1, -1), mask_add)


def _t5_bias_t(table, seq):
    idx = _t5_bucket_np(_window_diagonals(seq, TQ) - (seq - TQ))
    i = np.arange(TQ)[None, :]
    j = np.arange(seq)[:, None] - (seq - TQ)
    allowed = (j < 0) | ((j // CHUNK) <= (i // CHUNK))
    far_bucket = _t5_bucket_np(np.asarray(-seq))
    is_far = np.all((_t5_bucket_np(j - i) == far_bucket) & allowed, axis=1)
    const_rows = int(np.argmin(is_far)) if not is_far.all() else seq
    return _toeplitz_bias_t(table, idx, allowed), const_rows


def _load_vt(v_ref, vt_ref):
    n_v = v_ref.shape[1]
    vt_ref[0:n_v, :] = v_ref[...].T
    vt_ref[n_v:, :] = jnp.ones((ONES_ROWS, v_ref.shape[0]), BF16)


def _attend_pair_phases(k_ref, k0, n_k, kb_rows, q, lane_split, bias_rows, bias_const, vt_ref, s_ref, p_ref, finish):
    tq = q.shape[0]
    halves = (slice(0, tq), slice(tq, 2 * tq))
    n_v = vt_ref.shape[0] - ONES_ROWS
    state = {"m": [None, None], "o": None}

    def score(kr):
        def run():
            if kr == 0:
                zero = jnp.zeros_like(q)
                state["qq"] = jnp.concatenate([jnp.where(lane_split, q, zero), jnp.where(lane_split, zero, q)], axis=0)
            rows = min(MM_ROWS, n_k - kr)
            s_ref[kr:kr + rows, :] = _dot_nt(k_ref[k0 + kr:k0 + kr + rows, :], state["qq"])
        return run

    def running_max(kb, h):
        def run():
            cols = halves[h]
            if bias_const(kb):
                block_max = jnp.max(s_ref[kb:kb + kb_rows, cols], axis=0, keepdims=True) + bias_rows(h, kb)
            else:
                s = s_ref[kb:kb + kb_rows, cols] + bias_rows(h, kb)
                s_ref[kb:kb + kb_rows, cols] = s
                block_max = jnp.max(s, axis=0, keepdims=True)
            m = state["m"]
            m[h] = block_max if m[h] is None else jnp.maximum(m[h], block_max)
        return run

    def exponentials(kb, h):
        def run():
            cols = halves[h]
            m = state["m"][h]
            shift = m - bias_rows(h, kb) if bias_const(kb) else m
            p_ref[kb:kb + kb_rows, cols] = jnp.exp2(s_ref[kb:kb + kb_rows, cols] - shift).astype(BF16)
        return run

    def pv(kr):
        def run():
            rows = min(MM_ROWS, n_k - kr)
            part = jnp.dot(vt_ref[:, k0 + kr:k0 + kr + rows], p_ref[kr:kr + rows, :], preferred_element_type=F32)
            state["o"] = part if state["o"] is None else state["o"] + part
            if kr + rows == n_k:
                o_t = state["o"]
                finish([o_t[0:n_v, cols] * (1.0 / o_t[n_v:n_v + 1, cols]) for cols in halves])
        return run

    blocks = [(kb, h) for kb in range(0, n_k, kb_rows) for h in range(2)]
    return ([score(kr) for kr in range(0, n_k, MM_ROWS)], [running_max(kb, h) for kb, h in blocks],
            [exponentials(kb, h) for kb, h in blocks], [pv(kr) for kr in range(0, n_k, MM_ROWS)])


def _interleave(tiles):
    def merged(a, b):
        order = sorted([((i + 0.5) / len(a), 0, f) for i, f in enumerate(a)]
                       + [((i + 0.5) / len(b), 1, f) for i, f in enumerate(b)], key=lambda t: t[:2])
        for _, _, thunk in order:
            thunk()

    empty = ([], [], [], [])
    prev = empty
    for cur in list(tiles) + [empty]:
        merged(cur[0], prev[2])
        merged(cur[1], prev[3])
        prev = cur


def _diff_attn_kernel(q_ref, k_ref, v_ref, bias_ref, lam_ref, g_ref, o_ref, vt_ref, s_ref, p_ref, *,
                      lam_init, const_rows):
    lv = lam_ref[...]
    lam = (jnp.exp(jnp.sum(lv[0:1] * lv[1:2], axis=-1, keepdims=True))
           - jnp.exp(jnp.sum(lv[2:3] * lv[3:4], axis=-1, keepdims=True)) + lam_init)
    first_map = lax.broadcasted_iota(jnp.int32, (1, 2 * DA_HEAD_DIM), 1) < DA_HEAD_DIM
    seq = q_ref.shape[0]
    _load_vt(v_ref, vt_ref)

    def tile(qi):
        n_k = (qi + 1) * TQ
        row0 = seq - n_k

        def bias_const(kb):
            return row0 + kb + KB_T5 <= const_rows

        def bias_rows(h, kb):
            n = 1 if bias_const(kb) else KB_T5
            return bias_ref[row0 + kb:row0 + kb + n, :]

        def finish(outs):
            o_t = outs[0] - lam * outs[1]
            o_t = o_t * lax.rsqrt(jnp.mean(o_t * o_t, axis=0, keepdims=True) + SUBLN_EPS) * g_ref[...]
            o_ref[qi * TQ:(qi + 1) * TQ, :] = (o_t * (1.0 - lam_init)).T.astype(o_ref.dtype)

        slot = qi % N_SLOTS
        return _attend_pair_phases(k_ref, 0, n_k, KB_T5, q_ref[qi * TQ:(qi + 1) * TQ, :], first_map, bias_rows, bias_const,
                                  vt_ref, s_ref.at[slot], p_ref.at[slot], finish)

    _interleave(tile(qi) for qi in range(seq // TQ))


def _diff_attention(qkv, bias_t, const_rows, lam_vecs, subln_g, lam_init):
    b, s, _ = qkv.shape
    hd = 2 * DA_HEAD_DIM
    return pl.pallas_call(
        functools.partial(_diff_attn_kernel, lam_init=lam_init, const_rows=const_rows),
        grid=(b, DA_HEADS),
        in_specs=[pl.BlockSpec((None, s, hd), lambda i, h: (i, 0, h)),
                  pl.BlockSpec((None, s, hd), lambda i, h: (i, 0, DA_HEADS + h)),
                  pl.BlockSpec((None, s, hd), lambda i, h: (i, 0, 2 * DA_HEADS + h)),
                  pl.BlockSpec((None, s, TQ), lambda i, h: (h, 0, 0)),
                  pl.BlockSpec((4, DA_HEAD_DIM), lambda i, h: (0, 0)),
                  pl.BlockSpec((hd, 1), lambda i, h: (0, 0))],
        out_specs=pl.BlockSpec((None, s, hd), lambda i, h: (i, 0, h)),
        out_shape=jax.ShapeDtypeStruct((b, s, D_MODEL), BF16),
        scratch_shapes=[pltpu.VMEM((hd + ONES_ROWS, s), BF16), pltpu.VMEM((N_SLOTS, s, 2 * TQ), F32),
                        pltpu.VMEM((N_SLOTS, s, 2 * TQ), BF16)],
        compiler_params=_cparams("parallel", "parallel"),
        name="diff_attn",
    )(qkv, qkv, qkv, bias_t, lam_vecs, subln_g.reshape(hd, 1))


BAND_KEYS = BAND_PREV * CHUNK + TQ_BAND


def _band_bias_t(rel_table):
    dist = BAND_PREV * CHUNK - _window_diagonals(BAND_KEYS, TQ_BAND)
    q_chunk = np.arange(TQ_BAND)[None, :] // CHUNK
    k_chunk = np.arange(BAND_KEYS)[:, None] // CHUNK
    allowed = (k_chunk >= q_chunk) & (k_chunk <= q_chunk + BAND_PREV)
    return _toeplitz_bias_t(rel_table, np.clip(dist, -REL_CLIP, REL_CLIP) + REL_CLIP, allowed)


def _chunk_attn_kernel(q_ref, k_ref, v_ref, bias_ref, o_ref, vt_ref, s_ref, p_ref):
    first_head = lax.broadcasted_iota(jnp.int32, (1, PAIR), 1) < CA_HEAD_DIM
    first_head_rows = lax.broadcasted_iota(jnp.int32, (PAIR, 1), 0) < CA_HEAD_DIM
    seq = q_ref.shape[0]
    _load_vt(v_ref, vt_ref)

    def tile(qi):
        q0 = qi * TQ_BAND
        k0 = max(0, q0 - BAND_PREV * CHUNK)
        n_k = q0 + TQ_BAND - k0
        row0 = BAND_KEYS - n_k

        def bias_rows(h, kb):
            return bias_ref[h, row0 + kb:row0 + kb + KB_BAND, :]

        def finish(outs):
            o_ref[q0:q0 + TQ_BAND, :] = jnp.where(first_head_rows, outs[0], outs[1]).T.astype(o_ref.dtype)

        slot = qi % N_SLOTS
        return _attend_pair_phases(k_ref, k0, n_k, KB_BAND, q_ref[q0:q0 + TQ_BAND, :], first_head, bias_rows, lambda kb: False,
                                  vt_ref, s_ref.at[slot], p_ref.at[slot], finish)

    _interleave(tile(qi) for qi in range(seq // TQ_BAND))


def _chunk_attention(qkv, band_bias_t):
    b, s, _ = qkv.shape
    n_pairs = CA_HEADS // 2
    return pl.pallas_call(
        _chunk_attn_kernel,
        grid=(b, n_pairs),
        in_specs=[pl.BlockSpec((None, s, PAIR), lambda i, p: (i, 0, p)),
                  pl.BlockSpec((None, s, PAIR), lambda i, p: (i, 0, n_pairs + p)),
                  pl.BlockSpec((None, s, PAIR), lambda i, p: (i, 0, 2 * n_pairs + p)),
                  pl.BlockSpec((2, BAND_KEYS, TQ_BAND), lambda i, p: (p, 0, 0))],
        out_specs=pl.BlockSpec((None, s, PAIR), lambda i, p: (i, 0, p)),
        out_shape=jax.ShapeDtypeStruct((b, s, D_MODEL), BF16),
        scratch_shapes=[pltpu.VMEM((PAIR + ONES_ROWS, s), BF16), pltpu.VMEM((N_SLOTS, BAND_KEYS, 2 * TQ_BAND), F32),
                        pltpu.VMEM((N_SLOTS, BAND_KEYS, 2 * TQ_BAND), BF16)],
        compiler_params=_cparams("parallel", "parallel"),
        name="chunk_attn",
    )(qkv, qkv, qkv, band_bias_t)


def _swiglu_partial(h, wg, wu, wd):
    g = jnp.dot(h, wg, preferred_element_type=F32)
    u = jnp.dot(h, wu, preferred_element_type=F32)
    a = (g * (0.5 * jnp.tanh(0.5 * g) + 0.5) * u).astype(BF16)
    return jnp.dot(a, wd, preferred_element_type=F32)


def _dense_ffn_kernel(a_ref, wo_ref, x_ref, g_ref, wg_ref, wu_ref, wd_ref, o_ref, x1_ref, h_ref, acc_ref):
    f = pl.program_id(1)

    @pl.when(f == 0)
    def _():
        x1 = x_ref[...] + jnp.dot(a_ref[...], wo_ref[...], preferred_element_type=F32)
        x1_ref[...] = x1
        h_ref[...] = _rmsnorm(x1, g_ref[...], RMS_EPS).astype(BF16)
        acc_ref[...] = jnp.zeros_like(acc_ref)

    acc_ref[...] += _swiglu_partial(h_ref[...], wg_ref[...], wu_ref[...], wd_ref[...])

    @pl.when(f == pl.num_programs(1) - 1)
    def _():
        o_ref[...] = x1_ref[...] + acc_ref[...]


def _attn_out_dense_ffn(a, w_o, x, g, w_gate_up, w_down, tf):
    n, d = x.shape
    ff = w_down.shape[0]
    nf = ff // tf
    return pl.pallas_call(
        _dense_ffn_kernel,
        grid=(n // TM, nf),
        in_specs=[pl.BlockSpec((TM, d), lambda i, f: (i, 0)),
                  pl.BlockSpec((d, d), lambda i, f: (0, 0)),
                  pl.BlockSpec((TM, d), lambda i, f: (i, 0)),
                  pl.BlockSpec((1, d), lambda i, f: (0, 0)),
                  pl.BlockSpec((d, tf), lambda i, f: (0, f)),
                  pl.BlockSpec((d, tf), lambda i, f: (0, nf + f)),
                  pl.BlockSpec((tf, d), lambda i, f: (f, 0))],
        out_specs=pl.BlockSpec((TM, d), lambda i, f: (i, 0)),
        out_shape=jax.ShapeDtypeStruct((n, d), F32),
        scratch_shapes=[pltpu.VMEM((TM, d), F32), pltpu.VMEM((TM, d), BF16), pltpu.VMEM((TM, d), F32)],
        compiler_params=_cparams("parallel", "arbitrary"),
        name="dense_ffn",
    )(a, w_o, x, g.reshape(1, d), w_gate_up, w_gate_up, w_down)


def _router_kernel(a_ref, wo_ref, x_ref, g_ref, r_ref, x1_ref, h_ref, gate_ref, mask_ref, mask_t_ref, cnt_ref):
    x1 = x_ref[...] + jnp.dot(a_ref[...], wo_ref[...], preferred_element_type=F32)
    x1_ref[...] = x1
    h = _rmsnorm(x1, g_ref[...], RMS_EPS)
    h_hi = h.astype(BF16)
    h_ref[...] = h_hi
    r = r_ref[...]
    r_hi = r.astype(BF16)
    r_lo = (r - r_hi.astype(F32)).astype(BF16)
    h_lo = (h - h_hi.astype(F32)).astype(BF16)
    logits = (jnp.dot(h_hi, r_hi, preferred_element_type=F32)
              + (jnp.dot(h_hi, r_lo, preferred_element_type=F32) + jnp.dot(h_lo, r_hi, preferred_element_type=F32)))
    lane = lax.broadcasted_iota(jnp.int32, logits.shape, 1)
    logits = jnp.where(lane < N_EXPERTS, logits, -jnp.inf)
    v1 = jnp.max(logits, axis=-1, keepdims=True)
    i1 = jnp.min(jnp.where(logits == v1, lane, LANES), axis=-1, keepdims=True)
    m1 = lane == i1
    rest = jnp.where(m1, -jnp.inf, logits)
    v2 = jnp.max(rest, axis=-1, keepdims=True)
    i2 = jnp.min(jnp.where(rest == v2, lane, LANES), axis=-1, keepdims=True)
    m2 = lane == i2
    e2 = jnp.exp(v2 - v1)
    g1 = 1.0 / (1.0 + e2)
    g2 = e2 / (1.0 + e2)
    gate_ref[...] = jnp.where(m1, g1, 0.0) + jnp.where(m2, g2, 0.0)
    mask = jnp.where(m1 | m2, 1.0, 0.0)
    mask_ref[...] = mask
    mask_t_ref[...] = mask.T[0:N_EXPERTS, :]
    cnt_ref[...] = jnp.broadcast_to(jnp.sum(mask, axis=0, keepdims=True), cnt_ref.shape)


def _attn_out_router(a, w_o, x, g, router):
    n, d = x.shape
    t = n // TT
    r_pad = jnp.zeros((d, LANES), F32).at[:, :N_EXPERTS].set(router)
    return pl.pallas_call(
        _router_kernel,
        grid=(t,),
        in_specs=[pl.BlockSpec((TT, d), lambda i: (i, 0)),
                  pl.BlockSpec((d, d), lambda i: (0, 0)),
                  pl.BlockSpec((TT, d), lambda i: (i, 0)),
                  pl.BlockSpec((1, d), lambda i: (0, 0)),
                  pl.BlockSpec((d, LANES), lambda i: (0, 0))],
        out_specs=[pl.BlockSpec((TT, d), lambda i: (i, 0)),
                   pl.BlockSpec((TT, d), lambda i: (i, 0)),
                   pl.BlockSpec((TT, LANES), lambda i: (i, 0)),
                   pl.BlockSpec((TT, LANES), lambda i: (i, 0)),
                   pl.BlockSpec((N_EXPERTS, TT), lambda i: (0, i)),
                   pl.BlockSpec((None, 8, LANES), lambda i: (i, 0, 0))],
        out_shape=[jax.ShapeDtypeStruct((n, d), F32),
                   jax.ShapeDtypeStruct((n, d), BF16),
                   jax.ShapeDtypeStruct((n, LANES), F32),
                   jax.ShapeDtypeStruct((n, LANES), F32),
                   jax.ShapeDtypeStruct((N_EXPERTS, n), F32),
                   jax.ShapeDtypeStruct((t, 8, LANES), F32)],
        compiler_params=_cparams("parallel"),
        name="router",
    )(a, w_o, x, g.reshape(1, d), r_pad)


def _dispatch_kernel(start_ref, mask_t_ref, h_ref, xs_in_ref, xs_ref, blk_ref, sem_ref):
    del xs_in_ref
    t = pl.program_id(0)
    slot = t % 2
    mask_t = mask_t_ref[...]
    row = lax.broadcasted_iota(jnp.int32, (TT, TT), 0)
    col = lax.broadcasted_iota(jnp.int32, (TT, TT), 1)
    before = jnp.where(row < col, 1.0, 0.0).astype(BF16)
    rank = jnp.dot(mask_t.astype(BF16), before, preferred_element_type=F32)
    row_f = row.astype(F32)
    sels = []
    for e in range(N_EXPERTS):
        pick = (row_f == rank[e:e + 1, :]) & (mask_t[e:e + 1, :] > 0.0)
        sels.append(jnp.where(pick, 1.0, 0.0).astype(BF16))
    packed = jnp.dot(jnp.concatenate(sels, axis=0), h_ref[...], preferred_element_type=F32)
    blk_ref[slot] = packed.astype(BF16).reshape(blk_ref.shape[1:])

    def copies(step, step_slot, wait):
        for e in range(N_EXPERTS):
            start = pl.multiple_of(start_ref[step * N_EXPERTS + e], BF16_ROWS)
            cp = pltpu.make_async_copy(blk_ref.at[step_slot, e], xs_ref.at[pl.ds(start, TT)], sem_ref.at[step_slot])
            cp.wait() if wait else cp.start()

    @pl.when(t > 0)
    def _():
        copies(t - 1, 1 - slot, wait=True)

    copies(t, slot, wait=False)

    @pl.when(t == pl.num_programs(0) - 1)
    def _():
        copies(t, slot, wait=True)


def _dispatch(seg_start, mask_t, h, n_rows):
    n, d = h.shape
    t = n // TT
    xs0 = jnp.zeros((n_rows, d), BF16)
    grid_spec = pltpu.PrefetchScalarGridSpec(
        num_scalar_prefetch=1,
        grid=(t,),
        in_specs=[pl.BlockSpec((N_EXPERTS, TT), lambda i, *_: (0, i)),
                  pl.BlockSpec((TT, d), lambda i, *_: (i, 0)),
                  pl.BlockSpec(memory_space=pl.ANY)],
        out_specs=pl.BlockSpec(memory_space=pl.ANY),
        scratch_shapes=[pltpu.VMEM((2, N_EXPERTS, TT, d), BF16), pltpu.SemaphoreType.DMA((2,))],
    )
    return pl.pallas_call(
        _dispatch_kernel,
        grid_spec=grid_spec,
        out_shape=jax.ShapeDtypeStruct((n_rows, d), BF16),
        input_output_aliases={3: 0},
        compiler_params=_cparams("arbitrary"),
        name="moe_dispatch",
    )(seg_start, mask_t, h, xs0)


def _expert_kernel(tile_expert_ref, tile_rows_ref, xs_ref, wg_ref, wu_ref, wd_ref, o_ref,
                   wg_bf_ref, wu_bf_ref, wd_bf_ref, acc_ref):
    del tile_expert_ref
    j = pl.program_id(0)
    f = pl.program_id(1)
    last = pl.num_programs(1) - 1
    rows = tile_rows_ref[j]

    @pl.when(f == 0)
    def _():
        acc_ref[...] = jnp.zeros_like(acc_ref)

    @pl.when(rows == TE)
    def _():
        acc_ref[...] += _swiglu_partial(xs_ref[...], wg_ref[...].astype(BF16), wu_ref[...].astype(BF16),
                                        wd_ref[...].astype(BF16))

    @pl.when((rows > 0) & (rows < TE))
    def _():
        wg_bf_ref[...] = wg_ref[...].astype(BF16)
        wu_bf_ref[...] = wu_ref[...].astype(BF16)
        wd_bf_ref[...] = wd_ref[...].astype(BF16)
        for lo in range(0, TE, TM):
            @pl.when(rows > lo)
            def _(lo=lo):
                acc_ref[lo:lo + TM, :] += _swiglu_partial(xs_ref[lo:lo + TM, :], wg_bf_ref[...], wu_bf_ref[...],
                                                          wd_bf_ref[...])

    @pl.when(f == last)
    def _():
        o_ref[...] = acc_ref[...].astype(o_ref.dtype)


def _expert_ffn(tile_expert, tile_rows, xs, w_gate_up, w_down, layer, tf):
    n_rows, d = xs.shape
    ff = w_down.shape[2]
    nf = ff // tf

    def f_eff(j, f, tr):
        return jnp.where(tr[j] > 0, f, nf - 1)

    grid_spec = pltpu.PrefetchScalarGridSpec(
        num_scalar_prefetch=2,
        grid=(n_rows // TE, nf),
        in_specs=[pl.BlockSpec((TE, d), lambda j, f, te, tr: (j, 0)),
                  pl.BlockSpec((None, None, d, tf), lambda j, f, te, tr: (layer, te[j], 0, f_eff(j, f, tr))),
                  pl.BlockSpec((None, None, d, tf), lambda j, f, te, tr: (layer, te[j], 0, nf + f_eff(j, f, tr))),
                  pl.BlockSpec((None, None, tf, d), lambda j, f, te, tr: (layer, te[j], f_eff(j, f, tr), 0))],
        out_specs=pl.BlockSpec((TE, d), lambda j, f, te, tr: (j, 0)),
        scratch_shapes=[pltpu.VMEM((d, tf), BF16), pltpu.VMEM((d, tf), BF16), pltpu.VMEM((tf, d), BF16),
                        pltpu.VMEM((TE, d), F32)],
    )
    return pl.pallas_call(
        _expert_kernel,
        grid_spec=grid_spec,
        out_shape=jax.ShapeDtypeStruct((n_rows, d), BF16),
        compiler_params=_cparams("parallel", "arbitrary"),
        name="moe_experts",
    )(tile_expert, tile_rows, xs, w_gate_up, w_gate_up, w_down)


def _combine_kernel(start_ref, x_ref, gate_ref, mask_ref, out_gain_ref, ys_ref, o_ref, win_ref, sem_ref,
                    *, out_norm):
    t = pl.program_id(0)
    n_t = pl.num_programs(0)
    slot = t % 2

    def copies(step, step_slot, wait):
        for e in range(N_EXPERTS):
            start = pl.multiple_of(start_ref[step * N_EXPERTS + e], BF16_ROWS)
            cp = pltpu.make_async_copy(ys_ref.at[pl.ds(start, TT)], win_ref.at[step_slot, e], sem_ref.at[step_slot])
            cp.wait() if wait else cp.start()

    @pl.when(t == 0)
    def _():
        copies(0, 0, wait=False)

    @pl.when(t + 1 < n_t)
    def _():
        copies(t + 1, 1 - slot, wait=False)

    copies(t, slot, wait=True)

    mask = mask_ref[...]
    gate = gate_ref[...]
    row = lax.broadcasted_iota(jnp.int32, (TT, TT), 0)
    col = lax.broadcasted_iota(jnp.int32, (TT, TT), 1)
    before = jnp.where(col < row, 1.0, 0.0).astype(BF16)
    rank = jnp.dot(before, mask.astype(BF16), preferred_element_type=F32)
    col_f = col.astype(F32)
    y = x_ref[...]
    for e in range(N_EXPERTS):
        pick = (col_f == rank[:, e:e + 1]) & (mask[:, e:e + 1] > 0.0)
        sel = jnp.where(pick, 1.0, 0.0).astype(BF16)
        y = y + gate[:, e:e + 1] * jnp.dot(sel, win_ref[slot, e], preferred_element_type=F32)
    o_ref[...] = _rmsnorm(y, out_gain_ref[...], RMS_EPS) if out_norm else y


def _combine(seg_start, x, gate, mask, ys, out_gain, out_norm):
    n, d = x.shape
    t = n // TT
    grid_spec = pltpu.PrefetchScalarGridSpec(
        num_scalar_prefetch=1,
        grid=(t,),
        in_specs=[pl.BlockSpec((TT, d), lambda i, *_: (i, 0)),
                  pl.BlockSpec((TT, LANES), lambda i, *_: (i, 0)),
                  pl.BlockSpec((TT, LANES), lambda i, *_: (i, 0)),
                  pl.BlockSpec((1, d), lambda i, *_: (0, 0)),
                  pl.BlockSpec(memory_space=pl.ANY)],
        out_specs=pl.BlockSpec((TT, d), lambda i, *_: (i, 0)),
        scratch_shapes=[pltpu.VMEM((2, N_EXPERTS, TT, d), BF16), pltpu.SemaphoreType.DMA((2,))],
    )
    return pl.pallas_call(
        functools.partial(_combine_kernel, out_norm=out_norm),
        grid_spec=grid_spec,
        out_shape=jax.ShapeDtypeStruct((n, d), F32),
        compiler_params=_cparams("arbitrary"),
        name="moe_combine",
    )(seg_start, x, gate, mask, out_gain.reshape(1, d), ys)


def _attn_out_moe_ffn(a, w_o, x, g, router, w_gate_up, w_down, layer, tf, out_gain, out_norm):
    n, d = x.shape
    t = n // TT
    x, h, gate, mask, mask_t, cnt = _attn_out_router(a, w_o, x, g, router)
    cnt = cnt[:, 0, :N_EXPERTS].astype(jnp.int32)
    seg_rows = _round_up(cnt, BF16_ROWS)
    expert_rows = jnp.sum(seg_rows, axis=0)
    region = _round_up(expert_rows + TT, TE)
    region_end = jnp.cumsum(region)
    seg_start = (region_end - region)[None, :] + jnp.cumsum(seg_rows, axis=0) - seg_rows
    n_tiles = (_round_up(2 * n + t * N_EXPERTS * (BF16_ROWS - 1) + N_EXPERTS * TT, TE) + N_EXPERTS * TE) // TE
    tile_end = region_end // TE
    tile_ids = jnp.arange(n_tiles, dtype=jnp.int32)
    last_tile = jnp.minimum(tile_ids, tile_end[-1] - 1)
    tile_expert = jnp.sum((last_tile[:, None] >= tile_end[None, :]).astype(jnp.int32), axis=1)
    tile_expert = jnp.minimum(tile_expert, N_EXPERTS - 1)
    rows_before = (tile_ids - (tile_end - region // TE)[tile_expert]) * TE
    tile_rows = jnp.where(tile_ids < tile_end[-1], jnp.clip(expert_rows[tile_expert] - rows_before, 0, TE), 0)
    seg_start = seg_start.reshape(-1).astype(jnp.int32)

    xs = _dispatch(seg_start, mask_t, h, n_tiles * TE)
    ys = _expert_ffn(tile_expert, tile_rows.astype(jnp.int32), xs, w_gate_up, w_down, layer, tf)
    return _combine(seg_start, x, gate, mask, ys, out_gain, out_norm)


def kernel(x, attn_norm, ffn_norm, final_norm, t5_rel_bias, diff_w_qkv, diff_w_o, diff_lambda_q1, diff_lambda_k1, diff_lambda_q2, diff_lambda_k2, diff_subln, chunk_w_qkv, chunk_w_o, chunk_rel_bias, dense_w_gate_up, dense_w_down, moe_router, moe_w_gate_up, moe_w_down):
    b, s, d = x.shape
    n = b * s
    xf = x.reshape(n, d)
    t5_bias, t5_const_rows = _t5_bias_t(t5_rel_bias, s)
    for i in range(DEPTH):
        j = i // 2
        if i % 2 == 0:
            qkv = _norm_matmul(xf, attn_norm[i], diff_w_qkv[j].astype(BF16))
            lam_init = 0.8 - 0.6 * math.exp(-0.3 * i)
            lam_vecs = jnp.stack([diff_lambda_q1[j], diff_lambda_k1[j], diff_lambda_q2[j], diff_lambda_k2[j]])
            a = _diff_attention(qkv.reshape(b, s, 3 * d), t5_bias, t5_const_rows, lam_vecs, diff_subln[j], lam_init)
            xf = _attn_out_dense_ffn(a.reshape(n, d), diff_w_o[j].astype(BF16), xf, ffn_norm[i],
                                     dense_w_gate_up[j].astype(BF16), dense_w_down[j].astype(BF16),
                                     tf=dense_w_down.shape[1] // 2)
        else:
            qkv = _norm_matmul(xf, attn_norm[i], chunk_w_qkv[j].astype(BF16))
            a = _chunk_attention(qkv.reshape(b, s, 3 * d), _band_bias_t(chunk_rel_bias[j]))
            xf = _attn_out_moe_ffn(a.reshape(n, d), chunk_w_o[j].astype(BF16), xf, ffn_norm[i], moe_router[j],
                                   moe_w_gate_up, moe_w_down, layer=j, tf=moe_w_down.shape[2] // 7,
                                   out_gain=final_norm, out_norm=i == DEPTH - 1)
    return xf.reshape(b, s, d)
```

```python
import functools
import math

import numpy as np
import jax
import jax.numpy as jnp
from jax import lax
from jax.experimental import pallas as pl
from jax.experimental.pallas import tpu as pltpu

F32 = jnp.float32
BF16 = jnp.bfloat16

D_MODEL = 1024
DEPTH = 4
CHUNK = 64
DA_HEADS = 8
DA_HEAD_DIM = 64
T5_BUCKETS = 32
T5_MAX_DIST = 128
CA_HEADS = 16
CA_HEAD_DIM = 64
BAND_PREV = 8
REL_CLIP = 4 * CHUNK
N_EXPERTS = 8
RMS_EPS = 1e-6
SUBLN_EPS = 1e-5
NEG_INF = -1e30

LANES = 128
BF16_ROWS = 16
VMEM_LIMIT = 56 * 1024 * 1024
TM = 512
TE = 4 * TM
TQ = 256
TQ_BAND = 256
TT = 256
KB_T5 = 128
KB_BAND = 256
MM_ROWS = 2048
N_SLOTS = 4
ONES_ROWS = BF16_ROWS
LOG2E = math.log2(math.e)
Q_SCALE = DA_HEAD_DIM ** -0.5 * LOG2E
PAIR = 2 * CA_HEAD_DIM


def _round_up(a, m):
    return (a + m - 1) // m * m


def _cparams(*sem):
    return pltpu.CompilerParams(dimension_semantics=sem, vmem_limit_bytes=VMEM_LIMIT)


def _rmsnorm(x32, g, eps):
    return x32 * lax.rsqrt(jnp.mean(x32 * x32, axis=-1, keepdims=True) + eps) * g


def _dot_nt(a, b):
    return lax.dot_general(a, b, (((1,), (1,)), ((), ())), preferred_element_type=F32)


def _norm_matmul_kernel(x_ref, g_ref, w_ref, o_ref):
    d = x_ref.shape[1]
    half = x_ref.shape[0] // 2
    for rows in (slice(0, half), slice(half, 2 * half)):
        h = _rmsnorm(x_ref[rows, :], g_ref[...], RMS_EPS).astype(BF16)
        q = jnp.dot(h, w_ref[:, :d], preferred_element_type=F32) * Q_SCALE
        o_ref[rows, :d] = q.astype(o_ref.dtype)
        o_ref[rows, d:] = jnp.dot(h, w_ref[:, d:], preferred_element_type=F32).astype(o_ref.dtype)


def _norm_matmul(x, g, w):
    n, d = x.shape
    n_out = w.shape[1]
    return pl.pallas_call(
        _norm_matmul_kernel,
        grid=(n // TM,),
        in_specs=[pl.BlockSpec((TM, d), lambda i: (i, 0)),
                  pl.BlockSpec((1, d), lambda i: (0, 0)),
                  pl.BlockSpec((d, n_out), lambda i: (0, 0))],
        out_specs=pl.BlockSpec((TM, n_out), lambda i: (i, 0)),
        out_shape=jax.ShapeDtypeStruct((n, n_out), BF16),
        compiler_params=_cparams("parallel"),
        name="norm_qkv",
    )(x, g.reshape(1, d), w)


def _t5_bucket_np(rel):
    nb = T5_BUCKETS // 2
    max_exact = nb // 2
    ret = (rel > 0).astype(np.int32) * nb
    n = np.abs(rel)
    n_f = np.maximum(n, 1).astype(np.float32)
    large = max_exact + (np.log(n_f / np.float32(max_exact)) / np.float32(math.log(T5_MAX_DIST / max_exact))
                         * np.float32(nb - max_exact)).astype(np.int32)
    large = np.minimum(large, nb - 1)
    return ret + np.where(n < max_exact, n, large)


def _lookup_kernel(idx_ref, t_ref, o_ref):
    rows = lax.broadcasted_iota(jnp.int32, (t_ref.shape[1], idx_ref.shape[1]), 0)
    onehot = jnp.where(rows == idx_ref[...], 1.0, 0.0)
    o_ref[...] = jnp.dot(t_ref[...], onehot, preferred_element_type=F32, precision=lax.Precision.HIGHEST)


def _table_lookup_t(table, idx):
    r, h = table.shape
    table_t = jnp.zeros((h, _round_up(r, LANES)), F32).at[:, :r].set(table.T)
    return pl.pallas_call(
        _lookup_kernel,
        out_shape=jax.ShapeDtypeStruct((h, len(idx)), F32),
        name="bias_lookup",
    )(jnp.asarray(idx, jnp.int32).reshape(1, -1), table_t)


def _toeplitz_kernel(win_ref, mask_ref, o_ref):
    tq = o_ref.shape[1]
    for blk in range(o_ref.shape[0] // tq):
        x = jnp.broadcast_to(win_ref[:, blk * 2 * tq:(blk + 1) * 2 * tq], (tq, 2 * tq))
        skew = pltpu.roll(x, 0, 1, stride=1, stride_axis=0)
        o_ref[blk * tq:(blk + 1) * tq, :] = skew[:, :tq] + mask_ref[blk * tq:(blk + 1) * tq, :]


def _window_diagonals(n_keys, tq):
    m = np.arange(2 * tq)
    back = np.where(m < tq, -m, 2 * tq - m)
    d = np.arange(0, n_keys, tq)[:, None] + back[None, :]
    return np.clip(d, 1 - tq, n_keys - 1).reshape(-1)


def _toeplitz_bias_t(table, idx, allowed):
    win = _table_lookup_t(table, idx) * LOG2E
    h = win.shape[0]
    n_keys, tq = allowed.shape
    mask_add = jnp.asarray(np.where(allowed, 0.0, NEG_INF).astype(np.float32))
    return pl.pallas_call(
        _toeplitz_kernel,
        grid=(h,),
        in_specs=[pl.BlockSpec((None, 1, win.shape[1]), lambda i: (i, 0, 0)),
                  pl.BlockSpec((n_keys, tq), lambda i: (0, 0))],
        out_specs=pl.BlockSpec((None, n_keys, tq), lambda i: (i, 0, 0)),
        out_shape=jax.ShapeDtypeStruct((h, n_keys, tq), F32),
        compiler_params=_cparams("parallel"),
        name="toeplitz_bias",
    )(win.reshape(h, 1, -1), mask_add)


def _t5_bias_t(table, seq):
    idx = _t5_bucket_np(_window_diagonals(seq, TQ) - (seq - TQ))
    i = np.arange(TQ)[None, :]
    j = np.arange(seq)[:, None] - (seq - TQ)
    allowed = (j < 0) | ((j // CHUNK) <= (i // CHUNK))
    far_bucket = _t5_bucket_np(np.asarray(-seq))
    is_far = np.all((_t5_bucket_np(j - i) == far_bucket) & allowed, axis=1)
    const_rows = int(np.argmin(is_far)) if not is_far.all() else seq
    return _toeplitz_bias_t(table, idx, allowed), const_rows


def _load_vt(v_ref, vt_ref):
    n_v = v_ref.shape[1]
    vt_ref[0:n_v, :] = v_ref[...].T
    vt_ref[n_v:, :] = jnp.ones((ONES_ROWS, v_ref.shape[0]), BF16)


def _attend_pair_phases(k_ref, k0, n_k, kb_rows, q, lane_split, bias_rows, bias_const, vt_ref, s_ref, p_ref, finish):
    tq = q.shape[0]
    halves = (slice(0, tq), slice(tq, 2 * tq))
    n_v = vt_ref.shape[0] - ONES_ROWS
    state = {"m": [None, None], "o": None}

    def score(kr):
        def run():
            if kr == 0:
                zero = jnp.zeros_like(q)
                state["qq"] = jnp.concatenate([jnp.where(lane_split, q, zero), jnp.where(lane_split, zero, q)], axis=0)
            rows = min(MM_ROWS, n_k - kr)
            s_ref[kr:kr + rows, :] = _dot_nt(k_ref[k0 + kr:k0 + kr + rows, :], state["qq"])
        return run

    def running_max(kb, h):
        def run():
            cols = halves[h]
            if bias_const(kb):
                block_max = jnp.max(s_ref[kb:kb + kb_rows, cols], axis=0, keepdims=True) + bias_rows(h, kb)
            else:
                s = s_ref[kb:kb + kb_rows, cols] + bias_rows(h, kb)
                s_ref[kb:kb + kb_rows, cols] = s
                block_max = jnp.max(s, axis=0, keepdims=True)
            m = state["m"]
            m[h] = block_max if m[h] is None else jnp.maximum(m[h], block_max)
        return run

    def exponentials(kb, h):
        def run():
            cols = halves[h]
            m = state["m"][h]
            shift = m - bias_rows(h, kb) if bias_const(kb) else m
            p_ref[kb:kb + kb_rows, cols] = jnp.exp2(s_ref[kb:kb + kb_rows, cols] - shift).astype(BF16)
        return run

    def pv(kr):
        def run():
            rows = min(MM_ROWS, n_k - kr)
            part = jnp.dot(vt_ref[:, k0 + kr:k0 + kr + rows], p_ref[kr:kr + rows, :], preferred_element_type=F32)
            state["o"] = part if state["o"] is None else state["o"] + part
            if kr + rows == n_k:
                o_t = state["o"]
                finish([o_t[0:n_v, cols] * (1.0 / o_t[n_v:n_v + 1, cols]) for cols in halves])
        return run

    blocks = [(kb, h) for kb in range(0, n_k, kb_rows) for h in range(2)]
    return ([score(kr) for kr in range(0, n_k, MM_ROWS)], [running_max(kb, h) for kb, h in blocks],
            [exponentials(kb, h) for kb, h in blocks], [pv(kr) for kr in range(0, n_k, MM_ROWS)])


def _interleave(tiles):
    def merged(a, b):
        order = sorted([((i + 0.5) / len(a), 0, f) for i, f in enumerate(a)]
                       + [((i + 0.5) / len(b), 1, f) for i, f in enumerate(b)], key=lambda t: t[:2])
        for _, _, thunk in order:
            thunk()

    empty = ([], [], [], [])
    prev = empty
    for cur in list(tiles) + [empty]:
        merged(cur[0], prev[2])
        merged(cur[1], prev[3])
        prev = cur


def _diff_attn_kernel(q_ref, k_ref, v_ref, bias_ref, lam_ref, g_ref, o_ref, vt_ref, s_ref, p_ref, *,
                      lam_init, const_rows):
    lv = lam_ref[...]
    lam = (jnp.exp(jnp.sum(lv[0:1] * lv[1:2], axis=-1, keepdims=True))
           - jnp.exp(jnp.sum(lv[2:3] * lv[3:4], axis=-1, keepdims=True)) + lam_init)
    first_map = lax.broadcasted_iota(jnp.int32, (1, 2 * DA_HEAD_DIM), 1) < DA_HEAD_DIM
    seq = q_ref.shape[0]
    _load_vt(v_ref, vt_ref)

    def tile(qi):
        n_k = (qi + 1) * TQ
        row0 = seq - n_k

        def bias_const(kb):
            return row0 + kb + KB_T5 <= const_rows

        def bias_rows(h, kb):
            n = 1 if bias_const(kb) else KB_T5
            return bias_ref[row0 + kb:row0 + kb + n, :]

        def finish(outs):
            o_t = outs[0] - lam * outs[1]
            o_t = o_t * lax.rsqrt(jnp.mean(o_t * o_t, axis=0, keepdims=True) + SUBLN_EPS) * g_ref[...]
            o_ref[qi * TQ:(qi + 1) * TQ, :] = (o_t * (1.0 - lam_init)).T.astype(o_ref.dtype)

        slot = qi % N_SLOTS
        return _attend_pair_phases(k_ref, 0, n_k, KB_T5, q_ref[qi * TQ:(qi + 1) * TQ, :], first_map, bias_rows, bias_const,
                                  vt_ref, s_ref.at[slot], p_ref.at[slot], finish)

    _interleave(tile(qi) for qi in range(seq // TQ))


def _diff_attention(qkv, bias_t, const_rows, lam_vecs, subln_g, lam_init):
    b, s, _ = qkv.shape
    hd = 2 * DA_HEAD_DIM
    return pl.pallas_call(
        functools.partial(_diff_attn_kernel, lam_init=lam_init, const_rows=const_rows),
        grid=(b, DA_HEADS),
        in_specs=[pl.BlockSpec((None, s, hd), lambda i, h: (i, 0, h)),
                  pl.BlockSpec((None, s, hd), lambda i, h: (i, 0, DA_HEADS + h)),
                  pl.BlockSpec((None, s, hd), lambda i, h: (i, 0, 2 * DA_HEADS + h)),
                  pl.BlockSpec((None, s, TQ), lambda i, h: (h, 0, 0)),
                  pl.BlockSpec((4, DA_HEAD_DIM), lambda i, h: (0, 0)),
                  pl.BlockSpec((hd, 1), lambda i, h: (0, 0))],
        out_specs=pl.BlockSpec((None, s, hd), lambda i, h: (i, 0, h)),
        out_shape=jax.ShapeDtypeStruct((b, s, D_MODEL), BF16),
        scratch_shapes=[pltpu.VMEM((hd + ONES_ROWS, s), BF16), pltpu.VMEM((N_SLOTS, s, 2 * TQ), F32),
                        pltpu.VMEM((N_SLOTS, s, 2 * TQ), BF16)],
        compiler_params=_cparams("parallel", "parallel"),
        name="diff_attn",
    )(qkv, qkv, qkv, bias_t, lam_vecs, subln_g.reshape(hd, 1))


BAND_KEYS = BAND_PREV * CHUNK + TQ_BAND


def _band_bias_t(rel_table):
    dist = BAND_PREV * CHUNK - _window_diagonals(BAND_KEYS, TQ_BAND)
    q_chunk = np.arange(TQ_BAND)[None, :] // CHUNK
    k_chunk = np.arange(BAND_KEYS)[:, None] // CHUNK
    allowed = (k_chunk >= q_chunk) & (k_chunk <= q_chunk + BAND_PREV)
    return _toeplitz_bias_t(rel_table, np.clip(dist, -REL_CLIP, REL_CLIP) + REL_CLIP, allowed)


def _chunk_attn_kernel(q_ref, k_ref, v_ref, bias_ref, o_ref, vt_ref, s_ref, p_ref):
    first_head = lax.broadcasted_iota(jnp.int32, (1, PAIR), 1) < CA_HEAD_DIM
    first_head_rows = lax.broadcasted_iota(jnp.int32, (PAIR, 1), 0) < CA_HEAD_DIM
    seq = q_ref.shape[0]
    _load_vt(v_ref, vt_ref)

    def tile(qi):
        q0 = qi * TQ_BAND
        k0 = max(0, q0 - BAND_PREV * CHUNK)
        n_k = q0 + TQ_BAND - k0
        row0 = BAND_KEYS - n_k

        def bias_rows(h, kb):
            return bias_ref[h, row0 + kb:row0 + kb + KB_BAND, :]

        def finish(outs):
            o_ref[q0:q0 + TQ_BAND, :] = jnp.where(first_head_rows, outs[0], outs[1]).T.astype(o_ref.dtype)

        slot = qi % N_SLOTS
        return _attend_pair_phases(k_ref, k0, n_k, KB_BAND, q_ref[q0:q0 + TQ_BAND, :], first_head, bias_rows, lambda kb: False,
                                  vt_ref, s_ref.at[slot], p_ref.at[slot], finish)

    _interleave(tile(qi) for qi in range(seq // TQ_BAND))


def _chunk_attention(qkv, band_bias_t):
    b, s, _ = qkv.shape
    n_pairs = CA_HEADS // 2
    return pl.pallas_call(
        _chunk_attn_kernel,
        grid=(b, n_pairs),
        in_specs=[pl.BlockSpec((None, s, PAIR), lambda i, p: (i, 0, p)),
                  pl.BlockSpec((None, s, PAIR), lambda i, p: (i, 0, n_pairs + p)),
                  pl.BlockSpec((None, s, PAIR), lambda i, p: (i, 0, 2 * n_pairs + p)),
                  pl.BlockSpec((2, BAND_KEYS, TQ_BAND), lambda i, p: (p, 0, 0))],
        out_specs=pl.BlockSpec((None, s, PAIR), lambda i, p: (i, 0, p)),
        out_shape=jax.ShapeDtypeStruct((b, s, D_MODEL), BF16),
        scratch_shapes=[pltpu.VMEM((PAIR + ONES_ROWS, s), BF16), pltpu.VMEM((N_SLOTS, BAND_KEYS, 2 * TQ_BAND), F32),
                        pltpu.VMEM((N_SLOTS, BAND_KEYS, 2 * TQ_BAND), BF16)],
        compiler_params=_cparams("parallel", "parallel"),
        name="chunk_attn",
    )(qkv, qkv, qkv, band_bias_t)


def _swiglu_partial(h, wg, wu, wd):
    g = jnp.dot(h, wg, preferred_element_type=F32)
    u = jnp.dot(h, wu, preferred_element_type=F32)
    a = (g * (0.5 * jnp.tanh(0.5 * g) + 0.5) * u).astype(BF16)
    return jnp.dot(a, wd, preferred_element_type=F32)


def _dense_ffn_kernel(a_ref, wo_ref, x_ref, g_ref, wg_ref, wu_ref, wd_ref, o_ref, x1_ref, h_ref, acc_ref):
    f = pl.program_id(1)

    @pl.when(f == 0)
    def _():
        x1 = x_ref[...] + jnp.dot(a_ref[...], wo_ref[...], preferred_element_type=F32)
        x1_ref[...] = x1
        h_ref[...] = _rmsnorm(x1, g_ref[...], RMS_EPS).astype(BF16)
        acc_ref[...] = jnp.zeros_like(acc_ref)

    acc_ref[...] += _swiglu_partial(h_ref[...], wg_ref[...], wu_ref[...], wd_ref[...])

    @pl.when(f == pl.num_programs(1) - 1)
    def _():
        o_ref[...] = x1_ref[...] + acc_ref[...]


def _attn_out_dense_ffn(a, w_o, x, g, w_gate_up, w_down, tf):
    n, d = x.shape
    ff = w_down.shape[0]
    nf = ff // tf
    return pl.pallas_call(
        _dense_ffn_kernel,
        grid=(n // TM, nf),
        in_specs=[pl.BlockSpec((TM, d), lambda i, f: (i, 0)),
                  pl.BlockSpec((d, d), lambda i, f: (0, 0)),
                  pl.BlockSpec((TM, d), lambda i, f: (i, 0)),
                  pl.BlockSpec((1, d), lambda i, f: (0, 0)),
                  pl.BlockSpec((d, tf), lambda i, f: (0, f)),
                  pl.BlockSpec((d, tf), lambda i, f: (0, nf + f)),
                  pl.BlockSpec((tf, d), lambda i, f: (f, 0))],
        out_specs=pl.BlockSpec((TM, d), lambda i, f: (i, 0)),
        out_shape=jax.ShapeDtypeStruct((n, d), F32),
        scratch_shapes=[pltpu.VMEM((TM, d), F32), pltpu.VMEM((TM, d), BF16), pltpu.VMEM((TM, d), F32)],
        compiler_params=_cparams("parallel", "arbitrary"),
        name="dense_ffn",
    )(a, w_o, x, g.reshape(1, d), w_gate_up, w_gate_up, w_down)


def _router_kernel(a_ref, wo_ref, x_ref, g_ref, r_ref, x1_ref, h_ref, gate_ref, mask_ref, mask_t_ref, cnt_ref):
    x1 = x_ref[...] + jnp.dot(a_ref[...], wo_ref[...], preferred_element_type=F32)
    x1_ref[...] = x1
    h = _rmsnorm(x1, g_ref[...], RMS_EPS)
    h_hi = h.astype(BF16)
    h_ref[...] = h_hi
    r = r_ref[...]
    r_hi = r.astype(BF16)
    r_lo = (r - r_hi.astype(F32)).astype(BF16)
    h_lo = (h - h_hi.astype(F32)).astype(BF16)
    logits = (jnp.dot(h_hi, r_hi, preferred_element_type=F32)
              + (jnp.dot(h_hi, r_lo, preferred_element_type=F32) + jnp.dot(h_lo, r_hi, preferred_element_type=F32)))
    lane = lax.broadcasted_iota(jnp.int32, logits.shape, 1)
    logits = jnp.where(lane < N_EXPERTS, logits, -jnp.inf)
    v1 = jnp.max(logits, axis=-1, keepdims=True)
    i1 = jnp.min(jnp.where(logits == v1, lane, LANES), axis=-1, keepdims=True)
    m1 = lane == i1
    rest = jnp.where(m1, -jnp.inf, logits)
    v2 = jnp.max(rest, axis=-1, keepdims=True)
    i2 = jnp.min(jnp.where(rest == v2, lane, LANES), axis=-1, keepdims=True)
    m2 = lane == i2
    e2 = jnp.exp(v2 - v1)
    g1 = 1.0 / (1.0 + e2)
    g2 = e2 / (1.0 + e2)
    gate_ref[...] = jnp.where(m1, g1, 0.0) + jnp.where(m2, g2, 0.0)
    mask = jnp.where(m1 | m2, 1.0, 0.0)
    mask_ref[...] = mask
    mask_t_ref[...] = mask.T[0:N_EXPERTS, :]
    cnt_ref[...] = jnp.broadcast_to(jnp.sum(mask, axis=0, keepdims=True), cnt_ref.shape)


def _attn_out_router(a, w_o, x, g, router):
    n, d = x.shape
    t = n // TT
    r_pad = jnp.zeros((d, LANES), F32).at[:, :N_EXPERTS].set(router)
    return pl.pallas_call(
        _router_kernel,
        grid=(t,),
        in_specs=[pl.BlockSpec((TT, d), lambda i: (i, 0)),
                  pl.BlockSpec((d, d), lambda i: (0, 0)),
                  pl.BlockSpec((TT, d), lambda i: (i, 0)),
                  pl.BlockSpec((1, d), lambda i: (0, 0)),
                  pl.BlockSpec((d, LANES), lambda i: (0, 0))],
        out_specs=[pl.BlockSpec((TT, d), lambda i: (i, 0)),
                   pl.BlockSpec((TT, d), lambda i: (i, 0)),
                   pl.BlockSpec((TT, LANES), lambda i: (i, 0)),
                   pl.BlockSpec((TT, LANES), lambda i: (i, 0)),
                   pl.BlockSpec((N_EXPERTS, TT), lambda i: (0, i)),
                   pl.BlockSpec((None, 8, LANES), lambda i: (i, 0, 0))],
        out_shape=[jax.ShapeDtypeStruct((n, d), F32),
                   jax.ShapeDtypeStruct((n, d), BF16),
                   jax.ShapeDtypeStruct((n, LANES), F32),
                   jax.ShapeDtypeStruct((n, LANES), F32),
                   jax.ShapeDtypeStruct((N_EXPERTS, n), F32),
                   jax.ShapeDtypeStruct((t, 8, LANES), F32)],
        compiler_params=_cparams("parallel"),
        name="router",
    )(a, w_o, x, g.reshape(1, d), r_pad)


def _half_block_copies(n_rows, wait, make_copy):
    for lo in (0, TT // 2):
        def run(lo=lo):
            cp = make_copy(lo)
            cp.wait() if wait else cp.start()

        if lo == 0:
            run()
        else:
            pl.when(n_rows > lo)(run)


def _dispatch_kernel(start_ref, rows_ref, mask_t_ref, h_ref, xs_in_ref, xs_ref, blk_ref, sem_ref):
    del xs_in_ref
    t = pl.program_id(0)
    slot = t % 2
    mask_t = mask_t_ref[...]
    row = lax.broadcasted_iota(jnp.int32, (TT, TT), 0)
    col = lax.broadcasted_iota(jnp.int32, (TT, TT), 1)
    before = jnp.where(row < col, 1.0, 0.0).astype(BF16)
    rank = jnp.dot(mask_t.astype(BF16), before, preferred_element_type=F32)
    row_f = row.astype(F32)
    sels = []
    for e in range(N_EXPERTS):
        pick = (row_f == rank[e:e + 1, :]) & (mask_t[e:e + 1, :] > 0.0)
        sels.append(jnp.where(pick, 1.0, 0.0).astype(BF16))
    packed = jnp.dot(jnp.concatenate(sels, axis=0), h_ref[...], preferred_element_type=F32)
    blk_ref[slot] = packed.astype(BF16).reshape(blk_ref.shape[1:])

    def copies(step, step_slot, wait):
        half = TT // 2
        for e in range(N_EXPERTS):
            start = pl.multiple_of(start_ref[step * N_EXPERTS + e], BF16_ROWS)

            def make_copy(lo, e=e, start=start):
                return pltpu.make_async_copy(blk_ref.at[step_slot, e, pl.ds(lo, half)],
                                             xs_ref.at[pl.ds(pl.multiple_of(start + lo, BF16_ROWS), half)],
                                             sem_ref.at[step_slot])

            _half_block_copies(rows_ref[step * N_EXPERTS + e], wait, make_copy)

    @pl.when(t > 0)
    def _():
        copies(t - 1, 1 - slot, wait=True)

    copies(t, slot, wait=False)

    @pl.when(t == pl.num_programs(0) - 1)
    def _():
        copies(t, slot, wait=True)


def _dispatch(seg_start, seg_rows, mask_t, h, n_rows):
    n, d = h.shape
    t = n // TT
    xs0 = jnp.zeros((n_rows, d), BF16)
    grid_spec = pltpu.PrefetchScalarGridSpec(
        num_scalar_prefetch=2,
        grid=(t,),
        in_specs=[pl.BlockSpec((N_EXPERTS, TT), lambda i, *_: (0, i)),
                  pl.BlockSpec((TT, d), lambda i, *_: (i, 0)),
                  pl.BlockSpec(memory_space=pl.ANY)],
        out_specs=pl.BlockSpec(memory_space=pl.ANY),
        scratch_shapes=[pltpu.VMEM((2, N_EXPERTS, TT, d), BF16), pltpu.SemaphoreType.DMA((2,))],
    )
    return pl.pallas_call(
        _dispatch_kernel,
        grid_spec=grid_spec,
        out_shape=jax.ShapeDtypeStruct((n_rows, d), BF16),
        input_output_aliases={4: 0},
        compiler_params=_cparams("arbitrary"),
        name="moe_dispatch",
    )(seg_start, seg_rows, mask_t, h, xs0)


def _expert_kernel(tile_expert_ref, tile_rows_ref, xs_ref, wg_ref, wu_ref, wd_ref, o_ref,
                   wg_bf_ref, wu_bf_ref, wd_bf_ref, acc_ref):
    del tile_expert_ref
    j = pl.program_id(0)
    f = pl.program_id(1)
    last = pl.num_programs(1) - 1
    rows = tile_rows_ref[j]

    @pl.when(f == 0)
    def _():
        acc_ref[...] = jnp.zeros_like(acc_ref)

    @pl.when(rows == TE)
    def _():
        acc_ref[...] += _swiglu_partial(xs_ref[...], wg_ref[...].astype(BF16), wu_ref[...].astype(BF16),
                                        wd_ref[...].astype(BF16))

    @pl.when((rows > 0) & (rows < TE))
    def _():
        wg_bf_ref[...] = wg_ref[...].astype(BF16)
        wu_bf_ref[...] = wu_ref[...].astype(BF16)
        wd_bf_ref[...] = wd_ref[...].astype(BF16)
        for lo in range(0, TE, TM):
            @pl.when(rows > lo)
            def _(lo=lo):
                acc_ref[lo:lo + TM, :] += _swiglu_partial(xs_ref[lo:lo + TM, :], wg_bf_ref[...], wu_bf_ref[...],
                                                          wd_bf_ref[...])

    @pl.when(f == last)
    def _():
        o_ref[...] = acc_ref[...].astype(o_ref.dtype)


def _expert_ffn(tile_expert, tile_rows, xs, w_gate_up, w_down, layer, tf):
    n_rows, d = xs.shape
    ff = w_down.shape[2]
    nf = ff // tf

    def f_eff(j, f, tr):
        return jnp.where(tr[j] > 0, f, nf - 1)

    grid_spec = pltpu.PrefetchScalarGridSpec(
        num_scalar_prefetch=2,
        grid=(n_rows // TE, nf),
        in_specs=[pl.BlockSpec((TE, d), lambda j, f, te, tr: (j, 0)),
                  pl.BlockSpec((None, None, d, tf), lambda j, f, te, tr: (layer, te[j], 0, f_eff(j, f, tr))),
                  pl.BlockSpec((None, None, d, tf), lambda j, f, te, tr: (layer, te[j], 0, nf + f_eff(j, f, tr))),
                  pl.BlockSpec((None, None, tf, d), lambda j, f, te, tr: (layer, te[j], f_eff(j, f, tr), 0))],
        out_specs=pl.BlockSpec((TE, d), lambda j, f, te, tr: (j, 0)),
        scratch_shapes=[pltpu.VMEM((d, tf), BF16), pltpu.VMEM((d, tf), BF16), pltpu.VMEM((tf, d), BF16),
                        pltpu.VMEM((TE, d), F32)],
    )
    return pl.pallas_call(
        _expert_kernel,
        grid_spec=grid_spec,
        out_shape=jax.ShapeDtypeStruct((n_rows, d), BF16),
        compiler_params=_cparams("parallel", "arbitrary"),
        name="moe_experts",
    )(tile_expert, tile_rows, xs, w_gate_up, w_gate_up, w_down)


def _combine_kernel(start_ref, rows_ref, x_ref, gate_ref, mask_ref, out_gain_ref, ys_ref, o_ref, win_ref, sem_ref,
                    *, out_norm):
    t = pl.program_id(0)
    n_t = pl.num_programs(0)
    slot = t % 2

    def copies(step, step_slot, wait):
        half = TT // 2
        for e in range(N_EXPERTS):
            start = pl.multiple_of(start_ref[step * N_EXPERTS + e], BF16_ROWS)

            def make_copy(lo, e=e, start=start):
                return pltpu.make_async_copy(ys_ref.at[pl.ds(pl.multiple_of(start + lo, BF16_ROWS), half)],
                                             win_ref.at[step_slot, e, pl.ds(lo, half)], sem_ref.at[step_slot])

            _half_block_copies(rows_ref[step * N_EXPERTS + e], wait, make_copy)

    @pl.when(t == 0)
    def _():
        win_ref[...] = jnp.zeros_like(win_ref)
        copies(0, 0, wait=False)

    @pl.when(t + 1 < n_t)
    def _():
        copies(t + 1, 1 - slot, wait=False)

    copies(t, slot, wait=True)

    mask = mask_ref[...]
    gate = gate_ref[...]
    row = lax.broadcasted_iota(jnp.int32, (TT, TT), 0)
    col = lax.broadcasted_iota(jnp.int32, (TT, TT), 1)
    before = jnp.where(col < row, 1.0, 0.0).astype(BF16)
    rank = jnp.dot(before, mask.astype(BF16), preferred_element_type=F32)
    col_f = col.astype(F32)
    y = x_ref[...]
    for e in range(N_EXPERTS):
        pick = (col_f == rank[:, e:e + 1]) & (mask[:, e:e + 1] > 0.0)
        sel = jnp.where(pick, 1.0, 0.0).astype(BF16)
        y = y + gate[:, e:e + 1] * jnp.dot(sel, win_ref[slot, e], preferred_element_type=F32)
    o_ref[...] = _rmsnorm(y, out_gain_ref[...], RMS_EPS) if out_norm else y


def _combine(seg_start, seg_rows, x, gate, mask, ys, out_gain, out_norm):
    n, d = x.shape
    t = n // TT
    grid_spec = pltpu.PrefetchScalarGridSpec(
        num_scalar_prefetch=2,
        grid=(t,),
        in_specs=[pl.BlockSpec((TT, d), lambda i, *_: (i, 0)),
                  pl.BlockSpec((TT, LANES), lambda i, *_: (i, 0)),
                  pl.BlockSpec((TT, LANES), lambda i, *_: (i, 0)),
                  pl.BlockSpec((1, d), lambda i, *_: (0, 0)),
                  pl.BlockSpec(memory_space=pl.ANY)],
        out_specs=pl.BlockSpec((TT, d), lambda i, *_: (i, 0)),
        scratch_shapes=[pltpu.VMEM((2, N_EXPERTS, TT, d), BF16), pltpu.SemaphoreType.DMA((2,))],
    )
    return pl.pallas_call(
        functools.partial(_combine_kernel, out_norm=out_norm),
        grid_spec=grid_spec,
        out_shape=jax.ShapeDtypeStruct((n, d), F32),
        compiler_params=_cparams("arbitrary"),
        name="moe_combine",
    )(seg_start, seg_rows, x, gate, mask, out_gain.reshape(1, d), ys)


def _attn_out_moe_ffn(a, w_o, x, g, router, w_gate_up, w_down, layer, tf, out_gain, out_norm):
    n, d = x.shape
    t = n // TT
    x, h, gate, mask, mask_t, cnt = _attn_out_router(a, w_o, x, g, router)
    cnt = cnt[:, 0, :N_EXPERTS].astype(jnp.int32)
    seg_rows = _round_up(cnt, BF16_ROWS)
    expert_rows = jnp.sum(seg_rows, axis=0)
    region = _round_up(expert_rows + TT, TE)
    region_end = jnp.cumsum(region)
    seg_start = (region_end - region)[None, :] + jnp.cumsum(seg_rows, axis=0) - seg_rows
    n_tiles = (_round_up(2 * n + t * N_EXPERTS * (BF16_ROWS - 1) + N_EXPERTS * TT, TE) + N_EXPERTS * TE) // TE
    tile_end = region_end // TE
    tile_ids = jnp.arange(n_tiles, dtype=jnp.int32)
    last_tile = jnp.minimum(tile_ids, tile_end[-1] - 1)
    tile_expert = jnp.sum((last_tile[:, None] >= tile_end[None, :]).astype(jnp.int32), axis=1)
    tile_expert = jnp.minimum(tile_expert, N_EXPERTS - 1)
    rows_before = (tile_ids - (tile_end - region // TE)[tile_expert]) * TE
    tile_rows = jnp.where(tile_ids < tile_end[-1], jnp.clip(expert_rows[tile_expert] - rows_before, 0, TE), 0)
    seg_start = seg_start.reshape(-1).astype(jnp.int32)
    seg_rows = seg_rows.reshape(-1).astype(jnp.int32)

    xs = _dispatch(seg_start, seg_rows, mask_t, h, n_tiles * TE)
    ys = _expert_ffn(tile_expert, tile_rows.astype(jnp.int32), xs, w_gate_up, w_down, layer, tf)
    return _combine(seg_start, seg_rows, x, gate, mask, ys, out_gain, out_norm)


def kernel(x, attn_norm, ffn_norm, final_norm, t5_rel_bias, diff_w_qkv, diff_w_o, diff_lambda_q1, diff_lambda_k1, diff_lambda_q2, diff_lambda_k2, diff_subln, chunk_w_qkv, chunk_w_o, chunk_rel_bias, dense_w_gate_up, dense_w_down, moe_router, moe_w_gate_up, moe_w_down):
    b, s, d = x.shape
    n = b * s
    xf = x.reshape(n, d)
    t5_bias, t5_const_rows = _t5_bias_t(t5_rel_bias, s)
    for i in range(DEPTH):
        j = i // 2
        if i % 2 == 0:
            qkv = _norm_matmul(xf, attn_norm[i], diff_w_qkv[j].astype(BF16))
            lam_init = 0.8 - 0.6 * math.exp(-0.3 * i)
            lam_vecs = jnp.stack([diff_lambda_q1[j], diff_lambda_k1[j], diff_lambda_q2[j], diff_lambda_k2[j]])
            a = _diff_attention(qkv.reshape(b, s, 3 * d), t5_bias, t5_const_rows, lam_vecs, diff_subln[j], lam_init)
            xf = _attn_out_dense_ffn(a.reshape(n, d), diff_w_o[j].astype(BF16), xf, ffn_norm[i],
                                     dense_w_gate_up[j].astype(BF16), dense_w_down[j].astype(BF16),
                                     tf=dense_w_down.shape[1] // 2)
        else:
            qkv = _norm_matmul(xf, attn_norm[i], chunk_w_qkv[j].astype(BF16))
            a = _chunk_attention(qkv.reshape(b, s, 3 * d), _band_bias_t(chunk_rel_bias[j]))
            xf = _attn_out_moe_ffn(a.reshape(n, d), chunk_w_o[j].astype(BF16), xf, ffn_norm[i], moe_router[j],
                                   moe_w_gate_up, moe_w_down, layer=j, tf=moe_w_down.shape[2] // 7,
                                   out_gain=final_norm, out_norm=i == DEPTH - 1)
    return xf.reshape(b, s, d)
```

```python
import functools
import math

import numpy as np
import jax
import jax.numpy as jnp
from jax import lax
from jax.experimental import pallas as pl
from jax.experimental.pallas import tpu as pltpu

F32 = jnp.float32
BF16 = jnp.bfloat16

D_MODEL = 1024
DEPTH = 4
CHUNK = 64
DA_HEADS = 8
DA_HEAD_DIM = 64
T5_BUCKETS = 32
T5_MAX_DIST = 128
CA_HEADS = 16
CA_HEAD_DIM = 64
BAND_PREV = 8
REL_CLIP = 4 * CHUNK
N_EXPERTS = 8
RMS_EPS = 1e-6
SUBLN_EPS = 1e-5
NEG_INF = -1e30

LANES = 128
BF16_ROWS = 16
VMEM_LIMIT = 56 * 1024 * 1024
TM = 512
TE = 4 * TM
TQ = 256
TQ_BAND = 256
TT = 256
KB_T5 = 128
KB_BAND = 256
MM_ROWS = 2048
N_SLOTS = 4
ONES_ROWS = BF16_ROWS
LOG2E = math.log2(math.e)
Q_SCALE = DA_HEAD_DIM ** -0.5 * LOG2E
PAIR = 2 * CA_HEAD_DIM


def _round_up(a, m):
    return (a + m - 1) // m * m


def _cparams(*sem):
    return pltpu.CompilerParams(dimension_semantics=sem, vmem_limit_bytes=VMEM_LIMIT)


def _rmsnorm(x32, g, eps):
    return x32 * lax.rsqrt(jnp.mean(x32 * x32, axis=-1, keepdims=True) + eps) * g


def _dot_nt(a, b):
    return lax.dot_general(a, b, (((1,), (1,)), ((), ())), preferred_element_type=F32)


def _norm_matmul_kernel(x_ref, g_ref, w_ref, o_ref):
    d = x_ref.shape[1]
    half = x_ref.shape[0] // 2
    for rows in (slice(0, half), slice(half, 2 * half)):
        h = _rmsnorm(x_ref[rows, :], g_ref[...], RMS_EPS).astype(BF16)
        q = jnp.dot(h, w_ref[:, :d], preferred_element_type=F32) * Q_SCALE
        o_ref[rows, :d] = q.astype(o_ref.dtype)
        o_ref[rows, d:] = jnp.dot(h, w_ref[:, d:], preferred_element_type=F32).astype(o_ref.dtype)


def _norm_matmul(x, g, w):
    n, d = x.shape
    n_out = w.shape[1]
    return pl.pallas_call(
        _norm_matmul_kernel,
        grid=(n // TM,),
        in_specs=[pl.BlockSpec((TM, d), lambda i: (i, 0)),
                  pl.BlockSpec((1, d), lambda i: (0, 0)),
                  pl.BlockSpec((d, n_out), lambda i: (0, 0))],
        out_specs=pl.BlockSpec((TM, n_out), lambda i: (i, 0)),
        out_shape=jax.ShapeDtypeStruct((n, n_out), BF16),
        compiler_params=_cparams("parallel"),
        name="norm_qkv",
    )(x, g.reshape(1, d), w)


def _t5_bucket_np(rel):
    nb = T5_BUCKETS // 2
    max_exact = nb // 2
    ret = (rel > 0).astype(np.int32) * nb
    n = np.abs(rel)
    n_f = np.maximum(n, 1).astype(np.float32)
    large = max_exact + (np.log(n_f / np.float32(max_exact)) / np.float32(math.log(T5_MAX_DIST / max_exact))
                         * np.float32(nb - max_exact)).astype(np.int32)
    large = np.minimum(large, nb - 1)
    return ret + np.where(n < max_exact, n, large)


def _lookup_kernel(idx_ref, t_ref, o_ref):
    rows = lax.broadcasted_iota(jnp.int32, (t_ref.shape[1], idx_ref.shape[1]), 0)
    onehot = jnp.where(rows == idx_ref[...], 1.0, 0.0)
    o_ref[...] = jnp.dot(t_ref[...], onehot, preferred_element_type=F32, precision=lax.Precision.HIGHEST)


def _table_lookup_t(table, idx):
    r, h = table.shape
    table_t = jnp.zeros((h, _round_up(r, LANES)), F32).at[:, :r].set(table.T)
    return pl.pallas_call(
        _lookup_kernel,
        out_shape=jax.ShapeDtypeStruct((h, len(idx)), F32),
        name="bias_lookup",
    )(jnp.asarray(idx, jnp.int32).reshape(1, -1), table_t)


def _toeplitz_kernel(win_ref, mask_ref, o_ref):
    tq = o_ref.shape[1]
    for blk in range(o_ref.shape[0] // tq):
        x = jnp.broadcast_to(win_ref[:, blk * 2 * tq:(blk + 1) * 2 * tq], (tq, 2 * tq))
        skew = pltpu.roll(x, 0, 1, stride=1, stride_axis=0)
        o_ref[blk * tq:(blk + 1) * tq, :] = skew[:, :tq] + mask_ref[blk * tq:(blk + 1) * tq, :]


def _window_diagonals(n_keys, tq):
    m = np.arange(2 * tq)
    back = np.where(m < tq, -m, 2 * tq - m)
    d = np.arange(0, n_keys, tq)[:, None] + back[None, :]
    return np.clip(d, 1 - tq, n_keys - 1).reshape(-1)


def _toeplitz_bias_t(table, idx, allowed):
    win = _table_lookup_t(table, idx) * LOG2E
    h = win.shape[0]
    n_keys, tq = allowed.shape
    mask_add = jnp.asarray(np.where(allowed, 0.0, NEG_INF).astype(np.float32))
    return pl.pallas_call(
        _toeplitz_kernel,
        grid=(h,),
        in_specs=[pl.BlockSpec((None, 1, win.shape[1]), lambda i: (i, 0, 0)),
                  pl.BlockSpec((n_keys, tq), lambda i: (0, 0))],
        out_specs=pl.BlockSpec((None, n_keys, tq), lambda i: (i, 0, 0)),
        out_shape=jax.ShapeDtypeStruct((h, n_keys, tq), F32),
        compiler_params=_cparams("parallel"),
        name="toeplitz_bias",
    )(win.reshape(h, 1, -1), mask_add)


def _t5_bias_t(table, seq):
    idx = _t5_bucket_np(_window_diagonals(seq, TQ) - (seq - TQ))
    i = np.arange(TQ)[None, :]
    j = np.arange(seq)[:, None] - (seq - TQ)
    allowed = (j < 0) | ((j // CHUNK) <= (i // CHUNK))
    far_bucket = _t5_bucket_np(np.asarray(-seq))
    is_far = np.all((_t5_bucket_np(j - i) == far_bucket) & allowed, axis=1)
    const_rows = int(np.argmin(is_far)) if not is_far.all() else seq
    return _toeplitz_bias_t(table, idx, allowed), const_rows


def _load_vt(v_ref, vt_ref):
    n_v = v_ref.shape[1]
    vt_ref[0:n_v, :] = v_ref[...].T
    vt_ref[n_v:, :] = jnp.ones((ONES_ROWS, v_ref.shape[0]), BF16)


def _attend_pair_phases(k_ref, k0, n_k, kb_rows, q, lane_split, bias_rows, bias_const, vt_ref, s_ref, p_ref, finish):
    tq = q.shape[0]
    halves = (slice(0, tq), slice(tq, 2 * tq))
    n_v = vt_ref.shape[0] - ONES_ROWS
    state = {"m": [None, None], "o": None}

    def score(kr):
        def run():
            if kr == 0:
                zero = jnp.zeros_like(q)
                state["qq"] = jnp.concatenate([jnp.where(lane_split, q, zero), jnp.where(lane_split, zero, q)], axis=0)
            rows = min(MM_ROWS, n_k - kr)
            s_ref[kr:kr + rows, :] = _dot_nt(k_ref[k0 + kr:k0 + kr + rows, :], state["qq"])
        return run

    def running_max(kb, h):
        def run():
            cols = halves[h]
            if bias_const(kb):
                block_max = jnp.max(s_ref[kb:kb + kb_rows, cols], axis=0, keepdims=True) + bias_rows(h, kb)
            else:
                s = s_ref[kb:kb + kb_rows, cols] + bias_rows(h, kb)
                s_ref[kb:kb + kb_rows, cols] = s
                block_max = jnp.max(s, axis=0, keepdims=True)
            m = state["m"]
            m[h] = block_max if m[h] is None else jnp.maximum(m[h], block_max)
        return run

    def exponentials(kb, h):
        def run():
            cols = halves[h]
            m = state["m"][h]
            shift = m - bias_rows(h, kb) if bias_const(kb) else m
            p_ref[kb:kb + kb_rows, cols] = jnp.exp2(s_ref[kb:kb + kb_rows, cols] - shift).astype(BF16)
        return run

    def pv(kr):
        def run():
            rows = min(MM_ROWS, n_k - kr)
            part = jnp.dot(vt_ref[:, k0 + kr:k0 + kr + rows], p_ref[kr:kr + rows, :], preferred_element_type=F32)
            state["o"] = part if state["o"] is None else state["o"] + part
            if kr + rows == n_k:
                o_t = state["o"]
                finish([o_t[0:n_v, cols] * (1.0 / o_t[n_v:n_v + 1, cols]) for cols in halves])
        return run

    blocks = [(kb, h) for kb in range(0, n_k, kb_rows) for h in range(2)]
    return ([score(kr) for kr in range(0, n_k, MM_ROWS)], [running_max(kb, h) for kb, h in blocks],
            [exponentials(kb, h) for kb, h in blocks], [pv(kr) for kr in range(0, n_k, MM_ROWS)])


def _interleave(tiles):
    def merged(a, b):
        order = sorted([((i + 0.5) / len(a), 0, f) for i, f in enumerate(a)]
                       + [((i + 0.5) / len(b), 1, f) for i, f in enumerate(b)], key=lambda t: t[:2])
        for _, _, thunk in order:
            thunk()

    empty = ([], [], [], [])
    prev = empty
    for cur in list(tiles) + [empty]:
        merged(cur[0], prev[2])
        merged(cur[1], prev[3])
        prev = cur


def _diff_attn_kernel(q_ref, k_ref, v_ref, bias_ref, lam_ref, g_ref, o_ref, vt_ref, s_ref, p_ref, *,
                      lam_init, const_rows):
    lv = lam_ref[...]
    lam = (jnp.exp(jnp.sum(lv[0:1] * lv[1:2], axis=-1, keepdims=True))
           - jnp.exp(jnp.sum(lv[2:3] * lv[3:4], axis=-1, keepdims=True)) + lam_init)
    first_map = lax.broadcasted_iota(jnp.int32, (1, 2 * DA_HEAD_DIM), 1) < DA_HEAD_DIM
    seq = q_ref.shape[0]
    _load_vt(v_ref, vt_ref)

    def tile(qi):
        n_k = (qi + 1) * TQ
        row0 = seq - n_k

        def bias_const(kb):
            return row0 + kb + KB_T5 <= const_rows

        def bias_rows(h, kb):
            n = 1 if bias_const(kb) else KB_T5
            return bias_ref[row0 + kb:row0 + kb + n, :]

        def finish(outs):
            o_t = outs[0] - lam * outs[1]
            o_t = o_t * lax.rsqrt(jnp.mean(o_t * o_t, axis=0, keepdims=True) + SUBLN_EPS) * g_ref[...]
            o_ref[qi * TQ:(qi + 1) * TQ, :] = (o_t * (1.0 - lam_init)).T.astype(o_ref.dtype)

        slot = qi % N_SLOTS
        return _attend_pair_phases(k_ref, 0, n_k, KB_T5, q_ref[qi * TQ:(qi + 1) * TQ, :], first_map, bias_rows, bias_const,
                                  vt_ref, s_ref.at[slot], p_ref.at[slot], finish)

    _interleave(tile(qi) for qi in range(seq // TQ))


def _diff_attention(qkv, bias_t, const_rows, lam_vecs, subln_g, lam_init):
    b, s, _ = qkv.shape
    hd = 2 * DA_HEAD_DIM
    return pl.pallas_call(
        functools.partial(_diff_attn_kernel, lam_init=lam_init, const_rows=const_rows),
        grid=(b, DA_HEADS),
        in_specs=[pl.BlockSpec((None, s, hd), lambda i, h: (i, 0, h)),
                  pl.BlockSpec((None, s, hd), lambda i, h: (i, 0, DA_HEADS + h)),
                  pl.BlockSpec((None, s, hd), lambda i, h: (i, 0, 2 * DA_HEADS + h)),
                  pl.BlockSpec((None, s, TQ), lambda i, h: (h, 0, 0)),
                  pl.BlockSpec((4, DA_HEAD_DIM), lambda i, h: (0, 0)),
                  pl.BlockSpec((hd, 1), lambda i, h: (0, 0))],
        out_specs=pl.BlockSpec((None, s, hd), lambda i, h: (i, 0, h)),
        out_shape=jax.ShapeDtypeStruct((b, s, D_MODEL), BF16),
        scratch_shapes=[pltpu.VMEM((hd + ONES_ROWS, s), BF16), pltpu.VMEM((N_SLOTS, s, 2 * TQ), F32),
                        pltpu.VMEM((N_SLOTS, s, 2 * TQ), BF16)],
        compiler_params=_cparams("parallel", "parallel"),
        name="diff_attn",
    )(qkv, qkv, qkv, bias_t, lam_vecs, subln_g.reshape(hd, 1))


BAND_KEYS = BAND_PREV * CHUNK + TQ_BAND


def _band_bias_t(rel_table):
    dist = BAND_PREV * CHUNK - _window_diagonals(BAND_KEYS, TQ_BAND)
    q_chunk = np.arange(TQ_BAND)[None, :] // CHUNK
    k_chunk = np.arange(BAND_KEYS)[:, None] // CHUNK
    allowed = (k_chunk >= q_chunk) & (k_chunk <= q_chunk + BAND_PREV)
    return _toeplitz_bias_t(rel_table, np.clip(dist, -REL_CLIP, REL_CLIP) + REL_CLIP, allowed)


def _chunk_attn_kernel(q_ref, k_ref, v_ref, bias_ref, o_ref, vt_ref, s_ref, p_ref):
    first_head = lax.broadcasted_iota(jnp.int32, (1, PAIR), 1) < CA_HEAD_DIM
    first_head_rows = lax.broadcasted_iota(jnp.int32, (PAIR, 1), 0) < CA_HEAD_DIM
    seq = q_ref.shape[0]
    _load_vt(v_ref, vt_ref)

    def tile(qi):
        q0 = qi * TQ_BAND
        k0 = max(0, q0 - BAND_PREV * CHUNK)
        n_k = q0 + TQ_BAND - k0
        row0 = BAND_KEYS - n_k

        def bias_rows(h, kb):
            return bias_ref[h, row0 + kb:row0 + kb + KB_BAND, :]

        def finish(outs):
            o_ref[q0:q0 + TQ_BAND, :] = jnp.where(first_head_rows, outs[0], outs[1]).T.astype(o_ref.dtype)

        slot = qi % N_SLOTS
        return _attend_pair_phases(k_ref, k0, n_k, KB_BAND, q_ref[q0:q0 + TQ_BAND, :], first_head, bias_rows, lambda kb: False,
                                  vt_ref, s_ref.at[slot], p_ref.at[slot], finish)

    _interleave(tile(qi) for qi in range(seq // TQ_BAND))


def _chunk_attention(qkv, band_bias_t):
    b, s, _ = qkv.shape
    n_pairs = CA_HEADS // 2
    return pl.pallas_call(
        _chunk_attn_kernel,
        grid=(b, n_pairs),
        in_specs=[pl.BlockSpec((None, s, PAIR), lambda i, p: (i, 0, p)),
                  pl.BlockSpec((None, s, PAIR), lambda i, p: (i, 0, n_pairs + p)),
                  pl.BlockSpec((None, s, PAIR), lambda i, p: (i, 0, 2 * n_pairs + p)),
                  pl.BlockSpec((2, BAND_KEYS, TQ_BAND), lambda i, p: (p, 0, 0))],
        out_specs=pl.BlockSpec((None, s, PAIR), lambda i, p: (i, 0, p)),
        out_shape=jax.ShapeDtypeStruct((b, s, D_MODEL), BF16),
        scratch_shapes=[pltpu.VMEM((PAIR + ONES_ROWS, s), BF16), pltpu.VMEM((N_SLOTS, BAND_KEYS, 2 * TQ_BAND), F32),
                        pltpu.VMEM((N_SLOTS, BAND_KEYS, 2 * TQ_BAND), BF16)],
        compiler_params=_cparams("parallel", "parallel"),
        name="chunk_attn",
    )(qkv, qkv, qkv, band_bias_t)


def _swiglu_partial(h, wg, wu, wd):
    g = jnp.dot(h, wg, preferred_element_type=F32)
    u = jnp.dot(h, wu, preferred_element_type=F32)
    a = (g * (0.5 * jnp.tanh(0.5 * g) + 0.5) * u).astype(BF16)
    return jnp.dot(a, wd, preferred_element_type=F32)


def _dense_ffn_kernel(a_ref, wo_ref, x_ref, g_ref, wgu_ref, wd_ref, o_ref, *, tf):
    x1 = x_ref[...] + jnp.dot(a_ref[...], wo_ref[...], preferred_element_type=F32)
    h = _rmsnorm(x1, g_ref[...], RMS_EPS).astype(BF16)
    ff = wd_ref.shape[0]
    y = x1
    for lo in range(0, ff, tf):
        y = y + _swiglu_partial(h, wgu_ref[:, lo:lo + tf], wgu_ref[:, ff + lo:ff + lo + tf], wd_ref[lo:lo + tf, :])
    o_ref[...] = y


def _attn_out_dense_ffn(a, w_o, x, g, w_gate_up, w_down, tf):
    n, d = x.shape

    def resident(w):
        return pl.BlockSpec(w.shape, lambda i: (0, 0), pipeline_mode=pl.Buffered(1))

    return pl.pallas_call(
        functools.partial(_dense_ffn_kernel, tf=tf),
        grid=(n // TM,),
        in_specs=[pl.BlockSpec((TM, d), lambda i: (i, 0)),
                  resident(w_o),
                  pl.BlockSpec((TM, d), lambda i: (i, 0)),
                  pl.BlockSpec((1, d), lambda i: (0, 0)),
                  resident(w_gate_up),
                  resident(w_down)],
        out_specs=pl.BlockSpec((TM, d), lambda i: (i, 0)),
        out_shape=jax.ShapeDtypeStruct((n, d), F32),
        compiler_params=_cparams("parallel"),
        name="dense_ffn",
    )(a, w_o, x, g.reshape(1, d), w_gate_up, w_down)


def _router_kernel(a_ref, wo_ref, x_ref, g_ref, r_ref, x1_ref, h_ref, gate_ref, mask_ref, mask_t_ref, cnt_ref):
    x1 = x_ref[...] + jnp.dot(a_ref[...], wo_ref[...], preferred_element_type=F32)
    x1_ref[...] = x1
    h = _rmsnorm(x1, g_ref[...], RMS_EPS)
    h_hi = h.astype(BF16)
    h_ref[...] = h_hi
    r = r_ref[...]
    r_hi = r.astype(BF16)
    r_lo = (r - r_hi.astype(F32)).astype(BF16)
    h_lo = (h - h_hi.astype(F32)).astype(BF16)
    logits = (jnp.dot(h_hi, r_hi, preferred_element_type=F32)
              + (jnp.dot(h_hi, r_lo, preferred_element_type=F32) + jnp.dot(h_lo, r_hi, preferred_element_type=F32)))
    lane = lax.broadcasted_iota(jnp.int32, logits.shape, 1)
    logits = jnp.where(lane < N_EXPERTS, logits, -jnp.inf)
    v1 = jnp.max(logits, axis=-1, keepdims=True)
    i1 = jnp.min(jnp.where(logits == v1, lane, LANES), axis=-1, keepdims=True)
    m1 = lane == i1
    rest = jnp.where(m1, -jnp.inf, logits)
    v2 = jnp.max(rest, axis=-1, keepdims=True)
    i2 = jnp.min(jnp.where(rest == v2, lane, LANES), axis=-1, keepdims=True)
    m2 = lane == i2
    e2 = jnp.exp(v2 - v1)
    g1 = 1.0 / (1.0 + e2)
    g2 = e2 / (1.0 + e2)
    gate_ref[...] = jnp.where(m1, g1, 0.0) + jnp.where(m2, g2, 0.0)
    mask = jnp.where(m1 | m2, 1.0, 0.0)
    mask_ref[...] = mask
    mask_t_ref[...] = mask.T[0:N_EXPERTS, :]
    cnt_ref[...] = jnp.broadcast_to(jnp.sum(mask, axis=0, keepdims=True), cnt_ref.shape)


def _attn_out_router(a, w_o, x, g, router):
    n, d = x.shape
    t = n // TT
    r_pad = jnp.zeros((d, LANES), F32).at[:, :N_EXPERTS].set(router)
    return pl.pallas_call(
        _router_kernel,
        grid=(t,),
        in_specs=[pl.BlockSpec((TT, d), lambda i: (i, 0)),
                  pl.BlockSpec((d, d), lambda i: (0, 0)),
                  pl.BlockSpec((TT, d), lambda i: (i, 0)),
                  pl.BlockSpec((1, d), lambda i: (0, 0)),
                  pl.BlockSpec((d, LANES), lambda i: (0, 0))],
        out_specs=[pl.BlockSpec((TT, d), lambda i: (i, 0)),
                   pl.BlockSpec((TT, d), lambda i: (i, 0)),
                   pl.BlockSpec((TT, LANES), lambda i: (i, 0)),
                   pl.BlockSpec((TT, LANES), lambda i: (i, 0)),
                   pl.BlockSpec((N_EXPERTS, TT), lambda i: (0, i)),
                   pl.BlockSpec((None, 8, LANES), lambda i: (i, 0, 0))],
        out_shape=[jax.ShapeDtypeStruct((n, d), F32),
                   jax.ShapeDtypeStruct((n, d), BF16),
                   jax.ShapeDtypeStruct((n, LANES), F32),
                   jax.ShapeDtypeStruct((n, LANES), F32),
                   jax.ShapeDtypeStruct((N_EXPERTS, n), F32),
                   jax.ShapeDtypeStruct((t, 8, LANES), F32)],
        compiler_params=_cparams("parallel"),
        name="router",
    )(a, w_o, x, g.reshape(1, d), r_pad)


def _segment_copies(n_rows, make_copy):
    off = jnp.int32(0)
    size = TT
    while size >= BF16_ROWS:
        take = (n_rows & size) != 0

        @pl.when(take)
        def _(off=off, size=size):
            make_copy(off, size)

        off = off + jnp.where(take, size, 0)
        size //= 2


def _dispatch_kernel(start_ref, rows_ref, mask_t_ref, h_ref, xs_in_ref, xs_ref, blk_ref, sem_ref):
    del xs_in_ref
    t = pl.program_id(0)
    slot = t % 2
    mask_t = mask_t_ref[...]
    row = lax.broadcasted_iota(jnp.int32, (TT, TT), 0)
    col = lax.broadcasted_iota(jnp.int32, (TT, TT), 1)
    before = jnp.where(row < col, 1.0, 0.0).astype(BF16)
    rank = jnp.dot(mask_t.astype(BF16), before, preferred_element_type=F32)
    row_f = row.astype(F32)
    sels = []
    for e in range(N_EXPERTS):
        pick = (row_f == rank[e:e + 1, :]) & (mask_t[e:e + 1, :] > 0.0)
        sels.append(jnp.where(pick, 1.0, 0.0).astype(BF16))
    packed = jnp.dot(jnp.concatenate(sels, axis=0), h_ref[...], preferred_element_type=F32)
    blk_ref[slot] = packed.astype(BF16).reshape(blk_ref.shape[1:])

    def copies(step, step_slot, wait):
        for e in range(N_EXPERTS):
            start = pl.multiple_of(start_ref[step * N_EXPERTS + e], BF16_ROWS)

            def make_copy(off, size, e=e, start=start):
                src = blk_ref.at[step_slot, e, pl.ds(pl.multiple_of(off, BF16_ROWS), size)]
                dst = xs_ref.at[pl.ds(pl.multiple_of(start + off, BF16_ROWS), size)]
                cp = pltpu.make_async_copy(src, dst, sem_ref.at[step_slot])
                cp.wait() if wait else cp.start()

            _segment_copies(rows_ref[step * N_EXPERTS + e], make_copy)

    copies(t, slot, wait=False)

    @pl.when(t > 0)
    def _():
        copies(t - 1, 1 - slot, wait=True)

    @pl.when(t == pl.num_programs(0) - 1)
    def _():
        copies(t, slot, wait=True)


def _dispatch(seg_start, seg_rows, mask_t, h, n_rows):
    n, d = h.shape
    t = n // TT
    xs0 = jnp.zeros((n_rows, d), BF16)
    grid_spec = pltpu.PrefetchScalarGridSpec(
        num_scalar_prefetch=2,
        grid=(t,),
        in_specs=[pl.BlockSpec((N_EXPERTS, TT), lambda i, *_: (0, i)),
                  pl.BlockSpec((TT, d), lambda i, *_: (i, 0)),
                  pl.BlockSpec(memory_space=pl.ANY)],
        out_specs=pl.BlockSpec(memory_space=pl.ANY),
        scratch_shapes=[pltpu.VMEM((2, N_EXPERTS, TT, d), BF16), pltpu.SemaphoreType.DMA((2,))],
    )
    return pl.pallas_call(
        _dispatch_kernel,
        grid_spec=grid_spec,
        out_shape=jax.ShapeDtypeStruct((n_rows, d), BF16),
        input_output_aliases={4: 0},
        compiler_params=_cparams("arbitrary"),
        name="moe_dispatch",
    )(seg_start, seg_rows, mask_t, h, xs0)


def _expert_kernel(tile_expert_ref, tile_rows_ref, xs_ref, wg_ref, wu_ref, wd_ref, o_ref,
                   wg_bf_ref, wu_bf_ref, wd_bf_ref, acc_ref):
    del tile_expert_ref
    j = pl.program_id(0)
    f = pl.program_id(1)
    last = pl.num_programs(1) - 1
    rows = tile_rows_ref[j]

    @pl.when(f == 0)
    def _():
        acc_ref[...] = jnp.zeros_like(acc_ref)

    @pl.when(rows == TE)
    def _():
        acc_ref[...] += _swiglu_partial(xs_ref[...], wg_ref[...].astype(BF16), wu_ref[...].astype(BF16),
                                        wd_ref[...].astype(BF16))

    @pl.when((rows > 0) & (rows < TE))
    def _():
        wg_bf_ref[...] = wg_ref[...].astype(BF16)
        wu_bf_ref[...] = wu_ref[...].astype(BF16)
        wd_bf_ref[...] = wd_ref[...].astype(BF16)
        for lo in range(0, TE, TM):
            @pl.when(rows > lo)
            def _(lo=lo):
                acc_ref[lo:lo + TM, :] += _swiglu_partial(xs_ref[lo:lo + TM, :], wg_bf_ref[...], wu_bf_ref[...],
                                                          wd_bf_ref[...])

    @pl.when(f == last)
    def _():
        o_ref[...] = acc_ref[...].astype(o_ref.dtype)


def _expert_ffn(tile_expert, tile_rows, xs, w_gate_up, w_down, layer, tf):
    n_rows, d = xs.shape
    ff = w_down.shape[2]
    nf = ff // tf

    def f_eff(j, f, tr):
        return jnp.where(tr[j] > 0, f, nf - 1)

    grid_spec = pltpu.PrefetchScalarGridSpec(
        num_scalar_prefetch=2,
        grid=(n_rows // TE, nf),
        in_specs=[pl.BlockSpec((TE, d), lambda j, f, te, tr: (j, 0)),
                  pl.BlockSpec((None, None, d, tf), lambda j, f, te, tr: (layer, te[j], 0, f_eff(j, f, tr))),
                  pl.BlockSpec((None, None, d, tf), lambda j, f, te, tr: (layer, te[j], 0, nf + f_eff(j, f, tr))),
                  pl.BlockSpec((None, None, tf, d), lambda j, f, te, tr: (layer, te[j], f_eff(j, f, tr), 0))],
        out_specs=pl.BlockSpec((TE, d), lambda j, f, te, tr: (j, 0)),
        scratch_shapes=[pltpu.VMEM((d, tf), BF16), pltpu.VMEM((d, tf), BF16), pltpu.VMEM((tf, d), BF16),
                        pltpu.VMEM((TE, d), F32)],
    )
    return pl.pallas_call(
        _expert_kernel,
        grid_spec=grid_spec,
        out_shape=jax.ShapeDtypeStruct((n_rows, d), BF16),
        compiler_params=_cparams("parallel", "arbitrary"),
        name="moe_experts",
    )(tile_expert, tile_rows, xs, w_gate_up, w_gate_up, w_down)


def _combine_kernel(start_ref, rows_ref, x_ref, gate_ref, mask_ref, out_gain_ref, ys_ref, o_ref, win_ref, sem_ref,
                    *, out_norm):
    t = pl.program_id(0)
    n_t = pl.num_programs(0)
    slot = t % 2

    def copies(step, step_slot, wait):
        for e in range(N_EXPERTS):
            start = pl.multiple_of(start_ref[step * N_EXPERTS + e], BF16_ROWS)

            def make_copy(off, size, e=e, start=start):
                src = ys_ref.at[pl.ds(pl.multiple_of(start + off, BF16_ROWS), size)]
                dst = win_ref.at[step_slot, e, pl.ds(pl.multiple_of(off, BF16_ROWS), size)]
                cp = pltpu.make_async_copy(src, dst, sem_ref.at[step_slot])
                cp.wait() if wait else cp.start()

            _segment_copies(rows_ref[step * N_EXPERTS + e], make_copy)

    @pl.when(t == 0)
    def _():
        win_ref[...] = jnp.zeros_like(win_ref)
        copies(0, 0, wait=False)

    @pl.when(t + 1 < n_t)
    def _():
        copies(t + 1, 1 - slot, wait=False)

    copies(t, slot, wait=True)

    mask = mask_ref[...]
    gate = gate_ref[...]
    row = lax.broadcasted_iota(jnp.int32, (TT, TT), 0)
    col = lax.broadcasted_iota(jnp.int32, (TT, TT), 1)
    before = jnp.where(col < row, 1.0, 0.0).astype(BF16)
    rank = jnp.dot(before, mask.astype(BF16), preferred_element_type=F32)
    col_f = col.astype(F32)
    y = x_ref[...]
    for e in range(N_EXPERTS):
        pick = (col_f == rank[:, e:e + 1]) & (mask[:, e:e + 1] > 0.0)
        sel = jnp.where(pick, 1.0, 0.0).astype(BF16)
        y = y + gate[:, e:e + 1] * jnp.dot(sel, win_ref[slot, e], preferred_element_type=F32)
    o_ref[...] = _rmsnorm(y, out_gain_ref[...], RMS_EPS) if out_norm else y


def _combine(seg_start, seg_rows, x, gate, mask, ys, out_gain, out_norm):
    n, d = x.shape
    t = n // TT
    grid_spec = pltpu.PrefetchScalarGridSpec(
        num_scalar_prefetch=2,
        grid=(t,),
        in_specs=[pl.BlockSpec((TT, d), lambda i, *_: (i, 0)),
                  pl.BlockSpec((TT, LANES), lambda i, *_: (i, 0)),
                  pl.BlockSpec((TT, LANES), lambda i, *_: (i, 0)),
                  pl.BlockSpec((1, d), lambda i, *_: (0, 0)),
                  pl.BlockSpec(memory_space=pl.ANY)],
        out_specs=pl.BlockSpec((TT, d), lambda i, *_: (i, 0)),
        scratch_shapes=[pltpu.VMEM((2, N_EXPERTS, TT, d), BF16), pltpu.SemaphoreType.DMA((2,))],
    )
    return pl.pallas_call(
        functools.partial(_combine_kernel, out_norm=out_norm),
        grid_spec=grid_spec,
        out_shape=jax.ShapeDtypeStruct((n, d), F32),
        compiler_params=_cparams("arbitrary"),
        name="moe_combine",
    )(seg_start, seg_rows, x, gate, mask, out_gain.reshape(1, d), ys)


def _attn_out_moe_ffn(a, w_o, x, g, router, w_gate_up, w_down, layer, tf, out_gain, out_norm):
    n, d = x.shape
    t = n // TT
    x, h, gate, mask, mask_t, cnt = _attn_out_router(a, w_o, x, g, router)
    cnt = cnt[:, 0, :N_EXPERTS].astype(jnp.int32)
    seg_rows = _round_up(cnt, BF16_ROWS)
    expert_rows = jnp.sum(seg_rows, axis=0)
    region = _round_up(expert_rows, TE)
    region_end = jnp.cumsum(region)
    seg_start = (region_end - region)[None, :] + jnp.cumsum(seg_rows, axis=0) - seg_rows
    n_tiles = (_round_up(2 * n + t * N_EXPERTS * (BF16_ROWS - 1), TE) + N_EXPERTS * TE) // TE
    tile_end = region_end // TE
    tile_ids = jnp.arange(n_tiles, dtype=jnp.int32)
    last_tile = jnp.minimum(tile_ids, tile_end[-1] - 1)
    tile_expert = jnp.sum((last_tile[:, None] >= tile_end[None, :]).astype(jnp.int32), axis=1)
    tile_expert = jnp.minimum(tile_expert, N_EXPERTS - 1)
    rows_before = (tile_ids - (tile_end - region // TE)[tile_expert]) * TE
    tile_rows = jnp.where(tile_ids < tile_end[-1], jnp.clip(expert_rows[tile_expert] - rows_before, 0, TE), 0)
    seg_start = seg_start.reshape(-1).astype(jnp.int32)
    seg_rows = seg_rows.reshape(-1).astype(jnp.int32)

    xs = _dispatch(seg_start, seg_rows, mask_t, h, n_tiles * TE)
    ys = _expert_ffn(tile_expert, tile_rows.astype(jnp.int32), xs, w_gate_up, w_down, layer, tf)
    return _combine(seg_start, seg_rows, x, gate, mask, ys, out_gain, out_norm)


def kernel(x, attn_norm, ffn_norm, final_norm, t5_rel_bias, diff_w_qkv, diff_w_o, diff_lambda_q1, diff_lambda_k1, diff_lambda_q2, diff_lambda_k2, diff_subln, chunk_w_qkv, chunk_w_o, chunk_rel_bias, dense_w_gate_up, dense_w_down, moe_router, moe_w_gate_up, moe_w_down):
    b, s, d = x.shape
    n = b * s
    xf = x.reshape(n, d)
    t5_bias, t5_const_rows = _t5_bias_t(t5_rel_bias, s)
    for i in range(DEPTH):
        j = i // 2
        if i % 2 == 0:
            qkv = _norm_matmul(xf, attn_norm[i], diff_w_qkv[j].astype(BF16))
            lam_init = 0.8 - 0.6 * math.exp(-0.3 * i)
            lam_vecs = jnp.stack([diff_lambda_q1[j], diff_lambda_k1[j], diff_lambda_q2[j], diff_lambda_k2[j]])
            a = _diff_attention(qkv.reshape(b, s, 3 * d), t5_bias, t5_const_rows, lam_vecs, diff_subln[j], lam_init)
            xf = _attn_out_dense_ffn(a.reshape(n, d), diff_w_o[j].astype(BF16), xf, ffn_norm[i],
                                     dense_w_gate_up[j].astype(BF16), dense_w_down[j].astype(BF16),
                                     tf=dense_w_down.shape[1] // 2)
        else:
            qkv = _norm_matmul(xf, attn_norm[i], chunk_w_qkv[j].astype(BF16))
            a = _chunk_attention(qkv.reshape(b, s, 3 * d), _band_bias_t(chunk_rel_bias[j]))
            xf = _attn_out_moe_ffn(a.reshape(n, d), chunk_w_o[j].astype(BF16), xf, ffn_norm[i], moe_router[j],
                                   moe_w_gate_up, moe_w_down, layer=j, tf=moe_w_down.shape[2] // 7,
                                   out_gain=final_norm, out_norm=i == DEPTH - 1)
    return xf.reshape(b, s, d)
```

```python
import functools
import math

import numpy as np
import jax
import jax.numpy as jnp
from jax import lax
from jax.experimental import pallas as pl
from jax.experimental.pallas import tpu as pltpu

F32 = jnp.float32
BF16 = jnp.bfloat16

D_MODEL = 1024
DEPTH = 4
CHUNK = 64
DA_HEADS = 8
DA_HEAD_DIM = 64
T5_BUCKETS = 32
T5_MAX_DIST = 128
CA_HEADS = 16
CA_HEAD_DIM = 64
BAND_PREV = 8
REL_CLIP = 4 * CHUNK
N_EXPERTS = 8
RMS_EPS = 1e-6
SUBLN_EPS = 1e-5
NEG_INF = -1e30

LANES = 128
BF16_ROWS = 16
VMEM_LIMIT = 56 * 1024 * 1024
TM = 512
TE = 4 * TM
TQ = 256
TQ_BAND = 256
TT = 256
KB_T5 = 128
KB_BAND = 256
MM_ROWS = 2048
N_SLOTS = 4
ONES_ROWS = BF16_ROWS
LOG2E = math.log2(math.e)
Q_SCALE = DA_HEAD_DIM ** -0.5 * LOG2E
PAIR = 2 * CA_HEAD_DIM


def _round_up(a, m):
    return (a + m - 1) // m * m


def _cparams(*sem):
    return pltpu.CompilerParams(dimension_semantics=sem, vmem_limit_bytes=VMEM_LIMIT)


def _rmsnorm(x32, g, eps):
    return x32 * lax.rsqrt(jnp.mean(x32 * x32, axis=-1, keepdims=True) + eps) * g


def _dot_nt(a, b):
    return lax.dot_general(a, b, (((1,), (1,)), ((), ())), preferred_element_type=F32)


def _norm_matmul_kernel(x_ref, g_ref, w_ref, o_ref):
    d = x_ref.shape[1]
    half = x_ref.shape[0] // 2
    for rows in (slice(0, half), slice(half, 2 * half)):
        h = _rmsnorm(x_ref[rows, :], g_ref[...], RMS_EPS).astype(BF16)
        q = jnp.dot(h, w_ref[:, :d], preferred_element_type=F32) * Q_SCALE
        o_ref[rows, :d] = q.astype(o_ref.dtype)
        o_ref[rows, d:] = jnp.dot(h, w_ref[:, d:], preferred_element_type=F32).astype(o_ref.dtype)


def _norm_matmul(x, g, w):
    n, d = x.shape
    n_out = w.shape[1]
    return pl.pallas_call(
        _norm_matmul_kernel,
        grid=(n // (2 * TM),),
        in_specs=[pl.BlockSpec((2 * TM, d), lambda i: (i, 0)),
                  pl.BlockSpec((1, d), lambda i: (0, 0)),
                  pl.BlockSpec((d, n_out), lambda i: (0, 0), pipeline_mode=pl.Buffered(1))],
        out_specs=pl.BlockSpec((2 * TM, n_out), lambda i: (i, 0)),
        out_shape=jax.ShapeDtypeStruct((n, n_out), BF16),
        compiler_params=_cparams("parallel"),
        name="norm_qkv",
    )(x, g.reshape(1, d), w)


def _t5_bucket_np(rel):
    nb = T5_BUCKETS // 2
    max_exact = nb // 2
    ret = (rel > 0).astype(np.int32) * nb
    n = np.abs(rel)
    n_f = np.maximum(n, 1).astype(np.float32)
    large = max_exact + (np.log(n_f / np.float32(max_exact)) / np.float32(math.log(T5_MAX_DIST / max_exact))
                         * np.float32(nb - max_exact)).astype(np.int32)
    large = np.minimum(large, nb - 1)
    return ret + np.where(n < max_exact, n, large)


def _lookup_kernel(idx_ref, t_ref, o_ref):
    rows = lax.broadcasted_iota(jnp.int32, (t_ref.shape[1], idx_ref.shape[1]), 0)
    onehot = jnp.where(rows == idx_ref[...], 1.0, 0.0)
    o_ref[...] = jnp.dot(t_ref[...], onehot, preferred_element_type=F32, precision=lax.Precision.HIGHEST)


def _table_lookup_t(table, idx):
    r, h = table.shape
    table_t = jnp.zeros((h, _round_up(r, LANES)), F32).at[:, :r].set(table.T)
    return pl.pallas_call(
        _lookup_kernel,
        out_shape=jax.ShapeDtypeStruct((h, len(idx)), F32),
        name="bias_lookup",
    )(jnp.asarray(idx, jnp.int32).reshape(1, -1), table_t)


def _toeplitz_kernel(win_ref, mask_ref, o_ref):
    tq = o_ref.shape[1]
    for blk in range(o_ref.shape[0] // tq):
        x = jnp.broadcast_to(win_ref[:, blk * 2 * tq:(blk + 1) * 2 * tq], (tq, 2 * tq))
        skew = pltpu.roll(x, 0, 1, stride=1, stride_axis=0)
        o_ref[blk * tq:(blk + 1) * tq, :] = skew[:, :tq] + mask_ref[blk * tq:(blk + 1) * tq, :]


def _window_diagonals(n_keys, tq):
    m = np.arange(2 * tq)
    back = np.where(m < tq, -m, 2 * tq - m)
    d = np.arange(0, n_keys, tq)[:, None] + back[None, :]
    return np.clip(d, 1 - tq, n_keys - 1).reshape(-1)


def _toeplitz_bias_t(table, idx, allowed):
    win = _table_lookup_t(table, idx) * LOG2E
    h = win.shape[0]
    n_keys, tq = allowed.shape
    mask_add = jnp.asarray(np.where(allowed, 0.0, NEG_INF).astype(np.float32))
    return pl.pallas_call(
        _toeplitz_kernel,
        grid=(h,),
        in_specs=[pl.BlockSpec((None, 1, win.shape[1]), lambda i: (i, 0, 0)),
                  pl.BlockSpec((n_keys, tq), lambda i: (0, 0))],
        out_specs=pl.BlockSpec((None, n_keys, tq), lambda i: (i, 0, 0)),
        out_shape=jax.ShapeDtypeStruct((h, n_keys, tq), F32),
        compiler_params=_cparams("parallel"),
        name="toeplitz_bias",
    )(win.reshape(h, 1, -1), mask_add)


def _t5_bias_t(table, seq):
    idx = _t5_bucket_np(_window_diagonals(seq, TQ) - (seq - TQ))
    i = np.arange(TQ)[None, :]
    j = np.arange(seq)[:, None] - (seq - TQ)
    allowed = (j < 0) | ((j // CHUNK) <= (i // CHUNK))
    far_bucket = _t5_bucket_np(np.asarray(-seq))
    is_far = np.all((_t5_bucket_np(j - i) == far_bucket) & allowed, axis=1)
    const_rows = int(np.argmin(is_far)) if not is_far.all() else seq
    return _toeplitz_bias_t(table, idx, allowed), const_rows


def _load_vt(v_ref, vt_ref):
    n_v = v_ref.shape[1]
    vt_ref[0:n_v, :] = v_ref[...].T
    vt_ref[n_v:, :] = jnp.ones((ONES_ROWS, v_ref.shape[0]), BF16)


def _attend_pair_phases(k_ref, k0, n_k, kb_rows, q, lane_split, bias_rows, bias_const, vt_ref, s_ref, p_ref, finish):
    tq = q.shape[0]
    halves = (slice(0, tq), slice(tq, 2 * tq))
    n_v = vt_ref.shape[0] - ONES_ROWS
    state = {"m": [None, None], "o": None}

    def score(kr):
        def run():
            if kr == 0:
                zero = jnp.zeros_like(q)
                state["qq"] = jnp.concatenate([jnp.where(lane_split, q, zero), jnp.where(lane_split, zero, q)], axis=0)
            rows = min(MM_ROWS, n_k - kr)
            s_ref[kr:kr + rows, :] = _dot_nt(k_ref[k0 + kr:k0 + kr + rows, :], state["qq"])
        return run

    def running_max(kb, h):
        def run():
            cols = halves[h]
            if bias_const(kb):
                block_max = jnp.max(s_ref[kb:kb + kb_rows, cols], axis=0, keepdims=True) + bias_rows(h, kb)
            else:
                s = s_ref[kb:kb + kb_rows, cols] + bias_rows(h, kb)
                s_ref[kb:kb + kb_rows, cols] = s
                block_max = jnp.max(s, axis=0, keepdims=True)
            m = state["m"]
            m[h] = block_max if m[h] is None else jnp.maximum(m[h], block_max)
        return run

    def exponentials(kb, h):
        def run():
            cols = halves[h]
            m = state["m"][h]
            shift = m - bias_rows(h, kb) if bias_const(kb) else m
            p_ref[kb:kb + kb_rows, cols] = jnp.exp2(s_ref[kb:kb + kb_rows, cols] - shift).astype(BF16)
        return run

    def pv(kr):
        def run():
            rows = min(MM_ROWS, n_k - kr)
            part = jnp.dot(vt_ref[:, k0 + kr:k0 + kr + rows], p_ref[kr:kr + rows, :], preferred_element_type=F32)
            state["o"] = part if state["o"] is None else state["o"] + part
            if kr + rows == n_k:
                o_t = state["o"]
                finish([o_t[0:n_v, cols] * (1.0 / o_t[n_v:n_v + 1, cols]) for cols in halves])
        return run

    blocks = [(kb, h) for kb in range(0, n_k, kb_rows) for h in range(2)]
    return ([score(kr) for kr in range(0, n_k, MM_ROWS)], [running_max(kb, h) for kb, h in blocks],
            [exponentials(kb, h) for kb, h in blocks], [pv(kr) for kr in range(0, n_k, MM_ROWS)])


def _interleave(tiles):
    def merged(a, b):
        order = sorted([((i + 0.5) / len(a), 0, f) for i, f in enumerate(a)]
                       + [((i + 0.5) / len(b), 1, f) for i, f in enumerate(b)], key=lambda t: t[:2])
        for _, _, thunk in order:
            thunk()

    empty = ([], [], [], [])
    prev = empty
    for cur in list(tiles) + [empty]:
        merged(cur[0], prev[2])
        merged(cur[1], prev[3])
        prev = cur


def _diff_attn_kernel(q_ref, k_ref, v_ref, bias_ref, lam_ref, g_ref, o_ref, vt_ref, s_ref, p_ref, *,
                      lam_init, const_rows):
    lv = lam_ref[...]
    lam = (jnp.exp(jnp.sum(lv[0:1] * lv[1:2], axis=-1, keepdims=True))
           - jnp.exp(jnp.sum(lv[2:3] * lv[3:4], axis=-1, keepdims=True)) + lam_init)
    first_map = lax.broadcasted_iota(jnp.int32, (1, 2 * DA_HEAD_DIM), 1) < DA_HEAD_DIM
    seq = q_ref.shape[0]
    _load_vt(v_ref, vt_ref)

    def tile(qi):
        n_k = (qi + 1) * TQ
        row0 = seq - n_k

        def bias_const(kb):
            return row0 + kb + KB_T5 <= const_rows

        def bias_rows(h, kb):
            n = 1 if bias_const(kb) else KB_T5
            return bias_ref[row0 + kb:row0 + kb + n, :]

        def finish(outs):
            o_t = outs[0] - lam * outs[1]
            o_t = o_t * lax.rsqrt(jnp.mean(o_t * o_t, axis=0, keepdims=True) + SUBLN_EPS) * g_ref[...]
            o_ref[qi * TQ:(qi + 1) * TQ, :] = (o_t * (1.0 - lam_init)).T.astype(o_ref.dtype)

        slot = qi % N_SLOTS
        return _attend_pair_phases(k_ref, 0, n_k, KB_T5, q_ref[qi * TQ:(qi + 1) * TQ, :], first_map, bias_rows, bias_const,
                                  vt_ref, s_ref.at[slot], p_ref.at[slot], finish)

    _interleave(tile(qi) for qi in range(seq // TQ))


def _diff_attention(qkv, bias_t, const_rows, lam_vecs, subln_g, lam_init):
    b, s, _ = qkv.shape
    hd = 2 * DA_HEAD_DIM
    return pl.pallas_call(
        functools.partial(_diff_attn_kernel, lam_init=lam_init, const_rows=const_rows),
        grid=(DA_HEADS, b),
        in_specs=[pl.BlockSpec((None, s, hd), lambda h, i: (i, 0, h)),
                  pl.BlockSpec((None, s, hd), lambda h, i: (i, 0, DA_HEADS + h)),
                  pl.BlockSpec((None, s, hd), lambda h, i: (i, 0, 2 * DA_HEADS + h)),
                  pl.BlockSpec((None, s, TQ), lambda h, i: (h, 0, 0)),
                  pl.BlockSpec((4, DA_HEAD_DIM), lambda h, i: (0, 0)),
                  pl.BlockSpec((hd, 1), lambda h, i: (0, 0))],
        out_specs=pl.BlockSpec((None, s, hd), lambda h, i: (i, 0, h)),
        out_shape=jax.ShapeDtypeStruct((b, s, D_MODEL), BF16),
        scratch_shapes=[pltpu.VMEM((hd + ONES_ROWS, s), BF16), pltpu.VMEM((N_SLOTS, s, 2 * TQ), F32),
                        pltpu.VMEM((N_SLOTS, s, 2 * TQ), BF16)],
        compiler_params=_cparams("parallel", "parallel"),
        name="diff_attn",
    )(qkv, qkv, qkv, bias_t, lam_vecs, subln_g.reshape(hd, 1))


BAND_KEYS = BAND_PREV * CHUNK + TQ_BAND


def _band_bias_t(rel_table):
    dist = BAND_PREV * CHUNK - _window_diagonals(BAND_KEYS, TQ_BAND)
    q_chunk = np.arange(TQ_BAND)[None, :] // CHUNK
    k_chunk = np.arange(BAND_KEYS)[:, None] // CHUNK
    allowed = (k_chunk >= q_chunk) & (k_chunk <= q_chunk + BAND_PREV)
    return _toeplitz_bias_t(rel_table, np.clip(dist, -REL_CLIP, REL_CLIP) + REL_CLIP, allowed)


def _chunk_attn_kernel(q_ref, k_ref, v_ref, bias_ref, o_ref, vt_ref, s_ref, p_ref):
    first_head = lax.broadcasted_iota(jnp.int32, (1, PAIR), 1) < CA_HEAD_DIM
    first_head_rows = lax.broadcasted_iota(jnp.int32, (PAIR, 1), 0) < CA_HEAD_DIM
    seq = q_ref.shape[0]
    _load_vt(v_ref, vt_ref)

    def tile(qi):
        q0 = qi * TQ_BAND
        k0 = max(0, q0 - BAND_PREV * CHUNK)
        n_k = q0 + TQ_BAND - k0
        row0 = BAND_KEYS - n_k

        def bias_rows(h, kb):
            return bias_ref[h, row0 + kb:row0 + kb + KB_BAND, :]

        def finish(outs):
            o_ref[q0:q0 + TQ_BAND, :] = jnp.where(first_head_rows, outs[0], outs[1]).T.astype(o_ref.dtype)

        slot = qi % N_SLOTS
        return _attend_pair_phases(k_ref, k0, n_k, KB_BAND, q_ref[q0:q0 + TQ_BAND, :], first_head, bias_rows, lambda kb: False,
                                  vt_ref, s_ref.at[slot], p_ref.at[slot], finish)

    _interleave(tile(qi) for qi in range(seq // TQ_BAND))


def _chunk_attention(qkv, band_bias_t):
    b, s, _ = qkv.shape
    n_pairs = CA_HEADS // 2
    return pl.pallas_call(
        _chunk_attn_kernel,
        grid=(n_pairs, b),
        in_specs=[pl.BlockSpec((None, s, PAIR), lambda p, i: (i, 0, p)),
                  pl.BlockSpec((None, s, PAIR), lambda p, i: (i, 0, n_pairs + p)),
                  pl.BlockSpec((None, s, PAIR), lambda p, i: (i, 0, 2 * n_pairs + p)),
                  pl.BlockSpec((2, BAND_KEYS, TQ_BAND), lambda p, i: (p, 0, 0))],
        out_specs=pl.BlockSpec((None, s, PAIR), lambda p, i: (i, 0, p)),
        out_shape=jax.ShapeDtypeStruct((b, s, D_MODEL), BF16),
        scratch_shapes=[pltpu.VMEM((PAIR + ONES_ROWS, s), BF16), pltpu.VMEM((N_SLOTS, BAND_KEYS, 2 * TQ_BAND), F32),
                        pltpu.VMEM((N_SLOTS, BAND_KEYS, 2 * TQ_BAND), BF16)],
        compiler_params=_cparams("parallel", "parallel"),
        name="chunk_attn",
    )(qkv, qkv, qkv, band_bias_t)


def _swiglu_partial(h, wg, wu, wd):
    g = jnp.dot(h, wg, preferred_element_type=F32)
    u = jnp.dot(h, wu, preferred_element_type=F32)
    a = (g * (0.5 * jnp.tanh(0.5 * g) + 0.5) * u).astype(BF16)
    return jnp.dot(a, wd, preferred_element_type=F32)


def _dense_ffn_kernel(a_ref, wo_ref, x_ref, g_ref, wgu_ref, wd_ref, o_ref, *, tf):
    x1 = x_ref[...] + jnp.dot(a_ref[...], wo_ref[...], preferred_element_type=F32)
    h = _rmsnorm(x1, g_ref[...], RMS_EPS).astype(BF16)
    ff = wd_ref.shape[0]
    y = x1
    for lo in range(0, ff, tf):
        y = y + _swiglu_partial(h, wgu_ref[:, lo:lo + tf], wgu_ref[:, ff + lo:ff + lo + tf], wd_ref[lo:lo + tf, :])
    o_ref[...] = y


def _attn_out_dense_ffn(a, w_o, x, g, w_gate_up, w_down, tf):
    n, d = x.shape

    def resident(w):
        return pl.BlockSpec(w.shape, lambda i: (0, 0), pipeline_mode=pl.Buffered(1))

    return pl.pallas_call(
        functools.partial(_dense_ffn_kernel, tf=tf),
        grid=(n // TM,),
        in_specs=[pl.BlockSpec((TM, d), lambda i: (i, 0)),
                  resident(w_o),
                  pl.BlockSpec((TM, d), lambda i: (i, 0)),
                  pl.BlockSpec((1, d), lambda i: (0, 0)),
                  resident(w_gate_up),
                  resident(w_down)],
        out_specs=pl.BlockSpec((TM, d), lambda i: (i, 0)),
        out_shape=jax.ShapeDtypeStruct((n, d), F32),
        compiler_params=_cparams("parallel"),
        name="dense_ffn",
    )(a, w_o, x, g.reshape(1, d), w_gate_up, w_down)


def _router_kernel(a_ref, wo_ref, x_ref, g_ref, r_ref, x1_ref, h_ref, gate_ref, mask_ref, mask_t_ref, cnt_ref):
    x1 = x_ref[...] + jnp.dot(a_ref[...], wo_ref[...], preferred_element_type=F32)
    x1_ref[...] = x1
    h = _rmsnorm(x1, g_ref[...], RMS_EPS)
    h_hi = h.astype(BF16)
    h_ref[...] = h_hi
    r = r_ref[...]
    r_hi = r.astype(BF16)
    r_lo = (r - r_hi.astype(F32)).astype(BF16)
    h_lo = (h - h_hi.astype(F32)).astype(BF16)
    logits = (jnp.dot(h_hi, r_hi, preferred_element_type=F32)
              + (jnp.dot(h_hi, r_lo, preferred_element_type=F32) + jnp.dot(h_lo, r_hi, preferred_element_type=F32)))
    lane = lax.broadcasted_iota(jnp.int32, logits.shape, 1)
    logits = jnp.where(lane < N_EXPERTS, logits, -jnp.inf)
    v1 = jnp.max(logits, axis=-1, keepdims=True)
    i1 = jnp.min(jnp.where(logits == v1, lane, LANES), axis=-1, keepdims=True)
    m1 = lane == i1
    rest = jnp.where(m1, -jnp.inf, logits)
    v2 = jnp.max(rest, axis=-1, keepdims=True)
    i2 = jnp.min(jnp.where(rest == v2, lane, LANES), axis=-1, keepdims=True)
    m2 = lane == i2
    e2 = jnp.exp(v2 - v1)
    g1 = 1.0 / (1.0 + e2)
    g2 = e2 / (1.0 + e2)
    gate_ref[...] = jnp.where(m1, g1, 0.0) + jnp.where(m2, g2, 0.0)
    mask = jnp.where(m1 | m2, 1.0, 0.0)
    mask_ref[...] = mask
    mask_t_ref[...] = mask.T[0:N_EXPERTS, :]
    cnt_ref[...] = jnp.broadcast_to(jnp.sum(mask, axis=0, keepdims=True), cnt_ref.shape)


def _attn_out_router(a, w_o, x, g, router):
    n, d = x.shape
    t = n // TT
    r_pad = jnp.zeros((d, LANES), F32).at[:, :N_EXPERTS].set(router)
    return pl.pallas_call(
        _router_kernel,
        grid=(t,),
        in_specs=[pl.BlockSpec((TT, d), lambda i: (i, 0)),
                  pl.BlockSpec((d, d), lambda i: (0, 0)),
                  pl.BlockSpec((TT, d), lambda i: (i, 0)),
                  pl.BlockSpec((1, d), lambda i: (0, 0)),
                  pl.BlockSpec((d, LANES), lambda i: (0, 0))],
        out_specs=[pl.BlockSpec((TT, d), lambda i: (i, 0)),
                   pl.BlockSpec((TT, d), lambda i: (i, 0)),
                   pl.BlockSpec((TT, LANES), lambda i: (i, 0)),
                   pl.BlockSpec((TT, LANES), lambda i: (i, 0)),
                   pl.BlockSpec((N_EXPERTS, TT), lambda i: (0, i)),
                   pl.BlockSpec((None, 8, LANES), lambda i: (i, 0, 0))],
        out_shape=[jax.ShapeDtypeStruct((n, d), F32),
                   jax.ShapeDtypeStruct((n, d), BF16),
                   jax.ShapeDtypeStruct((n, LANES), F32),
                   jax.ShapeDtypeStruct((n, LANES), F32),
                   jax.ShapeDtypeStruct((N_EXPERTS, n), F32),
                   jax.ShapeDtypeStruct((t, 8, LANES), F32)],
        compiler_params=_cparams("parallel"),
        name="router",
    )(a, w_o, x, g.reshape(1, d), r_pad)


def _segment_copies(n_rows, make_copy):
    off = jnp.int32(0)
    size = TT
    while size >= BF16_ROWS:
        take = (n_rows & size) != 0

        @pl.when(take)
        def _(off=off, size=size):
            make_copy(off, size)

        off = off + jnp.where(take, size, 0)
        size //= 2


def _dispatch_kernel(start_ref, rows_ref, mask_t_ref, h_ref, xs_in_ref, xs_ref, blk_ref, sem_ref):
    del xs_in_ref
    t = pl.program_id(0)
    slot = t % 2
    mask_t = mask_t_ref[...]
    row = lax.broadcasted_iota(jnp.int32, (TT, TT), 0)
    col = lax.broadcasted_iota(jnp.int32, (TT, TT), 1)
    before = jnp.where(row < col, 1.0, 0.0).astype(BF16)
    rank = jnp.dot(mask_t.astype(BF16), before, preferred_element_type=F32)
    row_f = row.astype(F32)
    sels = []
    for e in range(N_EXPERTS):
        pick = (row_f == rank[e:e + 1, :]) & (mask_t[e:e + 1, :] > 0.0)
        sels.append(jnp.where(pick, 1.0, 0.0).astype(BF16))
    packed = jnp.dot(jnp.concatenate(sels, axis=0), h_ref[...], preferred_element_type=F32)
    blk_ref[slot] = packed.astype(BF16).reshape(blk_ref.shape[1:])

    def copies(step, step_slot, wait):
        for e in range(N_EXPERTS):
            start = pl.multiple_of(start_ref[step * N_EXPERTS + e], BF16_ROWS)

            def make_copy(off, size, e=e, start=start):
                src = blk_ref.at[step_slot, e, pl.ds(pl.multiple_of(off, BF16_ROWS), size)]
                dst = xs_ref.at[pl.ds(pl.multiple_of(start + off, BF16_ROWS), size)]
                cp = pltpu.make_async_copy(src, dst, sem_ref.at[step_slot])
                cp.wait() if wait else cp.start()

            _segment_copies(rows_ref[step * N_EXPERTS + e], make_copy)

    copies(t, slot, wait=False)

    @pl.when(t > 0)
    def _():
        copies(t - 1, 1 - slot, wait=True)

    @pl.when(t == pl.num_programs(0) - 1)
    def _():
        copies(t, slot, wait=True)


def _dispatch(seg_start, seg_rows, mask_t, h, n_rows):
    n, d = h.shape
    t = n // TT
    xs0 = jnp.zeros((n_rows, d), BF16)
    grid_spec = pltpu.PrefetchScalarGridSpec(
        num_scalar_prefetch=2,
        grid=(t,),
        in_specs=[pl.BlockSpec((N_EXPERTS, TT), lambda i, *_: (0, i)),
                  pl.BlockSpec((TT, d), lambda i, *_: (i, 0)),
                  pl.BlockSpec(memory_space=pl.ANY)],
        out_specs=pl.BlockSpec(memory_space=pl.ANY),
        scratch_shapes=[pltpu.VMEM((2, N_EXPERTS, TT, d), BF16), pltpu.SemaphoreType.DMA((2,))],
    )
    return pl.pallas_call(
        _dispatch_kernel,
        grid_spec=grid_spec,
        out_shape=jax.ShapeDtypeStruct((n_rows, d), BF16),
        input_output_aliases={4: 0},
        compiler_params=_cparams("arbitrary"),
        name="moe_dispatch",
    )(seg_start, seg_rows, mask_t, h, xs0)


def _expert_kernel(tile_expert_ref, tile_rows_ref, xs_ref, wg_ref, wu_ref, wd_ref, o_ref,
                   wg_bf_ref, wu_bf_ref, wd_bf_ref, acc_ref):
    del tile_expert_ref
    j = pl.program_id(0)
    f = pl.program_id(1)
    last = pl.num_programs(1) - 1
    rows = tile_rows_ref[j]

    @pl.when(f == 0)
    def _():
        acc_ref[...] = jnp.zeros_like(acc_ref)

    @pl.when(rows == TE)
    def _():
        acc_ref[...] += _swiglu_partial(xs_ref[...], wg_ref[...].astype(BF16), wu_ref[...].astype(BF16),
                                        wd_ref[...].astype(BF16))

    @pl.when((rows > 0) & (rows < TE))
    def _():
        wg_bf_ref[...] = wg_ref[...].astype(BF16)
        wu_bf_ref[...] = wu_ref[...].astype(BF16)
        wd_bf_ref[...] = wd_ref[...].astype(BF16)
        for lo in range(0, TE, TM):
            @pl.when(rows > lo)
            def _(lo=lo):
                acc_ref[lo:lo + TM, :] += _swiglu_partial(xs_ref[lo:lo + TM, :], wg_bf_ref[...], wu_bf_ref[...],
                                                          wd_bf_ref[...])

    @pl.when(f == last)
    def _():
        o_ref[...] = acc_ref[...].astype(o_ref.dtype)


def _expert_ffn(tile_expert, tile_rows, xs, w_gate_up, w_down, layer, tf):
    n_rows, d = xs.shape
    ff = w_down.shape[2]
    nf = ff // tf

    def f_eff(j, f, tr):
        return jnp.where(tr[j] > 0, f, nf - 1)

    grid_spec = pltpu.PrefetchScalarGridSpec(
        num_scalar_prefetch=2,
        grid=(n_rows // TE, nf),
        in_specs=[pl.BlockSpec((TE, d), lambda j, f, te, tr: (j, 0)),
                  pl.BlockSpec((None, None, d, tf), lambda j, f, te, tr: (layer, te[j], 0, f_eff(j, f, tr))),
                  pl.BlockSpec((None, None, d, tf), lambda j, f, te, tr: (layer, te[j], 0, nf + f_eff(j, f, tr))),
                  pl.BlockSpec((None, None, tf, d), lambda j, f, te, tr: (layer, te[j], f_eff(j, f, tr), 0))],
        out_specs=pl.BlockSpec((TE, d), lambda j, f, te, tr: (j, 0)),
        scratch_shapes=[pltpu.VMEM((d, tf), BF16), pltpu.VMEM((d, tf), BF16), pltpu.VMEM((tf, d), BF16),
                        pltpu.VMEM((TE, d), F32)],
    )
    return pl.pallas_call(
        _expert_kernel,
        grid_spec=grid_spec,
        out_shape=jax.ShapeDtypeStruct((n_rows, d), BF16),
        compiler_params=_cparams("parallel", "arbitrary"),
        name="moe_experts",
    )(tile_expert, tile_rows, xs, w_gate_up, w_gate_up, w_down)


def _combine_kernel(start_ref, rows_ref, x_ref, gate_ref, mask_ref, out_gain_ref, ys_ref, o_ref, win_ref, sem_ref,
                    *, out_norm):
    t = pl.program_id(0)
    n_t = pl.num_programs(0)
    slot = t % 2

    def copies(step, step_slot, wait):
        for e in range(N_EXPERTS):
            start = pl.multiple_of(start_ref[step * N_EXPERTS + e], BF16_ROWS)

            def make_copy(off, size, e=e, start=start):
                src = ys_ref.at[pl.ds(pl.multiple_of(start + off, BF16_ROWS), size)]
                dst = win_ref.at[step_slot, e, pl.ds(pl.multiple_of(off, BF16_ROWS), size)]
                cp = pltpu.make_async_copy(src, dst, sem_ref.at[step_slot])
                cp.wait() if wait else cp.start()

            _segment_copies(rows_ref[step * N_EXPERTS + e], make_copy)

    @pl.when(t == 0)
    def _():
        win_ref[...] = jnp.zeros_like(win_ref)
        copies(0, 0, wait=False)

    @pl.when(t + 1 < n_t)
    def _():
        copies(t + 1, 1 - slot, wait=False)

    copies(t, slot, wait=True)

    mask = mask_ref[...]
    gate = gate_ref[...]
    row = lax.broadcasted_iota(jnp.int32, (TT, TT), 0)
    col = lax.broadcasted_iota(jnp.int32, (TT, TT), 1)
    before = jnp.where(col < row, 1.0, 0.0).astype(BF16)
    rank = jnp.dot(before, mask.astype(BF16), preferred_element_type=F32)
    col_f = col.astype(F32)
    y = x_ref[...]
    for e in range(N_EXPERTS):
        pick = (col_f == rank[:, e:e + 1]) & (mask[:, e:e + 1] > 0.0)
        sel = jnp.where(pick, 1.0, 0.0).astype(BF16)
        y = y + gate[:, e:e + 1] * jnp.dot(sel, win_ref[slot, e], preferred_element_type=F32)
    o_ref[...] = _rmsnorm(y, out_gain_ref[...], RMS_EPS) if out_norm else y


def _combine(seg_start, seg_rows, x, gate, mask, ys, out_gain, out_norm):
    n, d = x.shape
    t = n // TT
    grid_spec = pltpu.PrefetchScalarGridSpec(
        num_scalar_prefetch=2,
        grid=(t,),
        in_specs=[pl.BlockSpec((TT, d), lambda i, *_: (i, 0)),
                  pl.BlockSpec((TT, LANES), lambda i, *_: (i, 0)),
                  pl.BlockSpec((TT, LANES), lambda i, *_: (i, 0)),
                  pl.BlockSpec((1, d), lambda i, *_: (0, 0)),
                  pl.BlockSpec(memory_space=pl.ANY)],
        out_specs=pl.BlockSpec((TT, d), lambda i, *_: (i, 0)),
        scratch_shapes=[pltpu.VMEM((2, N_EXPERTS, TT, d), BF16), pltpu.SemaphoreType.DMA((2,))],
    )
    return pl.pallas_call(
        functools.partial(_combine_kernel, out_norm=out_norm),
        grid_spec=grid_spec,
        out_shape=jax.ShapeDtypeStruct((n, d), F32),
        compiler_params=_cparams("arbitrary"),
        name="moe_combine",
    )(seg_start, seg_rows, x, gate, mask, out_gain.reshape(1, d), ys)


def _attn_out_moe_ffn(a, w_o, x, g, router, w_gate_up, w_down, layer, tf, out_gain, out_norm):
    n, d = x.shape
    t = n // TT
    x, h, gate, mask, mask_t, cnt = _attn_out_router(a, w_o, x, g, router)
    cnt = cnt[:, 0, :N_EXPERTS].astype(jnp.int32)
    seg_rows = _round_up(cnt, BF16_ROWS)
    expert_rows = jnp.sum(seg_rows, axis=0)
    region = _round_up(expert_rows, TE)
    region_end = jnp.cumsum(region)
    seg_start = (region_end - region)[None, :] + jnp.cumsum(seg_rows, axis=0) - seg_rows
    n_tiles = (_round_up(2 * n + t * N_EXPERTS * (BF16_ROWS - 1), TE) + N_EXPERTS * TE) // TE
    tile_end = region_end // TE
    tile_ids = jnp.arange(n_tiles, dtype=jnp.int32)
    last_tile = jnp.minimum(tile_ids, tile_end[-1] - 1)
    tile_expert = jnp.sum((last_tile[:, None] >= tile_end[None, :]).astype(jnp.int32), axis=1)
    tile_expert = jnp.minimum(tile_expert, N_EXPERTS - 1)
    rows_before = (tile_ids - (tile_end - region // TE)[tile_expert]) * TE
    tile_rows = jnp.where(tile_ids < tile_end[-1], jnp.clip(expert_rows[tile_expert] - rows_before, 0, TE), 0)
    seg_start = seg_start.reshape(-1).astype(jnp.int32)
    seg_rows = seg_rows.reshape(-1).astype(jnp.int32)

    xs = _dispatch(seg_start, seg_rows, mask_t, h, n_tiles * TE)
    ys = _expert_ffn(tile_expert, tile_rows.astype(jnp.int32), xs, w_gate_up, w_down, layer, tf)
    return _combine(seg_start, seg_rows, x, gate, mask, ys, out_gain, out_norm)


def kernel(x, attn_norm, ffn_norm, final_norm, t5_rel_bias, diff_w_qkv, diff_w_o, diff_lambda_q1, diff_lambda_k1, diff_lambda_q2, diff_lambda_k2, diff_subln, chunk_w_qkv, chunk_w_o, chunk_rel_bias, dense_w_gate_up, dense_w_down, moe_router, moe_w_gate_up, moe_w_down):
    b, s, d = x.shape
    n = b * s
    xf = x.reshape(n, d)
    t5_bias, t5_const_rows = _t5_bias_t(t5_rel_bias, s)
    for i in range(DEPTH):
        j = i // 2
        if i % 2 == 0:
            qkv = _norm_matmul(xf, attn_norm[i], diff_w_qkv[j].astype(BF16))
            lam_init = 0.8 - 0.6 * math.exp(-0.3 * i)
            lam_vecs = jnp.stack([diff_lambda_q1[j], diff_lambda_k1[j], diff_lambda_q2[j], diff_lambda_k2[j]])
            a = _diff_attention(qkv.reshape(b, s, 3 * d), t5_bias, t5_const_rows, lam_vecs, diff_subln[j], lam_init)
            xf = _attn_out_dense_ffn(a.reshape(n, d), diff_w_o[j].astype(BF16), xf, ffn_norm[i],
                                     dense_w_gate_up[j].astype(BF16), dense_w_down[j].astype(BF16),
                                     tf=dense_w_down.shape[1] // 2)
        else:
            qkv = _norm_matmul(xf, attn_norm[i], chunk_w_qkv[j].astype(BF16))
            a = _chunk_attention(qkv.reshape(b, s, 3 * d), _band_bias_t(chunk_rel_bias[j]))
            xf = _attn_out_moe_ffn(a.reshape(n, d), chunk_w_o[j].astype(BF16), xf, ffn_norm[i], moe_router[j],
                                   moe_w_gate_up, moe_w_down, layer=j, tf=moe_w_down.shape[2] // 7,
                                   out_gain=final_norm, out_norm=i == DEPTH - 1)
    return xf.reshape(b, s, d)
```

```python
import functools
import math

import numpy as np
import jax
import jax.numpy as jnp
from jax import lax
from jax.experimental import pallas as pl
from jax.experimental.pallas import tpu as pltpu

F32 = jnp.float32
BF16 = jnp.bfloat16

D_MODEL = 1024
DEPTH = 4
CHUNK = 64
DA_HEADS = 8
DA_HEAD_DIM = 64
T5_BUCKETS = 32
T5_MAX_DIST = 128
CA_HEADS = 16
CA_HEAD_DIM = 64
BAND_PREV = 8
REL_CLIP = 4 * CHUNK
N_EXPERTS = 8
RMS_EPS = 1e-6
SUBLN_EPS = 1e-5
NEG_INF = -1e30

LANES = 128
BF16_ROWS = 16
VMEM_LIMIT = 56 * 1024 * 1024
TM = 512
TE = 4 * TM
TQ = 256
TQ_BAND = 256
TT = 256
KB_T5 = 128
KB_BAND = 256
MM_ROWS = 2048
HEAD_BLOCKS = 2
N_SLOTS = 4
ONES_ROWS = BF16_ROWS
LOG2E = math.log2(math.e)
Q_SCALE = DA_HEAD_DIM ** -0.5 * LOG2E
PAIR = 2 * CA_HEAD_DIM


def _round_up(a, m):
    return (a + m - 1) // m * m


def _cparams(*sem):
    return pltpu.CompilerParams(dimension_semantics=sem, vmem_limit_bytes=VMEM_LIMIT)


def _rmsnorm(x32, g, eps):
    return x32 * lax.rsqrt(jnp.mean(x32 * x32, axis=-1, keepdims=True) + eps) * g


def _dot_nt(a, b):
    return lax.dot_general(a, b, (((1,), (1,)), ((), ())), preferred_element_type=F32)


def _norm_matmul_kernel(x_ref, g_ref, w_ref, o_ref):
    d = x_ref.shape[1]
    half = x_ref.shape[0] // 2
    for rows in (slice(0, half), slice(half, 2 * half)):
        h = _rmsnorm(x_ref[rows, :], g_ref[...], RMS_EPS).astype(BF16)
        q = jnp.dot(h, w_ref[:, :d], preferred_element_type=F32) * Q_SCALE
        o_ref[rows, :d] = q.astype(o_ref.dtype)
        o_ref[rows, d:] = jnp.dot(h, w_ref[:, d:], preferred_element_type=F32).astype(o_ref.dtype)


def _norm_matmul(x, g, w):
    n, d = x.shape
    n_out = w.shape[1]
    return pl.pallas_call(
        _norm_matmul_kernel,
        grid=(n // (2 * TM),),
        in_specs=[pl.BlockSpec((2 * TM, d), lambda i: (i, 0)),
                  pl.BlockSpec((1, d), lambda i: (0, 0)),
                  pl.BlockSpec((d, n_out), lambda i: (0, 0), pipeline_mode=pl.Buffered(1))],
        out_specs=pl.BlockSpec((2 * TM, n_out), lambda i: (i, 0)),
        out_shape=jax.ShapeDtypeStruct((n, n_out), BF16),
        compiler_params=_cparams("parallel"),
        name="norm_qkv",
    )(x, g.reshape(1, d), w)


def _t5_bucket_np(rel):
    nb = T5_BUCKETS // 2
    max_exact = nb // 2
    ret = (rel > 0).astype(np.int32) * nb
    n = np.abs(rel)
    n_f = np.maximum(n, 1).astype(np.float32)
    large = max_exact + (np.log(n_f / np.float32(max_exact)) / np.float32(math.log(T5_MAX_DIST / max_exact))
                         * np.float32(nb - max_exact)).astype(np.int32)
    large = np.minimum(large, nb - 1)
    return ret + np.where(n < max_exact, n, large)


def _lookup_kernel(idx_ref, t_ref, o_ref):
    rows = lax.broadcasted_iota(jnp.int32, (t_ref.shape[1], idx_ref.shape[1]), 0)
    onehot = jnp.where(rows == idx_ref[...], 1.0, 0.0)
    o_ref[...] = jnp.dot(t_ref[...], onehot, preferred_element_type=F32, precision=lax.Precision.HIGHEST)


def _table_lookup_t(table, idx):
    r, h = table.shape
    table_t = jnp.zeros((h, _round_up(r, LANES)), F32).at[:, :r].set(table.T)
    return pl.pallas_call(
        _lookup_kernel,
        out_shape=jax.ShapeDtypeStruct((h, len(idx)), F32),
        name="bias_lookup",
    )(jnp.asarray(idx, jnp.int32).reshape(1, -1), table_t)


def _toeplitz_kernel(win_ref, mask_ref, o_ref):
    tq = o_ref.shape[1]
    for blk in range(o_ref.shape[0] // tq):
        x = jnp.broadcast_to(win_ref[:, blk * 2 * tq:(blk + 1) * 2 * tq], (tq, 2 * tq))
        skew = pltpu.roll(x, 0, 1, stride=1, stride_axis=0)
        o_ref[blk * tq:(blk + 1) * tq, :] = skew[:, :tq] + mask_ref[blk * tq:(blk + 1) * tq, :]


def _window_diagonals(n_keys, tq):
    m = np.arange(2 * tq)
    back = np.where(m < tq, -m, 2 * tq - m)
    d = np.arange(0, n_keys, tq)[:, None] + back[None, :]
    return np.clip(d, 1 - tq, n_keys - 1).reshape(-1)


def _toeplitz_bias_t(table, idx, allowed):
    win = _table_lookup_t(table, idx) * LOG2E
    h = win.shape[0]
    n_keys, tq = allowed.shape
    mask_add = jnp.asarray(np.where(allowed, 0.0, NEG_INF).astype(np.float32))
    return pl.pallas_call(
        _toeplitz_kernel,
        grid=(h,),
        in_specs=[pl.BlockSpec((None, 1, win.shape[1]), lambda i: (i, 0, 0)),
                  pl.BlockSpec((n_keys, tq), lambda i: (0, 0))],
        out_specs=pl.BlockSpec((None, n_keys, tq), lambda i: (i, 0, 0)),
        out_shape=jax.ShapeDtypeStruct((h, n_keys, tq), F32),
        compiler_params=_cparams("parallel"),
        name="toeplitz_bias",
    )(win.reshape(h, 1, -1), mask_add)


def _t5_bias_t(table, seq):
    idx = _t5_bucket_np(_window_diagonals(seq, TQ) - (seq - TQ))
    i = np.arange(TQ)[None, :]
    j = np.arange(seq)[:, None] - (seq - TQ)
    allowed = (j < 0) | ((j // CHUNK) <= (i // CHUNK))
    far_bucket = _t5_bucket_np(np.asarray(-seq))
    is_far = np.all((_t5_bucket_np(j - i) == far_bucket) & allowed, axis=1)
    const_rows = int(np.argmin(is_far)) if not is_far.all() else seq
    return _toeplitz_bias_t(table, idx, allowed), const_rows


def _load_vt(v, vt_ref):
    n_v = v.shape[1]
    vt_ref[0:n_v, :] = v.T
    vt_ref[n_v:, :] = jnp.ones((ONES_ROWS, v.shape[0]), BF16)


def _attend_pair_phases(k_ref, k_cols, k0, n_k, kb_rows, q, lane_split, bias_rows, bias_const, vt_ref, s_ref, p_ref,
                        finish):
    tq = q.shape[0]
    halves = (slice(0, tq), slice(tq, 2 * tq))
    n_v = vt_ref.shape[0] - ONES_ROWS
    state = {"m": [None, None], "o": None}

    def score(kr):
        def run():
            if kr == 0:
                zero = jnp.zeros_like(q)
                state["qq"] = jnp.concatenate([jnp.where(lane_split, q, zero), jnp.where(lane_split, zero, q)], axis=0)
            rows = min(MM_ROWS, n_k - kr)
            s_ref[kr:kr + rows, :] = _dot_nt(k_ref[k0 + kr:k0 + kr + rows, k_cols], state["qq"])
        return run

    def running_max(kb, h):
        def run():
            cols = halves[h]
            if bias_const(kb):
                block_max = jnp.max(s_ref[kb:kb + kb_rows, cols], axis=0, keepdims=True) + bias_rows(h, kb)
            else:
                s = s_ref[kb:kb + kb_rows, cols] + bias_rows(h, kb)
                s_ref[kb:kb + kb_rows, cols] = s
                block_max = jnp.max(s, axis=0, keepdims=True)
            m = state["m"]
            m[h] = block_max if m[h] is None else jnp.maximum(m[h], block_max)
        return run

    def exponentials(kb, h):
        def run():
            cols = halves[h]
            m = state["m"][h]
            shift = m - bias_rows(h, kb) if bias_const(kb) else m
            p_ref[kb:kb + kb_rows, cols] = jnp.exp2(s_ref[kb:kb + kb_rows, cols] - shift).astype(BF16)
        return run

    def pv(kr):
        def run():
            rows = min(MM_ROWS, n_k - kr)
            part = jnp.dot(vt_ref[:, k0 + kr:k0 + kr + rows], p_ref[kr:kr + rows, :], preferred_element_type=F32)
            state["o"] = part if state["o"] is None else state["o"] + part
            if kr + rows == n_k:
                o_t = state["o"]
                finish([o_t[0:n_v, cols] * (1.0 / o_t[n_v:n_v + 1, cols]) for cols in halves])
        return run

    blocks = [(kb, h) for kb in range(0, n_k, kb_rows) for h in range(2)]
    return ([score(kr) for kr in range(0, n_k, MM_ROWS)], [running_max(kb, h) for kb, h in blocks],
            [exponentials(kb, h) for kb, h in blocks], [pv(kr) for kr in range(0, n_k, MM_ROWS)])


def _interleave(tiles):
    def merged(a, b):
        order = sorted([((i + 0.5) / len(a), 0, f) for i, f in enumerate(a)]
                       + [((i + 0.5) / len(b), 1, f) for i, f in enumerate(b)], key=lambda t: t[:2])
        for _, _, thunk in order:
            thunk()

    empty = ([], [], [], [])
    prev = empty
    for cur in list(tiles) + [empty]:
        merged(cur[0], prev[2])
        merged(cur[1], prev[3])
        prev = cur


def _diff_attn_kernel(q_ref, k_ref, v_ref, bias_ref, lam_ref, g_ref, o_ref, vt_ref, s_ref, p_ref, *,
                      lam_init, const_rows):
    lv = lam_ref[...]
    lam = (jnp.exp(jnp.sum(lv[0:1] * lv[1:2], axis=-1, keepdims=True))
           - jnp.exp(jnp.sum(lv[2:3] * lv[3:4], axis=-1, keepdims=True)) + lam_init)
    first_map = lax.broadcasted_iota(jnp.int32, (1, 2 * DA_HEAD_DIM), 1) < DA_HEAD_DIM
    seq = q_ref.shape[0]
    hd = 2 * DA_HEAD_DIM
    n_q = seq // TQ

    def tile(blk, qi):
        cols = slice(blk * hd, (blk + 1) * hd)
        n_k = (qi + 1) * TQ
        row0 = seq - n_k

        def bias_const(kb):
            return row0 + kb + KB_T5 <= const_rows

        def bias_rows(h, kb):
            n = 1 if bias_const(kb) else KB_T5
            return bias_ref[blk, row0 + kb:row0 + kb + n, :]

        def finish(outs):
            o_t = outs[0] - lam * outs[1]
            o_t = o_t * lax.rsqrt(jnp.mean(o_t * o_t, axis=0, keepdims=True) + SUBLN_EPS) * g_ref[...]
            o_ref[qi * TQ:(qi + 1) * TQ, cols] = (o_t * (1.0 - lam_init)).T.astype(o_ref.dtype)

        slot = (blk * n_q + qi) % N_SLOTS
        return _attend_pair_phases(k_ref, cols, 0, n_k, KB_T5, q_ref[qi * TQ:(qi + 1) * TQ, cols], first_map, bias_rows,
                                   bias_const, vt_ref.at[blk], s_ref.at[slot], p_ref.at[slot], finish)

    for blk in range(HEAD_BLOCKS):
        _load_vt(v_ref[:, blk * hd:(blk + 1) * hd], vt_ref.at[blk])
    _interleave(tile(blk, qi) for blk in range(HEAD_BLOCKS) for qi in range(n_q))


def _diff_attention(qkv, bias_t, const_rows, lam_vecs, subln_g, lam_init):
    b, s, _ = qkv.shape
    hd = 2 * DA_HEAD_DIM
    width = HEAD_BLOCKS * hd
    n_steps = DA_HEADS // HEAD_BLOCKS
    return pl.pallas_call(
        functools.partial(_diff_attn_kernel, lam_init=lam_init, const_rows=const_rows),
        grid=(n_steps, b),
        in_specs=[pl.BlockSpec((None, s, width), lambda h, i: (i, 0, h)),
                  pl.BlockSpec((None, s, width), lambda h, i: (i, 0, n_steps + h)),
                  pl.BlockSpec((None, s, width), lambda h, i: (i, 0, 2 * n_steps + h)),
                  pl.BlockSpec((HEAD_BLOCKS, s, TQ), lambda h, i: (h, 0, 0)),
                  pl.BlockSpec((4, DA_HEAD_DIM), lambda h, i: (0, 0)),
                  pl.BlockSpec((hd, 1), lambda h, i: (0, 0))],
        out_specs=pl.BlockSpec((None, s, width), lambda h, i: (i, 0, h)),
        out_shape=jax.ShapeDtypeStruct((b, s, D_MODEL), BF16),
        scratch_shapes=[pltpu.VMEM((HEAD_BLOCKS, hd + ONES_ROWS, s), BF16), pltpu.VMEM((N_SLOTS, s, 2 * TQ), F32),
                        pltpu.VMEM((N_SLOTS, s, 2 * TQ), BF16)],
        compiler_params=_cparams("parallel", "parallel"),
        name="diff_attn",
    )(qkv, qkv, qkv, bias_t, lam_vecs, subln_g.reshape(hd, 1))


BAND_KEYS = BAND_PREV * CHUNK + TQ_BAND


def _band_bias_t(rel_table):
    dist = BAND_PREV * CHUNK - _window_diagonals(BAND_KEYS, TQ_BAND)
    q_chunk = np.arange(TQ_BAND)[None, :] // CHUNK
    k_chunk = np.arange(BAND_KEYS)[:, None] // CHUNK
    allowed = (k_chunk >= q_chunk) & (k_chunk <= q_chunk + BAND_PREV)
    return _toeplitz_bias_t(rel_table, np.clip(dist, -REL_CLIP, REL_CLIP) + REL_CLIP, allowed)


def _chunk_attn_kernel(q_ref, k_ref, v_ref, bias_ref, o_ref, vt_ref, s_ref, p_ref):
    first_head = lax.broadcasted_iota(jnp.int32, (1, PAIR), 1) < CA_HEAD_DIM
    first_head_rows = lax.broadcasted_iota(jnp.int32, (PAIR, 1), 0) < CA_HEAD_DIM
    seq = q_ref.shape[0]
    n_q = seq // TQ_BAND

    def tile(blk, qi):
        cols = slice(blk * PAIR, (blk + 1) * PAIR)
        q0 = qi * TQ_BAND
        k0 = max(0, q0 - BAND_PREV * CHUNK)
        n_k = q0 + TQ_BAND - k0
        row0 = BAND_KEYS - n_k

        def bias_rows(h, kb):
            return bias_ref[2 * blk + h, row0 + kb:row0 + kb + KB_BAND, :]

        def finish(outs):
            o_ref[q0:q0 + TQ_BAND, cols] = jnp.where(first_head_rows, outs[0], outs[1]).T.astype(o_ref.dtype)

        slot = (blk * n_q + qi) % N_SLOTS
        return _attend_pair_phases(k_ref, cols, k0, n_k, KB_BAND, q_ref[q0:q0 + TQ_BAND, cols], first_head, bias_rows,
                                   lambda kb: False, vt_ref.at[blk], s_ref.at[slot], p_ref.at[slot], finish)

    for blk in range(HEAD_BLOCKS):
        _load_vt(v_ref[:, blk * PAIR:(blk + 1) * PAIR], vt_ref.at[blk])
    _interleave(tile(blk, qi) for blk in range(HEAD_BLOCKS) for qi in range(n_q))


def _chunk_attention(qkv, band_bias_t):
    b, s, _ = qkv.shape
    width = HEAD_BLOCKS * PAIR
    n_steps = CA_HEADS // 2 // HEAD_BLOCKS
    return pl.pallas_call(
        _chunk_attn_kernel,
        grid=(n_steps, b),
        in_specs=[pl.BlockSpec((None, s, width), lambda p, i: (i, 0, p)),
                  pl.BlockSpec((None, s, width), lambda p, i: (i, 0, n_steps + p)),
                  pl.BlockSpec((None, s, width), lambda p, i: (i, 0, 2 * n_steps + p)),
                  pl.BlockSpec((2 * HEAD_BLOCKS, BAND_KEYS, TQ_BAND), lambda p, i: (p, 0, 0))],
        out_specs=pl.BlockSpec((None, s, width), lambda p, i: (i, 0, p)),
        out_shape=jax.ShapeDtypeStruct((b, s, D_MODEL), BF16),
        scratch_shapes=[pltpu.VMEM((HEAD_BLOCKS, PAIR + ONES_ROWS, s), BF16), pltpu.VMEM((N_SLOTS, BAND_KEYS, 2 * TQ_BAND), F32),
                        pltpu.VMEM((N_SLOTS, BAND_KEYS, 2 * TQ_BAND), BF16)],
        compiler_params=_cparams("parallel", "parallel"),
        name="chunk_attn",
    )(qkv, qkv, qkv, band_bias_t)


def _swiglu_partial(h, wg, wu, wd):
    g = jnp.dot(h, wg, preferred_element_type=F32)
    u = jnp.dot(h, wu, preferred_element_type=F32)
    a = (g * (0.5 * jnp.tanh(0.5 * g) + 0.5) * u).astype(BF16)
    return jnp.dot(a, wd, preferred_element_type=F32)


def _dense_ffn_kernel(a_ref, wo_ref, x_ref, g_ref, wgu_ref, wd_ref, o_ref, *, tf):
    x1 = x_ref[...] + jnp.dot(a_ref[...], wo_ref[...], preferred_element_type=F32)
    h = _rmsnorm(x1, g_ref[...], RMS_EPS).astype(BF16)
    ff = wd_ref.shape[0]
    y = x1
    for lo in range(0, ff, tf):
        y = y + _swiglu_partial(h, wgu_ref[:, lo:lo + tf], wgu_ref[:, ff + lo:ff + lo + tf], wd_ref[lo:lo + tf, :])
    o_ref[...] = y


def _attn_out_dense_ffn(a, w_o, x, g, w_gate_up, w_down, tf):
    n, d = x.shape

    def resident(w):
        return pl.BlockSpec(w.shape, lambda i: (0, 0), pipeline_mode=pl.Buffered(1))

    return pl.pallas_call(
        functools.partial(_dense_ffn_kernel, tf=tf),
        grid=(n // TM,),
        in_specs=[pl.BlockSpec((TM, d), lambda i: (i, 0)),
                  resident(w_o),
                  pl.BlockSpec((TM, d), lambda i: (i, 0)),
                  pl.BlockSpec((1, d), lambda i: (0, 0)),
                  resident(w_gate_up),
                  resident(w_down)],
        out_specs=pl.BlockSpec((TM, d), lambda i: (i, 0)),
        out_shape=jax.ShapeDtypeStruct((n, d), F32),
        compiler_params=_cparams("parallel"),
        name="dense_ffn",
    )(a, w_o, x, g.reshape(1, d), w_gate_up, w_down)


def _router_kernel(a_ref, wo_ref, x_ref, g_ref, r_ref, x1_ref, h_ref, gate_ref, mask_ref, mask_t_ref, cnt_ref):
    x1 = x_ref[...] + jnp.dot(a_ref[...], wo_ref[...], preferred_element_type=F32)
    x1_ref[...] = x1
    h = _rmsnorm(x1, g_ref[...], RMS_EPS)
    h_hi = h.astype(BF16)
    h_ref[...] = h_hi
    r = r_ref[...]
    r_hi = r.astype(BF16)
    r_lo = (r - r_hi.astype(F32)).astype(BF16)
    h_lo = (h - h_hi.astype(F32)).astype(BF16)
    logits = (jnp.dot(h_hi, r_hi, preferred_element_type=F32)
              + (jnp.dot(h_hi, r_lo, preferred_element_type=F32) + jnp.dot(h_lo, r_hi, preferred_element_type=F32)))
    lane = lax.broadcasted_iota(jnp.int32, logits.shape, 1)
    logits = jnp.where(lane < N_EXPERTS, logits, -jnp.inf)
    v1 = jnp.max(logits, axis=-1, keepdims=True)
    i1 = jnp.min(jnp.where(logits == v1, lane, LANES), axis=-1, keepdims=True)
    m1 = lane == i1
    rest = jnp.where(m1, -jnp.inf, logits)
    v2 = jnp.max(rest, axis=-1, keepdims=True)
    i2 = jnp.min(jnp.where(rest == v2, lane, LANES), axis=-1, keepdims=True)
    m2 = lane == i2
    e2 = jnp.exp(v2 - v1)
    g1 = 1.0 / (1.0 + e2)
    g2 = e2 / (1.0 + e2)
    gate_ref[...] = jnp.where(m1, g1, 0.0) + jnp.where(m2, g2, 0.0)
    mask = jnp.where(m1 | m2, 1.0, 0.0)
    mask_ref[...] = mask
    mask_t_ref[...] = mask.T[0:N_EXPERTS, :]
    cnt_ref[...] = jnp.broadcast_to(jnp.sum(mask, axis=0, keepdims=True), cnt_ref.shape)


def _attn_out_router(a, w_o, x, g, router):
    n, d = x.shape
    t = n // TT
    r_pad = jnp.zeros((d, LANES), F32).at[:, :N_EXPERTS].set(router)
    return pl.pallas_call(
        _router_kernel,
        grid=(t,),
        in_specs=[pl.BlockSpec((TT, d), lambda i: (i, 0)),
                  pl.BlockSpec((d, d), lambda i: (0, 0)),
                  pl.BlockSpec((TT, d), lambda i: (i, 0)),
                  pl.BlockSpec((1, d), lambda i: (0, 0)),
                  pl.BlockSpec((d, LANES), lambda i: (0, 0))],
        out_specs=[pl.BlockSpec((TT, d), lambda i: (i, 0)),
                   pl.BlockSpec((TT, d), lambda i: (i, 0)),
                   pl.BlockSpec((TT, LANES), lambda i: (i, 0)),
                   pl.BlockSpec((TT, LANES), lambda i: (i, 0)),
                   pl.BlockSpec((N_EXPERTS, TT), lambda i: (0, i)),
                   pl.BlockSpec((None, 8, LANES), lambda i: (i, 0, 0))],
        out_shape=[jax.ShapeDtypeStruct((n, d), F32),
                   jax.ShapeDtypeStruct((n, d), BF16),
                   jax.ShapeDtypeStruct((n, LANES), F32),
                   jax.ShapeDtypeStruct((n, LANES), F32),
                   jax.ShapeDtypeStruct((N_EXPERTS, n), F32),
                   jax.ShapeDtypeStruct((t, 8, LANES), F32)],
        compiler_params=_cparams("parallel"),
        name="router",
    )(a, w_o, x, g.reshape(1, d), r_pad)


def _segment_copies(n_rows, make_copy):
    off = jnp.int32(0)
    size = TT
    while size >= BF16_ROWS:
        take = (n_rows & size) != 0

        @pl.when(take)
        def _(off=off, size=size):
            make_copy(off, size)

        off = off + jnp.where(take, size, 0)
        size //= 2


def _dispatch_kernel(start_ref, rows_ref, mask_t_ref, h_ref, xs_in_ref, xs_ref, blk_ref, sem_ref):
    del xs_in_ref
    t = pl.program_id(0)
    slot = t % 2
    mask_t = mask_t_ref[...]
    row = lax.broadcasted_iota(jnp.int32, (TT, TT), 0)
    col = lax.broadcasted_iota(jnp.int32, (TT, TT), 1)
    before = jnp.where(row < col, 1.0, 0.0).astype(BF16)
    rank = jnp.dot(mask_t.astype(BF16), before, preferred_element_type=F32)
    row_f = row.astype(F32)
    sels = []
    for e in range(N_EXPERTS):
        pick = (row_f == rank[e:e + 1, :]) & (mask_t[e:e + 1, :] > 0.0)
        sels.append(jnp.where(pick, 1.0, 0.0).astype(BF16))
    packed = jnp.dot(jnp.concatenate(sels, axis=0), h_ref[...], preferred_element_type=F32)
    blk_ref[slot] = packed.astype(BF16).reshape(blk_ref.shape[1:])

    def copies(step, step_slot, wait):
        for e in range(N_EXPERTS):
            start = pl.multiple_of(start_ref[step * N_EXPERTS + e], BF16_ROWS)

            def make_copy(off, size, e=e, start=start):
                src = blk_ref.at[step_slot, e, pl.ds(pl.multiple_of(off, BF16_ROWS), size)]
                dst = xs_ref.at[pl.ds(pl.multiple_of(start + off, BF16_ROWS), size)]
                cp = pltpu.make_async_copy(src, dst, sem_ref.at[step_slot])
                cp.wait() if wait else cp.start()

            _segment_copies(rows_ref[step * N_EXPERTS + e], make_copy)

    copies(t, slot, wait=False)

    @pl.when(t > 0)
    def _():
        copies(t - 1, 1 - slot, wait=True)

    @pl.when(t == pl.num_programs(0) - 1)
    def _():
        copies(t, slot, wait=True)


def _dispatch(seg_start, seg_rows, mask_t, h, n_rows):
    n, d = h.shape
    t = n // TT
    xs0 = jnp.zeros((n_rows, d), BF16)
    grid_spec = pltpu.PrefetchScalarGridSpec(
        num_scalar_prefetch=2,
        grid=(t,),
        in_specs=[pl.BlockSpec((N_EXPERTS, TT), lambda i, *_: (0, i)),
                  pl.BlockSpec((TT, d), lambda i, *_: (i, 0)),
                  pl.BlockSpec(memory_space=pl.ANY)],
        out_specs=pl.BlockSpec(memory_space=pl.ANY),
        scratch_shapes=[pltpu.VMEM((2, N_EXPERTS, TT, d), BF16), pltpu.SemaphoreType.DMA((2,))],
    )
    return pl.pallas_call(
        _dispatch_kernel,
        grid_spec=grid_spec,
        out_shape=jax.ShapeDtypeStruct((n_rows, d), BF16),
        input_output_aliases={4: 0},
        compiler_params=_cparams("arbitrary"),
        name="moe_dispatch",
    )(seg_start, seg_rows, mask_t, h, xs0)


def _expert_kernel(tile_expert_ref, tile_rows_ref, xs_ref, wg_ref, wu_ref, wd_ref, o_ref,
                   wg_bf_ref, wu_bf_ref, wd_bf_ref, acc_ref):
    del tile_expert_ref
    j = pl.program_id(0)
    f = pl.program_id(1)
    last = pl.num_programs(1) - 1
    rows = tile_rows_ref[j]

    @pl.when(f == 0)
    def _():
        acc_ref[...] = jnp.zeros_like(acc_ref)

    @pl.when(rows == TE)
    def _():
        acc_ref[...] += _swiglu_partial(xs_ref[...], wg_ref[...].astype(BF16), wu_ref[...].astype(BF16),
                                        wd_ref[...].astype(BF16))

    @pl.when((rows > 0) & (rows < TE))
    def _():
        wg_bf_ref[...] = wg_ref[...].astype(BF16)
        wu_bf_ref[...] = wu_ref[...].astype(BF16)
        wd_bf_ref[...] = wd_ref[...].astype(BF16)
        for lo in range(0, TE, TM):
            @pl.when(rows > lo)
            def _(lo=lo):
                acc_ref[lo:lo + TM, :] += _swiglu_partial(xs_ref[lo:lo + TM, :], wg_bf_ref[...], wu_bf_ref[...],
                                                          wd_bf_ref[...])

    @pl.when(f == last)
    def _():
        o_ref[...] = acc_ref[...].astype(o_ref.dtype)


def _expert_ffn(tile_expert, tile_rows, xs, w_gate_up, w_down, layer, tf):
    n_rows, d = xs.shape
    ff = w_down.shape[2]
    nf = ff // tf

    def f_eff(j, f, tr):
        return jnp.where(tr[j] > 0, f, nf - 1)

    grid_spec = pltpu.PrefetchScalarGridSpec(
        num_scalar_prefetch=2,
        grid=(n_rows // TE, nf),
        in_specs=[pl.BlockSpec((TE, d), lambda j, f, te, tr: (j, 0)),
                  pl.BlockSpec((None, None, d, tf), lambda j, f, te, tr: (layer, te[j], 0, f_eff(j, f, tr))),
                  pl.BlockSpec((None, None, d, tf), lambda j, f, te, tr: (layer, te[j], 0, nf + f_eff(j, f, tr))),
                  pl.BlockSpec((None, None, tf, d), lambda j, f, te, tr: (layer, te[j], f_eff(j, f, tr), 0))],
        out_specs=pl.BlockSpec((TE, d), lambda j, f, te, tr: (j, 0)),
        scratch_shapes=[pltpu.VMEM((d, tf), BF16), pltpu.VMEM((d, tf), BF16), pltpu.VMEM((tf, d), BF16),
                        pltpu.VMEM((TE, d), F32)],
    )
    return pl.pallas_call(
        _expert_kernel,
        grid_spec=grid_spec,
        out_shape=jax.ShapeDtypeStruct((n_rows, d), BF16),
        compiler_params=_cparams("parallel", "arbitrary"),
        name="moe_experts",
    )(tile_expert, tile_rows, xs, w_gate_up, w_gate_up, w_down)


def _combine_kernel(start_ref, rows_ref, x_ref, gate_ref, mask_ref, out_gain_ref, ys_ref, o_ref, win_ref, sem_ref,
                    *, out_norm):
    t = pl.program_id(0)
    n_t = pl.num_programs(0)
    slot = t % 2

    def copies(step, step_slot, wait):
        for e in range(N_EXPERTS):
            start = pl.multiple_of(start_ref[step * N_EXPERTS + e], BF16_ROWS)

            def make_copy(off, size, e=e, start=start):
                src = ys_ref.at[pl.ds(pl.multiple_of(start + off, BF16_ROWS), size)]
                dst = win_ref.at[step_slot, e, pl.ds(pl.multiple_of(off, BF16_ROWS), size)]
                cp = pltpu.make_async_copy(src, dst, sem_ref.at[step_slot])
                cp.wait() if wait else cp.start()

            _segment_copies(rows_ref[step * N_EXPERTS + e], make_copy)

    @pl.when(t == 0)
    def _():
        win_ref[...] = jnp.zeros_like(win_ref)
        copies(0, 0, wait=False)

    @pl.when(t + 1 < n_t)
    def _():
        copies(t + 1, 1 - slot, wait=False)

    copies(t, slot, wait=True)

    mask = mask_ref[...]
    gate = gate_ref[...]
    row = lax.broadcasted_iota(jnp.int32, (TT, TT), 0)
    col = lax.broadcasted_iota(jnp.int32, (TT, TT), 1)
    before = jnp.where(col < row, 1.0, 0.0).astype(BF16)
    rank = jnp.dot(before, mask.astype(BF16), preferred_element_type=F32)
    col_f = col.astype(F32)
    y = x_ref[...]
    for e in range(N_EXPERTS):
        pick = (col_f == rank[:, e:e + 1]) & (mask[:, e:e + 1] > 0.0)
        sel = jnp.where(pick, 1.0, 0.0).astype(BF16)
        y = y + gate[:, e:e + 1] * jnp.dot(sel, win_ref[slot, e], preferred_element_type=F32)
    o_ref[...] = _rmsnorm(y, out_gain_ref[...], RMS_EPS) if out_norm else y


def _combine(seg_start, seg_rows, x, gate, mask, ys, out_gain, out_norm):
    n, d = x.shape
    t = n // TT
    grid_spec = pltpu.PrefetchScalarGridSpec(
        num_scalar_prefetch=2,
        grid=(t,),
        in_specs=[pl.BlockSpec((TT, d), lambda i, *_: (i, 0)),
                  pl.BlockSpec((TT, LANES), lambda i, *_: (i, 0)),
                  pl.BlockSpec((TT, LANES), lambda i, *_: (i, 0)),
                  pl.BlockSpec((1, d), lambda i, *_: (0, 0)),
                  pl.BlockSpec(memory_space=pl.ANY)],
        out_specs=pl.BlockSpec((TT, d), lambda i, *_: (i, 0)),
        scratch_shapes=[pltpu.VMEM((2, N_EXPERTS, TT, d), BF16), pltpu.SemaphoreType.DMA((2,))],
    )
    return pl.pallas_call(
        functools.partial(_combine_kernel, out_norm=out_norm),
        grid_spec=grid_spec,
        out_shape=jax.ShapeDtypeStruct((n, d), F32),
        compiler_params=_cparams("arbitrary"),
        name="moe_combine",
    )(seg_start, seg_rows, x, gate, mask, out_gain.reshape(1, d), ys)


def _attn_out_moe_ffn(a, w_o, x, g, router, w_gate_up, w_down, layer, tf, out_gain, out_norm):
    n, d = x.shape
    t = n // TT
    x, h, gate, mask, mask_t, cnt = _attn_out_router(a, w_o, x, g, router)
    cnt = cnt[:, 0, :N_EXPERTS].astype(jnp.int32)
    seg_rows = _round_up(cnt, BF16_ROWS)
    expert_rows = jnp.sum(seg_rows, axis=0)
    region = _round_up(expert_rows, TE)
    region_end = jnp.cumsum(region)
    seg_start = (region_end - region)[None, :] + jnp.cumsum(seg_rows, axis=0) - seg_rows
    n_tiles = (_round_up(2 * n + t * N_EXPERTS * (BF16_ROWS - 1), TE) + N_EXPERTS * TE) // TE
    tile_end = region_end // TE
    tile_ids = jnp.arange(n_tiles, dtype=jnp.int32)
    last_tile = jnp.minimum(tile_ids, tile_end[-1] - 1)
    tile_expert = jnp.sum((last_tile[:, None] >= tile_end[None, :]).astype(jnp.int32), axis=1)
    tile_expert = jnp.minimum(tile_expert, N_EXPERTS - 1)
    rows_before = (tile_ids - (tile_end - region // TE)[tile_expert]) * TE
    tile_rows = jnp.where(tile_ids < tile_end[-1], jnp.clip(expert_rows[tile_expert] - rows_before, 0, TE), 0)
    seg_start = seg_start.reshape(-1).astype(jnp.int32)
    seg_rows = seg_rows.reshape(-1).astype(jnp.int32)

    xs = _dispatch(seg_start, seg_rows, mask_t, h, n_tiles * TE)
    ys = _expert_ffn(tile_expert, tile_rows.astype(jnp.int32), xs, w_gate_up, w_down, layer, tf)
    return _combine(seg_start, seg_rows, x, gate, mask, ys, out_gain, out_norm)


def kernel(x, attn_norm, ffn_norm, final_norm, t5_rel_bias, diff_w_qkv, diff_w_o, diff_lambda_q1, diff_lambda_k1, diff_lambda_q2, diff_lambda_k2, diff_subln, chunk_w_qkv, chunk_w_o, chunk_rel_bias, dense_w_gate_up, dense_w_down, moe_router, moe_w_gate_up, moe_w_down):
    b, s, d = x.shape
    n = b * s
    xf = x.reshape(n, d)
    t5_bias, t5_const_rows = _t5_bias_t(t5_rel_bias, s)
    for i in range(DEPTH):
        j = i // 2
        if i % 2 == 0:
            qkv = _norm_matmul(xf, attn_norm[i], diff_w_qkv[j].astype(BF16))
            lam_init = 0.8 - 0.6 * math.exp(-0.3 * i)
            lam_vecs = jnp.stack([diff_lambda_q1[j], diff_lambda_k1[j], diff_lambda_q2[j], diff_lambda_k2[j]])
            a = _diff_attention(qkv.reshape(b, s, 3 * d), t5_bias, t5_const_rows, lam_vecs, diff_subln[j], lam_init)
            xf = _attn_out_dense_ffn(a.reshape(n, d), diff_w_o[j].astype(BF16), xf, ffn_norm[i],
                                     dense_w_gate_up[j].astype(BF16), dense_w_down[j].astype(BF16),
                                     tf=dense_w_down.shape[1] // 2)
        else:
            qkv = _norm_matmul(xf, attn_norm[i], chunk_w_qkv[j].astype(BF16))
            a = _chunk_attention(qkv.reshape(b, s, 3 * d), _band_bias_t(chunk_rel_bias[j]))
            xf = _attn_out_moe_ffn(a.reshape(n, d), chunk_w_o[j].astype(BF16), xf, ffn_norm[i], moe_router[j],
                                   moe_w_gate_up, moe_w_down, layer=j, tf=moe_w_down.shape[2] // 7,
                                   out_gain=final_norm, out_norm=i == DEPTH - 1)
    return xf.reshape(b, s, d)
```

```python
import functools
import math

import numpy as np
import jax
import jax.numpy as jnp
from jax import lax
from jax.experimental import pallas as pl
from jax.experimental.pallas import tpu as pltpu

F32 = jnp.float32
BF16 = jnp.bfloat16

D_MODEL = 1024
DEPTH = 4
CHUNK = 64
DA_HEADS = 8
DA_HEAD_DIM = 64
T5_BUCKETS = 32
T5_MAX_DIST = 128
CA_HEADS = 16
CA_HEAD_DIM = 64
BAND_PREV = 8
REL_CLIP = 4 * CHUNK
N_EXPERTS = 8
RMS_EPS = 1e-6
SUBLN_EPS = 1e-5
NEG_INF = -1e30

LANES = 128
BF16_ROWS = 16
VMEM_LIMIT = 56 * 1024 * 1024
TM = 512
TE = 4 * TM
TQ = 256
TQ_BAND = 256
TT = 256
KB_T5 = 128
KB_BAND = 256
MM_ROWS = 2048
HEAD_BLOCKS = 2
PAIR_BLOCKS = 4
N_SLOTS = 4
ONES_ROWS = BF16_ROWS
LOG2E = math.log2(math.e)
Q_SCALE = DA_HEAD_DIM ** -0.5 * LOG2E
PAIR = 2 * CA_HEAD_DIM


def _round_up(a, m):
    return (a + m - 1) // m * m


def _cparams(*sem):
    return pltpu.CompilerParams(dimension_semantics=sem, vmem_limit_bytes=VMEM_LIMIT)


def _rmsnorm(x32, g, eps):
    return x32 * lax.rsqrt(jnp.mean(x32 * x32, axis=-1, keepdims=True) + eps) * g


def _dot_nt(a, b):
    return lax.dot_general(a, b, (((1,), (1,)), ((), ())), preferred_element_type=F32)


def _norm_matmul_kernel(x_ref, g_ref, w_ref, o_ref):
    d = x_ref.shape[1]
    half = x_ref.shape[0] // 2
    for rows in (slice(0, half), slice(half, 2 * half)):
        h = _rmsnorm(x_ref[rows, :], g_ref[...], RMS_EPS).astype(BF16)
        q = jnp.dot(h, w_ref[:, :d], preferred_element_type=F32) * Q_SCALE
        o_ref[rows, :d] = q.astype(o_ref.dtype)
        o_ref[rows, d:] = jnp.dot(h, w_ref[:, d:], preferred_element_type=F32).astype(o_ref.dtype)


def _norm_matmul(x, g, w):
    n, d = x.shape
    n_out = w.shape[1]
    return pl.pallas_call(
        _norm_matmul_kernel,
        grid=(n // (2 * TM),),
        in_specs=[pl.BlockSpec((2 * TM, d), lambda i: (i, 0)),
                  pl.BlockSpec((1, d), lambda i: (0, 0)),
                  pl.BlockSpec((d, n_out), lambda i: (0, 0), pipeline_mode=pl.Buffered(1))],
        out_specs=pl.BlockSpec((2 * TM, n_out), lambda i: (i, 0)),
        out_shape=jax.ShapeDtypeStruct((n, n_out), BF16),
        compiler_params=_cparams("parallel"),
        name="norm_qkv",
    )(x, g.reshape(1, d), w)


def _t5_bucket_np(rel):
    nb = T5_BUCKETS // 2
    max_exact = nb // 2
    ret = (rel > 0).astype(np.int32) * nb
    n = np.abs(rel)
    n_f = np.maximum(n, 1).astype(np.float32)
    large = max_exact + (np.log(n_f / np.float32(max_exact)) / np.float32(math.log(T5_MAX_DIST / max_exact))
                         * np.float32(nb - max_exact)).astype(np.int32)
    large = np.minimum(large, nb - 1)
    return ret + np.where(n < max_exact, n, large)


def _lookup_kernel(idx_ref, t_ref, o_ref):
    rows = lax.broadcasted_iota(jnp.int32, (t_ref.shape[1], idx_ref.shape[1]), 0)
    onehot = jnp.where(rows == idx_ref[...], 1.0, 0.0)
    o_ref[...] = jnp.dot(t_ref[...], onehot, preferred_element_type=F32, precision=lax.Precision.HIGHEST)


def _table_lookup_t(table, idx):
    r, h = table.shape
    table_t = jnp.zeros((h, _round_up(r, LANES)), F32).at[:, :r].set(table.T)
    return pl.pallas_call(
        _lookup_kernel,
        out_shape=jax.ShapeDtypeStruct((h, len(idx)), F32),
        name="bias_lookup",
    )(jnp.asarray(idx, jnp.int32).reshape(1, -1), table_t)


def _toeplitz_kernel(win_ref, mask_ref, o_ref):
    tq = o_ref.shape[1]
    for blk in range(o_ref.shape[0] // tq):
        x = jnp.broadcast_to(win_ref[:, blk * 2 * tq:(blk + 1) * 2 * tq], (tq, 2 * tq))
        skew = pltpu.roll(x, 0, 1, stride=1, stride_axis=0)
        o_ref[blk * tq:(blk + 1) * tq, :] = skew[:, :tq] + mask_ref[blk * tq:(blk + 1) * tq, :]


def _window_diagonals(n_keys, tq):
    m = np.arange(2 * tq)
    back = np.where(m < tq, -m, 2 * tq - m)
    d = np.arange(0, n_keys, tq)[:, None] + back[None, :]
    return np.clip(d, 1 - tq, n_keys - 1).reshape(-1)


def _toeplitz_bias_t(table, idx, allowed):
    win = _table_lookup_t(table, idx) * LOG2E
    h = win.shape[0]
    n_keys, tq = allowed.shape
    mask_add = jnp.asarray(np.where(allowed, 0.0, NEG_INF).astype(np.float32))
    return pl.pallas_call(
        _toeplitz_kernel,
        grid=(h,),
        in_specs=[pl.BlockSpec((None, 1, win.shape[1]), lambda i: (i, 0, 0)),
                  pl.BlockSpec((n_keys, tq), lambda i: (0, 0))],
        out_specs=pl.BlockSpec((None, n_keys, tq), lambda i: (i, 0, 0)),
        out_shape=jax.ShapeDtypeStruct((h, n_keys, tq), F32),
        compiler_params=_cparams("parallel"),
        name="toeplitz_bias",
    )(win.reshape(h, 1, -1), mask_add)


def _t5_bias_t(table, seq):
    idx = _t5_bucket_np(_window_diagonals(seq, TQ) - (seq - TQ))
    i = np.arange(TQ)[None, :]
    j = np.arange(seq)[:, None] - (seq - TQ)
    allowed = (j < 0) | ((j // CHUNK) <= (i // CHUNK))
    far_bucket = _t5_bucket_np(np.asarray(-seq))
    is_far = np.all((_t5_bucket_np(j - i) == far_bucket) & allowed, axis=1)
    const_rows = int(np.argmin(is_far)) if not is_far.all() else seq
    return _toeplitz_bias_t(table, idx, allowed), const_rows


def _load_vt(v, vt_ref):
    n_v = v.shape[1]
    vt_ref[0:n_v, :] = v.T
    vt_ref[n_v:, :] = jnp.ones((ONES_ROWS, v.shape[0]), BF16)


def _attend_pair_phases(k_ref, k_cols, k0, n_k, kb_rows, q, lane_split, bias_rows, bias_const, vt_ref, s_ref, p_ref,
                        finish):
    tq = q.shape[0]
    halves = (slice(0, tq), slice(tq, 2 * tq))
    n_v = vt_ref.shape[0] - ONES_ROWS
    state = {"m": [None, None], "o": None}

    def score(kr):
        def run():
            if kr == 0:
                zero = jnp.zeros_like(q)
                state["qq"] = jnp.concatenate([jnp.where(lane_split, q, zero), jnp.where(lane_split, zero, q)], axis=0)
            rows = min(MM_ROWS, n_k - kr)
            s_ref[kr:kr + rows, :] = _dot_nt(k_ref[k0 + kr:k0 + kr + rows, k_cols], state["qq"])
        return run

    def running_max(kb, h):
        def run():
            cols = halves[h]
            if bias_const(kb):
                block_max = jnp.max(s_ref[kb:kb + kb_rows, cols], axis=0, keepdims=True) + bias_rows(h, kb)
            else:
                s = s_ref[kb:kb + kb_rows, cols] + bias_rows(h, kb)
                s_ref[kb:kb + kb_rows, cols] = s
                block_max = jnp.max(s, axis=0, keepdims=True)
            m = state["m"]
            m[h] = block_max if m[h] is None else jnp.maximum(m[h], block_max)
        return run

    def exponentials(kb, h):
        def run():
            cols = halves[h]
            m = state["m"][h]
            shift = m - bias_rows(h, kb) if bias_const(kb) else m
            p_ref[kb:kb + kb_rows, cols] = jnp.exp2(s_ref[kb:kb + kb_rows, cols] - shift).astype(BF16)
        return run

    def pv(kr):
        def run():
            rows = min(MM_ROWS, n_k - kr)
            part = jnp.dot(vt_ref[:, k0 + kr:k0 + kr + rows], p_ref[kr:kr + rows, :], preferred_element_type=F32)
            state["o"] = part if state["o"] is None else state["o"] + part
            if kr + rows == n_k:
                o_t = state["o"]
                finish([o_t[0:n_v, cols] * (1.0 / o_t[n_v:n_v + 1, cols]) for cols in halves])
        return run

    blocks = [(kb, h) for kb in range(0, n_k, kb_rows) for h in range(2)]
    return ([score(kr) for kr in range(0, n_k, MM_ROWS)], [running_max(kb, h) for kb, h in blocks],
            [exponentials(kb, h) for kb, h in blocks], [pv(kr) for kr in range(0, n_k, MM_ROWS)])


def _interleave(tiles):
    def merged(a, b):
        order = sorted([((i + 0.5) / len(a), 0, f) for i, f in enumerate(a)]
                       + [((i + 0.5) / len(b), 1, f) for i, f in enumerate(b)], key=lambda t: t[:2])
        for _, _, thunk in order:
            thunk()

    empty = ([], [], [], [])
    prev = empty
    for cur in list(tiles) + [empty]:
        merged(cur[0], prev[2])
        merged(cur[1], prev[3])
        prev = cur


def _diff_attn_kernel(q_ref, k_ref, v_ref, bias_ref, lam_ref, g_ref, o_ref, vt_ref, s_ref, p_ref, *,
                      lam_init, const_rows):
    lv = lam_ref[...]
    lam = (jnp.exp(jnp.sum(lv[0:1] * lv[1:2], axis=-1, keepdims=True))
           - jnp.exp(jnp.sum(lv[2:3] * lv[3:4], axis=-1, keepdims=True)) + lam_init)
    first_map = lax.broadcasted_iota(jnp.int32, (1, 2 * DA_HEAD_DIM), 1) < DA_HEAD_DIM
    seq = q_ref.shape[0]
    hd = 2 * DA_HEAD_DIM
    n_q = seq // TQ

    def tile(blk, qi):
        cols = slice(blk * hd, (blk + 1) * hd)
        n_k = (qi + 1) * TQ
        row0 = seq - n_k

        def bias_const(kb):
            return row0 + kb + KB_T5 <= const_rows

        def bias_rows(h, kb):
            n = 1 if bias_const(kb) else KB_T5
            return bias_ref[blk, row0 + kb:row0 + kb + n, :]

        def finish(outs):
            o_t = outs[0] - lam * outs[1]
            o_t = o_t * lax.rsqrt(jnp.mean(o_t * o_t, axis=0, keepdims=True) + SUBLN_EPS) * g_ref[...]
            o_ref[qi * TQ:(qi + 1) * TQ, cols] = (o_t * (1.0 - lam_init)).T.astype(o_ref.dtype)

        slot = (blk * n_q + qi) % N_SLOTS
        return _attend_pair_phases(k_ref, cols, 0, n_k, KB_T5, q_ref[qi * TQ:(qi + 1) * TQ, cols], first_map, bias_rows,
                                   bias_const, vt_ref.at[blk], s_ref.at[slot], p_ref.at[slot], finish)

    for blk in range(HEAD_BLOCKS):
        _load_vt(v_ref[:, blk * hd:(blk + 1) * hd], vt_ref.at[blk])
    _interleave(tile(blk, qi) for blk in range(HEAD_BLOCKS) for qi in range(n_q))


def _diff_attention(qkv, bias_t, const_rows, lam_vecs, subln_g, lam_init):
    b, s, _ = qkv.shape
    hd = 2 * DA_HEAD_DIM
    width = HEAD_BLOCKS * hd
    n_steps = DA_HEADS // HEAD_BLOCKS
    return pl.pallas_call(
        functools.partial(_diff_attn_kernel, lam_init=lam_init, const_rows=const_rows),
        grid=(n_steps, b),
        in_specs=[pl.BlockSpec((None, s, width), lambda h, i: (i, 0, h)),
                  pl.BlockSpec((None, s, width), lambda h, i: (i, 0, n_steps + h)),
                  pl.BlockSpec((None, s, width), lambda h, i: (i, 0, 2 * n_steps + h)),
                  pl.BlockSpec((HEAD_BLOCKS, s, TQ), lambda h, i: (h, 0, 0)),
                  pl.BlockSpec((4, DA_HEAD_DIM), lambda h, i: (0, 0)),
                  pl.BlockSpec((hd, 1), lambda h, i: (0, 0))],
        out_specs=pl.BlockSpec((None, s, width), lambda h, i: (i, 0, h)),
        out_shape=jax.ShapeDtypeStruct((b, s, D_MODEL), BF16),
        scratch_shapes=[pltpu.VMEM((HEAD_BLOCKS, hd + ONES_ROWS, s), BF16), pltpu.VMEM((N_SLOTS, s, 2 * TQ), F32),
                        pltpu.VMEM((N_SLOTS, s, 2 * TQ), BF16)],
        compiler_params=_cparams("parallel", "parallel"),
        name="diff_attn",
    )(qkv, qkv, qkv, bias_t, lam_vecs, subln_g.reshape(hd, 1))


BAND_KEYS = BAND_PREV * CHUNK + TQ_BAND


def _band_bias_t(rel_table):
    dist = BAND_PREV * CHUNK - _window_diagonals(BAND_KEYS, TQ_BAND)
    q_chunk = np.arange(TQ_BAND)[None, :] // CHUNK
    k_chunk = np.arange(BAND_KEYS)[:, None] // CHUNK
    allowed = (k_chunk >= q_chunk) & (k_chunk <= q_chunk + BAND_PREV)
    return _toeplitz_bias_t(rel_table, np.clip(dist, -REL_CLIP, REL_CLIP) + REL_CLIP, allowed)


def _chunk_attn_kernel(q_ref, k_ref, v_ref, bias_ref, o_ref, vt_ref, s_ref, p_ref):
    first_head = lax.broadcasted_iota(jnp.int32, (1, PAIR), 1) < CA_HEAD_DIM
    first_head_rows = lax.broadcasted_iota(jnp.int32, (PAIR, 1), 0) < CA_HEAD_DIM
    seq = q_ref.shape[0]
    n_q = seq // TQ_BAND

    def tile(blk, qi):
        cols = slice(blk * PAIR, (blk + 1) * PAIR)
        q0 = qi * TQ_BAND
        k0 = max(0, q0 - BAND_PREV * CHUNK)
        n_k = q0 + TQ_BAND - k0
        row0 = BAND_KEYS - n_k

        def bias_rows(h, kb):
            return bias_ref[2 * blk + h, row0 + kb:row0 + kb + KB_BAND, :]

        def finish(outs):
            o_ref[q0:q0 + TQ_BAND, cols] = jnp.where(first_head_rows, outs[0], outs[1]).T.astype(o_ref.dtype)

        slot = (blk * n_q + qi) % N_SLOTS
        return _attend_pair_phases(k_ref, cols, k0, n_k, KB_BAND, q_ref[q0:q0 + TQ_BAND, cols], first_head, bias_rows,
                                   lambda kb: False, vt_ref.at[blk], s_ref.at[slot], p_ref.at[slot], finish)

    for blk in range(PAIR_BLOCKS):
        _load_vt(v_ref[:, blk * PAIR:(blk + 1) * PAIR], vt_ref.at[blk])
    _interleave(tile(blk, qi) for blk in range(PAIR_BLOCKS) for qi in range(n_q))


def _chunk_attention(qkv, band_bias_t):
    b, s, _ = qkv.shape
    width = PAIR_BLOCKS * PAIR
    n_steps = CA_HEADS // 2 // PAIR_BLOCKS
    return pl.pallas_call(
        _chunk_attn_kernel,
        grid=(n_steps, b),
        in_specs=[pl.BlockSpec((None, s, width), lambda p, i: (i, 0, p)),
                  pl.BlockSpec((None, s, width), lambda p, i: (i, 0, n_steps + p)),
                  pl.BlockSpec((None, s, width), lambda p, i: (i, 0, 2 * n_steps + p)),
                  pl.BlockSpec((2 * PAIR_BLOCKS, BAND_KEYS, TQ_BAND), lambda p, i: (p, 0, 0))],
        out_specs=pl.BlockSpec((None, s, width), lambda p, i: (i, 0, p)),
        out_shape=jax.ShapeDtypeStruct((b, s, D_MODEL), BF16),
        scratch_shapes=[pltpu.VMEM((PAIR_BLOCKS, PAIR + ONES_ROWS, s), BF16), pltpu.VMEM((N_SLOTS, BAND_KEYS, 2 * TQ_BAND), F32),
                        pltpu.VMEM((N_SLOTS, BAND_KEYS, 2 * TQ_BAND), BF16)],
        compiler_params=_cparams("parallel", "parallel"),
        name="chunk_attn",
    )(qkv, qkv, qkv, band_bias_t)


def _swiglu_partial(h, wg, wu, wd):
    g = jnp.dot(h, wg, preferred_element_type=F32)
    u = jnp.dot(h, wu, preferred_element_type=F32)
    a = (g * (0.5 * jnp.tanh(0.5 * g) + 0.5) * u).astype(BF16)
    return jnp.dot(a, wd, preferred_element_type=F32)


def _dense_ffn_kernel(a_ref, wo_ref, x_ref, g_ref, wgu_ref, wd_ref, o_ref, *, tf):
    x1 = x_ref[...] + jnp.dot(a_ref[...], wo_ref[...], preferred_element_type=F32)
    h = _rmsnorm(x1, g_ref[...], RMS_EPS).astype(BF16)
    ff = wd_ref.shape[0]
    y = x1
    for lo in range(0, ff, tf):
        y = y + _swiglu_partial(h, wgu_ref[:, lo:lo + tf], wgu_ref[:, ff + lo:ff + lo + tf], wd_ref[lo:lo + tf, :])
    o_ref[...] = y


def _attn_out_dense_ffn(a, w_o, x, g, w_gate_up, w_down, tf):
    n, d = x.shape

    def resident(w):
        return pl.BlockSpec(w.shape, lambda i: (0, 0), pipeline_mode=pl.Buffered(1))

    return pl.pallas_call(
        functools.partial(_dense_ffn_kernel, tf=tf),
        grid=(n // TM,),
        in_specs=[pl.BlockSpec((TM, d), lambda i: (i, 0)),
                  resident(w_o),
                  pl.BlockSpec((TM, d), lambda i: (i, 0)),
                  pl.BlockSpec((1, d), lambda i: (0, 0)),
                  resident(w_gate_up),
                  resident(w_down)],
        out_specs=pl.BlockSpec((TM, d), lambda i: (i, 0)),
        out_shape=jax.ShapeDtypeStruct((n, d), F32),
        compiler_params=_cparams("parallel"),
        name="dense_ffn",
    )(a, w_o, x, g.reshape(1, d), w_gate_up, w_down)


def _router_kernel(a_ref, wo_ref, x_ref, g_ref, r_ref, x1_ref, h_ref, gate_ref, mask_ref, mask_t_ref, cnt_ref):
    x1 = x_ref[...] + jnp.dot(a_ref[...], wo_ref[...], preferred_element_type=F32)
    x1_ref[...] = x1
    h = _rmsnorm(x1, g_ref[...], RMS_EPS)
    h_hi = h.astype(BF16)
    h_ref[...] = h_hi
    r = r_ref[...]
    r_hi = r.astype(BF16)
    r_lo = (r - r_hi.astype(F32)).astype(BF16)
    h_lo = (h - h_hi.astype(F32)).astype(BF16)
    logits = (jnp.dot(h_hi, r_hi, preferred_element_type=F32)
              + (jnp.dot(h_hi, r_lo, preferred_element_type=F32) + jnp.dot(h_lo, r_hi, preferred_element_type=F32)))
    lane = lax.broadcasted_iota(jnp.int32, logits.shape, 1)
    logits = jnp.where(lane < N_EXPERTS, logits, -jnp.inf)
    v1 = jnp.max(logits, axis=-1, keepdims=True)
    i1 = jnp.min(jnp.where(logits == v1, lane, LANES), axis=-1, keepdims=True)
    m1 = lane == i1
    rest = jnp.where(m1, -jnp.inf, logits)
    v2 = jnp.max(rest, axis=-1, keepdims=True)
    i2 = jnp.min(jnp.where(rest == v2, lane, LANES), axis=-1, keepdims=True)
    m2 = lane == i2
    e2 = jnp.exp(v2 - v1)
    g1 = 1.0 / (1.0 + e2)
    g2 = e2 / (1.0 + e2)
    gate_ref[...] = jnp.where(m1, g1, 0.0) + jnp.where(m2, g2, 0.0)
    mask = jnp.where(m1 | m2, 1.0, 0.0)
    mask_ref[...] = mask
    mask_t_ref[...] = mask.T[0:N_EXPERTS, :]
    cnt_ref[...] = jnp.broadcast_to(jnp.sum(mask, axis=0, keepdims=True), cnt_ref.shape)


def _attn_out_router(a, w_o, x, g, router):
    n, d = x.shape
    t = n // TT
    r_pad = jnp.zeros((d, LANES), F32).at[:, :N_EXPERTS].set(router)
    return pl.pallas_call(
        _router_kernel,
        grid=(t,),
        in_specs=[pl.BlockSpec((TT, d), lambda i: (i, 0)),
                  pl.BlockSpec((d, d), lambda i: (0, 0)),
                  pl.BlockSpec((TT, d), lambda i: (i, 0)),
                  pl.BlockSpec((1, d), lambda i: (0, 0)),
                  pl.BlockSpec((d, LANES), lambda i: (0, 0))],
        out_specs=[pl.BlockSpec((TT, d), lambda i: (i, 0)),
                   pl.BlockSpec((TT, d), lambda i: (i, 0)),
                   pl.BlockSpec((TT, LANES), lambda i: (i, 0)),
                   pl.BlockSpec((TT, LANES), lambda i: (i, 0)),
                   pl.BlockSpec((N_EXPERTS, TT), lambda i: (0, i)),
                   pl.BlockSpec((None, 8, LANES), lambda i: (i, 0, 0))],
        out_shape=[jax.ShapeDtypeStruct((n, d), F32),
                   jax.ShapeDtypeStruct((n, d), BF16),
                   jax.ShapeDtypeStruct((n, LANES), F32),
                   jax.ShapeDtypeStruct((n, LANES), F32),
                   jax.ShapeDtypeStruct((N_EXPERTS, n), F32),
                   jax.ShapeDtypeStruct((t, 8, LANES), F32)],
        compiler_params=_cparams("parallel"),
        name="router",
    )(a, w_o, x, g.reshape(1, d), r_pad)


def _segment_copies(n_rows, make_copy):
    off = jnp.int32(0)
    size = TT
    while size >= BF16_ROWS:
        take = (n_rows & size) != 0

        @pl.when(take)
        def _(off=off, size=size):
            make_copy(off, size)

        off = off + jnp.where(take, size, 0)
        size //= 2


def _dispatch_kernel(start_ref, rows_ref, mask_t_ref, h_ref, xs_in_ref, xs_ref, blk_ref, sem_ref):
    del xs_in_ref
    t = pl.program_id(0)
    slot = t % 2
    mask_t = mask_t_ref[...]
    row = lax.broadcasted_iota(jnp.int32, (TT, TT), 0)
    col = lax.broadcasted_iota(jnp.int32, (TT, TT), 1)
    before = jnp.where(row < col, 1.0, 0.0).astype(BF16)
    rank = jnp.dot(mask_t.astype(BF16), before, preferred_element_type=F32)
    row_f = row.astype(F32)
    sels = []
    for e in range(N_EXPERTS):
        pick = (row_f == rank[e:e + 1, :]) & (mask_t[e:e + 1, :] > 0.0)
        sels.append(jnp.where(pick, 1.0, 0.0).astype(BF16))
    packed = jnp.dot(jnp.concatenate(sels, axis=0), h_ref[...], preferred_element_type=F32)
    blk_ref[slot] = packed.astype(BF16).reshape(blk_ref.shape[1:])

    def copies(step, step_slot, wait):
        for e in range(N_EXPERTS):
            start = pl.multiple_of(start_ref[step * N_EXPERTS + e], BF16_ROWS)

            def make_copy(off, size, e=e, start=start):
                src = blk_ref.at[step_slot, e, pl.ds(pl.multiple_of(off, BF16_ROWS), size)]
                dst = xs_ref.at[pl.ds(pl.multiple_of(start + off, BF16_ROWS), size)]
                cp = pltpu.make_async_copy(src, dst, sem_ref.at[step_slot])
                cp.wait() if wait else cp.start()

            _segment_copies(rows_ref[step * N_EXPERTS + e], make_copy)

    copies(t, slot, wait=False)

    @pl.when(t > 0)
    def _():
        copies(t - 1, 1 - slot, wait=True)

    @pl.when(t == pl.num_programs(0) - 1)
    def _():
        copies(t, slot, wait=True)


def _dispatch(seg_start, seg_rows, mask_t, h, n_rows):
    n, d = h.shape
    t = n // TT
    xs0 = jnp.zeros((n_rows, d), BF16)
    grid_spec = pltpu.PrefetchScalarGridSpec(
        num_scalar_prefetch=2,
        grid=(t,),
        in_specs=[pl.BlockSpec((N_EXPERTS, TT), lambda i, *_: (0, i)),
                  pl.BlockSpec((TT, d), lambda i, *_: (i, 0)),
                  pl.BlockSpec(memory_space=pl.ANY)],
        out_specs=pl.BlockSpec(memory_space=pl.ANY),
        scratch_shapes=[pltpu.VMEM((2, N_EXPERTS, TT, d), BF16), pltpu.SemaphoreType.DMA((2,))],
    )
    return pl.pallas_call(
        _dispatch_kernel,
        grid_spec=grid_spec,
        out_shape=jax.ShapeDtypeStruct((n_rows, d), BF16),
        input_output_aliases={4: 0},
        compiler_params=_cparams("arbitrary"),
        name="moe_dispatch",
    )(seg_start, seg_rows, mask_t, h, xs0)


def _expert_kernel(tile_expert_ref, tile_rows_ref, xs_ref, wg_ref, wu_ref, wd_ref, o_ref,
                   wg_bf_ref, wu_bf_ref, wd_bf_ref, acc_ref):
    del tile_expert_ref
    j = pl.program_id(0)
    f = pl.program_id(1)
    last = pl.num_programs(1) - 1
    rows = tile_rows_ref[j]

    @pl.when(f == 0)
    def _():
        acc_ref[...] = jnp.zeros_like(acc_ref)

    @pl.when(rows == TE)
    def _():
        acc_ref[...] += _swiglu_partial(xs_ref[...], wg_ref[...].astype(BF16), wu_ref[...].astype(BF16),
                                        wd_ref[...].astype(BF16))

    @pl.when((rows > 0) & (rows < TE))
    def _():
        wg_bf_ref[...] = wg_ref[...].astype(BF16)
        wu_bf_ref[...] = wu_ref[...].astype(BF16)
        wd_bf_ref[...] = wd_ref[...].astype(BF16)
        for lo in range(0, TE, TM):
            @pl.when(rows > lo)
            def _(lo=lo):
                acc_ref[lo:lo + TM, :] += _swiglu_partial(xs_ref[lo:lo + TM, :], wg_bf_ref[...], wu_bf_ref[...],
                                                          wd_bf_ref[...])

    @pl.when(f == last)
    def _():
        o_ref[...] = acc_ref[...].astype(o_ref.dtype)


def _expert_ffn(tile_expert, tile_rows, xs, w_gate_up, w_down, layer, tf):
    n_rows, d = xs.shape
    ff = w_down.shape[2]
    nf = ff // tf

    def f_eff(j, f, tr):
        return jnp.where(tr[j] > 0, f, nf - 1)

    grid_spec = pltpu.PrefetchScalarGridSpec(
        num_scalar_prefetch=2,
        grid=(n_rows // TE, nf),
        in_specs=[pl.BlockSpec((TE, d), lambda j, f, te, tr: (j, 0)),
                  pl.BlockSpec((None, None, d, tf), lambda j, f, te, tr: (layer, te[j], 0, f_eff(j, f, tr))),
                  pl.BlockSpec((None, None, d, tf), lambda j, f, te, tr: (layer, te[j], 0, nf + f_eff(j, f, tr))),
                  pl.BlockSpec((None, None, tf, d), lambda j, f, te, tr: (layer, te[j], f_eff(j, f, tr), 0))],
        out_specs=pl.BlockSpec((TE, d), lambda j, f, te, tr: (j, 0)),
        scratch_shapes=[pltpu.VMEM((d, tf), BF16), pltpu.VMEM((d, tf), BF16), pltpu.VMEM((tf, d), BF16),
                        pltpu.VMEM((TE, d), F32)],
    )
    return pl.pallas_call(
        _expert_kernel,
        grid_spec=grid_spec,
        out_shape=jax.ShapeDtypeStruct((n_rows, d), BF16),
        compiler_params=_cparams("parallel", "arbitrary"),
        name="moe_experts",
    )(tile_expert, tile_rows, xs, w_gate_up, w_gate_up, w_down)


def _combine_kernel(start_ref, rows_ref, x_ref, gate_ref, mask_ref, out_gain_ref, ys_ref, o_ref, win_ref, sem_ref,
                    *, out_norm):
    t = pl.program_id(0)
    n_t = pl.num_programs(0)
    slot = t % 2

    def copies(step, step_slot, wait):
        for e in range(N_EXPERTS):
            start = pl.multiple_of(start_ref[step * N_EXPERTS + e], BF16_ROWS)

            def make_copy(off, size, e=e, start=start):
                src = ys_ref.at[pl.ds(pl.multiple_of(start + off, BF16_ROWS), size)]
                dst = win_ref.at[step_slot, e, pl.ds(pl.multiple_of(off, BF16_ROWS), size)]
                cp = pltpu.make_async_copy(src, dst, sem_ref.at[step_slot])
                cp.wait() if wait else cp.start()

            _segment_copies(rows_ref[step * N_EXPERTS + e], make_copy)

    @pl.when(t == 0)
    def _():
        win_ref[...] = jnp.zeros_like(win_ref)
        copies(0, 0, wait=False)

    @pl.when(t + 1 < n_t)
    def _():
        copies(t + 1, 1 - slot, wait=False)

    copies(t, slot, wait=True)

    mask = mask_ref[...]
    gate = gate_ref[...]
    row = lax.broadcasted_iota(jnp.int32, (TT, TT), 0)
    col = lax.broadcasted_iota(jnp.int32, (TT, TT), 1)
    before = jnp.where(col < row, 1.0, 0.0).astype(BF16)
    rank = jnp.dot(before, mask.astype(BF16), preferred_element_type=F32)
    col_f = col.astype(F32)
    y = x_ref[...]
    for e in range(N_EXPERTS):
        pick = (col_f == rank[:, e:e + 1]) & (mask[:, e:e + 1] > 0.0)
        sel = jnp.where(pick, 1.0, 0.0).astype(BF16)
        y = y + gate[:, e:e + 1] * jnp.dot(sel, win_ref[slot, e], preferred_element_type=F32)
    o_ref[...] = _rmsnorm(y, out_gain_ref[...], RMS_EPS) if out_norm else y


def _combine(seg_start, seg_rows, x, gate, mask, ys, out_gain, out_norm):
    n, d = x.shape
    t = n // TT
    grid_spec = pltpu.PrefetchScalarGridSpec(
        num_scalar_prefetch=2,
        grid=(t,),
        in_specs=[pl.BlockSpec((TT, d), lambda i, *_: (i, 0)),
                  pl.BlockSpec((TT, LANES), lambda i, *_: (i, 0)),
                  pl.BlockSpec((TT, LANES), lambda i, *_: (i, 0)),
                  pl.BlockSpec((1, d), lambda i, *_: (0, 0)),
                  pl.BlockSpec(memory_space=pl.ANY)],
        out_specs=pl.BlockSpec((TT, d), lambda i, *_: (i, 0)),
        scratch_shapes=[pltpu.VMEM((2, N_EXPERTS, TT, d), BF16), pltpu.SemaphoreType.DMA((2,))],
    )
    return pl.pallas_call(
        functools.partial(_combine_kernel, out_norm=out_norm),
        grid_spec=grid_spec,
        out_shape=jax.ShapeDtypeStruct((n, d), F32),
        compiler_params=_cparams("arbitrary"),
        name="moe_combine",
    )(seg_start, seg_rows, x, gate, mask, out_gain.reshape(1, d), ys)


def _attn_out_moe_ffn(a, w_o, x, g, router, w_gate_up, w_down, layer, tf, out_gain, out_norm):
    n, d = x.shape
    t = n // TT
    x, h, gate, mask, mask_t, cnt = _attn_out_router(a, w_o, x, g, router)
    cnt = cnt[:, 0, :N_EXPERTS].astype(jnp.int32)
    seg_rows = _round_up(cnt, BF16_ROWS)
    expert_rows = jnp.sum(seg_rows, axis=0)
    region = _round_up(expert_rows, TE)
    region_end = jnp.cumsum(region)
    seg_start = (region_end - region)[None, :] + jnp.cumsum(seg_rows, axis=0) - seg_rows
    n_tiles = (_round_up(2 * n + t * N_EXPERTS * (BF16_ROWS - 1), TE) + N_EXPERTS * TE) // TE
    tile_end = region_end // TE
    tile_ids = jnp.arange(n_tiles, dtype=jnp.int32)
    last_tile = jnp.minimum(tile_ids, tile_end[-1] - 1)
    tile_expert = jnp.sum((last_tile[:, None] >= tile_end[None, :]).astype(jnp.int32), axis=1)
    tile_expert = jnp.minimum(tile_expert, N_EXPERTS - 1)
    rows_before = (tile_ids - (tile_end - region // TE)[tile_expert]) * TE
    tile_rows = jnp.where(tile_ids < tile_end[-1], jnp.clip(expert_rows[tile_expert] - rows_before, 0, TE), 0)
    seg_start = seg_start.reshape(-1).astype(jnp.int32)
    seg_rows = seg_rows.reshape(-1).astype(jnp.int32)

    xs = _dispatch(seg_start, seg_rows, mask_t, h, n_tiles * TE)
    ys = _expert_ffn(tile_expert, tile_rows.astype(jnp.int32), xs, w_gate_up, w_down, layer, tf)
    return _combine(seg_start, seg_rows, x, gate, mask, ys, out_gain, out_norm)


def kernel(x, attn_norm, ffn_norm, final_norm, t5_rel_bias, diff_w_qkv, diff_w_o, diff_lambda_q1, diff_lambda_k1, diff_lambda_q2, diff_lambda_k2, diff_subln, chunk_w_qkv, chunk_w_o, chunk_rel_bias, dense_w_gate_up, dense_w_down, moe_router, moe_w_gate_up, moe_w_down):
    b, s, d = x.shape
    n = b * s
    xf = x.reshape(n, d)
    t5_bias, t5_const_rows = _t5_bias_t(t5_rel_bias, s)
    for i in range(DEPTH):
        j = i // 2
        if i % 2 == 0:
            qkv = _norm_matmul(xf, attn_norm[i], diff_w_qkv[j].astype(BF16))
            lam_init = 0.8 - 0.6 * math.exp(-0.3 * i)
            lam_vecs = jnp.stack([diff_lambda_q1[j], diff_lambda_k1[j], diff_lambda_q2[j], diff_lambda_k2[j]])
            a = _diff_attention(qkv.reshape(b, s, 3 * d), t5_bias, t5_const_rows, lam_vecs, diff_subln[j], lam_init)
            xf = _attn_out_dense_ffn(a.reshape(n, d), diff_w_o[j].astype(BF16), xf, ffn_norm[i],
                                     dense_w_gate_up[j].astype(BF16), dense_w_down[j].astype(BF16),
                                     tf=dense_w_down.shape[1] // 2)
        else:
            qkv = _norm_matmul(xf, attn_norm[i], chunk_w_qkv[j].astype(BF16))
            a = _chunk_attention(qkv.reshape(b, s, 3 * d), _band_bias_t(chunk_rel_bias[j]))
            xf = _attn_out_moe_ffn(a.reshape(n, d), chunk_w_o[j].astype(BF16), xf, ffn_norm[i], moe_router[j],
                                   moe_w_gate_up, moe_w_down, layer=j, tf=moe_w_down.shape[2] // 7,
                                   out_gain=final_norm, out_norm=i == DEPTH - 1)
    return xf.reshape(b, s, d)
```
